```python
import jax, jax.numpy as jnp
from jax import lax
import numpy as np

D_MODEL = 1024
BATCH = 8
SEQ = 8192
DEPTH = 2

D_MIX = D_MODEL
HEAD_DIM = 64
ATTN_WIDTH = D_MIX // 2
N_HEADS = ATTN_WIDTH // HEAD_DIM
N_KV_HEADS = 2
GROUP = N_HEADS // N_KV_HEADS
KV_WIDTH = N_KV_HEADS * HEAD_DIM
CONV_CHANNELS = D_MIX - ATTN_WIDTH
CONV_WIDTH = 31
WINDOW = 128
BLOCK = 128
ROPE_THETA = 10000.0
D_FF = ((8 * D_MODEL // 3 + 127) // 128) * 128
D_IN = ATTN_WIDTH + 2 * KV_WIDTH + 2 * CONV_CHANNELS
EPS = 1e-5

kernel_name = "hybrid_swa_sink_conformer_conv_macaron"


def rms_norm(x, g):
    xf = x.astype(jnp.float32)
    y = xf * lax.rsqrt(jnp.mean(xf * xf, axis=-1, keepdims=True) + EPS)
    return (y * g.astype(jnp.float32)).astype(x.dtype)


def layer_norm(x, g, b):
    xf = x.astype(jnp.float32)
    mu = jnp.mean(xf, axis=-1, keepdims=True)
    xc = xf - mu
    y = xc * lax.rsqrt(jnp.mean(xc * xc, axis=-1, keepdims=True) + EPS)
    return (y * g.astype(jnp.float32) + b.astype(jnp.float32)).astype(x.dtype)


def swiglu(h, w_gate, w_up, w_down):
    return (jax.nn.silu(h @ w_gate) * (h @ w_up)) @ w_down


def rope_tables(positions):
    inv_freq = 1.0 / (ROPE_THETA ** (jnp.arange(0, HEAD_DIM, 2, dtype=jnp.float32) / HEAD_DIM))
    ang = positions.astype(jnp.float32)[..., None] * inv_freq
    return jnp.cos(ang), jnp.sin(ang)


def apply_rope(t, cos, sin):
    tf = t.astype(jnp.float32)
    t1, t2 = jnp.split(tf, 2, axis=-1)
    c = cos[:, :, None, :]
    s = sin[:, :, None, :]
    return jnp.concatenate([t1 * c - t2 * s, t2 * c + t1 * s], axis=-1).astype(t.dtype)


def sliding_window_attention(q, k, v, sinks):
    B, S = q.shape[0], q.shape[1]
    nb = S // BLOCK
    qb = q.reshape(B, nb, BLOCK, N_KV_HEADS, GROUP, HEAD_DIM).astype(jnp.float32)

    def band(t):
        tb = t.reshape(B, nb, BLOCK, N_KV_HEADS, HEAD_DIM)
        prev = jnp.pad(tb[:, :-1], ((0, 0), (1, 0), (0, 0), (0, 0), (0, 0)))
        return jnp.concatenate([prev, tb], axis=2).astype(jnp.float32)

    kb, vb = band(k), band(v)
    scores = jnp.einsum('bnqkgd,bnjkd->bnkgqj', qb, kb) * (HEAD_DIM ** -0.5)

    q_local = jnp.arange(BLOCK)[:, None] + BLOCK
    k_local = jnp.arange(2 * BLOCK)[None, :]
    rel = q_local - k_local
    in_window = (rel >= 0) & (rel < WINDOW)
    block_valid = (jnp.arange(nb)[:, None] > 0) | (k_local >= BLOCK)
    mask = in_window[None, :, :] & block_valid[:, None, :]
    neg = jnp.finfo(jnp.float32).min
    scores = jnp.where(mask[None, :, None, None, :, :], scores, neg)

    sink = sinks.astype(jnp.float32).reshape(N_KV_HEADS, GROUP)[None, None, :, :, None, None]
    m = jnp.maximum(jnp.max(scores, axis=-1, keepdims=True), sink)
    p = jnp.exp(scores - m)
    denom = jnp.sum(p, axis=-1, keepdims=True) + jnp.exp(sink - m)
    probs = p / denom
    out = jnp.einsum('bnkgqj,bnjkd->bnqkgd', probs, vb)
    return out.reshape(B, S, N_HEADS * HEAD_DIM).astype(q.dtype)


def conformer_conv(u, conv_w, conv_b, ln_g, ln_b):
    a, gate = jnp.split(u, 2, axis=-1)
    h = a * jax.nn.sigmoid(gate)
    h = lax.conv_general_dilated(
        h, conv_w[:, None, :].astype(h.dtype),
        window_strides=(1,), padding=[(CONV_WIDTH - 1, 0)],
        dimension_numbers=('NWC', 'WIO', 'NWC'),
        feature_group_count=CONV_CHANNELS) + conv_b
    h = layer_norm(h, ln_g, ln_b)
    return jax.nn.silu(h)


def _fwd_setup_inputs(seed: int = 0) -> dict:
    key = jax.random.key(seed)
    ks = jax.random.split(key, 20)
    f32 = jnp.float32

    def w(k, shape, fan_in):
        return jax.random.normal(k, shape, f32) * (fan_in ** -0.5)

    def gain(k, shape):
        return 1.0 + 0.05 * jax.random.normal(k, shape, f32)

    x = jax.random.normal(ks[0], (BATCH, SEQ, D_MODEL), f32)
    positions = jnp.broadcast_to(jnp.arange(SEQ, dtype=jnp.int32), (BATCH, SEQ))
    return {
        "x": x,
        "positions": positions,
        "ffn1_norm": gain(ks[1], (DEPTH, D_MODEL)),
        "ffn1_w_gate": w(ks[2], (DEPTH, D_MODEL, D_FF), D_MODEL),
        "ffn1_w_up": w(ks[3], (DEPTH, D_MODEL, D_FF), D_MODEL),
        "ffn1_w_down": w(ks[4], (DEPTH, D_FF, D_MODEL), D_FF),
        "mix_norm": gain(ks[5], (DEPTH, D_MODEL)),
        "w_in": w(ks[6], (DEPTH, D_MODEL, D_IN), D_MODEL),
        "conv_w": w(ks[7], (DEPTH, CONV_WIDTH, CONV_CHANNELS), CONV_WIDTH),
        "conv_b": 0.02 * jax.random.normal(ks[8], (DEPTH, CONV_CHANNELS), f32),
        "conv_ln_g": gain(ks[9], (DEPTH, CONV_CHANNELS)),
        "conv_ln_b": 0.02 * jax.random.normal(ks[10], (DEPTH, CONV_CHANNELS), f32),
        "attn_sinks": 0.5 * jax.random.normal(ks[11], (DEPTH, N_HEADS), f32),
        "w_out": w(ks[12], (DEPTH, D_MIX, D_MODEL), D_MIX),
        "ffn2_norm": gain(ks[13], (DEPTH, D_MODEL)),
        "ffn2_w_gate": w(ks[14], (DEPTH, D_MODEL, D_FF), D_MODEL),
        "ffn2_w_up": w(ks[15], (DEPTH, D_MODEL, D_FF), D_MODEL),
        "ffn2_w_down": w(ks[16], (DEPTH, D_FF, D_MODEL), D_FF),
        "final_norm": gain(ks[17], (D_MODEL,)),
    }


def _fwd_reference(x, positions, ffn1_norm, ffn1_w_gate, ffn1_w_up, ffn1_w_down,
              mix_norm, w_in, conv_w, conv_b, conv_ln_g, conv_ln_b, attn_sinks, w_out,
              ffn2_norm, ffn2_w_gate, ffn2_w_up, ffn2_w_down, final_norm):
    B, S = x.shape[0], x.shape[1]
    cos, sin = rope_tables(positions)
    q_end = ATTN_WIDTH
    k_end = q_end + KV_WIDTH
    v_end = k_end + KV_WIDTH
    for l in range(DEPTH):
        x = x + 0.5 * swiglu(rms_norm(x, ffn1_norm[l]), ffn1_w_gate[l], ffn1_w_up[l], ffn1_w_down[l])
        h = rms_norm(x, mix_norm[l])
        p = h @ w_in[l]
        q = apply_rope(p[..., :q_end].reshape(B, S, N_HEADS, HEAD_DIM), cos, sin)
        k = apply_rope(p[..., q_end:k_end].reshape(B, S, N_KV_HEADS, HEAD_DIM), cos, sin)
        v = p[..., k_end:v_end].reshape(B, S, N_KV_HEADS, HEAD_DIM)
        u = p[..., v_end:]
        attn_out = sliding_window_attention(q, k, v, attn_sinks[l])
        conv_out = conformer_conv(u, conv_w[l], conv_b[l], conv_ln_g[l], conv_ln_b[l])
        x = x + jnp.concatenate([attn_out, conv_out], axis=-1) @ w_out[l]
        x = x + 0.5 * swiglu(rms_norm(x, ffn2_norm[l]), ffn2_w_gate[l], ffn2_w_up[l], ffn2_w_down[l])
    return rms_norm(x, final_norm)


import jax as _jax
import jax.numpy as _jnp

TWIN_FORMAT = 'train_step'
FWD_PARAMS = ['x', 'positions', 'ffn1_norm', 'ffn1_w_gate', 'ffn1_w_up', 'ffn1_w_down', 'mix_norm', 'w_in', 'conv_w', 'conv_b', 'conv_ln_g', 'conv_ln_b', 'attn_sinks', 'w_out', 'ffn2_norm', 'ffn2_w_gate', 'ffn2_w_up', 'ffn2_w_down', 'final_norm']
TWIN_WEIGHTS = ['ffn1_norm', 'ffn1_w_gate', 'ffn1_w_up', 'ffn1_w_down', 'mix_norm', 'w_in', 'conv_w', 'conv_b', 'conv_ln_g', 'conv_ln_b', 'attn_sinks', 'w_out', 'ffn2_norm', 'ffn2_w_gate', 'ffn2_w_up', 'ffn2_w_down', 'final_norm']
TWIN_DIFF_INPUT = 'x'
TWIN_INPUTS = ['x', 'positions', 'ffn1_norm', 'ffn1_w_gate', 'ffn1_w_up', 'ffn1_w_down', 'mix_norm', 'w_in', 'conv_w', 'conv_b', 'conv_ln_g', 'conv_ln_b', 'attn_sinks', 'w_out', 'ffn2_norm', 'ffn2_w_gate', 'ffn2_w_up', 'ffn2_w_down', 'final_norm', 'loss_target', 'm_ffn1_norm', 'm_ffn1_w_gate', 'm_ffn1_w_up', 'm_ffn1_w_down', 'm_mix_norm', 'm_w_in', 'm_conv_w', 'm_conv_b', 'm_conv_ln_g', 'm_conv_ln_b', 'm_attn_sinks', 'm_w_out', 'm_ffn2_norm', 'm_ffn2_w_gate', 'm_ffn2_w_up', 'm_ffn2_w_down', 'm_final_norm', 'v_ffn1_norm', 'v_ffn1_w_gate', 'v_ffn1_w_up', 'v_ffn1_w_down', 'v_mix_norm', 'v_w_in', 'v_conv_w', 'v_conv_b', 'v_conv_ln_g', 'v_conv_ln_b', 'v_attn_sinks', 'v_w_out', 'v_ffn2_norm', 'v_ffn2_w_gate', 'v_ffn2_w_up', 'v_ffn2_w_down', 'v_final_norm']
TWIN_OUTPUTS = ['loss', 'grad_x', 'grad_ffn1_norm', 'grad_ffn1_w_gate', 'grad_ffn1_w_up', 'grad_ffn1_w_down', 'grad_mix_norm', 'grad_w_in', 'grad_conv_w', 'grad_conv_b', 'grad_conv_ln_g', 'grad_conv_ln_b', 'grad_attn_sinks', 'grad_w_out', 'grad_ffn2_norm', 'grad_ffn2_w_gate', 'grad_ffn2_w_up', 'grad_ffn2_w_down', 'grad_final_norm', 'delta_ffn1_norm', 'delta_ffn1_w_gate', 'delta_ffn1_w_up', 'delta_ffn1_w_down', 'delta_mix_norm', 'delta_w_in', 'delta_conv_w', 'delta_conv_b', 'delta_conv_ln_g', 'delta_conv_ln_b', 'delta_attn_sinks', 'delta_w_out', 'delta_ffn2_norm', 'delta_ffn2_w_gate', 'delta_ffn2_w_up', 'delta_ffn2_w_down', 'delta_final_norm', 'new_m_ffn1_norm', 'new_m_ffn1_w_gate', 'new_m_ffn1_w_up', 'new_m_ffn1_w_down', 'new_m_mix_norm', 'new_m_w_in', 'new_m_conv_w', 'new_m_conv_b', 'new_m_conv_ln_g', 'new_m_conv_ln_b', 'new_m_attn_sinks', 'new_m_w_out', 'new_m_ffn2_norm', 'new_m_ffn2_w_gate', 'new_m_ffn2_w_up', 'new_m_ffn2_w_down', 'new_m_final_norm', 'new_v_ffn1_norm', 'new_v_ffn1_w_gate', 'new_v_ffn1_w_up', 'new_v_ffn1_w_down', 'new_v_mix_norm', 'new_v_w_in', 'new_v_conv_w', 'new_v_conv_b', 'new_v_conv_ln_g', 'new_v_conv_ln_b', 'new_v_attn_sinks', 'new_v_w_out', 'new_v_ffn2_norm', 'new_v_ffn2_w_gate', 'new_v_ffn2_w_up', 'new_v_ffn2_w_down', 'new_v_final_norm']
TWIN_LEAF_KINDS = {'loss': 'loss', 'grad_x': 'grad_x', 'grad_ffn1_norm': 'grad_w', 'grad_ffn1_w_gate': 'grad_w', 'grad_ffn1_w_up': 'grad_w', 'grad_ffn1_w_down': 'grad_w', 'grad_mix_norm': 'grad_w', 'grad_w_in': 'grad_w', 'grad_conv_w': 'grad_w', 'grad_conv_b': 'grad_w', 'grad_conv_ln_g': 'grad_w', 'grad_conv_ln_b': 'grad_w', 'grad_attn_sinks': 'grad_w', 'grad_w_out': 'grad_w', 'grad_ffn2_norm': 'grad_w', 'grad_ffn2_w_gate': 'grad_w', 'grad_ffn2_w_up': 'grad_w', 'grad_ffn2_w_down': 'grad_w', 'grad_final_norm': 'grad_w', 'delta_ffn1_norm': 'delta_w', 'delta_ffn1_w_gate': 'delta_w', 'delta_ffn1_w_up': 'delta_w', 'delta_ffn1_w_down': 'delta_w', 'delta_mix_norm': 'delta_w', 'delta_w_in': 'delta_w', 'delta_conv_w': 'delta_w', 'delta_conv_b': 'delta_w', 'delta_conv_ln_g': 'delta_w', 'delta_conv_ln_b': 'delta_w', 'delta_attn_sinks': 'delta_w', 'delta_w_out': 'delta_w', 'delta_ffn2_norm': 'delta_w', 'delta_ffn2_w_gate': 'delta_w', 'delta_ffn2_w_up': 'delta_w', 'delta_ffn2_w_down': 'delta_w', 'delta_final_norm': 'delta_w', 'new_m_ffn1_norm': 'new_m', 'new_m_ffn1_w_gate': 'new_m', 'new_m_ffn1_w_up': 'new_m', 'new_m_ffn1_w_down': 'new_m', 'new_m_mix_norm': 'new_m', 'new_m_w_in': 'new_m', 'new_m_conv_w': 'new_m', 'new_m_conv_b': 'new_m', 'new_m_conv_ln_g': 'new_m', 'new_m_conv_ln_b': 'new_m', 'new_m_attn_sinks': 'new_m', 'new_m_w_out': 'new_m', 'new_m_ffn2_norm': 'new_m', 'new_m_ffn2_w_gate': 'new_m', 'new_m_ffn2_w_up': 'new_m', 'new_m_ffn2_w_down': 'new_m', 'new_m_final_norm': 'new_m', 'new_v_ffn1_norm': 'new_v', 'new_v_ffn1_w_gate': 'new_v', 'new_v_ffn1_w_up': 'new_v', 'new_v_ffn1_w_down': 'new_v', 'new_v_mix_norm': 'new_v', 'new_v_w_in': 'new_v', 'new_v_conv_w': 'new_v', 'new_v_conv_b': 'new_v', 'new_v_conv_ln_g': 'new_v', 'new_v_conv_ln_b': 'new_v', 'new_v_attn_sinks': 'new_v', 'new_v_w_out': 'new_v', 'new_v_ffn2_norm': 'new_v', 'new_v_ffn2_w_gate': 'new_v', 'new_v_ffn2_w_up': 'new_v', 'new_v_ffn2_w_down': 'new_v', 'new_v_final_norm': 'new_v'}


def _forward(args):
    return _fwd_reference(*[args[k] for k in FWD_PARAMS])


def _output_shape():
    def fwd():
        inp = _fwd_setup_inputs(0)
        return _fwd_reference(*[inp[k] for k in FWD_PARAMS])
    out = _jax.eval_shape(fwd)
    return out.shape, out.dtype

N_MICROBATCH = 1
ADAM_LR = 0.001
ADAM_B1 = 0.9
ADAM_B2 = 0.999
ADAM_EPS = 1e-08
ADAM_WD = 0.01
ADAM_STEP = 10
PER_EXAMPLE_BATCH_AXIS = {'x': 0, 'positions': 0, 'loss_target': 0}
SHARED_INPUTS = []
_WEIGHT_DTYPES = {'ffn1_norm': _jnp.float32, 'ffn1_w_gate': _jnp.float32, 'ffn1_w_up': _jnp.float32, 'ffn1_w_down': _jnp.float32, 'mix_norm': _jnp.float32, 'w_in': _jnp.float32, 'conv_w': _jnp.float32, 'conv_b': _jnp.float32, 'conv_ln_g': _jnp.float32, 'conv_ln_b': _jnp.float32, 'attn_sinks': _jnp.float32, 'w_out': _jnp.float32, 'ffn2_norm': _jnp.float32, 'ffn2_w_gate': _jnp.float32, 'ffn2_w_up': _jnp.float32, 'ffn2_w_down': _jnp.float32, 'final_norm': _jnp.float32}
MOMENT_SCALE = {'ffn1_norm': 1.119939e-01, 'ffn1_w_gate': 4.629047e-02, 'ffn1_w_up': 4.495928e-02, 'ffn1_w_down': 7.472878e-02, 'mix_norm': 1.198357e-01, 'w_in': 9.016512e-02, 'conv_w': 1.464689e-01, 'conv_b': 6.073758e-01, 'conv_ln_g': 2.650334e-01, 'conv_ln_b': 3.807662e-01, 'attn_sinks': 3.674935e-02, 'w_out': 1.336280e-01, 'ffn2_norm': 9.768461e-02, 'ffn2_w_gate': 4.001373e-02, 'ffn2_w_up': 3.930751e-02, 'ffn2_w_down': 6.532120e-02, 'final_norm': 6.414653e+01}


def _to_microbatches(a, axis):
    t = _jnp.moveaxis(a, axis, 0)
    t = t.reshape((N_MICROBATCH, t.shape[0] // N_MICROBATCH) + t.shape[1:])
    return _jnp.moveaxis(t, 1, axis + 1)


def setup_inputs(seed: int = 0) -> dict:
    inp = _fwd_setup_inputs(seed)
    key = _jax.random.fold_in(_jax.random.key(seed), 7919)
    shape, _ = _output_shape()
    out = dict(inp)
    out["loss_target"] = _jax.random.normal(_jax.random.fold_in(key, 0), shape, _jnp.float32)
    for i, name in enumerate(TWIN_WEIGHTS):
        w = inp[name].astype(_jnp.float32)
        if MOMENT_SCALE is None:
            s = _jnp.sqrt(_jnp.mean(_jnp.square(w)) + 1e-30)
        else:
            s = MOMENT_SCALE[name]
        km, kv = _jax.random.split(_jax.random.fold_in(key, i + 1))
        out[name] = w
        out["m_" + name] = s * _jax.random.normal(km, w.shape, _jnp.float32)
        out["v_" + name] = (s * s) * _jax.random.uniform(kv, w.shape, _jnp.float32, 0.5, 1.5)
    if N_MICROBATCH > 1:
        for name, axis in PER_EXAMPLE_BATCH_AXIS.items():
            out[name] = _to_microbatches(out[name], axis)
    return {'x': out['x'], 'positions': out['positions'], 'ffn1_norm': out['ffn1_norm'], 'ffn1_w_gate': out['ffn1_w_gate'], 'ffn1_w_up': out['ffn1_w_up'], 'ffn1_w_down': out['ffn1_w_down'], 'mix_norm': out['mix_norm'], 'w_in': out['w_in'], 'conv_w': out['conv_w'], 'conv_b': out['conv_b'], 'conv_ln_g': out['conv_ln_g'], 'conv_ln_b': out['conv_ln_b'], 'attn_sinks': out['attn_sinks'], 'w_out': out['w_out'], 'ffn2_norm': out['ffn2_norm'], 'ffn2_w_gate': out['ffn2_w_gate'], 'ffn2_w_up': out['ffn2_w_up'], 'ffn2_w_down': out['ffn2_w_down'], 'final_norm': out['final_norm'], 'loss_target': out['loss_target'], 'm_ffn1_norm': out['m_ffn1_norm'], 'm_ffn1_w_gate': out['m_ffn1_w_gate'], 'm_ffn1_w_up': out['m_ffn1_w_up'], 'm_ffn1_w_down': out['m_ffn1_w_down'], 'm_mix_norm': out['m_mix_norm'], 'm_w_in': out['m_w_in'], 'm_conv_w': out['m_conv_w'], 'm_conv_b': out['m_conv_b'], 'm_conv_ln_g': out['m_conv_ln_g'], 'm_conv_ln_b': out['m_conv_ln_b'], 'm_attn_sinks': out['m_attn_sinks'], 'm_w_out': out['m_w_out'], 'm_ffn2_norm': out['m_ffn2_norm'], 'm_ffn2_w_gate': out['m_ffn2_w_gate'], 'm_ffn2_w_up': out['m_ffn2_w_up'], 'm_ffn2_w_down': out['m_ffn2_w_down'], 'm_final_norm': out['m_final_norm'], 'v_ffn1_norm': out['v_ffn1_norm'], 'v_ffn1_w_gate': out['v_ffn1_w_gate'], 'v_ffn1_w_up': out['v_ffn1_w_up'], 'v_ffn1_w_down': out['v_ffn1_w_down'], 'v_mix_norm': out['v_mix_norm'], 'v_w_in': out['v_w_in'], 'v_conv_w': out['v_conv_w'], 'v_conv_b': out['v_conv_b'], 'v_conv_ln_g': out['v_conv_ln_g'], 'v_conv_ln_b': out['v_conv_ln_b'], 'v_attn_sinks': out['v_attn_sinks'], 'v_w_out': out['v_w_out'], 'v_ffn2_norm': out['v_ffn2_norm'], 'v_ffn2_w_gate': out['v_ffn2_w_gate'], 'v_ffn2_w_up': out['v_ffn2_w_up'], 'v_ffn2_w_down': out['v_ffn2_w_down'], 'v_final_norm': out['v_final_norm']}


def _loss(weights, diff, rest, loss_target):
    with _jax.named_scope("forward"):
        args = {**rest, TWIN_DIFF_INPUT: diff, **{k: w.astype(_WEIGHT_DTYPES[k]) for k, w in weights.items()}}
        y = _forward(args)
    with _jax.named_scope("loss_head"):
        err = _jnp.square(y.astype(_jnp.float32) - loss_target)
        return 0.5 * _jnp.sum(_jnp.mean(err, axis=-1)) if err.ndim else 0.5 * err


def _adamw(w, g, m, v):
    m = ADAM_B1 * m + (1.0 - ADAM_B1) * g
    v = ADAM_B2 * v + (1.0 - ADAM_B2) * _jnp.square(g)
    m_hat = m / (1.0 - ADAM_B1 ** ADAM_STEP)
    v_hat = v / (1.0 - ADAM_B2 ** ADAM_STEP)
    delta = -ADAM_LR * (m_hat / (_jnp.sqrt(v_hat) + ADAM_EPS) + ADAM_WD * w)
    return delta, m, v


def reference(x, positions, ffn1_norm, ffn1_w_gate, ffn1_w_up, ffn1_w_down, mix_norm, w_in, conv_w, conv_b, conv_ln_g, conv_ln_b, attn_sinks, w_out, ffn2_norm, ffn2_w_gate, ffn2_w_up, ffn2_w_down, final_norm, loss_target, m_ffn1_norm, m_ffn1_w_gate, m_ffn1_w_up, m_ffn1_w_down, m_mix_norm, m_w_in, m_conv_w, m_conv_b, m_conv_ln_g, m_conv_ln_b, m_attn_sinks, m_w_out, m_ffn2_norm, m_ffn2_w_gate, m_ffn2_w_up, m_ffn2_w_down, m_final_norm, v_ffn1_norm, v_ffn1_w_gate, v_ffn1_w_up, v_ffn1_w_down, v_mix_norm, v_w_in, v_conv_w, v_conv_b, v_conv_ln_g, v_conv_ln_b, v_attn_sinks, v_w_out, v_ffn2_norm, v_ffn2_w_gate, v_ffn2_w_up, v_ffn2_w_down, v_final_norm):
    given = dict(x=x, positions=positions, ffn1_norm=ffn1_norm, ffn1_w_gate=ffn1_w_gate, ffn1_w_up=ffn1_w_up, ffn1_w_down=ffn1_w_down, mix_norm=mix_norm, w_in=w_in, conv_w=conv_w, conv_b=conv_b, conv_ln_g=conv_ln_g, conv_ln_b=conv_ln_b, attn_sinks=attn_sinks, w_out=w_out, ffn2_norm=ffn2_norm, ffn2_w_gate=ffn2_w_gate, ffn2_w_up=ffn2_w_up, ffn2_w_down=ffn2_w_down, final_norm=final_norm, loss_target=loss_target, m_ffn1_norm=m_ffn1_norm, m_ffn1_w_gate=m_ffn1_w_gate, m_ffn1_w_up=m_ffn1_w_up, m_ffn1_w_down=m_ffn1_w_down, m_mix_norm=m_mix_norm, m_w_in=m_w_in, m_conv_w=m_conv_w, m_conv_b=m_conv_b, m_conv_ln_g=m_conv_ln_g, m_conv_ln_b=m_conv_ln_b, m_attn_sinks=m_attn_sinks, m_w_out=m_w_out, m_ffn2_norm=m_ffn2_norm, m_ffn2_w_gate=m_ffn2_w_gate, m_ffn2_w_up=m_ffn2_w_up, m_ffn2_w_down=m_ffn2_w_down, m_final_norm=m_final_norm, v_ffn1_norm=v_ffn1_norm, v_ffn1_w_gate=v_ffn1_w_gate, v_ffn1_w_up=v_ffn1_w_up, v_ffn1_w_down=v_ffn1_w_down, v_mix_norm=v_mix_norm, v_w_in=v_w_in, v_conv_w=v_conv_w, v_conv_b=v_conv_b, v_conv_ln_g=v_conv_ln_g, v_conv_ln_b=v_conv_ln_b, v_attn_sinks=v_attn_sinks, v_w_out=v_w_out, v_ffn2_norm=v_ffn2_norm, v_ffn2_w_gate=v_ffn2_w_gate, v_ffn2_w_up=v_ffn2_w_up, v_ffn2_w_down=v_ffn2_w_down, v_final_norm=v_final_norm)
    weights = {n: given[n] for n in TWIN_WEIGHTS}
    shared = {n: given[n] for n in SHARED_INPUTS}
    per_example = {n: given[n] for n in ['x', 'positions']}
    grad_fn = _jax.value_and_grad(_loss, argnums=(0, 1))

    def one_microbatch(ex, loss_target):
        ex = dict(ex)
        diff = ex.pop(TWIN_DIFF_INPUT)
        return grad_fn(weights, diff, {**shared, **ex}, loss_target)

    if N_MICROBATCH == 1:
        loss, (grad_w, grad_x) = one_microbatch(per_example, given["loss_target"])
    else:
        def body(carry, xs):
            loss_sum, grad_sum = carry
            l_k, (gw_k, gx_k) = one_microbatch(xs[0], xs[1])
            with _jax.named_scope("update"):
                return (loss_sum + l_k, _jax.tree.map(_jnp.add, grad_sum, gw_k)), gx_k

        init = (_jnp.zeros((), _jnp.float32), _jax.tree.map(_jnp.zeros_like, weights))
        (loss, grad_w), grad_x = _jax.lax.scan(body, init, (per_example, given["loss_target"]))
    with _jax.named_scope("update"):
        delta_w, new_m, new_v = {}, {}, {}
        for n in TWIN_WEIGHTS:
            delta_w[n], new_m[n], new_v[n] = _adamw(weights[n], grad_w[n], given["m_" + n], given["v_" + n])
    return (loss, grad_x, *[grad_w[n] for n in TWIN_WEIGHTS], *[delta_w[n] for n in TWIN_WEIGHTS],
            *[new_m[n] for n in TWIN_WEIGHTS], *[new_v[n] for n in TWIN_WEIGHTS])
```

```python
import functools

import jax
import jax.numpy as jnp
from jax import lax
from jax.experimental import pallas as pl
from jax.experimental.pallas import tpu as pltpu

F32 = jnp.float32
BF16 = jnp.bfloat16
MESH = pl.DeviceIdType.MESH

N_DEV = 8
N_CHIP = 4
D = 1024
FF = 2816
HD = 64
NH = 8
NKV = 2
GROUP = NH // NKV
AW = NH * HD
KVW = NKV * HD
QKW = AW + KVW
QKVW = AW + 2 * KVW
CC = 512
CW = 31
DIN = QKVW + 2 * CC
BLK = 128
HALO = 32
EPS = 1e-5
SCALE = HD ** -0.5
NEG = float(jnp.finfo(jnp.float32).min)

LR, B1, B2, ADAM_EPS, WD, STEP = 0.001, 0.9, 0.999, 1e-08, 0.01, 10

TM = 256
TK = 512
FC = 256
VMEM_LIMIT = 56 * 1024 * 1024


def _cp(*sem):
    return pltpu.CompilerParams(dimension_semantics=sem, vmem_limit_bytes=VMEM_LIMIT)


def _row(tm, c):
    return pl.BlockSpec((tm, c), lambda i: (i, 0))


def _slab(shape, k, single=False):
    zeros = (0,) * len(shape)
    kw = dict(pipeline_mode=pl.Buffered(1)) if single else {}
    return pl.BlockSpec((None, *shape), lambda i: (k, *zeros), **kw)


def _acc(shape):
    return pl.BlockSpec(shape, lambda i: (0,) * len(shape))


def _nt(a, b):
    return lax.dot_general(a, b, (((1,), (1,)), ((), ())), preferred_element_type=F32)


def _tn(a, b):
    return lax.dot_general(a, b, (((0,), (0,)), ((), ())), preferred_element_type=F32)


def _nn(a, b):
    return jnp.dot(a, b, preferred_element_type=F32)


def _sigmoid(x):
    return jax.nn.sigmoid(x)


def _dsilu(z):
    s = _sigmoid(z)
    return s * (1.0 + z * (1.0 - s))


def _rms(x, g):
    r = lax.rsqrt(jnp.mean(x * x, axis=-1, keepdims=True) + EPS)
    xh = x * r
    return xh, r, xh * g


def _rms_bwd(dh, xh, r, g):
    dxh = dh * g
    return r * (dxh - xh * jnp.mean(dxh * xh, axis=-1, keepdims=True))


def _rope(t, c128, s128):
    w = t.shape[1]
    lane = lax.broadcasted_iota(jnp.int32, t.shape, 1)
    rot = jnp.where(lane % HD < HD // 2, pltpu.roll(t, w - HD // 2, 1), pltpu.roll(t, HD // 2, 1))
    return t * jnp.tile(c128, (1, w // 128)) + rot * jnp.tile(s128, (1, w // 128))


def _ffn_up(x, norm, wffn, sg, su, layer):
    T = x.shape[0]

    def body(x_ref, g_ref, wg_ref, wu_ref, h_ref, G_ref, U_ref, A_ref):
        _, _, hn = _rms(x_ref[...], g_ref[...])
        h = hn.astype(BF16)
        h_ref[...] = h
        for c in range(FF // FC):
            sl = slice(c * FC, (c + 1) * FC)
            g = _nt(h, wg_ref[sl, :])
            u = _nt(h, wu_ref[sl, :])
            G_ref[:, sl] = g.astype(BF16)
            U_ref[:, sl] = u.astype(BF16)
            A_ref[:, sl] = (g * _sigmoid(g) * u).astype(BF16)

    return pl.pallas_call(
        body, name="ffn_up", grid=(T // TM,),
        in_specs=[_row(TM, D), _slab((1, D), layer), _slab((FF, D), sg, True), _slab((FF, D), su, True)],
        out_specs=[_row(TM, D), _row(TM, FF), _row(TM, FF), _row(TM, FF)],
        out_shape=[jax.ShapeDtypeStruct((T, D), BF16)] + [jax.ShapeDtypeStruct((T, FF), BF16)] * 3,
        compiler_params=_cp("parallel"),
    )(x, norm, wffn, wffn)


def _ffn_down(a, x, wffn, sd):
    T = x.shape[0]

    def body(a_ref, x_ref, w_ref, o_ref):
        o_ref[...] = x_ref[...] + 0.5 * _nn(a_ref[...], w_ref[...])

    return pl.pallas_call(
        body, name="ffn_down", grid=(T // TM,),
        in_specs=[_row(TM, FF), _row(TM, D), _slab((FF, D), sd, True)],
        out_specs=_row(TM, D),
        out_shape=jax.ShapeDtypeStruct((T, D), F32),
        compiler_params=_cp("parallel"),
    )(a, x, wffn)


def _mix_in(x, norm, win, rc, rs, layer):
    T = x.shape[0]

    def body(x_ref, g_ref, w_ref, c_ref, s_ref, h_ref, qkv_ref, u_ref):
        _, _, hn = _rms(x_ref[...], g_ref[...])
        h = hn.astype(BF16)
        h_ref[...] = h
        qk = _nt(h, w_ref[0:QKW, :])
        qkv_ref[:, 0:QKW] = _rope(qk, c_ref[...], s_ref[...]).astype(BF16)
        qkv_ref[:, QKW:QKVW] = _nt(h, w_ref[QKW:QKVW, :]).astype(BF16)
        for c in range(2 * CC // FC):
            u_ref[:, c * FC:(c + 1) * FC] = _nt(h, w_ref[QKVW + c * FC:QKVW + (c + 1) * FC, :]).astype(BF16)

    return pl.pallas_call(
        body, name="mix_in", grid=(T // TM,),
        in_specs=[_row(TM, D), _slab((1, D), layer), _slab((DIN, D), layer, True), _row(TM, 128), _row(TM, 128)],
        out_specs=[_row(TM, D), _row(TM, QKVW), _row(TM, 2 * CC)],
        out_shape=[jax.ShapeDtypeStruct((T, D), BF16), jax.ShapeDtypeStruct((T, QKVW), BF16),
                   jax.ShapeDtypeStruct((T, 2 * CC), BF16)],
        compiler_params=_cp("parallel"),
    )(x, norm, win, rc, rs)


def _band_mask(i):
    r = lax.broadcasted_iota(jnp.int32, (BLK, 2 * BLK), 0) + BLK
    c = lax.broadcasted_iota(jnp.int32, (BLK, 2 * BLK), 1)
    rel = r - c
    return (rel >= 0) & (rel < BLK) & ((i > 0) | (c >= BLK))


def _band(prev_ref, cur_ref, col):
    return jnp.concatenate([prev_ref[:, col:col + HD], cur_ref[:, col:col + HD]], axis=0)


def _probs(q, kb, mask, sink):
    s = jnp.where(mask, _nt(q, kb) * SCALE, NEG)
    m = jnp.maximum(jnp.max(s, axis=-1, keepdims=True), sink)
    p = jnp.exp(s - m)
    e = jnp.exp(sink - m)
    den = jnp.sum(p, axis=-1, keepdims=True) + e
    return p / den, e / den


def _attn_fwd(qkv, sinks, layer):
    T = qkv.shape[0]

    def body(sink_ref, cur_ref, prev_ref, o_ref):
        mask = _band_mask(pl.program_id(0))
        outs = []
        for kv in range(NKV):
            kb = _band(prev_ref, cur_ref, AW + kv * HD)
            vb = _band(prev_ref, cur_ref, QKW + kv * HD)
            for gq in range(GROUP):
                hh = kv * GROUP + gq
                P, _ = _probs(cur_ref[:, hh * HD:(hh + 1) * HD], kb, mask, sink_ref[layer, hh])
                outs.append(_nn(P.astype(BF16), vb))
        o_ref[...] = jnp.concatenate(outs, axis=1).astype(BF16)

    return pl.pallas_call(
        body, name="attn_fwd", grid=(T // BLK,),
        in_specs=[pl.BlockSpec(memory_space=pltpu.SMEM), _row(BLK, QKVW),
                  pl.BlockSpec((BLK, QKVW), lambda i: (jnp.maximum(i - 1, 0), 0))],
        out_specs=_row(BLK, AW),
        out_shape=jax.ShapeDtypeStruct((T, AW), BF16),
        compiler_params=_cp("parallel"),
    )(sinks, qkv, qkv)


def _glu(u):
    u = u.astype(F32)
    return u[:, :CC] * _sigmoid(u[:, CC:])


def _conv_fwd(u, cw, cb, lg, lb, layer):
    T = u.shape[0]

    def body(u_ref, up_ref, w_ref, b_ref, g_ref, bb_ref, y_ref, o_ref, ext_ref):
        i = pl.program_id(0)
        ext_ref[0:HALO, :] = jnp.where(i > 0, _glu(up_ref[...]), 0.0)
        ext_ref[HALO:, :] = _glu(u_ref[...])
        acc = jnp.zeros((TM, CC), F32)
        for k in range(CW):
            acc = acc + w_ref[k:k + 1, :] * ext_ref[pl.ds(HALO - (CW - 1) + k, TM), :]
        y = acc + b_ref[...]
        y_ref[...] = y
        xc = y - jnp.mean(y, axis=-1, keepdims=True)
        z = xc * lax.rsqrt(jnp.mean(xc * xc, axis=-1, keepdims=True) + EPS) * g_ref[...] + bb_ref[...]
        o_ref[...] = (z * _sigmoid(z)).astype(BF16)

    return pl.pallas_call(
        body, name="conv_fwd", grid=(T // TM,),
        in_specs=[_row(TM, 2 * CC),
                  pl.BlockSpec((HALO, 2 * CC), lambda i: (jnp.maximum(i * (TM // HALO) - 1, 0), 0)),
                  _slab((CW, CC), layer), _slab((1, CC), layer), _slab((1, CC), layer), _slab((1, CC), layer)],
        out_specs=[_row(TM, CC), _row(TM, CC)],
        out_shape=[jax.ShapeDtypeStruct((T, CC), F32), jax.ShapeDtypeStruct((T, CC), BF16)],
        scratch_shapes=[pltpu.VMEM((TM + HALO, CC), F32)],
        compiler_params=_cp("parallel"),
    )(u, u, cw, cb, lg, lb)


def _mix_out(ao, co, x, wout, layer):
    T = x.shape[0]

    def body(ao_ref, co_ref, x_ref, w_ref, o_ref, cat_ref):
        cat = jnp.concatenate([ao_ref[...], co_ref[...]], axis=1)
        cat_ref[...] = cat
        o_ref[...] = x_ref[...] + _nn(cat, w_ref[...])

    return pl.pallas_call(
        body, name="mix_out", grid=(T // TM,),
        in_specs=[_row(TM, AW), _row(TM, CC), _row(TM, D), _slab((D, D), layer, True)],
        out_specs=[_row(TM, D), _row(TM, D)],
        out_shape=[jax.ShapeDtypeStruct((T, D), F32), jax.ShapeDtypeStruct((T, D), BF16)],
        compiler_params=_cp("parallel"),
    )(ao, co, x, wout)


def _final(x, norm, target):
    T = x.shape[0]

    def body(x_ref, g_ref, t_ref, dx_ref, loss_ref, dg_ref):
        @pl.when(pl.program_id(0) == 0)
        def _():
            loss_ref[...] = jnp.zeros_like(loss_ref)
            dg_ref[...] = jnp.zeros_like(dg_ref)

        g = g_ref[...]
        xh, r, y = _rms(x_ref[...], g)
        err = y - t_ref[...]
        loss_ref[...] += jnp.full(loss_ref.shape, (0.5 / D) * jnp.sum(err * err), F32)
        dy = err * (1.0 / D)
        dg_ref[...] += jnp.sum(dy * xh, axis=0, keepdims=True)
        dx_ref[...] = _rms_bwd(dy, xh, r, g)

    return pl.pallas_call(
        body, name="final_loss", grid=(T // TM,),
        in_specs=[_row(TM, D), _acc((1, D)), _row(TM, D)],
        out_specs=[_row(TM, D), _acc((1, 128)), _acc((1, D))],
        out_shape=[jax.ShapeDtypeStruct((T, D), F32), jax.ShapeDtypeStruct((1, 128), F32),
                   jax.ShapeDtypeStruct((1, D), F32)],
        compiler_params=_cp("arbitrary"),
    )(x, norm, target)


def _ffn_bwd_act(dx, G, U, wffn, sd):
    T = dx.shape[0]

    def body(dx_ref, G_ref, U_ref, w_ref, d_ref, dG_ref, dU_ref):
        d = (0.5 * dx_ref[...]).astype(BF16)
        d_ref[...] = d
        for c in range(FF // FC):
            sl = slice(c * FC, (c + 1) * FC)
            da = _nt(d, w_ref[sl, :])
            g = G_ref[:, sl].astype(F32)
            u = U_ref[:, sl].astype(F32)
            dU_ref[:, sl] = (da * g * _sigmoid(g)).astype(BF16)
            dG_ref[:, sl] = (da * u * _dsilu(g)).astype(BF16)

    return pl.pallas_call(
        body, name="ffn_bwd_act", grid=(T // TM,),
        in_specs=[_row(TM, D), _row(TM, FF), _row(TM, FF), _slab((FF, D), sd, True)],
        out_specs=[_row(TM, D), _row(TM, FF), _row(TM, FF)],
        out_shape=[jax.ShapeDtypeStruct((T, D), BF16)] + [jax.ShapeDtypeStruct((T, FF), BF16)] * 2,
        compiler_params=_cp("parallel"),
    )(dx, G, U, wffn)


def _ffn_bwd_in(dG, dU, x, dx, norm, wffn, sg, su, layer):
    T = x.shape[0]

    def body(dG_ref, dU_ref, x_ref, dx_ref, g_ref, wg_ref, wu_ref, o_ref, dg_ref):
        @pl.when(pl.program_id(0) == 0)
        def _():
            dg_ref[...] = jnp.zeros_like(dg_ref)

        dh = _nn(dG_ref[...], wg_ref[...]) + _nn(dU_ref[...], wu_ref[...])
        g = g_ref[...]
        xh, r, _ = _rms(x_ref[...], g)
        dg_ref[...] += jnp.sum(dh * xh, axis=0, keepdims=True)
        o_ref[...] = dx_ref[...] + _rms_bwd(dh, xh, r, g)

    return pl.pallas_call(
        body, name="ffn_bwd_in", grid=(T // TM,),
        in_specs=[_row(TM, FF), _row(TM, FF), _row(TM, D), _row(TM, D), _slab((1, D), layer),
                  _slab((FF, D), sg, True), _slab((FF, D), su, True)],
        out_specs=[_row(TM, D), _acc((1, D))],
        out_shape=[jax.ShapeDtypeStruct((T, D), F32), jax.ShapeDtypeStruct((1, D), F32)],
        compiler_params=_cp("arbitrary"),
    )(dG, dU, x, dx, norm, wffn, wffn)


def _mix_out_bwd(dx, wout, layer):
    T = dx.shape[0]

    def body(dx_ref, w_ref, d_ref, dao_ref, dco_ref):
        d = dx_ref[...].astype(BF16)
        d_ref[...] = d
        dcat = _nt(d, w_ref[...])
        dao_ref[...] = dcat[:, :AW].astype(BF16)
        dco_ref[...] = dcat[:, AW:].astype(BF16)

    return pl.pallas_call(
        body, name="mix_out_bwd", grid=(T // TM,),
        in_specs=[_row(TM, D), _slab((D, D), layer, True)],
        out_specs=[_row(TM, D), _row(TM, AW), _row(TM, CC)],
        out_shape=[jax.ShapeDtypeStruct((T, D), BF16), jax.ShapeDtypeStruct((T, AW), BF16),
                   jax.ShapeDtypeStruct((T, CC), BF16)],
        compiler_params=_cp("parallel"),
    )(dx, wout)


def _conv_bwd_norm(dco, y, lg, lb, layer):
    T = y.shape[0]

    def body(dco_ref, y_ref, g_ref, bb_ref, dy_ref, dlg_ref, dlb_ref, dcb_ref):
        @pl.when(pl.program_id(0) == 0)
        def _():
            dlg_ref[...] = jnp.zeros_like(dlg_ref)
            dlb_ref[...] = jnp.zeros_like(dlb_ref)
            dcb_ref[...] = jnp.zeros_like(dcb_ref)

        y = y_ref[...]
        g = g_ref[...]
        xc = y - jnp.mean(y, axis=-1, keepdims=True)
        rs = lax.rsqrt(jnp.mean(xc * xc, axis=-1, keepdims=True) + EPS)
        xn = xc * rs
        z = xn * g + bb_ref[...]
        dz = dco_ref[...].astype(F32) * _dsilu(z)
        dlg_ref[...] += jnp.sum(dz * xn, axis=0, keepdims=True)
        dlb_ref[...] += jnp.sum(dz, axis=0, keepdims=True)
        dxn = dz * g
        dy = rs * (dxn - jnp.mean(dxn, axis=-1, keepdims=True) - xn * jnp.mean(dxn * xn, axis=-1, keepdims=True))
        dcb_ref[...] += jnp.sum(dy, axis=0, keepdims=True)
        dy_ref[...] = dy

    return pl.pallas_call(
        body, name="conv_bwd_norm", grid=(T // TM,),
        in_specs=[_row(TM, CC), _row(TM, CC), _slab((1, CC), layer), _slab((1, CC), layer)],
        out_specs=[_row(TM, CC), _acc((1, CC)), _acc((1, CC)), _acc((1, CC))],
        out_shape=[jax.ShapeDtypeStruct((T, CC), F32)] + [jax.ShapeDtypeStruct((1, CC), F32)] * 3,
        compiler_params=_cp("arbitrary"),
    )(dco, y, lg, lb)


def _conv_bwd_taps(dy, u, cw, layer):
    T = u.shape[0]
    n_halo = T // HALO

    def body(dy_ref, dyn_ref, u_ref, up_ref, w_ref, du_ref, dw_ref, hext_ref, dext_ref):
        i = pl.program_id(0)

        @pl.when(i == 0)
        def _():
            dw_ref[...] = jnp.zeros_like(dw_ref)

        hext_ref[0:HALO, :] = jnp.where(i > 0, _glu(up_ref[...]), 0.0)
        hext_ref[HALO:, :] = _glu(u_ref[...])
        dy = dy_ref[...]
        dext_ref[0:TM, :] = dy
        dext_ref[TM:, :] = jnp.where(i < pl.num_programs(0) - 1, dyn_ref[...], 0.0)
        dh = jnp.zeros((TM, CC), F32)
        for k in range(CW):
            dh = dh + w_ref[k:k + 1, :] * dext_ref[pl.ds(CW - 1 - k, TM), :]
            dw_ref[k:k + 1, :] += jnp.sum(dy * hext_ref[pl.ds(HALO - (CW - 1) + k, TM), :], axis=0, keepdims=True)
        uu = u_ref[...].astype(F32)
        a = uu[:, :CC]
        sg = _sigmoid(uu[:, CC:])
        du_ref[:, :CC] = (dh * sg).astype(BF16)
        du_ref[:, CC:] = (dh * a * sg * (1.0 - sg)).astype(BF16)

    return pl.pallas_call(
        body, name="conv_bwd_taps", grid=(T // TM,),
        in_specs=[_row(TM, CC),
                  pl.BlockSpec((HALO, CC), lambda i: (jnp.minimum((i + 1) * (TM // HALO), n_halo - 1), 0)),
                  _row(TM, 2 * CC),
                  pl.BlockSpec((HALO, 2 * CC), lambda i: (jnp.maximum(i * (TM // HALO) - 1, 0), 0)),
                  _slab((CW, CC), layer)],
        out_specs=[_row(TM, 2 * CC), _acc((CW, CC))],
        out_shape=[jax.ShapeDtypeStruct((T, 2 * CC), BF16), jax.ShapeDtypeStruct((CW, CC), F32)],
        scratch_shapes=[pltpu.VMEM((TM + HALO, CC), F32), pltpu.VMEM((TM + HALO, CC), F32)],
        compiler_params=_cp("arbitrary"),
    )(dy, dy, u, u, cw)


def _attn_bwd(qkv, dao, sinks, layer):
    T = qkv.shape[0]

    def body(sink_ref, cur_ref, prev_ref, do_ref, dq_ref, dk_ref, dv_ref, ds_ref):
        i = pl.program_id(0)

        @pl.when(i == 0)
        def _():
            dk_ref[...] = jnp.zeros_like(dk_ref)
            dv_ref[...] = jnp.zeros_like(dv_ref)
            ds_ref[...] = jnp.zeros_like(ds_ref)

        mask = _band_mask(i)
        dqs, dks, dvs = [], [], []
        for kv in range(NKV):
            kb = _band(prev_ref, cur_ref, AW + kv * HD)
            vb = _band(prev_ref, cur_ref, QKW + kv * HD)
            dkb = jnp.zeros((2 * BLK, HD), F32)
            dvb = jnp.zeros((2 * BLK, HD), F32)
            for gq in range(GROUP):
                hh = kv * GROUP + gq
                q = cur_ref[:, hh * HD:(hh + 1) * HD]
                do = do_ref[:, hh * HD:(hh + 1) * HD]
                P, psink = _probs(q, kb, mask, sink_ref[layer, hh])
                dP = _nt(do, vb)
                dd = jnp.sum(P * dP, axis=-1, keepdims=True)
                dS = (P * (dP - dd) * SCALE).astype(BF16)
                ds_ref[hh:hh + 1, :] += jnp.full((1, 128), -jnp.sum(psink * dd), F32)
                dqs.append(_nn(dS, kb))
                dkb = dkb + _tn(dS, q)
                dvb = dvb + _tn(P.astype(BF16), do)
            dks.append(dkb)
            dvs.append(dvb)
        dq_ref[...] = jnp.concatenate(dqs, axis=1).astype(BF16)
        dkband = jnp.concatenate(dks, axis=1)
        dvband = jnp.concatenate(dvs, axis=1)
        p0 = pl.multiple_of(jnp.maximum(i - 1, 0) * BLK, BLK)
        c0 = pl.multiple_of(i * BLK, BLK)
        dk_ref[pl.ds(p0, BLK), :] += dkband[:BLK]
        dv_ref[pl.ds(p0, BLK), :] += dvband[:BLK]
        dk_ref[pl.ds(c0, BLK), :] += dkband[BLK:]
        dv_ref[pl.ds(c0, BLK), :] += dvband[BLK:]

    return pl.pallas_call(
        body, name="attn_bwd", grid=(T // BLK,),
        in_specs=[pl.BlockSpec(memory_space=pltpu.SMEM), _row(BLK, QKVW),
                  pl.BlockSpec((BLK, QKVW), lambda i: (jnp.maximum(i - 1, 0), 0)), _row(BLK, AW)],
        out_specs=[_row(BLK, AW), _acc((T, KVW)), _acc((T, KVW)), _acc((NH, 128))],
        out_shape=[jax.ShapeDtypeStruct((T, AW), BF16), jax.ShapeDtypeStruct((T, KVW), F32),
                   jax.ShapeDtypeStruct((T, KVW), F32), jax.ShapeDtypeStruct((NH, 128), F32)],
        compiler_params=_cp("arbitrary"),
    )(sinks, qkv, qkv, dao)


def _mix_in_bwd(dq, dk, dv, du, rc, rs, x, dx, norm, win, layer):
    T = x.shape[0]

    def body(dq_ref, dk_ref, dv_ref, du_ref, c_ref, s_ref, x_ref, dx_ref, g_ref, w_ref, dp_ref, o_ref, dg_ref):
        @pl.when(pl.program_id(0) == 0)
        def _():
            dg_ref[...] = jnp.zeros_like(dg_ref)

        dqk = jnp.concatenate([dq_ref[...].astype(F32), dk_ref[...]], axis=1)
        dqk = _rope(dqk, c_ref[...], -s_ref[...])
        dp = jnp.concatenate([dqk.astype(BF16), dv_ref[...].astype(BF16), du_ref[...]], axis=1)
        dp_ref[...] = dp
        dh = _nn(dp, w_ref[...])
        g = g_ref[...]
        xh, r, _ = _rms(x_ref[...], g)
        dg_ref[...] += jnp.sum(dh * xh, axis=0, keepdims=True)
        o_ref[...] = dx_ref[...] + _rms_bwd(dh, xh, r, g)

    return pl.pallas_call(
        body, name="mix_in_bwd", grid=(T // TM,),
        in_specs=[_row(TM, AW), _row(TM, KVW), _row(TM, KVW), _row(TM, 2 * CC), _row(TM, 128), _row(TM, 128),
                  _row(TM, D), _row(TM, D), _slab((1, D), layer), _slab((DIN, D), layer, True)],
        out_specs=[_row(TM, DIN), _row(TM, D), _acc((1, D))],
        out_shape=[jax.ShapeDtypeStruct((T, DIN), BF16), jax.ShapeDtypeStruct((T, D), F32),
                   jax.ShapeDtypeStruct((1, D), F32)],
        compiler_params=_cp("arbitrary"),
    )(dq, dk, dv, du, rc, rs, x, dx, norm, win)


def _wgrad(buf, slab, a, b):
    T, M = a.shape
    N = b.shape[1]
    tmm = M // 2 if M > 1024 else M

    def body(buf_ref, a_ref, b_ref, o_ref):
        @pl.when(pl.program_id(1) == 0)
        def _():
            o_ref[...] = jnp.zeros_like(o_ref)

        o_ref[...] += _tn(a_ref[...], b_ref[...])

    return pl.pallas_call(
        body, name="wgrad", grid=(M // tmm, T // TK),
        in_specs=[pl.BlockSpec(memory_space=pl.ANY),
                  pl.BlockSpec((TK, tmm), lambda i, k: (k, i)), pl.BlockSpec((TK, N), lambda i, k: (k, 0))],
        out_specs=pl.BlockSpec((None, tmm, N), lambda i, k: (slab, i, 0)),
        out_shape=jax.ShapeDtypeStruct(buf.shape, F32),
        input_output_aliases={0: 0},
        compiler_params=_cp("parallel", "arbitrary"),
    )(buf, a, b)


HBM = pl.BlockSpec(memory_space=pl.ANY)


def _coords():
    return lax.axis_index("x"), lax.axis_index("y"), lax.axis_index("c")


def _other_chips(x, y):
    return [(1 - x, y), (x, 1 - y), (1 - x, 1 - y)]


def _all_gather(shards):
    n = len(shards)

    def body(*refs):
        ins, outs = refs[:n], refs[n:2 * n]
        send_sems, recv_sems, local_sems = refs[2 * n:]
        x, y, c = _coords()
        me, sibling = (x, y, c), (x, y, 1 - c)
        chips = _other_chips(x, y)

        def rows(a, dev):
            r = ins[a].shape[1]
            return outs[a].at[:, pl.ds(pl.multiple_of((4 * dev[0] + 2 * dev[1] + dev[2]) * r, r), r), :]

        def copy(a, k, block, to, src=None):
            return pltpu.make_async_remote_copy(
                src_ref=rows(a, block) if src is None else src, dst_ref=rows(a, block),
                send_sem=send_sems.at[a * 7 + k], recv_sem=recv_sems.at[a * 7 + k],
                device_id=to, device_id_type=MESH)

        mine = [pltpu.make_async_copy(ins[a], rows(a, me), local_sems.at[a]) for a in range(n)]
        for cp in mine:
            cp.start()
        first = []
        for a in range(n):
            first.append(copy(a, 0, me, sibling, src=ins[a]))
            first += [copy(a, 1 + j, me, (*chip, c), src=ins[a]) for j, chip in enumerate(chips)]
        for cp in first:
            cp.start()
        passed = []
        for j, chip in enumerate(chips):
            for a in range(n):
                copy(a, 1 + j, (*chip, c), me).wait_recv()
                fwd = copy(a, 4 + j, (*chip, c), sibling)
                fwd.start()
                passed.append(fwd)
        for a in range(n):
            copy(a, 0, sibling, me).wait_recv()
            for j, chip in enumerate(chips):
                copy(a, 4 + j, (*chip, 1 - c), me).wait_recv()
        for cp in first + passed:
            cp.wait_send()
        for cp in mine:
            cp.wait()

    return pl.pallas_call(
        body, name="all_gather_weights",
        in_specs=[HBM] * n, out_specs=[HBM] * n,
        out_shape=[jax.ShapeDtypeStruct((s.shape[0], N_DEV * s.shape[1], s.shape[2]), s.dtype) for s in shards],
        scratch_shapes=[pltpu.SemaphoreType.DMA((7 * n,)), pltpu.SemaphoreType.DMA((7 * n,)),
                        pltpu.SemaphoreType.DMA((n,))],
    )(*shards)


def _pair_exchange(grads):
    n = len(grads)

    def body(*refs):
        ins, own, got = refs[:n], refs[n:2 * n], refs[2 * n:3 * n]
        send_sems, recv_sems, local_sems = refs[3 * n:]
        x, y, c = _coords()
        sibling = (x, y, 1 - c)

        def block(a, q, core):
            r = ins[a].shape[1] // N_DEV
            return ins[a].at[:, pl.ds(pl.multiple_of((2 * q + core) * r, r), r), :]

        def remote(a, q):
            return pltpu.make_async_remote_copy(
                src_ref=block(a, q, 1 - c), dst_ref=got[a].at[q],
                send_sem=send_sems.at[a * N_CHIP + q], recv_sem=recv_sems.at[a * N_CHIP + q],
                device_id=sibling, device_id_type=MESH)

        sends = [remote(a, q) for a in range(n) for q in range(N_CHIP)]
        keeps = [pltpu.make_async_copy(block(a, q, c), own[a].at[q], local_sems.at[a * N_CHIP + q])
                 for a in range(n) for q in range(N_CHIP)]
        for cp in sends + keeps:
            cp.start()
        for cp in sends:
            cp.wait_recv()
        for cp in sends:
            cp.wait_send()
        for cp in keeps:
            cp.wait()

    def part(g):
        return jax.ShapeDtypeStruct((N_CHIP, g.shape[0], g.shape[1] // N_DEV, g.shape[2]), g.dtype)

    outs = pl.pallas_call(
        body, name="grad_pair_exchange",
        in_specs=[HBM] * n, out_specs=[HBM] * (2 * n),
        out_shape=[part(g) for g in grads] * 2,
        scratch_shapes=[pltpu.SemaphoreType.DMA((N_CHIP * n,)), pltpu.SemaphoreType.DMA((N_CHIP * n,)),
                        pltpu.SemaphoreType.DMA((N_CHIP * n,))],
    )(*grads)
    return outs[:n], outs[n:]


def _chip_exchange(parts):
    n = len(parts)

    def body(*refs):
        ins, outs = refs[:n], refs[n:2 * n]
        send_sems, recv_sems, local_sems = refs[2 * n:]
        x, y, c = _coords()
        mq = 2 * x + y
        chips = _other_chips(x, y)

        def remote(a, j, src_slot, dst_slot, chip):
            return pltpu.make_async_remote_copy(
                src_ref=ins[a].at[src_slot], dst_ref=outs[a].at[dst_slot],
                send_sem=send_sems.at[a * 3 + j], recv_sem=recv_sems.at[a * 3 + j],
                device_id=(*chip, c), device_id_type=MESH)

        keeps = [pltpu.make_async_copy(ins[a].at[mq], outs[a].at[mq], local_sems.at[a]) for a in range(n)]
        sends = [remote(a, j, 2 * chip[0] + chip[1], mq, chip) for a in range(n) for j, chip in enumerate(chips)]
        for cp in keeps + sends:
            cp.start()
        for a in range(n):
            for j, chip in enumerate(chips):
                q = 2 * chip[0] + chip[1]
                remote(a, j, q, q, chip).wait_recv()
        for cp in sends:
            cp.wait_send()
        for cp in keeps:
            cp.wait()

    return pl.pallas_call(
        body, name="grad_chip_exchange",
        in_specs=[HBM] * n, out_specs=[HBM] * n,
        out_shape=[jax.ShapeDtypeStruct(p.shape, p.dtype) for p in parts],
        scratch_shapes=[pltpu.SemaphoreType.DMA((3 * n,)), pltpu.SemaphoreType.DMA((3 * n,)),
                        pltpu.SemaphoreType.DMA((n,))],
    )(*parts)


def _all_reduce_small(pack):
    R = pack.shape[0]

    def body(p_ref, tot_ref, all_ref, send_sems, recv_sems):
        x, y, c = _coords()
        me = 4 * x + 2 * y + c
        all_ref[me] = p_ref[...]
        peers = []
        for k in range(1, N_DEV):
            bx, by, bc = (k >> 2) & 1, (k >> 1) & 1, k & 1
            peers.append((x ^ bx, y ^ by, c ^ bc))

        def copy(k, slot, to):
            return pltpu.make_async_remote_copy(
                src_ref=p_ref, dst_ref=all_ref.at[slot], send_sem=send_sems.at[k], recv_sem=recv_sems.at[k],
                device_id=to, device_id_type=MESH)

        sends = [copy(k, me, peer) for k, peer in enumerate(peers)]
        for cp in sends:
            cp.start()
        for k, peer in enumerate(peers):
            copy(k, 4 * peer[0] + 2 * peer[1] + peer[2], peer).wait_recv()
        for cp in sends:
            cp.wait_send()
        tot = all_ref[0]
        for d in range(1, N_DEV):
            tot = tot + all_ref[d]
        tot_ref[...] = tot

    vmem = pl.BlockSpec(memory_space=pltpu.VMEM)
    return pl.pallas_call(
        body, name="all_reduce_small",
        in_specs=[vmem], out_specs=vmem,
        out_shape=jax.ShapeDtypeStruct((R, 128), F32),
        scratch_shapes=[pltpu.VMEM((N_DEV, R, 128), F32), pltpu.SemaphoreType.DMA((N_DEV - 1,)),
                        pltpu.SemaphoreType.DMA((N_DEV - 1,))],
    )(pack)


def _tile_rows(n, cap=512):
    t = min(n, cap)
    while n % t or t % 8:
        t -= 1
        if t < 8:
            return n
    return t


def _add2(a, b):
    shape = a.shape
    a2, b2 = a.reshape(-1, shape[-1]), b.reshape(-1, shape[-1])
    n, c = a2.shape
    tr = _tile_rows(n)

    def body(a_ref, b_ref, o_ref):
        o_ref[...] = a_ref[...] + b_ref[...]

    out = pl.pallas_call(
        body, name="pair_sum", grid=(n // tr,),
        in_specs=[_row(tr, c), _row(tr, c)], out_specs=_row(tr, c),
        out_shape=jax.ShapeDtypeStruct((n, c), F32),
        compiler_params=_cp("parallel"),
    )(a2, b2)
    return out.reshape(shape)


def _sum_chips(parts):
    _, s, r, c = parts.shape
    p = parts.reshape(N_CHIP, s * r, c)
    n = s * r
    tr = _tile_rows(n)

    def body(p0, p1, p2, p3, o_ref):
        o_ref[...] = ((p0[...] + p1[...]) + p2[...]) + p3[...]

    def spec(q):
        return pl.BlockSpec((None, tr, c), lambda i: (q, i, 0))

    out = pl.pallas_call(
        body, name="chip_sum", grid=(n // tr,),
        in_specs=[spec(q) for q in range(N_CHIP)], out_specs=_row(tr, c),
        out_shape=jax.ShapeDtypeStruct((n, c), F32),
        compiler_params=_cp("parallel"),
    )(p, p, p, p)
    return out.reshape(s, r, c)


def _adamw(w, g, m, v):
    shape = w.shape
    c = shape[-1] if w.ndim > 1 else w.shape[0]
    args = [t.reshape(-1, c) for t in (w, g, m, v)]
    n = args[0].shape[0]
    tr = _tile_rows(n)

    def body(w_ref, g_ref, m_ref, v_ref, d_ref, mo_ref, vo_ref):
        g = g_ref[...]
        m = B1 * m_ref[...] + (1.0 - B1) * g
        v = B2 * v_ref[...] + (1.0 - B2) * jnp.square(g)
        m_hat = m / (1.0 - B1 ** STEP)
        v_hat = v / (1.0 - B2 ** STEP)
        d_ref[...] = -LR * (m_hat / (jnp.sqrt(v_hat) + ADAM_EPS) + WD * w_ref[...])
        mo_ref[...] = m
        vo_ref[...] = v

    outs = pl.pallas_call(
        body, name="adamw", grid=(n // tr,),
        in_specs=[_row(tr, c)] * 4, out_specs=[_row(tr, c)] * 3,
        out_shape=[jax.ShapeDtypeStruct((n, c), F32)] * 3,
        compiler_params=_cp("parallel"),
    )(*args)
    return [o.reshape(shape) for o in outs]


def _ffn_slab(ffn, which, layer):
    return (ffn * 3 + which) * 2 + layer


def _pack(pieces):
    flat = []
    for p in pieces:
        f = p.reshape(-1)
        flat.append(jnp.pad(f, (0, (-f.shape[0]) % 1024)))
    return jnp.concatenate(flat).reshape(-1, 128)


def _unpack(pack, shapes):
    flat = pack.reshape(-1)
    out, off = [], 0
    for s in shapes:
        size = 1
        for d in s:
            size *= d
        out.append(flat[off:off + size].reshape(s))
        off += size + (-size) % 1024
    return out


def kernel(x, positions, ffn1_norm, ffn1_w_gate, ffn1_w_up, ffn1_w_down, mix_norm, w_in, conv_w, conv_b, conv_ln_g, conv_ln_b, attn_sinks, w_out, ffn2_norm, ffn2_w_gate, ffn2_w_up, ffn2_w_down, final_norm, loss_target, m_ffn1_norm, m_ffn1_w_gate, m_ffn1_w_up, m_ffn1_w_down, m_mix_norm, m_w_in, m_conv_w, m_conv_b, m_conv_ln_g, m_conv_ln_b, m_attn_sinks, m_w_out, m_ffn2_norm, m_ffn2_w_gate, m_ffn2_w_up, m_ffn2_w_down, m_final_norm, v_ffn1_norm, v_ffn1_w_gate, v_ffn1_w_up, v_ffn1_w_down, v_mix_norm, v_w_in, v_conv_w, v_conv_b, v_conv_ln_g, v_conv_ln_b, v_attn_sinks, v_w_out, v_ffn2_norm, v_ffn2_w_gate, v_ffn2_w_up, v_ffn2_w_down, v_final_norm):
    L = ffn1_norm.shape[0]
    T = x.shape[1]
    x0 = x.reshape(T, D)
    target = loss_target.reshape(T, D)
    dev = 4 * lax.axis_index("x") + 2 * lax.axis_index("y") + lax.axis_index("c")

    def t_(w):
        return jnp.swapaxes(w, 1, 2)

    ffn_sh = jnp.stack([t_(ffn1_w_gate), t_(ffn1_w_up), ffn1_w_down,
                        t_(ffn2_w_gate), t_(ffn2_w_up), ffn2_w_down]).astype(BF16)
    ffn_sh = ffn_sh.reshape(6 * L, FF // N_DEV, D)
    win_sh = t_(w_in).astype(BF16)
    wout_sh = w_out.astype(BF16)
    cw_cols = CC // N_DEV
    cw_sh = jnp.pad(conv_w.reshape(-1), (0, (-L * CW * cw_cols) % 1024)).reshape(1, -1, 128)

    wffn, win, wout, cw_all = _all_gather([ffn_sh, win_sh, wout_sh, cw_sh])
    cw_rows = cw_sh.shape[1]
    cw_full = cw_all.reshape(N_DEV, cw_rows * 128)[:, :L * CW * cw_cols].reshape(N_DEV, L, CW, cw_cols)
    cw_full = jnp.transpose(cw_full, (1, 2, 0, 3)).reshape(L, CW, CC)

    inv_freq = 1.0 / (10000.0 ** (jnp.arange(0, HD, 2, dtype=F32) / HD))
    ang = positions.reshape(T).astype(F32)[:, None] * inv_freq
    cos, sin = jnp.cos(ang), jnp.sin(ang)
    rc = jnp.concatenate([cos, cos, cos, cos], axis=1)
    rs = jnp.concatenate([-sin, sin, -sin, sin], axis=1)

    n1 = ffn1_norm.reshape(L, 1, D)
    nm = mix_norm.reshape(L, 1, D)
    n2 = ffn2_norm.reshape(L, 1, D)
    cb = conv_b.reshape(L, 1, CC)
    lg = conv_ln_g.reshape(L, 1, CC)
    lb = conv_ln_b.reshape(L, 1, CC)

    saved = []
    xa = x0
    for l in range(L):
        h1, G1, U1, A1 = _ffn_up(xa, n1, wffn, _ffn_slab(0, 0, l), _ffn_slab(0, 1, l), l)
        xb = _ffn_down(A1, xa, wffn, _ffn_slab(0, 2, l))
        hm, qkv, u = _mix_in(xb, nm, win, rc, rs, l)
        ao = _attn_fwd(qkv, attn_sinks, l)
        y, co = _conv_fwd(u, cw_full, cb, lg, lb, l)
        xc, cat = _mix_out(ao, co, xb, wout, l)
        h2, G2, U2, A2 = _ffn_up(xc, n2, wffn, _ffn_slab(1, 0, l), _ffn_slab(1, 1, l), l)
        xd = _ffn_down(A2, xc, wffn, _ffn_slab(1, 2, l))
        saved.append((xa, h1, G1, U1, A1, xb, hm, qkv, u, y, cat, xc, h2, G2, U2, A2))
        xa = xd

    dx, loss_part, g_final = _final(xa, final_norm.reshape(1, D), target)

    gffn = lax.empty((6 * L, FF, D), F32)
    gin = lax.empty((L, DIN, D), F32)
    gout = lax.empty((L, D, D), F32)
    g_n1, g_nm, g_n2 = [None] * L, [None] * L, [None] * L
    g_cb, g_lg, g_lb, g_sink, g_cw = [None] * L, [None] * L, [None] * L, [None] * L, [None] * L
    for l in reversed(range(L)):
        xa, h1, G1, U1, A1, xb, hm, qkv, u, y, cat, xc, h2, G2, U2, A2 = saved[l]
        d, dG, dU = _ffn_bwd_act(dx, G2, U2, wffn, _ffn_slab(1, 2, l))
        gffn = _wgrad(gffn, _ffn_slab(1, 2, l), A2, d)
        gffn = _wgrad(gffn, _ffn_slab(1, 0, l), dG, h2)
        gffn = _wgrad(gffn, _ffn_slab(1, 1, l), dU, h2)
        dx, g_n2[l] = _ffn_bwd_in(dG, dU, xc, dx, n2, wffn, _ffn_slab(1, 0, l), _ffn_slab(1, 1, l), l)
        d, dao, dco = _mix_out_bwd(dx, wout, l)
        gout = _wgrad(gout, l, cat, d)
        dy, g_lg[l], g_lb[l], g_cb[l] = _conv_bwd_norm(dco, y, lg, lb, l)
        du, g_cw[l] = _conv_bwd_taps(dy, u, cw_full, l)
        dq, dk, dv, g_sink[l] = _attn_bwd(qkv, dao, attn_sinks, l)
        dp, dx, g_nm[l] = _mix_in_bwd(dq, dk, dv, du, rc, rs, xb, dx, nm, win, l)
        gin = _wgrad(gin, l, dp, hm)
        d, dG, dU = _ffn_bwd_act(dx, G1, U1, wffn, _ffn_slab(0, 2, l))
        gffn = _wgrad(gffn, _ffn_slab(0, 2, l), A1, d)
        gffn = _wgrad(gffn, _ffn_slab(0, 0, l), dG, h1)
        gffn = _wgrad(gffn, _ffn_slab(0, 1, l), dU, h1)
        dx, g_n1[l] = _ffn_bwd_in(dG, dU, xa, dx, n1, wffn, _ffn_slab(0, 0, l), _ffn_slab(0, 1, l), l)

    grad_x = dx.reshape(1, T, D)

    own, got = _pair_exchange([gffn, gin, gout])
    sums = _chip_exchange([_add2(o, g) for o, g in zip(own, got)])
    gffn_sh, gin_sh, gout_sh = [_sum_chips(s) for s in sums]

    small = [loss_part,
             jnp.concatenate(g_n1), jnp.concatenate(g_nm), jnp.concatenate(g_n2), g_final,
             jnp.concatenate(g_cb), jnp.concatenate(g_lg), jnp.concatenate(g_lb),
             jnp.stack(g_sink)[:, :, 0], jnp.stack(g_cw)]
    small_shapes = [(1, 128), (L, D), (L, D), (L, D), (D,), (L, CC), (L, CC), (L, CC), (L, NH), (L, CW, CC)]
    tot = _unpack(_all_reduce_small(_pack(small)), small_shapes)
    loss = tot[0][0, 0]
    gr_n1, gr_nm, gr_n2, gr_final, gr_cb, gr_lg, gr_lb, gr_sink, gr_cw_full = tot[1:]
    gr_cw = lax.dynamic_slice_in_dim(gr_cw_full, dev * cw_cols, cw_cols, axis=2)

    gf = gffn_sh.reshape(2, 3, L, FF // N_DEV, D)
    grads = {
        "ffn1_norm": gr_n1, "ffn1_w_gate": t_(gf[0, 0]), "ffn1_w_up": t_(gf[0, 1]), "ffn1_w_down": gf[0, 2],
        "mix_norm": gr_nm, "w_in": t_(gin_sh), "conv_w": gr_cw, "conv_b": gr_cb, "conv_ln_g": gr_lg,
        "conv_ln_b": gr_lb, "attn_sinks": gr_sink, "w_out": gout_sh,
        "ffn2_norm": gr_n2, "ffn2_w_gate": t_(gf[1, 0]), "ffn2_w_up": t_(gf[1, 1]), "ffn2_w_down": gf[1, 2],
        "final_norm": gr_final,
    }
    weights = dict(ffn1_norm=ffn1_norm, ffn1_w_gate=ffn1_w_gate, ffn1_w_up=ffn1_w_up, ffn1_w_down=ffn1_w_down, mix_norm=mix_norm, w_in=w_in, conv_w=conv_w, conv_b=conv_b, conv_ln_g=conv_ln_g, conv_ln_b=conv_ln_b, attn_sinks=attn_sinks, w_out=w_out, ffn2_norm=ffn2_norm, ffn2_w_gate=ffn2_w_gate, ffn2_w_up=ffn2_w_up, ffn2_w_down=ffn2_w_down, final_norm=final_norm)
    moms = dict(ffn1_norm=m_ffn1_norm, ffn1_w_gate=m_ffn1_w_gate, ffn1_w_up=m_ffn1_w_up, ffn1_w_down=m_ffn1_w_down, mix_norm=m_mix_norm, w_in=m_w_in, conv_w=m_conv_w, conv_b=m_conv_b, conv_ln_g=m_conv_ln_g, conv_ln_b=m_conv_ln_b, attn_sinks=m_attn_sinks, w_out=m_w_out, ffn2_norm=m_ffn2_norm, ffn2_w_gate=m_ffn2_w_gate, ffn2_w_up=m_ffn2_w_up, ffn2_w_down=m_ffn2_w_down, final_norm=m_final_norm)
    vels = dict(ffn1_norm=v_ffn1_norm, ffn1_w_gate=v_ffn1_w_gate, ffn1_w_up=v_ffn1_w_up, ffn1_w_down=v_ffn1_w_down, mix_norm=v_mix_norm, w_in=v_w_in, conv_w=v_conv_w, conv_b=v_conv_b, conv_ln_g=v_conv_ln_g, conv_ln_b=v_conv_ln_b, attn_sinks=v_attn_sinks, w_out=v_w_out, ffn2_norm=v_ffn2_norm, ffn2_w_gate=v_ffn2_w_gate, ffn2_w_up=v_ffn2_w_up, ffn2_w_down=v_ffn2_w_down, final_norm=v_final_norm)

    names = list(weights)
    big = ("ffn1_w_gate", "ffn1_w_up", "ffn1_w_down", "w_in", "w_out", "ffn2_w_gate", "ffn2_w_up", "ffn2_w_down")
    delta, new_m, new_v = {}, {}, {}
    for k in big:
        delta[k], new_m[k], new_v[k] = _adamw(weights[k], grads[k], moms[k], vels[k])
    rest = [k for k in names if k not in big]
    rest_shapes = [weights[k].shape for k in rest]
    packed = _adamw(*[_pack([t[k] for k in rest]) for t in (weights, grads, moms, vels)])
    for res, packed_out in zip((delta, new_m, new_v), packed):
        for k, val in zip(rest, _unpack(packed_out, rest_shapes)):
            res[k] = val

    return (loss, grad_x, *[grads[k] for k in names], *[delta[k] for k in names],
            *[new_m[k] for k in names], *[new_v[k] for k in names])
```

```python
import functools

import jax
import jax.numpy as jnp
from jax import lax
from jax.experimental import pallas as pl
from jax.experimental.pallas import tpu as pltpu

F32 = jnp.float32
BF16 = jnp.bfloat16
MESH = pl.DeviceIdType.MESH

N_DEV = 8
N_CHIP = 4
D = 1024
FF = 2816
HD = 64
NH = 8
NKV = 2
GROUP = NH // NKV
AW = NH * HD
KVW = NKV * HD
QKW = AW + KVW
QKVW = AW + 2 * KVW
CC = 512
CW = 31
DIN = QKVW + 2 * CC
BLK = 128
HALO = 32
EPS = 1e-5
SCALE = HD ** -0.5
NEG = float(jnp.finfo(jnp.float32).min)

LR, B1, B2, ADAM_EPS, WD, STEP = 0.001, 0.9, 0.999, 1e-08, 0.01, 10

TM = 256
TK = 512
FC = 256
VMEM_LIMIT = 56 * 1024 * 1024


def _cp(*sem):
    return pltpu.CompilerParams(dimension_semantics=sem, vmem_limit_bytes=VMEM_LIMIT)


def _row(tm, c):
    return pl.BlockSpec((tm, c), lambda i: (i, 0))


def _slab(shape, k, single=False):
    zeros = (0,) * len(shape)
    kw = dict(pipeline_mode=pl.Buffered(1)) if single else {}
    return pl.BlockSpec((None, *shape), lambda i: (k, *zeros), **kw)


def _acc(shape):
    return pl.BlockSpec(shape, lambda i: (0,) * len(shape))


def _nt(a, b):
    return lax.dot_general(a, b, (((1,), (1,)), ((), ())), preferred_element_type=F32)


def _tn(a, b):
    return lax.dot_general(a, b, (((0,), (0,)), ((), ())), preferred_element_type=F32)


def _nn(a, b):
    return jnp.dot(a, b, preferred_element_type=F32)


def _sigmoid(x):
    return jax.nn.sigmoid(x)


def _dsilu(z):
    s = _sigmoid(z)
    return s * (1.0 + z * (1.0 - s))


def _rms(x, g):
    r = lax.rsqrt(jnp.mean(x * x, axis=-1, keepdims=True) + EPS)
    xh = x * r
    return xh, r, xh * g


def _rms_bwd(dh, xh, r, g):
    dxh = dh * g
    return r * (dxh - xh * jnp.mean(dxh * xh, axis=-1, keepdims=True))


def _rope(t, c128, s128):
    w = t.shape[1]
    lane = lax.broadcasted_iota(jnp.int32, t.shape, 1)
    rot = jnp.where(lane % HD < HD // 2, pltpu.roll(t, w - HD // 2, 1), pltpu.roll(t, HD // 2, 1))
    return t * jnp.tile(c128, (1, w // 128)) + rot * jnp.tile(s128, (1, w // 128))


def _ffn_up(x, norm, wffn, sg, su, layer):
    T = x.shape[0]

    def body(x_ref, g_ref, wg_ref, wu_ref, h_ref, G_ref, U_ref, A_ref):
        _, _, hn = _rms(x_ref[...], g_ref[...])
        h = hn.astype(BF16)
        h_ref[...] = h
        for c in range(FF // FC):
            sl = slice(c * FC, (c + 1) * FC)
            g = _nt(h, wg_ref[sl, :])
            u = _nt(h, wu_ref[sl, :])
            G_ref[:, sl] = g.astype(BF16)
            U_ref[:, sl] = u.astype(BF16)
            A_ref[:, sl] = (g * _sigmoid(g) * u).astype(BF16)

    return pl.pallas_call(
        body, name="ffn_up", grid=(T // TM,),
        in_specs=[_row(TM, D), _slab((1, D), layer), _slab((FF, D), sg, True), _slab((FF, D), su, True)],
        out_specs=[_row(TM, D), _row(TM, FF), _row(TM, FF), _row(TM, FF)],
        out_shape=[jax.ShapeDtypeStruct((T, D), BF16)] + [jax.ShapeDtypeStruct((T, FF), BF16)] * 3,
        compiler_params=_cp("parallel"),
    )(x, norm, wffn, wffn)


def _ffn_down(a, x, wffn, sd):
    T = x.shape[0]

    def body(a_ref, x_ref, w_ref, o_ref):
        o_ref[...] = x_ref[...] + 0.5 * _nn(a_ref[...], w_ref[...])

    return pl.pallas_call(
        body, name="ffn_down", grid=(T // TM,),
        in_specs=[_row(TM, FF), _row(TM, D), _slab((FF, D), sd, True)],
        out_specs=_row(TM, D),
        out_shape=jax.ShapeDtypeStruct((T, D), F32),
        compiler_params=_cp("parallel"),
    )(a, x, wffn)


def _mix_in(x, norm, win, rc, rs, layer):
    T = x.shape[0]

    def body(x_ref, g_ref, w_ref, c_ref, s_ref, h_ref, qkv_ref, u_ref):
        _, _, hn = _rms(x_ref[...], g_ref[...])
        h = hn.astype(BF16)
        h_ref[...] = h
        qk = _nt(h, w_ref[0:QKW, :])
        qkv_ref[:, 0:QKW] = _rope(qk, c_ref[...], s_ref[...]).astype(BF16)
        qkv_ref[:, QKW:QKVW] = _nt(h, w_ref[QKW:QKVW, :]).astype(BF16)
        for c in range(2 * CC // FC):
            u_ref[:, c * FC:(c + 1) * FC] = _nt(h, w_ref[QKVW + c * FC:QKVW + (c + 1) * FC, :]).astype(BF16)

    return pl.pallas_call(
        body, name="mix_in", grid=(T // TM,),
        in_specs=[_row(TM, D), _slab((1, D), layer), _slab((DIN, D), layer, True), _row(TM, 128), _row(TM, 128)],
        out_specs=[_row(TM, D), _row(TM, QKVW), _row(TM, 2 * CC)],
        out_shape=[jax.ShapeDtypeStruct((T, D), BF16), jax.ShapeDtypeStruct((T, QKVW), BF16),
                   jax.ShapeDtypeStruct((T, 2 * CC), BF16)],
        compiler_params=_cp("parallel"),
    )(x, norm, win, rc, rs)


def _band_mask(i):
    r = lax.broadcasted_iota(jnp.int32, (BLK, 2 * BLK), 0) + BLK
    c = lax.broadcasted_iota(jnp.int32, (BLK, 2 * BLK), 1)
    rel = r - c
    return (rel >= 0) & (rel < BLK) & ((i > 0) | (c >= BLK))


def _band(prev_ref, cur_ref, col):
    return jnp.concatenate([prev_ref[:, col:col + HD], cur_ref[:, col:col + HD]], axis=0)


def _probs(q, kb, mask, sink):
    s = jnp.where(mask, _nt(q, kb) * SCALE, NEG)
    m = jnp.maximum(jnp.max(s, axis=-1, keepdims=True), sink)
    p = jnp.exp(s - m)
    e = jnp.exp(sink - m)
    den = jnp.sum(p, axis=-1, keepdims=True) + e
    return p / den, e / den


def _attn_fwd(qkv, sinks, layer):
    T = qkv.shape[0]

    def body(sink_ref, cur_ref, prev_ref, o_ref):
        mask = _band_mask(pl.program_id(0))
        outs = []
        for kv in range(NKV):
            kb = _band(prev_ref, cur_ref, AW + kv * HD)
            vb = _band(prev_ref, cur_ref, QKW + kv * HD)
            for gq in range(GROUP):
                hh = kv * GROUP + gq
                P, _ = _probs(cur_ref[:, hh * HD:(hh + 1) * HD], kb, mask, sink_ref[layer, hh])
                outs.append(_nn(P.astype(BF16), vb))
        o_ref[...] = jnp.concatenate(outs, axis=1).astype(BF16)

    return pl.pallas_call(
        body, name="attn_fwd", grid=(T // BLK,),
        in_specs=[pl.BlockSpec(memory_space=pltpu.SMEM), _row(BLK, QKVW),
                  pl.BlockSpec((BLK, QKVW), lambda i: (jnp.maximum(i - 1, 0), 0))],
        out_specs=_row(BLK, AW),
        out_shape=jax.ShapeDtypeStruct((T, AW), BF16),
        compiler_params=_cp("parallel"),
    )(sinks, qkv, qkv)


def _glu(u):
    u = u.astype(F32)
    return u[:, :CC] * _sigmoid(u[:, CC:])


def _conv_fwd(u, cw, cb, lg, lb, layer):
    T = u.shape[0]

    def body(u_ref, up_ref, w_ref, b_ref, g_ref, bb_ref, y_ref, o_ref, ext_ref):
        i = pl.program_id(0)
        ext_ref[0:HALO, :] = jnp.where(i > 0, _glu(up_ref[...]), 0.0)
        ext_ref[HALO:, :] = _glu(u_ref[...])
        acc = jnp.zeros((TM, CC), F32)
        for k in range(CW):
            acc = acc + w_ref[k:k + 1, :] * ext_ref[pl.ds(HALO - (CW - 1) + k, TM), :]
        y = acc + b_ref[...]
        y_ref[...] = y
        xc = y - jnp.mean(y, axis=-1, keepdims=True)
        z = xc * lax.rsqrt(jnp.mean(xc * xc, axis=-1, keepdims=True) + EPS) * g_ref[...] + bb_ref[...]
        o_ref[...] = (z * _sigmoid(z)).astype(BF16)

    return pl.pallas_call(
        body, name="conv_fwd", grid=(T // TM,),
        in_specs=[_row(TM, 2 * CC),
                  pl.BlockSpec((HALO, 2 * CC), lambda i: (jnp.maximum(i * (TM // HALO) - 1, 0), 0)),
                  _slab((CW, CC), layer), _slab((1, CC), layer), _slab((1, CC), layer), _slab((1, CC), layer)],
        out_specs=[_row(TM, CC), _row(TM, CC)],
        out_shape=[jax.ShapeDtypeStruct((T, CC), F32), jax.ShapeDtypeStruct((T, CC), BF16)],
        scratch_shapes=[pltpu.VMEM((TM + HALO, CC), F32)],
        compiler_params=_cp("parallel"),
    )(u, u, cw, cb, lg, lb)


def _mix_out(ao, co, x, wout, layer):
    T = x.shape[0]

    def body(ao_ref, co_ref, x_ref, w_ref, o_ref, cat_ref):
        cat = jnp.concatenate([ao_ref[...], co_ref[...]], axis=1)
        cat_ref[...] = cat
        o_ref[...] = x_ref[...] + _nn(cat, w_ref[...])

    return pl.pallas_call(
        body, name="mix_out", grid=(T // TM,),
        in_specs=[_row(TM, AW), _row(TM, CC), _row(TM, D), _slab((D, D), layer, True)],
        out_specs=[_row(TM, D), _row(TM, D)],
        out_shape=[jax.ShapeDtypeStruct((T, D), F32), jax.ShapeDtypeStruct((T, D), BF16)],
        compiler_params=_cp("parallel"),
    )(ao, co, x, wout)


def _final(x, norm, target):
    T = x.shape[0]

    def body(x_ref, g_ref, t_ref, dx_ref, loss_ref, dg_ref):
        @pl.when(pl.program_id(0) == 0)
        def _():
            loss_ref[...] = jnp.zeros_like(loss_ref)
            dg_ref[...] = jnp.zeros_like(dg_ref)

        g = g_ref[...]
        xh, r, y = _rms(x_ref[...], g)
        err = y - t_ref[...]
        loss_ref[...] += jnp.full(loss_ref.shape, (0.5 / D) * jnp.sum(err * err), F32)
        dy = err * (1.0 / D)
        dg_ref[...] += jnp.sum(dy * xh, axis=0, keepdims=True)
        dx_ref[...] = _rms_bwd(dy, xh, r, g)

    return pl.pallas_call(
        body, name="final_loss", grid=(T // TM,),
        in_specs=[_row(TM, D), _acc((1, D)), _row(TM, D)],
        out_specs=[_row(TM, D), _acc((1, 128)), _acc((1, D))],
        out_shape=[jax.ShapeDtypeStruct((T, D), F32), jax.ShapeDtypeStruct((1, 128), F32),
                   jax.ShapeDtypeStruct((1, D), F32)],
        compiler_params=_cp("arbitrary"),
    )(x, norm, target)


def _ffn_bwd_act(dx, G, U, wffn, sd):
    T = dx.shape[0]

    def body(dx_ref, G_ref, U_ref, w_ref, d_ref, dG_ref, dU_ref):
        d = (0.5 * dx_ref[...]).astype(BF16)
        d_ref[...] = d
        for c in range(FF // FC):
            sl = slice(c * FC, (c + 1) * FC)
            da = _nt(d, w_ref[sl, :])
            g = G_ref[:, sl].astype(F32)
            u = U_ref[:, sl].astype(F32)
            dU_ref[:, sl] = (da * g * _sigmoid(g)).astype(BF16)
            dG_ref[:, sl] = (da * u * _dsilu(g)).astype(BF16)

    return pl.pallas_call(
        body, name="ffn_bwd_act", grid=(T // TM,),
        in_specs=[_row(TM, D), _row(TM, FF), _row(TM, FF), _slab((FF, D), sd, True)],
        out_specs=[_row(TM, D), _row(TM, FF), _row(TM, FF)],
        out_shape=[jax.ShapeDtypeStruct((T, D), BF16)] + [jax.ShapeDtypeStruct((T, FF), BF16)] * 2,
        compiler_params=_cp("parallel"),
    )(dx, G, U, wffn)


def _ffn_bwd_in(dG, dU, x, dx, norm, wffn, sg, su, layer):
    T = x.shape[0]

    def body(dG_ref, dU_ref, x_ref, dx_ref, g_ref, wg_ref, wu_ref, o_ref, dg_ref):
        @pl.when(pl.program_id(0) == 0)
        def _():
            dg_ref[...] = jnp.zeros_like(dg_ref)

        dh = _nn(dG_ref[...], wg_ref[...]) + _nn(dU_ref[...], wu_ref[...])
        g = g_ref[...]
        xh, r, _ = _rms(x_ref[...], g)
        dg_ref[...] += jnp.sum(dh * xh, axis=0, keepdims=True)
        o_ref[...] = dx_ref[...] + _rms_bwd(dh, xh, r, g)

    return pl.pallas_call(
        body, name="ffn_bwd_in", grid=(T // TM,),
        in_specs=[_row(TM, FF), _row(TM, FF), _row(TM, D), _row(TM, D), _slab((1, D), layer),
                  _slab((FF, D), sg, True), _slab((FF, D), su, True)],
        out_specs=[_row(TM, D), _acc((1, D))],
        out_shape=[jax.ShapeDtypeStruct((T, D), F32), jax.ShapeDtypeStruct((1, D), F32)],
        compiler_params=_cp("arbitrary"),
    )(dG, dU, x, dx, norm, wffn, wffn)


def _mix_out_bwd(dx, wout, layer):
    T = dx.shape[0]

    def body(dx_ref, w_ref, d_ref, dao_ref, dco_ref):
        d = dx_ref[...].astype(BF16)
        d_ref[...] = d
        dcat = _nt(d, w_ref[...])
        dao_ref[...] = dcat[:, :AW].astype(BF16)
        dco_ref[...] = dcat[:, AW:].astype(BF16)

    return pl.pallas_call(
        body, name="mix_out_bwd", grid=(T // TM,),
        in_specs=[_row(TM, D), _slab((D, D), layer, True)],
        out_specs=[_row(TM, D), _row(TM, AW), _row(TM, CC)],
        out_shape=[jax.ShapeDtypeStruct((T, D), BF16), jax.ShapeDtypeStruct((T, AW), BF16),
                   jax.ShapeDtypeStruct((T, CC), BF16)],
        compiler_params=_cp("parallel"),
    )(dx, wout)


def _conv_bwd_norm(dco, y, lg, lb, layer):
    T = y.shape[0]

    def body(dco_ref, y_ref, g_ref, bb_ref, dy_ref, dlg_ref, dlb_ref, dcb_ref):
        @pl.when(pl.program_id(0) == 0)
        def _():
            dlg_ref[...] = jnp.zeros_like(dlg_ref)
            dlb_ref[...] = jnp.zeros_like(dlb_ref)
            dcb_ref[...] = jnp.zeros_like(dcb_ref)

        y = y_ref[...]
        g = g_ref[...]
        xc = y - jnp.mean(y, axis=-1, keepdims=True)
        rs = lax.rsqrt(jnp.mean(xc * xc, axis=-1, keepdims=True) + EPS)
        xn = xc * rs
        z = xn * g + bb_ref[...]
        dz = dco_ref[...].astype(F32) * _dsilu(z)
        dlg_ref[...] += jnp.sum(dz * xn, axis=0, keepdims=True)
        dlb_ref[...] += jnp.sum(dz, axis=0, keepdims=True)
        dxn = dz * g
        dy = rs * (dxn - jnp.mean(dxn, axis=-1, keepdims=True) - xn * jnp.mean(dxn * xn, axis=-1, keepdims=True))
        dcb_ref[...] += jnp.sum(dy, axis=0, keepdims=True)
        dy_ref[...] = dy

    return pl.pallas_call(
        body, name="conv_bwd_norm", grid=(T // TM,),
        in_specs=[_row(TM, CC), _row(TM, CC), _slab((1, CC), layer), _slab((1, CC), layer)],
        out_specs=[_row(TM, CC), _acc((1, CC)), _acc((1, CC)), _acc((1, CC))],
        out_shape=[jax.ShapeDtypeStruct((T, CC), F32)] + [jax.ShapeDtypeStruct((1, CC), F32)] * 3,
        compiler_params=_cp("arbitrary"),
    )(dco, y, lg, lb)


def _conv_bwd_taps(dy, u, cw, layer):
    T = u.shape[0]
    n_halo = T // HALO

    def body(dy_ref, dyn_ref, u_ref, up_ref, w_ref, du_ref, dw_ref, hext_ref, dext_ref):
        i = pl.program_id(0)

        @pl.when(i == 0)
        def _():
            dw_ref[...] = jnp.zeros_like(dw_ref)

        hext_ref[0:HALO, :] = jnp.where(i > 0, _glu(up_ref[...]), 0.0)
        hext_ref[HALO:, :] = _glu(u_ref[...])
        dy = dy_ref[...]
        dext_ref[0:TM, :] = dy
        dext_ref[TM:, :] = jnp.where(i < pl.num_programs(0) - 1, dyn_ref[...], 0.0)
        dh = jnp.zeros((TM, CC), F32)
        for k in range(CW):
            dh = dh + w_ref[k:k + 1, :] * dext_ref[pl.ds(CW - 1 - k, TM), :]
            dw_ref[k:k + 1, :] += jnp.sum(dy * hext_ref[pl.ds(HALO - (CW - 1) + k, TM), :], axis=0, keepdims=True)
        uu = u_ref[...].astype(F32)
        a = uu[:, :CC]
        sg = _sigmoid(uu[:, CC:])
        du_ref[:, :CC] = (dh * sg).astype(BF16)
        du_ref[:, CC:] = (dh * a * sg * (1.0 - sg)).astype(BF16)

    return pl.pallas_call(
        body, name="conv_bwd_taps", grid=(T // TM,),
        in_specs=[_row(TM, CC),
                  pl.BlockSpec((HALO, CC), lambda i: (jnp.minimum((i + 1) * (TM // HALO), n_halo - 1), 0)),
                  _row(TM, 2 * CC),
                  pl.BlockSpec((HALO, 2 * CC), lambda i: (jnp.maximum(i * (TM // HALO) - 1, 0), 0)),
                  _slab((CW, CC), layer)],
        out_specs=[_row(TM, 2 * CC), _acc((CW, CC))],
        out_shape=[jax.ShapeDtypeStruct((T, 2 * CC), BF16), jax.ShapeDtypeStruct((CW, CC), F32)],
        scratch_shapes=[pltpu.VMEM((TM + HALO, CC), F32), pltpu.VMEM((TM + HALO, CC), F32)],
        compiler_params=_cp("arbitrary"),
    )(dy, dy, u, u, cw)


def _attn_bwd(qkv, dao, sinks, layer):
    T = qkv.shape[0]

    def body(sink_ref, cur_ref, prev_ref, do_ref, dq_ref, dk_ref, dv_ref, ds_ref):
        i = pl.program_id(0)

        @pl.when(i == 0)
        def _():
            dk_ref[...] = jnp.zeros_like(dk_ref)
            dv_ref[...] = jnp.zeros_like(dv_ref)
            ds_ref[...] = jnp.zeros_like(ds_ref)

        mask = _band_mask(i)
        dqs, dks, dvs = [], [], []
        for kv in range(NKV):
            kb = _band(prev_ref, cur_ref, AW + kv * HD)
            vb = _band(prev_ref, cur_ref, QKW + kv * HD)
            dkb = jnp.zeros((2 * BLK, HD), F32)
            dvb = jnp.zeros((2 * BLK, HD), F32)
            for gq in range(GROUP):
                hh = kv * GROUP + gq
                q = cur_ref[:, hh * HD:(hh + 1) * HD]
                do = do_ref[:, hh * HD:(hh + 1) * HD]
                P, psink = _probs(q, kb, mask, sink_ref[layer, hh])
                dP = _nt(do, vb)
                dd = jnp.sum(P * dP, axis=-1, keepdims=True)
                dS = (P * (dP - dd) * SCALE).astype(BF16)
                ds_ref[hh:hh + 1, :] += jnp.full((1, 128), -jnp.sum(psink * dd), F32)
                dqs.append(_nn(dS, kb))
                dkb = dkb + _tn(dS, q)
                dvb = dvb + _tn(P.astype(BF16), do)
            dks.append(dkb)
            dvs.append(dvb)
        dq_ref[...] = jnp.concatenate(dqs, axis=1).astype(BF16)
        dkband = jnp.concatenate(dks, axis=1)
        dvband = jnp.concatenate(dvs, axis=1)
        p0 = pl.multiple_of(jnp.maximum(i - 1, 0) * BLK, BLK)
        c0 = pl.multiple_of(i * BLK, BLK)
        dk_ref[pl.ds(p0, BLK), :] += dkband[:BLK]
        dv_ref[pl.ds(p0, BLK), :] += dvband[:BLK]
        dk_ref[pl.ds(c0, BLK), :] += dkband[BLK:]
        dv_ref[pl.ds(c0, BLK), :] += dvband[BLK:]

    return pl.pallas_call(
        body, name="attn_bwd", grid=(T // BLK,),
        in_specs=[pl.BlockSpec(memory_space=pltpu.SMEM), _row(BLK, QKVW),
                  pl.BlockSpec((BLK, QKVW), lambda i: (jnp.maximum(i - 1, 0), 0)), _row(BLK, AW)],
        out_specs=[_row(BLK, AW), _acc((T, KVW)), _acc((T, KVW)), _acc((NH, 128))],
        out_shape=[jax.ShapeDtypeStruct((T, AW), BF16), jax.ShapeDtypeStruct((T, KVW), F32),
                   jax.ShapeDtypeStruct((T, KVW), F32), jax.ShapeDtypeStruct((NH, 128), F32)],
        compiler_params=_cp("arbitrary"),
    )(sinks, qkv, qkv, dao)


def _mix_in_bwd(dq, dk, dv, du, rc, rs, x, dx, norm, win, layer):
    T = x.shape[0]

    def body(dq_ref, dk_ref, dv_ref, du_ref, c_ref, s_ref, x_ref, dx_ref, g_ref, w_ref, dp_ref, o_ref, dg_ref):
        @pl.when(pl.program_id(0) == 0)
        def _():
            dg_ref[...] = jnp.zeros_like(dg_ref)

        dqk = jnp.concatenate([dq_ref[...].astype(F32), dk_ref[...]], axis=1)
        dqk = _rope(dqk, c_ref[...], -s_ref[...])
        dp = jnp.concatenate([dqk.astype(BF16), dv_ref[...].astype(BF16), du_ref[...]], axis=1)
        dp_ref[...] = dp
        dh = _nn(dp, w_ref[...])
        g = g_ref[...]
        xh, r, _ = _rms(x_ref[...], g)
        dg_ref[...] += jnp.sum(dh * xh, axis=0, keepdims=True)
        o_ref[...] = dx_ref[...] + _rms_bwd(dh, xh, r, g)

    return pl.pallas_call(
        body, name="mix_in_bwd", grid=(T // TM,),
        in_specs=[_row(TM, AW), _row(TM, KVW), _row(TM, KVW), _row(TM, 2 * CC), _row(TM, 128), _row(TM, 128),
                  _row(TM, D), _row(TM, D), _slab((1, D), layer), _slab((DIN, D), layer, True)],
        out_specs=[_row(TM, DIN), _row(TM, D), _acc((1, D))],
        out_shape=[jax.ShapeDtypeStruct((T, DIN), BF16), jax.ShapeDtypeStruct((T, D), F32),
                   jax.ShapeDtypeStruct((1, D), F32)],
        compiler_params=_cp("arbitrary"),
    )(dq, dk, dv, du, rc, rs, x, dx, norm, win)


def _wgrad(buf, slab, a, b):
    T, M = a.shape
    N = b.shape[1]
    tmm = M // 2 if M > 1024 else M

    def body(buf_ref, a_ref, b_ref, o_ref):
        @pl.when(pl.program_id(1) == 0)
        def _():
            o_ref[...] = jnp.zeros_like(o_ref)

        o_ref[...] += _tn(a_ref[...], b_ref[...])

    return pl.pallas_call(
        body, name="wgrad", grid=(M // tmm, T // TK),
        in_specs=[pl.BlockSpec(memory_space=pl.ANY),
                  pl.BlockSpec((TK, tmm), lambda i, k: (k, i)), pl.BlockSpec((TK, N), lambda i, k: (k, 0))],
        out_specs=pl.BlockSpec((None, tmm, N), lambda i, k: (slab, i, 0)),
        out_shape=jax.ShapeDtypeStruct(buf.shape, F32),
        input_output_aliases={0: 0},
        compiler_params=_cp("parallel", "arbitrary"),
    )(buf, a, b)


HBM = pl.BlockSpec(memory_space=pl.ANY)


def _coords():
    return lax.axis_index("x"), lax.axis_index("y"), lax.axis_index("c")


def _other_chips(x, y):
    return [(1 - x, y), (x, 1 - y), (1 - x, 1 - y)]


def _all_gather(shards):
    n = len(shards)

    def body(*refs):
        ins, outs = refs[:n], refs[n:2 * n]
        send_sems, recv_sems, local_sems = refs[2 * n:]
        x, y, c = _coords()
        me, sibling = (x, y, c), (x, y, 1 - c)
        chips = _other_chips(x, y)

        def rows(a, dev):
            r = ins[a].shape[1]
            return outs[a].at[:, pl.ds(pl.multiple_of((4 * dev[0] + 2 * dev[1] + dev[2]) * r, r), r), :]

        def copy(a, k, block, to, src=None):
            return pltpu.make_async_remote_copy(
                src_ref=rows(a, block) if src is None else src, dst_ref=rows(a, block),
                send_sem=send_sems.at[a * 7 + k], recv_sem=recv_sems.at[a * 7 + k],
                device_id=to, device_id_type=MESH)

        mine = [pltpu.make_async_copy(ins[a], rows(a, me), local_sems.at[a]) for a in range(n)]
        for cp in mine:
            cp.start()
        first = []
        for a in range(n):
            first.append(copy(a, 0, me, sibling, src=ins[a]))
            first += [copy(a, 1 + j, me, (*chip, c), src=ins[a]) for j, chip in enumerate(chips)]
        for cp in first:
            cp.start()
        passed = []
        for j, chip in enumerate(chips):
            for a in range(n):
                copy(a, 1 + j, (*chip, c), me).wait_recv()
                fwd = copy(a, 4 + j, (*chip, c), sibling)
                fwd.start()
                passed.append(fwd)
        for a in range(n):
            copy(a, 0, sibling, me).wait_recv()
            for j, chip in enumerate(chips):
                copy(a, 4 + j, (*chip, 1 - c), me).wait_recv()
        for cp in first + passed:
            cp.wait_send()
        for cp in mine:
            cp.wait()

    return pl.pallas_call(
        body, name="all_gather_weights",
        in_specs=[HBM] * n, out_specs=[HBM] * n,
        out_shape=[jax.ShapeDtypeStruct((s.shape[0], N_DEV * s.shape[1], s.shape[2]), s.dtype) for s in shards],
        scratch_shapes=[pltpu.SemaphoreType.DMA((7 * n,)), pltpu.SemaphoreType.DMA((7 * n,)),
                        pltpu.SemaphoreType.DMA((n,))],
    )(*shards)


def _pair_exchange(grads):
    n = len(grads)

    def body(*refs):
        ins, got = refs[:n], refs[n:2 * n]
        send_sems, recv_sems = refs[2 * n:]
        x, y, c = _coords()
        sibling = (x, y, 1 - c)

        def remote(a, q):
            r = ins[a].shape[1] // N_DEV
            src = ins[a].at[:, pl.ds(pl.multiple_of((2 * q + 1 - c) * r, r), r), :]
            return pltpu.make_async_remote_copy(
                src_ref=src, dst_ref=got[a].at[q],
                send_sem=send_sems.at[a * N_CHIP + q], recv_sem=recv_sems.at[a * N_CHIP + q],
                device_id=sibling, device_id_type=MESH)

        sends = [remote(a, q) for a in range(n) for q in range(N_CHIP)]
        for cp in sends:
            cp.start()
        for cp in sends:
            cp.wait_recv()
        for cp in sends:
            cp.wait_send()

    return pl.pallas_call(
        body, name="grad_pair_exchange",
        in_specs=[HBM] * n, out_specs=[HBM] * n,
        out_shape=[jax.ShapeDtypeStruct((N_CHIP, g.shape[0], g.shape[1] // N_DEV, g.shape[2]), g.dtype) for g in grads],
        scratch_shapes=[pltpu.SemaphoreType.DMA((N_CHIP * n,)), pltpu.SemaphoreType.DMA((N_CHIP * n,))],
    )(*grads)


def _chip_exchange(parts):
    n = len(parts)

    def body(*refs):
        ins, outs = refs[:n], refs[n:2 * n]
        send_sems, recv_sems = refs[2 * n:]
        x, y, c = _coords()
        chips = _other_chips(x, y)

        def remote(a, j):
            return pltpu.make_async_remote_copy(
                src_ref=ins[a].at[j], dst_ref=outs[a].at[j],
                send_sem=send_sems.at[a * 3 + j], recv_sem=recv_sems.at[a * 3 + j],
                device_id=(*chips[j], c), device_id_type=MESH)

        sends = [remote(a, j) for a in range(n) for j in range(3)]
        for cp in sends:
            cp.start()
        for cp in sends:
            cp.wait_recv()
        for cp in sends:
            cp.wait_send()

    return pl.pallas_call(
        body, name="grad_chip_exchange",
        in_specs=[HBM] * n, out_specs=[HBM] * n,
        out_shape=[jax.ShapeDtypeStruct(p.shape, p.dtype) for p in parts],
        scratch_shapes=[pltpu.SemaphoreType.DMA((3 * n,)), pltpu.SemaphoreType.DMA((3 * n,))],
    )(*parts)


def _all_reduce_small(pack):
    R = pack.shape[0]

    def body(p_ref, tot_ref, all_ref, send_sems, recv_sems):
        x, y, c = _coords()
        me = 4 * x + 2 * y + c
        all_ref[me] = p_ref[...]
        peers = []
        for k in range(1, N_DEV):
            bx, by, bc = (k >> 2) & 1, (k >> 1) & 1, k & 1
            peers.append((x ^ bx, y ^ by, c ^ bc))

        def copy(k, slot, to):
            return pltpu.make_async_remote_copy(
                src_ref=p_ref, dst_ref=all_ref.at[slot], send_sem=send_sems.at[k], recv_sem=recv_sems.at[k],
                device_id=to, device_id_type=MESH)

        sends = [copy(k, me, peer) for k, peer in enumerate(peers)]
        for cp in sends:
            cp.start()
        for k, peer in enumerate(peers):
            copy(k, 4 * peer[0] + 2 * peer[1] + peer[2], peer).wait_recv()
        for cp in sends:
            cp.wait_send()
        tot = all_ref[0]
        for d in range(1, N_DEV):
            tot = tot + all_ref[d]
        tot_ref[...] = tot

    vmem = pl.BlockSpec(memory_space=pltpu.VMEM)
    return pl.pallas_call(
        body, name="all_reduce_small",
        in_specs=[vmem], out_specs=vmem,
        out_shape=jax.ShapeDtypeStruct((R, 128), F32),
        scratch_shapes=[pltpu.VMEM((N_DEV, R, 128), F32), pltpu.SemaphoreType.DMA((N_DEV - 1,)),
                        pltpu.SemaphoreType.DMA((N_DEV - 1,))],
    )(pack)


def _tile_rows(n, cap=512):
    t = min(n, cap)
    while n % t or t % 8:
        t -= 1
        if t < 8:
            return n
    return t


def _pair_sum(g, got, owner_dev, owner_chip, dtype):
    s, r8, c = g.shape
    r = r8 // N_DEV
    n = owner_dev.shape[0]

    def body(dev_ref, chip_ref, g_ref, got_ref, o_ref):
        o_ref[...] = (g_ref[...] + got_ref[...]).astype(dtype)

    return pl.pallas_call(
        body, name="pair_sum",
        grid_spec=pltpu.PrefetchScalarGridSpec(
            num_scalar_prefetch=2, grid=(n, s),
            in_specs=[pl.BlockSpec((None, r, c), lambda j, i, dev, chip: (i, dev[j], 0)),
                      pl.BlockSpec((None, None, r, c), lambda j, i, dev, chip: (chip[j], i, 0, 0))],
            out_specs=pl.BlockSpec((None, None, r, c), lambda j, i, dev, chip: (j, i, 0, 0))),
        out_shape=jax.ShapeDtypeStruct((n, s, r, c), dtype),
        compiler_params=_cp("parallel", "parallel"),
    )(owner_dev, owner_chip, g, got)


def _sum_chips(own, parts):
    _, s, r, c = parts.shape

    def body(o, p0, p1, p2, o_ref):
        o_ref[...] = ((o[...] + p0[...].astype(F32)) + p1[...].astype(F32)) + p2[...].astype(F32)

    def spec(q):
        return pl.BlockSpec((None, None, r, c), lambda i: (q, i, 0, 0))

    return pl.pallas_call(
        body, name="chip_sum", grid=(s,),
        in_specs=[spec(0), spec(0), spec(1), spec(2)], out_specs=pl.BlockSpec((None, r, c), lambda i: (i, 0, 0)),
        out_shape=jax.ShapeDtypeStruct((s, r, c), F32),
        compiler_params=_cp("parallel"),
    )(own, parts, parts, parts)


def _adamw(w, g, m, v):
    shape = w.shape
    c = shape[-1] if w.ndim > 1 else w.shape[0]
    args = [t.reshape(-1, c) for t in (w, g, m, v)]
    n = args[0].shape[0]
    tr = _tile_rows(n)

    def body(w_ref, g_ref, m_ref, v_ref, d_ref, mo_ref, vo_ref):
        g = g_ref[...]
        m = B1 * m_ref[...] + (1.0 - B1) * g
        v = B2 * v_ref[...] + (1.0 - B2) * jnp.square(g)
        m_hat = m / (1.0 - B1 ** STEP)
        v_hat = v / (1.0 - B2 ** STEP)
        d_ref[...] = -LR * (m_hat / (jnp.sqrt(v_hat) + ADAM_EPS) + WD * w_ref[...])
        mo_ref[...] = m
        vo_ref[...] = v

    outs = pl.pallas_call(
        body, name="adamw", grid=(n // tr,),
        in_specs=[_row(tr, c)] * 4, out_specs=[_row(tr, c)] * 3,
        out_shape=[jax.ShapeDtypeStruct((n, c), F32)] * 3,
        compiler_params=_cp("parallel"),
    )(*args)
    return [o.reshape(shape) for o in outs]


def _ffn_slab(ffn, which, layer):
    return (ffn * 3 + which) * 2 + layer


def _pack(pieces):
    flat = []
    for p in pieces:
        f = p.reshape(-1)
        flat.append(jnp.pad(f, (0, (-f.shape[0]) % 1024)))
    return jnp.concatenate(flat).reshape(-1, 128)


def _unpack(pack, shapes):
    flat = pack.reshape(-1)
    out, off = [], 0
    for s in shapes:
        size = 1
        for d in s:
            size *= d
        out.append(flat[off:off + size].reshape(s))
        off += size + (-size) % 1024
    return out


def kernel(x, positions, ffn1_norm, ffn1_w_gate, ffn1_w_up, ffn1_w_down, mix_norm, w_in, conv_w, conv_b, conv_ln_g, conv_ln_b, attn_sinks, w_out, ffn2_norm, ffn2_w_gate, ffn2_w_up, ffn2_w_down, final_norm, loss_target, m_ffn1_norm, m_ffn1_w_gate, m_ffn1_w_up, m_ffn1_w_down, m_mix_norm, m_w_in, m_conv_w, m_conv_b, m_conv_ln_g, m_conv_ln_b, m_attn_sinks, m_w_out, m_ffn2_norm, m_ffn2_w_gate, m_ffn2_w_up, m_ffn2_w_down, m_final_norm, v_ffn1_norm, v_ffn1_w_gate, v_ffn1_w_up, v_ffn1_w_down, v_mix_norm, v_w_in, v_conv_w, v_conv_b, v_conv_ln_g, v_conv_ln_b, v_attn_sinks, v_w_out, v_ffn2_norm, v_ffn2_w_gate, v_ffn2_w_up, v_ffn2_w_down, v_final_norm):
    L = ffn1_norm.shape[0]
    T = x.shape[1]
    x0 = x.reshape(T, D)
    target = loss_target.reshape(T, D)
    dev = 4 * lax.axis_index("x") + 2 * lax.axis_index("y") + lax.axis_index("c")

    def t_(w):
        return jnp.swapaxes(w, 1, 2)

    ffn_sh = jnp.stack([t_(ffn1_w_gate), t_(ffn1_w_up), ffn1_w_down,
                        t_(ffn2_w_gate), t_(ffn2_w_up), ffn2_w_down]).astype(BF16)
    ffn_sh = ffn_sh.reshape(6 * L, FF // N_DEV, D)
    win_sh = t_(w_in).astype(BF16)
    wout_sh = w_out.astype(BF16)
    cw_cols = CC // N_DEV
    cw_sh = jnp.pad(conv_w.reshape(-1), (0, (-L * CW * cw_cols) % 1024)).reshape(1, -1, 128)

    wffn, win, wout, cw_all = _all_gather([ffn_sh, win_sh, wout_sh, cw_sh])
    cw_rows = cw_sh.shape[1]
    cw_full = cw_all.reshape(N_DEV, cw_rows * 128)[:, :L * CW * cw_cols].reshape(N_DEV, L, CW, cw_cols)
    cw_full = jnp.transpose(cw_full, (1, 2, 0, 3)).reshape(L, CW, CC)

    inv_freq = 1.0 / (10000.0 ** (jnp.arange(0, HD, 2, dtype=F32) / HD))
    ang = positions.reshape(T).astype(F32)[:, None] * inv_freq
    cos, sin = jnp.cos(ang), jnp.sin(ang)
    rc = jnp.concatenate([cos, cos, cos, cos], axis=1)
    rs = jnp.concatenate([-sin, sin, -sin, sin], axis=1)

    n1 = ffn1_norm.reshape(L, 1, D)
    nm = mix_norm.reshape(L, 1, D)
    n2 = ffn2_norm.reshape(L, 1, D)
    cb = conv_b.reshape(L, 1, CC)
    lg = conv_ln_g.reshape(L, 1, CC)
    lb = conv_ln_b.reshape(L, 1, CC)

    saved = []
    xa = x0
    for l in range(L):
        h1, G1, U1, A1 = _ffn_up(xa, n1, wffn, _ffn_slab(0, 0, l), _ffn_slab(0, 1, l), l)
        xb = _ffn_down(A1, xa, wffn, _ffn_slab(0, 2, l))
        hm, qkv, u = _mix_in(xb, nm, win, rc, rs, l)
        ao = _attn_fwd(qkv, attn_sinks, l)
        y, co = _conv_fwd(u, cw_full, cb, lg, lb, l)
        xc, cat = _mix_out(ao, co, xb, wout, l)
        h2, G2, U2, A2 = _ffn_up(xc, n2, wffn, _ffn_slab(1, 0, l), _ffn_slab(1, 1, l), l)
        xd = _ffn_down(A2, xc, wffn, _ffn_slab(1, 2, l))
        saved.append((xa, h1, G1, U1, A1, xb, hm, qkv, u, y, cat, xc, h2, G2, U2, A2))
        xa = xd

    dx, loss_part, g_final = _final(xa, final_norm.reshape(1, D), target)

    gffn = lax.empty((6 * L, FF, D), F32)
    gin = lax.empty((L, DIN, D), F32)
    gout = lax.empty((L, D, D), F32)
    g_n1, g_nm, g_n2 = [None] * L, [None] * L, [None] * L
    g_cb, g_lg, g_lb, g_sink, g_cw = [None] * L, [None] * L, [None] * L, [None] * L, [None] * L
    for l in reversed(range(L)):
        xa, h1, G1, U1, A1, xb, hm, qkv, u, y, cat, xc, h2, G2, U2, A2 = saved[l]
        d, dG, dU = _ffn_bwd_act(dx, G2, U2, wffn, _ffn_slab(1, 2, l))
        gffn = _wgrad(gffn, _ffn_slab(1, 2, l), A2, d)
        gffn = _wgrad(gffn, _ffn_slab(1, 0, l), dG, h2)
        gffn = _wgrad(gffn, _ffn_slab(1, 1, l), dU, h2)
        dx, g_n2[l] = _ffn_bwd_in(dG, dU, xc, dx, n2, wffn, _ffn_slab(1, 0, l), _ffn_slab(1, 1, l), l)
        d, dao, dco = _mix_out_bwd(dx, wout, l)
        gout = _wgrad(gout, l, cat, d)
        dy, g_lg[l], g_lb[l], g_cb[l] = _conv_bwd_norm(dco, y, lg, lb, l)
        du, g_cw[l] = _conv_bwd_taps(dy, u, cw_full, l)
        dq, dk, dv, g_sink[l] = _attn_bwd(qkv, dao, attn_sinks, l)
        dp, dx, g_nm[l] = _mix_in_bwd(dq, dk, dv, du, rc, rs, xb, dx, nm, win, l)
        gin = _wgrad(gin, l, dp, hm)
        d, dG, dU = _ffn_bwd_act(dx, G1, U1, wffn, _ffn_slab(0, 2, l))
        gffn = _wgrad(gffn, _ffn_slab(0, 2, l), A1, d)
        gffn = _wgrad(gffn, _ffn_slab(0, 0, l), dG, h1)
        gffn = _wgrad(gffn, _ffn_slab(0, 1, l), dU, h1)
        dx, g_n1[l] = _ffn_bwd_in(dG, dU, xa, dx, n1, wffn, _ffn_slab(0, 0, l), _ffn_slab(0, 1, l), l)

    grad_x = dx.reshape(1, T, D)

    big_grads = [gffn, gin, gout]
    got = _pair_exchange(big_grads)
    cx, cy, cc = _coords()
    chip_of = [2 * cx + cy] + [2 * px + py for px, py in _other_chips(cx, cy)]
    own_chip = jnp.stack(chip_of[:1]).astype(jnp.int32)
    other_chips = jnp.stack(chip_of[1:]).astype(jnp.int32)
    own = [_pair_sum(g, r, 2 * own_chip + cc, own_chip, F32) for g, r in zip(big_grads, got)]
    sent = [_pair_sum(g, r, 2 * other_chips + cc, other_chips, BF16) for g, r in zip(big_grads, got)]
    gffn_sh, gin_sh, gout_sh = [_sum_chips(o, p) for o, p in zip(own, _chip_exchange(sent))]

    small = [loss_part,
             jnp.concatenate(g_n1), jnp.concatenate(g_nm), jnp.concatenate(g_n2), g_final,
             jnp.concatenate(g_cb), jnp.concatenate(g_lg), jnp.concatenate(g_lb),
             jnp.stack(g_sink)[:, :, 0], jnp.stack(g_cw)]
    small_shapes = [(1, 128), (L, D), (L, D), (L, D), (D,), (L, CC), (L, CC), (L, CC), (L, NH), (L, CW, CC)]
    tot = _unpack(_all_reduce_small(_pack(small)), small_shapes)
    loss = tot[0][0, 0]
    gr_n1, gr_nm, gr_n2, gr_final, gr_cb, gr_lg, gr_lb, gr_sink, gr_cw_full = tot[1:]
    gr_cw = lax.dynamic_slice_in_dim(gr_cw_full, dev * cw_cols, cw_cols, axis=2)

    gf = gffn_sh.reshape(2, 3, L, FF // N_DEV, D)
    grads = {
        "ffn1_norm": gr_n1, "ffn1_w_gate": t_(gf[0, 0]), "ffn1_w_up": t_(gf[0, 1]), "ffn1_w_down": gf[0, 2],
        "mix_norm": gr_nm, "w_in": t_(gin_sh), "conv_w": gr_cw, "conv_b": gr_cb, "conv_ln_g": gr_lg,
        "conv_ln_b": gr_lb, "attn_sinks": gr_sink, "w_out": gout_sh,
        "ffn2_norm": gr_n2, "ffn2_w_gate": t_(gf[1, 0]), "ffn2_w_up": t_(gf[1, 1]), "ffn2_w_down": gf[1, 2],
        "final_norm": gr_final,
    }
    weights = dict(ffn1_norm=ffn1_norm, ffn1_w_gate=ffn1_w_gate, ffn1_w_up=ffn1_w_up, ffn1_w_down=ffn1_w_down, mix_norm=mix_norm, w_in=w_in, conv_w=conv_w, conv_b=conv_b, conv_ln_g=conv_ln_g, conv_ln_b=conv_ln_b, attn_sinks=attn_sinks, w_out=w_out, ffn2_norm=ffn2_norm, ffn2_w_gate=ffn2_w_gate, ffn2_w_up=ffn2_w_up, ffn2_w_down=ffn2_w_down, final_norm=final_norm)
    moms = dict(ffn1_norm=m_ffn1_norm, ffn1_w_gate=m_ffn1_w_gate, ffn1_w_up=m_ffn1_w_up, ffn1_w_down=m_ffn1_w_down, mix_norm=m_mix_norm, w_in=m_w_in, conv_w=m_conv_w, conv_b=m_conv_b, conv_ln_g=m_conv_ln_g, conv_ln_b=m_conv_ln_b, attn_sinks=m_attn_sinks, w_out=m_w_out, ffn2_norm=m_ffn2_norm, ffn2_w_gate=m_ffn2_w_gate, ffn2_w_up=m_ffn2_w_up, ffn2_w_down=m_ffn2_w_down, final_norm=m_final_norm)
    vels = dict(ffn1_norm=v_ffn1_norm, ffn1_w_gate=v_ffn1_w_gate, ffn1_w_up=v_ffn1_w_up, ffn1_w_down=v_ffn1_w_down, mix_norm=v_mix_norm, w_in=v_w_in, conv_w=v_conv_w, conv_b=v_conv_b, conv_ln_g=v_conv_ln_g, conv_ln_b=v_conv_ln_b, attn_sinks=v_attn_sinks, w_out=v_w_out, ffn2_norm=v_ffn2_norm, ffn2_w_gate=v_ffn2_w_gate, ffn2_w_up=v_ffn2_w_up, ffn2_w_down=v_ffn2_w_down, final_norm=v_final_norm)

    names = list(weights)
    big = ("ffn1_w_gate", "ffn1_w_up", "ffn1_w_down", "w_in", "w_out", "ffn2_w_gate", "ffn2_w_up", "ffn2_w_down")
    delta, new_m, new_v = {}, {}, {}
    for k in big:
        delta[k], new_m[k], new_v[k] = _adamw(weights[k], grads[k], moms[k], vels[k])
    rest = [k for k in names if k not in big]
    rest_shapes = [weights[k].shape for k in rest]
    packed = _adamw(*[_pack([t[k] for k in rest]) for t in (weights, grads, moms, vels)])
    for res, packed_out in zip((delta, new_m, new_v), packed):
        for k, val in zip(rest, _unpack(packed_out, rest_shapes)):
            res[k] = val

    return (loss, grad_x, *[grads[k] for k in names], *[delta[k] for k in names],
            *[new_m[k] for k in names], *[new_v[k] for k in names])
```

```python
import functools

import jax
import jax.numpy as jnp
from jax import lax
from jax.experimental import pallas as pl
from jax.experimental.pallas import tpu as pltpu

F32 = jnp.float32
BF16 = jnp.bfloat16
MESH = pl.DeviceIdType.MESH

N_DEV = 8
N_CHIP = 4
D = 1024
FF = 2816
HD = 64
NH = 8
NKV = 2
GROUP = NH // NKV
AW = NH * HD
KVW = NKV * HD
QKW = AW + KVW
QKVW = AW + 2 * KVW
CC = 512
CW = 31
DIN = QKVW + 2 * CC
BLK = 128
HALO = 32
EPS = 1e-5
SCALE = HD ** -0.5
NEG = float(jnp.finfo(jnp.float32).min)

LR, B1, B2, ADAM_EPS, WD, STEP = 0.001, 0.9, 0.999, 1e-08, 0.01, 10

TM = 256
TK = 512
FC = 256
ATT_BLOCKS = 4
VMEM_LIMIT = 56 * 1024 * 1024


def _cp(*sem):
    return pltpu.CompilerParams(dimension_semantics=sem, vmem_limit_bytes=VMEM_LIMIT)


def _row(tm, c):
    return pl.BlockSpec((tm, c), lambda i: (i, 0))


def _slab(shape, k, single=False):
    zeros = (0,) * len(shape)
    kw = dict(pipeline_mode=pl.Buffered(1)) if single else {}
    return pl.BlockSpec((None, *shape), lambda i: (k, *zeros), **kw)


def _acc(shape):
    return pl.BlockSpec(shape, lambda i: (0,) * len(shape))


def _nt(a, b):
    return lax.dot_general(a, b, (((1,), (1,)), ((), ())), preferred_element_type=F32)


def _tn(a, b):
    return lax.dot_general(a, b, (((0,), (0,)), ((), ())), preferred_element_type=F32)


def _nn(a, b):
    return jnp.dot(a, b, preferred_element_type=F32)


def _sigmoid(x):
    return jax.nn.sigmoid(x)


def _dsilu(z):
    s = _sigmoid(z)
    return s * (1.0 + z * (1.0 - s))


def _rms(x, g):
    r = lax.rsqrt(jnp.mean(x * x, axis=-1, keepdims=True) + EPS)
    xh = x * r
    return xh, r, xh * g


def _rms_bwd(dh, xh, r, g):
    dxh = dh * g
    return r * (dxh - xh * jnp.mean(dxh * xh, axis=-1, keepdims=True))


def _rope(t, c128, s128):
    w = t.shape[1]
    lane = lax.broadcasted_iota(jnp.int32, t.shape, 1)
    rot = jnp.where(lane % HD < HD // 2, pltpu.roll(t, w - HD // 2, 1), pltpu.roll(t, HD // 2, 1))
    return t * jnp.tile(c128, (1, w // 128)) + rot * jnp.tile(s128, (1, w // 128))


def _ffn_up(x, norm, wffn, sg, su, layer):
    T = x.shape[0]

    def body(x_ref, g_ref, wg_ref, wu_ref, h_ref, G_ref, U_ref, A_ref):
        _, _, hn = _rms(x_ref[...], g_ref[...])
        h = hn.astype(BF16)
        h_ref[...] = h
        for c in range(FF // FC):
            sl = slice(c * FC, (c + 1) * FC)
            g = _nt(h, wg_ref[sl, :])
            u = _nt(h, wu_ref[sl, :])
            G_ref[:, sl] = g.astype(BF16)
            U_ref[:, sl] = u.astype(BF16)
            A_ref[:, sl] = (g * _sigmoid(g) * u).astype(BF16)

    return pl.pallas_call(
        body, name="ffn_up", grid=(T // TM,),
        in_specs=[_row(TM, D), _slab((1, D), layer), _slab((FF, D), sg, True), _slab((FF, D), su, True)],
        out_specs=[_row(TM, D), _row(TM, FF), _row(TM, FF), _row(TM, FF)],
        out_shape=[jax.ShapeDtypeStruct((T, D), BF16)] + [jax.ShapeDtypeStruct((T, FF), BF16)] * 3,
        compiler_params=_cp("parallel"),
    )(x, norm, wffn, wffn)


def _ffn_down(a, x, wffn, sd):
    T = x.shape[0]

    def body(a_ref, x_ref, w_ref, o_ref):
        o_ref[...] = x_ref[...] + 0.5 * _nn(a_ref[...], w_ref[...])

    return pl.pallas_call(
        body, name="ffn_down", grid=(T // TM,),
        in_specs=[_row(TM, FF), _row(TM, D), _slab((FF, D), sd, True)],
        out_specs=_row(TM, D),
        out_shape=jax.ShapeDtypeStruct((T, D), F32),
        compiler_params=_cp("parallel"),
    )(a, x, wffn)


def _mix_in(x, norm, win, rc, rs, layer):
    T = x.shape[0]

    def body(x_ref, g_ref, w_ref, c_ref, s_ref, h_ref, qkv_ref, u_ref):
        _, _, hn = _rms(x_ref[...], g_ref[...])
        h = hn.astype(BF16)
        h_ref[...] = h
        qk = _nt(h, w_ref[0:QKW, :])
        qkv_ref[:, 0:QKW] = _rope(qk, c_ref[...], s_ref[...]).astype(BF16)
        qkv_ref[:, QKW:QKVW] = _nt(h, w_ref[QKW:QKVW, :]).astype(BF16)
        for c in range(2 * CC // FC):
            u_ref[:, c * FC:(c + 1) * FC] = _nt(h, w_ref[QKVW + c * FC:QKVW + (c + 1) * FC, :]).astype(BF16)

    return pl.pallas_call(
        body, name="mix_in", grid=(T // TM,),
        in_specs=[_row(TM, D), _slab((1, D), layer), _slab((DIN, D), layer, True), _row(TM, 128), _row(TM, 128)],
        out_specs=[_row(TM, D), _row(TM, QKVW), _row(TM, 2 * CC)],
        out_shape=[jax.ShapeDtypeStruct((T, D), BF16), jax.ShapeDtypeStruct((T, QKVW), BF16),
                   jax.ShapeDtypeStruct((T, 2 * CC), BF16)],
        compiler_params=_cp("parallel"),
    )(x, norm, win, rc, rs)


def _band_mask(has_prev):
    j = lax.broadcasted_iota(jnp.int32, (2 * BLK, BLK), 0)
    r = lax.broadcasted_iota(jnp.int32, (2 * BLK, BLK), 1) + BLK
    rel = r - j
    return jnp.tile((rel >= 0) & (rel < BLK) & (has_prev | (j >= BLK)), (1, GROUP))


def _band(prev_ref, cur_ref, b, col):
    if b == 0:
        return jnp.concatenate([prev_ref[:, col:col + HD], cur_ref[0:BLK, col:col + HD]], axis=0)
    return cur_ref[(b - 1) * BLK:(b + 1) * BLK, col:col + HD]


def _stack_heads(ref, b, kv):
    cols = [(kv * GROUP + g) * HD for g in range(GROUP)]
    return jnp.concatenate([ref[b * BLK:(b + 1) * BLK, c:c + HD] for c in cols], axis=0)


def _unstack_t(xt):
    x = xt.T
    return jnp.concatenate([x[g * BLK:(g + 1) * BLK, :] for g in range(GROUP)], axis=1)


def _sink_row(sink_ref, layer, kv):
    return jnp.concatenate([jnp.full((1, BLK), sink_ref[layer, kv * GROUP + g], F32) for g in range(GROUP)], axis=1)


def _probs_t(q4, kb, mask, sink):
    s = jnp.where(mask, _nt(kb, q4) * SCALE, NEG)
    m = jnp.maximum(jnp.max(s, axis=0, keepdims=True), sink)
    p = jnp.exp(s - m)
    e = jnp.exp(sink - m)
    inv = 1.0 / (jnp.sum(p, axis=0, keepdims=True) + e)
    return p * inv, e * inv


def _attn_fwd(qkv, sinks, layer):
    T = qkv.shape[0]
    tq = ATT_BLOCKS * BLK

    def body(sink_ref, cur_ref, prev_ref, o_ref):
        first = _band_mask(pl.program_id(0) > 0)
        later = _band_mask(True)
        for b in range(ATT_BLOCKS):
            outs = []
            for kv in range(NKV):
                kb = _band(prev_ref, cur_ref, b, AW + kv * HD)
                vb = _band(prev_ref, cur_ref, b, QKW + kv * HD)
                pt, _ = _probs_t(_stack_heads(cur_ref, b, kv), kb, first if b == 0 else later,
                                 _sink_row(sink_ref, layer, kv))
                outs.append(_unstack_t(_nn(vb.T, pt.astype(BF16))))
            o_ref[b * BLK:(b + 1) * BLK, :] = jnp.concatenate(outs, axis=1).astype(BF16)

    return pl.pallas_call(
        body, name="attn_fwd", grid=(T // tq,),
        in_specs=[pl.BlockSpec(memory_space=pltpu.SMEM), _row(tq, QKVW),
                  pl.BlockSpec((BLK, QKVW), lambda i: (jnp.maximum(i * ATT_BLOCKS - 1, 0), 0))],
        out_specs=_row(tq, AW),
        out_shape=jax.ShapeDtypeStruct((T, AW), BF16),
        compiler_params=_cp("parallel"),
    )(sinks, qkv, qkv)


def _glu(u):
    u = u.astype(F32)
    return u[:, :CC] * _sigmoid(u[:, CC:])


def _conv_fwd(u, cw, cb, lg, lb, layer):
    T = u.shape[0]

    def body(u_ref, up_ref, w_ref, b_ref, g_ref, bb_ref, y_ref, o_ref, ext_ref):
        i = pl.program_id(0)
        ext_ref[0:HALO, :] = jnp.where(i > 0, _glu(up_ref[...]), 0.0)
        ext_ref[HALO:, :] = _glu(u_ref[...])
        acc = jnp.zeros((TM, CC), F32)
        for k in range(CW):
            acc = acc + w_ref[k:k + 1, :] * ext_ref[pl.ds(HALO - (CW - 1) + k, TM), :]
        y = acc + b_ref[...]
        y_ref[...] = y
        xc = y - jnp.mean(y, axis=-1, keepdims=True)
        z = xc * lax.rsqrt(jnp.mean(xc * xc, axis=-1, keepdims=True) + EPS) * g_ref[...] + bb_ref[...]
        o_ref[...] = (z * _sigmoid(z)).astype(BF16)

    return pl.pallas_call(
        body, name="conv_fwd", grid=(T // TM,),
        in_specs=[_row(TM, 2 * CC),
                  pl.BlockSpec((HALO, 2 * CC), lambda i: (jnp.maximum(i * (TM // HALO) - 1, 0), 0)),
                  _slab((CW, CC), layer), _slab((1, CC), layer), _slab((1, CC), layer), _slab((1, CC), layer)],
        out_specs=[_row(TM, CC), _row(TM, CC)],
        out_shape=[jax.ShapeDtypeStruct((T, CC), F32), jax.ShapeDtypeStruct((T, CC), BF16)],
        scratch_shapes=[pltpu.VMEM((TM + HALO, CC), F32)],
        compiler_params=_cp("parallel"),
    )(u, u, cw, cb, lg, lb)


def _mix_out(ao, co, x, wout, layer):
    T = x.shape[0]

    def body(ao_ref, co_ref, x_ref, w_ref, o_ref, cat_ref):
        cat = jnp.concatenate([ao_ref[...], co_ref[...]], axis=1)
        cat_ref[...] = cat
        o_ref[...] = x_ref[...] + _nn(cat, w_ref[...])

    return pl.pallas_call(
        body, name="mix_out", grid=(T // TM,),
        in_specs=[_row(TM, AW), _row(TM, CC), _row(TM, D), _slab((D, D), layer, True)],
        out_specs=[_row(TM, D), _row(TM, D)],
        out_shape=[jax.ShapeDtypeStruct((T, D), F32), jax.ShapeDtypeStruct((T, D), BF16)],
        compiler_params=_cp("parallel"),
    )(ao, co, x, wout)


def _final(x, norm, target):
    T = x.shape[0]

    def body(x_ref, g_ref, t_ref, dx_ref, loss_ref, dg_ref):
        @pl.when(pl.program_id(0) == 0)
        def _():
            loss_ref[...] = jnp.zeros_like(loss_ref)
            dg_ref[...] = jnp.zeros_like(dg_ref)

        g = g_ref[...]
        xh, r, y = _rms(x_ref[...], g)
        err = y - t_ref[...]
        loss_ref[...] += jnp.full(loss_ref.shape, (0.5 / D) * jnp.sum(err * err), F32)
        dy = err * (1.0 / D)
        dg_ref[...] += jnp.sum(dy * xh, axis=0, keepdims=True)
        dx_ref[...] = _rms_bwd(dy, xh, r, g)

    return pl.pallas_call(
        body, name="final_loss", grid=(T // TM,),
        in_specs=[_row(TM, D), _acc((1, D)), _row(TM, D)],
        out_specs=[_row(TM, D), _acc((1, 128)), _acc((1, D))],
        out_shape=[jax.ShapeDtypeStruct((T, D), F32), jax.ShapeDtypeStruct((1, 128), F32),
                   jax.ShapeDtypeStruct((1, D), F32)],
        compiler_params=_cp("arbitrary"),
    )(x, norm, target)


def _ffn_bwd_act(dx, G, U, wffn, sd):
    T = dx.shape[0]

    def body(dx_ref, G_ref, U_ref, w_ref, d_ref, dG_ref, dU_ref):
        d = (0.5 * dx_ref[...]).astype(BF16)
        d_ref[...] = d
        for c in range(FF // FC):
            sl = slice(c * FC, (c + 1) * FC)
            da = _nt(d, w_ref[sl, :])
            g = G_ref[:, sl].astype(F32)
            u = U_ref[:, sl].astype(F32)
            dU_ref[:, sl] = (da * g * _sigmoid(g)).astype(BF16)
            dG_ref[:, sl] = (da * u * _dsilu(g)).astype(BF16)

    return pl.pallas_call(
        body, name="ffn_bwd_act", grid=(T // TM,),
        in_specs=[_row(TM, D), _row(TM, FF), _row(TM, FF), _slab((FF, D), sd, True)],
        out_specs=[_row(TM, D), _row(TM, FF), _row(TM, FF)],
        out_shape=[jax.ShapeDtypeStruct((T, D), BF16)] + [jax.ShapeDtypeStruct((T, FF), BF16)] * 2,
        compiler_params=_cp("parallel"),
    )(dx, G, U, wffn)


def _ffn_bwd_in(dG, dU, x, dx, norm, wffn, sg, su, layer):
    T = x.shape[0]

    def body(dG_ref, dU_ref, x_ref, dx_ref, g_ref, wg_ref, wu_ref, o_ref, dg_ref):
        @pl.when(pl.program_id(0) == 0)
        def _():
            dg_ref[...] = jnp.zeros_like(dg_ref)

        dh = _nn(dG_ref[...], wg_ref[...]) + _nn(dU_ref[...], wu_ref[...])
        g = g_ref[...]
        xh, r, _ = _rms(x_ref[...], g)
        dg_ref[...] += jnp.sum(dh * xh, axis=0, keepdims=True)
        o_ref[...] = dx_ref[...] + _rms_bwd(dh, xh, r, g)

    return pl.pallas_call(
        body, name="ffn_bwd_in", grid=(T // TM,),
        in_specs=[_row(TM, FF), _row(TM, FF), _row(TM, D), _row(TM, D), _slab((1, D), layer),
                  _slab((FF, D), sg, True), _slab((FF, D), su, True)],
        out_specs=[_row(TM, D), _acc((1, D))],
        out_shape=[jax.ShapeDtypeStruct((T, D), F32), jax.ShapeDtypeStruct((1, D), F32)],
        compiler_params=_cp("arbitrary"),
    )(dG, dU, x, dx, norm, wffn, wffn)


def _mix_out_bwd(dx, wout, layer):
    T = dx.shape[0]

    def body(dx_ref, w_ref, d_ref, dao_ref, dco_ref):
        d = dx_ref[...].astype(BF16)
        d_ref[...] = d
        dcat = _nt(d, w_ref[...])
        dao_ref[...] = dcat[:, :AW].astype(BF16)
        dco_ref[...] = dcat[:, AW:].astype(BF16)

    return pl.pallas_call(
        body, name="mix_out_bwd", grid=(T // TM,),
        in_specs=[_row(TM, D), _slab((D, D), layer, True)],
        out_specs=[_row(TM, D), _row(TM, AW), _row(TM, CC)],
        out_shape=[jax.ShapeDtypeStruct((T, D), BF16), jax.ShapeDtypeStruct((T, AW), BF16),
                   jax.ShapeDtypeStruct((T, CC), BF16)],
        compiler_params=_cp("parallel"),
    )(dx, wout)


def _conv_bwd_norm(dco, y, lg, lb, layer):
    T = y.shape[0]

    def body(dco_ref, y_ref, g_ref, bb_ref, dy_ref, dlg_ref, dlb_ref, dcb_ref):
        @pl.when(pl.program_id(0) == 0)
        def _():
            dlg_ref[...] = jnp.zeros_like(dlg_ref)
            dlb_ref[...] = jnp.zeros_like(dlb_ref)
            dcb_ref[...] = jnp.zeros_like(dcb_ref)

        y = y_ref[...]
        g = g_ref[...]
        xc = y - jnp.mean(y, axis=-1, keepdims=True)
        rs = lax.rsqrt(jnp.mean(xc * xc, axis=-1, keepdims=True) + EPS)
        xn = xc * rs
        z = xn * g + bb_ref[...]
        dz = dco_ref[...].astype(F32) * _dsilu(z)
        dlg_ref[...] += jnp.sum(dz * xn, axis=0, keepdims=True)
        dlb_ref[...] += jnp.sum(dz, axis=0, keepdims=True)
        dxn = dz * g
        dy = rs * (dxn - jnp.mean(dxn, axis=-1, keepdims=True) - xn * jnp.mean(dxn * xn, axis=-1, keepdims=True))
        dcb_ref[...] += jnp.sum(dy, axis=0, keepdims=True)
        dy_ref[...] = dy

    return pl.pallas_call(
        body, name="conv_bwd_norm", grid=(T // TM,),
        in_specs=[_row(TM, CC), _row(TM, CC), _slab((1, CC), layer), _slab((1, CC), layer)],
        out_specs=[_row(TM, CC), _acc((1, CC)), _acc((1, CC)), _acc((1, CC))],
        out_shape=[jax.ShapeDtypeStruct((T, CC), F32)] + [jax.ShapeDtypeStruct((1, CC), F32)] * 3,
        compiler_params=_cp("arbitrary"),
    )(dco, y, lg, lb)


def _conv_bwd_taps(dy, u, cw, layer):
    T = u.shape[0]
    n_halo = T // HALO

    def body(dy_ref, dyn_ref, u_ref, up_ref, w_ref, du_ref, dw_ref, hext_ref, dext_ref):
        i = pl.program_id(0)

        @pl.when(i == 0)
        def _():
            dw_ref[...] = jnp.zeros_like(dw_ref)

        hext_ref[0:HALO, :] = jnp.where(i > 0, _glu(up_ref[...]), 0.0)
        hext_ref[HALO:, :] = _glu(u_ref[...])
        dy = dy_ref[...]
        dext_ref[0:TM, :] = dy
        dext_ref[TM:, :] = jnp.where(i < pl.num_programs(0) - 1, dyn_ref[...], 0.0)
        dh = jnp.zeros((TM, CC), F32)
        for k in range(CW):
            dh = dh + w_ref[k:k + 1, :] * dext_ref[pl.ds(CW - 1 - k, TM), :]
            dw_ref[k:k + 1, :] += jnp.sum(dy * hext_ref[pl.ds(HALO - (CW - 1) + k, TM), :], axis=0, keepdims=True)
        uu = u_ref[...].astype(F32)
        a = uu[:, :CC]
        sg = _sigmoid(uu[:, CC:])
        du_ref[:, :CC] = (dh * sg).astype(BF16)
        du_ref[:, CC:] = (dh * a * sg * (1.0 - sg)).astype(BF16)

    return pl.pallas_call(
        body, name="conv_bwd_taps", grid=(T // TM,),
        in_specs=[_row(TM, CC),
                  pl.BlockSpec((HALO, CC), lambda i: (jnp.minimum((i + 1) * (TM // HALO), n_halo - 1), 0)),
                  _row(TM, 2 * CC),
                  pl.BlockSpec((HALO, 2 * CC), lambda i: (jnp.maximum(i * (TM // HALO) - 1, 0), 0)),
                  _slab((CW, CC), layer)],
        out_specs=[_row(TM, 2 * CC), _acc((CW, CC))],
        out_shape=[jax.ShapeDtypeStruct((T, 2 * CC), BF16), jax.ShapeDtypeStruct((CW, CC), F32)],
        scratch_shapes=[pltpu.VMEM((TM + HALO, CC), F32), pltpu.VMEM((TM + HALO, CC), F32)],
        compiler_params=_cp("arbitrary"),
    )(dy, dy, u, u, cw)


def _attn_bwd(qkv, dao, sinks, layer):
    T = qkv.shape[0]
    tq = ATT_BLOCKS * BLK

    def body(sink_ref, cur_ref, prev_ref, do_ref, dq_ref, dk_ref, dv_ref, ds_ref):
        i = pl.program_id(0)

        @pl.when(i == 0)
        def _():
            dk_ref[...] = jnp.zeros_like(dk_ref)
            dv_ref[...] = jnp.zeros_like(dv_ref)
            ds_ref[...] = jnp.zeros_like(ds_ref)

        first = _band_mask(i > 0)
        later = _band_mask(True)
        base = pl.multiple_of(i * tq, tq)
        before = pl.multiple_of(jnp.maximum(i * ATT_BLOCKS - 1, 0) * BLK, BLK)
        for b in range(ATT_BLOCKS):
            dqs, dks, dvs = [], [], []
            for kv in range(NKV):
                kb = _band(prev_ref, cur_ref, b, AW + kv * HD)
                vb = _band(prev_ref, cur_ref, b, QKW + kv * HD)
                q4 = _stack_heads(cur_ref, b, kv)
                do4 = _stack_heads(do_ref, b, kv)
                pt, psink = _probs_t(q4, kb, first if b == 0 else later, _sink_row(sink_ref, layer, kv))
                dpt = _nt(vb, do4)
                dd = jnp.sum(pt * dpt, axis=0, keepdims=True)
                dst = (pt * (dpt - dd) * SCALE).astype(BF16)
                sd = psink * dd
                for g in range(GROUP):
                    hh = kv * GROUP + g
                    ds_ref[hh:hh + 1, :] += jnp.full((1, 128), -jnp.sum(sd[:, g * BLK:(g + 1) * BLK]), F32)
                dqs.append(_unstack_t(_nn(kb.T, dst)))
                dks.append(_nn(dst, q4))
                dvs.append(_nn(pt.astype(BF16), do4))
            dq_ref[b * BLK:(b + 1) * BLK, :] = jnp.concatenate(dqs, axis=1).astype(BF16)
            dkband = jnp.concatenate(dks, axis=1)
            dvband = jnp.concatenate(dvs, axis=1)
            if b == 0:
                dk_ref[pl.ds(before, BLK), :] += dkband[:BLK]
                dv_ref[pl.ds(before, BLK), :] += dvband[:BLK]
                dk_ref[pl.ds(base, BLK), :] += dkband[BLK:]
                dv_ref[pl.ds(base, BLK), :] += dvband[BLK:]
            else:
                r0 = pl.multiple_of(base + (b - 1) * BLK, BLK)
                dk_ref[pl.ds(r0, 2 * BLK), :] += dkband
                dv_ref[pl.ds(r0, 2 * BLK), :] += dvband

    return pl.pallas_call(
        body, name="attn_bwd", grid=(T // tq,),
        in_specs=[pl.BlockSpec(memory_space=pltpu.SMEM), _row(tq, QKVW),
                  pl.BlockSpec((BLK, QKVW), lambda i: (jnp.maximum(i * ATT_BLOCKS - 1, 0), 0)), _row(tq, AW)],
        out_specs=[_row(tq, AW), _acc((T, KVW)), _acc((T, KVW)), _acc((NH, 128))],
        out_shape=[jax.ShapeDtypeStruct((T, AW), BF16), jax.ShapeDtypeStruct((T, KVW), F32),
                   jax.ShapeDtypeStruct((T, KVW), F32), jax.ShapeDtypeStruct((NH, 128), F32)],
        compiler_params=_cp("arbitrary"),
    )(sinks, qkv, qkv, dao)


def _mix_in_bwd(dq, dk, dv, du, rc, rs, x, dx, norm, win, layer):
    T = x.shape[0]

    def body(dq_ref, dk_ref, dv_ref, du_ref, c_ref, s_ref, x_ref, dx_ref, g_ref, w_ref, dp_ref, o_ref, dg_ref):
        @pl.when(pl.program_id(0) == 0)
        def _():
            dg_ref[...] = jnp.zeros_like(dg_ref)

        dqk = jnp.concatenate([dq_ref[...].astype(F32), dk_ref[...]], axis=1)
        dqk = _rope(dqk, c_ref[...], -s_ref[...])
        dp = jnp.concatenate([dqk.astype(BF16), dv_ref[...].astype(BF16), du_ref[...]], axis=1)
        dp_ref[...] = dp
        dh = _nn(dp, w_ref[...])
        g = g_ref[...]
        xh, r, _ = _rms(x_ref[...], g)
        dg_ref[...] += jnp.sum(dh * xh, axis=0, keepdims=True)
        o_ref[...] = dx_ref[...] + _rms_bwd(dh, xh, r, g)

    return pl.pallas_call(
        body, name="mix_in_bwd", grid=(T // TM,),
        in_specs=[_row(TM, AW), _row(TM, KVW), _row(TM, KVW), _row(TM, 2 * CC), _row(TM, 128), _row(TM, 128),
                  _row(TM, D), _row(TM, D), _slab((1, D), layer), _slab((DIN, D), layer, True)],
        out_specs=[_row(TM, DIN), _row(TM, D), _acc((1, D))],
        out_shape=[jax.ShapeDtypeStruct((T, DIN), BF16), jax.ShapeDtypeStruct((T, D), F32),
                   jax.ShapeDtypeStruct((1, D), F32)],
        compiler_params=_cp("arbitrary"),
    )(dq, dk, dv, du, rc, rs, x, dx, norm, win)


def _wgrad(buf, slab, a, b):
    T, M = a.shape
    N = b.shape[1]
    tmm = M // 2 if M > 1024 else M

    def body(buf_ref, a_ref, b_ref, o_ref):
        @pl.when(pl.program_id(1) == 0)
        def _():
            o_ref[...] = jnp.zeros_like(o_ref)

        o_ref[...] += _tn(a_ref[...], b_ref[...])

    return pl.pallas_call(
        body, name="wgrad", grid=(M // tmm, T // TK),
        in_specs=[pl.BlockSpec(memory_space=pl.ANY),
                  pl.BlockSpec((TK, tmm), lambda i, k: (k, i)), pl.BlockSpec((TK, N), lambda i, k: (k, 0))],
        out_specs=pl.BlockSpec((None, tmm, N), lambda i, k: (slab, i, 0)),
        out_shape=jax.ShapeDtypeStruct(buf.shape, F32),
        input_output_aliases={0: 0},
        compiler_params=_cp("parallel", "arbitrary"),
    )(buf, a, b)


HBM = pl.BlockSpec(memory_space=pl.ANY)


def _coords():
    return lax.axis_index("x"), lax.axis_index("y"), lax.axis_index("c")


def _other_chips(x, y):
    return [(1 - x, y), (x, 1 - y), (1 - x, 1 - y)]


def _all_gather(shards):
    n = len(shards)

    def body(*refs):
        ins, outs = refs[:n], refs[n:2 * n]
        send_sems, recv_sems, local_sems = refs[2 * n:]
        x, y, c = _coords()
        me, sibling = (x, y, c), (x, y, 1 - c)
        chips = _other_chips(x, y)

        def rows(a, dev):
            r = ins[a].shape[1]
            return outs[a].at[:, pl.ds(pl.multiple_of((4 * dev[0] + 2 * dev[1] + dev[2]) * r, r), r), :]

        def copy(a, k, block, to, src=None):
            return pltpu.make_async_remote_copy(
                src_ref=rows(a, block) if src is None else src, dst_ref=rows(a, block),
                send_sem=send_sems.at[a * 7 + k], recv_sem=recv_sems.at[a * 7 + k],
                device_id=to, device_id_type=MESH)

        mine = [pltpu.make_async_copy(ins[a], rows(a, me), local_sems.at[a]) for a in range(n)]
        for cp in mine:
            cp.start()
        first = []
        for a in range(n):
            first.append(copy(a, 0, me, sibling, src=ins[a]))
            first += [copy(a, 1 + j, me, (*chip, c), src=ins[a]) for j, chip in enumerate(chips)]
        for cp in first:
            cp.start()
        passed = []
        for j, chip in enumerate(chips):
            for a in range(n):
                copy(a, 1 + j, (*chip, c), me).wait_recv()
                fwd = copy(a, 4 + j, (*chip, c), sibling)
                fwd.start()
                passed.append(fwd)
        for a in range(n):
            copy(a, 0, sibling, me).wait_recv()
            for j, chip in enumerate(chips):
                copy(a, 4 + j, (*chip, 1 - c), me).wait_recv()
        for cp in first + passed:
            cp.wait_send()
        for cp in mine:
            cp.wait()

    return pl.pallas_call(
        body, name="all_gather_weights",
        in_specs=[HBM] * n, out_specs=[HBM] * n,
        out_shape=[jax.ShapeDtypeStruct((s.shape[0], N_DEV * s.shape[1], s.shape[2]), s.dtype) for s in shards],
        scratch_shapes=[pltpu.SemaphoreType.DMA((7 * n,)), pltpu.SemaphoreType.DMA((7 * n,)),
                        pltpu.SemaphoreType.DMA((n,))],
    )(*shards)


def _pair_exchange(grads):
    n = len(grads)

    def body(*refs):
        ins, got = refs[:n], refs[n:2 * n]
        send_sems, recv_sems = refs[2 * n:]
        x, y, c = _coords()
        sibling = (x, y, 1 - c)

        def remote(a, q):
            r = ins[a].shape[1] // N_DEV
            src = ins[a].at[:, pl.ds(pl.multiple_of((2 * q + 1 - c) * r, r), r), :]
            return pltpu.make_async_remote_copy(
                src_ref=src, dst_ref=got[a].at[q],
                send_sem=send_sems.at[a * N_CHIP + q], recv_sem=recv_sems.at[a * N_CHIP + q],
                device_id=sibling, device_id_type=MESH)

        sends = [remote(a, q) for a in range(n) for q in range(N_CHIP)]
        for cp in sends:
            cp.start()
        for cp in sends:
            cp.wait_recv()
        for cp in sends:
            cp.wait_send()

    return pl.pallas_call(
        body, name="grad_pair_exchange",
        in_specs=[HBM] * n, out_specs=[HBM] * n,
        out_shape=[jax.ShapeDtypeStruct((N_CHIP, g.shape[0], g.shape[1] // N_DEV, g.shape[2]), g.dtype) for g in grads],
        scratch_shapes=[pltpu.SemaphoreType.DMA((N_CHIP * n,)), pltpu.SemaphoreType.DMA((N_CHIP * n,))],
    )(*grads)


def _chip_exchange(parts):
    n = len(parts)

    def body(*refs):
        ins, outs = refs[:n], refs[n:2 * n]
        send_sems, recv_sems = refs[2 * n:]
        x, y, c = _coords()
        chips = _other_chips(x, y)

        def remote(a, j):
            return pltpu.make_async_remote_copy(
                src_ref=ins[a].at[j], dst_ref=outs[a].at[j],
                send_sem=send_sems.at[a * 3 + j], recv_sem=recv_sems.at[a * 3 + j],
                device_id=(*chips[j], c), device_id_type=MESH)

        sends = [remote(a, j) for a in range(n) for j in range(3)]
        for cp in sends:
            cp.start()
        for cp in sends:
            cp.wait_recv()
        for cp in sends:
            cp.wait_send()

    return pl.pallas_call(
        body, name="grad_chip_exchange",
        in_specs=[HBM] * n, out_specs=[HBM] * n,
        out_shape=[jax.ShapeDtypeStruct(p.shape, p.dtype) for p in parts],
        scratch_shapes=[pltpu.SemaphoreType.DMA((3 * n,)), pltpu.SemaphoreType.DMA((3 * n,))],
    )(*parts)


def _all_reduce_small(pack):
    R = pack.shape[0]

    def body(p_ref, tot_ref, all_ref, send_sems, recv_sems):
        x, y, c = _coords()
        me = 4 * x + 2 * y + c
        all_ref[me] = p_ref[...]
        peers = []
        for k in range(1, N_DEV):
            bx, by, bc = (k >> 2) & 1, (k >> 1) & 1, k & 1
            peers.append((x ^ bx, y ^ by, c ^ bc))

        def copy(k, slot, to):
            return pltpu.make_async_remote_copy(
                src_ref=p_ref, dst_ref=all_ref.at[slot], send_sem=send_sems.at[k], recv_sem=recv_sems.at[k],
                device_id=to, device_id_type=MESH)

        sends = [copy(k, me, peer) for k, peer in enumerate(peers)]
        for cp in sends:
            cp.start()
        for k, peer in enumerate(peers):
            copy(k, 4 * peer[0] + 2 * peer[1] + peer[2], peer).wait_recv()
        for cp in sends:
            cp.wait_send()
        tot = all_ref[0]
        for d in range(1, N_DEV):
            tot = tot + all_ref[d]
        tot_ref[...] = tot

    vmem = pl.BlockSpec(memory_space=pltpu.VMEM)
    return pl.pallas_call(
        body, name="all_reduce_small",
        in_specs=[vmem], out_specs=vmem,
        out_shape=jax.ShapeDtypeStruct((R, 128), F32),
        scratch_shapes=[pltpu.VMEM((N_DEV, R, 128), F32), pltpu.SemaphoreType.DMA((N_DEV - 1,)),
                        pltpu.SemaphoreType.DMA((N_DEV - 1,))],
    )(pack)


def _tile_rows(n, cap=512):
    t = min(n, cap)
    while n % t or t % 8:
        t -= 1
        if t < 8:
            return n
    return t


def _pair_sum(g, got, owner_dev, owner_chip, dtype):
    s, r8, c = g.shape
    r = r8 // N_DEV
    n = owner_dev.shape[0]

    def body(dev_ref, chip_ref, g_ref, got_ref, o_ref):
        o_ref[...] = (g_ref[...] + got_ref[...]).astype(dtype)

    return pl.pallas_call(
        body, name="pair_sum",
        grid_spec=pltpu.PrefetchScalarGridSpec(
            num_scalar_prefetch=2, grid=(n, s),
            in_specs=[pl.BlockSpec((None, r, c), lambda j, i, dev, chip: (i, dev[j], 0)),
                      pl.BlockSpec((None, None, r, c), lambda j, i, dev, chip: (chip[j], i, 0, 0))],
            out_specs=pl.BlockSpec((None, None, r, c), lambda j, i, dev, chip: (j, i, 0, 0))),
        out_shape=jax.ShapeDtypeStruct((n, s, r, c), dtype),
        compiler_params=_cp("parallel", "parallel"),
    )(owner_dev, owner_chip, g, got)


def _sum_chips(own, parts):
    _, s, r, c = parts.shape

    def body(o, p0, p1, p2, o_ref):
        o_ref[...] = ((o[...] + p0[...].astype(F32)) + p1[...].astype(F32)) + p2[...].astype(F32)

    def spec(q):
        return pl.BlockSpec((None, None, r, c), lambda i: (q, i, 0, 0))

    return pl.pallas_call(
        body, name="chip_sum", grid=(s,),
        in_specs=[spec(0), spec(0), spec(1), spec(2)], out_specs=pl.BlockSpec((None, r, c), lambda i: (i, 0, 0)),
        out_shape=jax.ShapeDtypeStruct((s, r, c), F32),
        compiler_params=_cp("parallel"),
    )(own, parts, parts, parts)


def _adamw(w, g, m, v):
    shape = w.shape
    c = shape[-1] if w.ndim > 1 else w.shape[0]
    args = [t.reshape(-1, c) for t in (w, g, m, v)]
    n = args[0].shape[0]
    tr = _tile_rows(n)

    def body(w_ref, g_ref, m_ref, v_ref, d_ref, mo_ref, vo_ref):
        g = g_ref[...]
        m = B1 * m_ref[...] + (1.0 - B1) * g
        v = B2 * v_ref[...] + (1.0 - B2) * jnp.square(g)
        m_hat = m / (1.0 - B1 ** STEP)
        v_hat = v / (1.0 - B2 ** STEP)
        d_ref[...] = -LR * (m_hat / (jnp.sqrt(v_hat) + ADAM_EPS) + WD * w_ref[...])
        mo_ref[...] = m
        vo_ref[...] = v

    outs = pl.pallas_call(
        body, name="adamw", grid=(n // tr,),
        in_specs=[_row(tr, c)] * 4, out_specs=[_row(tr, c)] * 3,
        out_shape=[jax.ShapeDtypeStruct((n, c), F32)] * 3,
        compiler_params=_cp("parallel"),
    )(*args)
    return [o.reshape(shape) for o in outs]


def _ffn_slab(ffn, which, layer):
    return (ffn * 3 + which) * 2 + layer


def _pack(pieces):
    flat = []
    for p in pieces:
        f = p.reshape(-1)
        flat.append(jnp.pad(f, (0, (-f.shape[0]) % 1024)))
    return jnp.concatenate(flat).reshape(-1, 128)


def _unpack(pack, shapes):
    flat = pack.reshape(-1)
    out, off = [], 0
    for s in shapes:
        size = 1
        for d in s:
            size *= d
        out.append(flat[off:off + size].reshape(s))
        off += size + (-size) % 1024
    return out


def kernel(x, positions, ffn1_norm, ffn1_w_gate, ffn1_w_up, ffn1_w_down, mix_norm, w_in, conv_w, conv_b, conv_ln_g, conv_ln_b, attn_sinks, w_out, ffn2_norm, ffn2_w_gate, ffn2_w_up, ffn2_w_down, final_norm, loss_target, m_ffn1_norm, m_ffn1_w_gate, m_ffn1_w_up, m_ffn1_w_down, m_mix_norm, m_w_in, m_conv_w, m_conv_b, m_conv_ln_g, m_conv_ln_b, m_attn_sinks, m_w_out, m_ffn2_norm, m_ffn2_w_gate, m_ffn2_w_up, m_ffn2_w_down, m_final_norm, v_ffn1_norm, v_ffn1_w_gate, v_ffn1_w_up, v_ffn1_w_down, v_mix_norm, v_w_in, v_conv_w, v_conv_b, v_conv_ln_g, v_conv_ln_b, v_attn_sinks, v_w_out, v_ffn2_norm, v_ffn2_w_gate, v_ffn2_w_up, v_ffn2_w_down, v_final_norm):
    L = ffn1_norm.shape[0]
    T = x.shape[1]
    x0 = x.reshape(T, D)
    target = loss_target.reshape(T, D)
    dev = 4 * lax.axis_index("x") + 2 * lax.axis_index("y") + lax.axis_index("c")

    def t_(w):
        return jnp.swapaxes(w, 1, 2)

    ffn_sh = jnp.stack([t_(ffn1_w_gate), t_(ffn1_w_up), ffn1_w_down,
                        t_(ffn2_w_gate), t_(ffn2_w_up), ffn2_w_down]).astype(BF16)
    ffn_sh = ffn_sh.reshape(6 * L, FF // N_DEV, D)
    win_sh = t_(w_in).astype(BF16)
    wout_sh = w_out.astype(BF16)
    cw_cols = CC // N_DEV
    cw_sh = jnp.pad(conv_w.reshape(-1), (0, (-L * CW * cw_cols) % 1024)).reshape(1, -1, 128)

    wffn, win, wout, cw_all = _all_gather([ffn_sh, win_sh, wout_sh, cw_sh])
    cw_rows = cw_sh.shape[1]
    cw_full = cw_all.reshape(N_DEV, cw_rows * 128)[:, :L * CW * cw_cols].reshape(N_DEV, L, CW, cw_cols)
    cw_full = jnp.transpose(cw_full, (1, 2, 0, 3)).reshape(L, CW, CC)

    inv_freq = 1.0 / (10000.0 ** (jnp.arange(0, HD, 2, dtype=F32) / HD))
    ang = positions.reshape(T).astype(F32)[:, None] * inv_freq
    cos, sin = jnp.cos(ang), jnp.sin(ang)
    rc = jnp.concatenate([cos, cos, cos, cos], axis=1)
    rs = jnp.concatenate([-sin, sin, -sin, sin], axis=1)

    n1 = ffn1_norm.reshape(L, 1, D)
    nm = mix_norm.reshape(L, 1, D)
    n2 = ffn2_norm.reshape(L, 1, D)
    cb = conv_b.reshape(L, 1, CC)
    lg = conv_ln_g.reshape(L, 1, CC)
    lb = conv_ln_b.reshape(L, 1, CC)

    saved = []
    xa = x0
    for l in range(L):
        h1, G1, U1, A1 = _ffn_up(xa, n1, wffn, _ffn_slab(0, 0, l), _ffn_slab(0, 1, l), l)
        xb = _ffn_down(A1, xa, wffn, _ffn_slab(0, 2, l))
        hm, qkv, u = _mix_in(xb, nm, win, rc, rs, l)
        ao = _attn_fwd(qkv, attn_sinks, l)
        y, co = _conv_fwd(u, cw_full, cb, lg, lb, l)
        xc, cat = _mix_out(ao, co, xb, wout, l)
        h2, G2, U2, A2 = _ffn_up(xc, n2, wffn, _ffn_slab(1, 0, l), _ffn_slab(1, 1, l), l)
        xd = _ffn_down(A2, xc, wffn, _ffn_slab(1, 2, l))
        saved.append((xa, h1, G1, U1, A1, xb, hm, qkv, u, y, cat, xc, h2, G2, U2, A2))
        xa = xd

    dx, loss_part, g_final = _final(xa, final_norm.reshape(1, D), target)

    gffn = lax.empty((6 * L, FF, D), F32)
    gin = lax.empty((L, DIN, D), F32)
    gout = lax.empty((L, D, D), F32)
    g_n1, g_nm, g_n2 = [None] * L, [None] * L, [None] * L
    g_cb, g_lg, g_lb, g_sink, g_cw = [None] * L, [None] * L, [None] * L, [None] * L, [None] * L
    for l in reversed(range(L)):
        xa, h1, G1, U1, A1, xb, hm, qkv, u, y, cat, xc, h2, G2, U2, A2 = saved[l]
        d, dG, dU = _ffn_bwd_act(dx, G2, U2, wffn, _ffn_slab(1, 2, l))
        gffn = _wgrad(gffn, _ffn_slab(1, 2, l), A2, d)
        gffn = _wgrad(gffn, _ffn_slab(1, 0, l), dG, h2)
        gffn = _wgrad(gffn, _ffn_slab(1, 1, l), dU, h2)
        dx, g_n2[l] = _ffn_bwd_in(dG, dU, xc, dx, n2, wffn, _ffn_slab(1, 0, l), _ffn_slab(1, 1, l), l)
        d, dao, dco = _mix_out_bwd(dx, wout, l)
        gout = _wgrad(gout, l, cat, d)
        dy, g_lg[l], g_lb[l], g_cb[l] = _conv_bwd_norm(dco, y, lg, lb, l)
        du, g_cw[l] = _conv_bwd_taps(dy, u, cw_full, l)
        dq, dk, dv, g_sink[l] = _attn_bwd(qkv, dao, attn_sinks, l)
        dp, dx, g_nm[l] = _mix_in_bwd(dq, dk, dv, du, rc, rs, xb, dx, nm, win, l)
        gin = _wgrad(gin, l, dp, hm)
        d, dG, dU = _ffn_bwd_act(dx, G1, U1, wffn, _ffn_slab(0, 2, l))
        gffn = _wgrad(gffn, _ffn_slab(0, 2, l), A1, d)
        gffn = _wgrad(gffn, _ffn_slab(0, 0, l), dG, h1)
        gffn = _wgrad(gffn, _ffn_slab(0, 1, l), dU, h1)
        dx, g_n1[l] = _ffn_bwd_in(dG, dU, xa, dx, n1, wffn, _ffn_slab(0, 0, l), _ffn_slab(0, 1, l), l)

    grad_x = dx.reshape(1, T, D)

    big_grads = [gffn, gin, gout]
    got = _pair_exchange(big_grads)
    cx, cy, cc = _coords()
    chip_of = [2 * cx + cy] + [2 * px + py for px, py in _other_chips(cx, cy)]
    own_chip = jnp.stack(chip_of[:1]).astype(jnp.int32)
    other_chips = jnp.stack(chip_of[1:]).astype(jnp.int32)
    own = [_pair_sum(g, r, 2 * own_chip + cc, own_chip, F32) for g, r in zip(big_grads, got)]
    sent = [_pair_sum(g, r, 2 * other_chips + cc, other_chips, BF16) for g, r in zip(big_grads, got)]
    gffn_sh, gin_sh, gout_sh = [_sum_chips(o, p) for o, p in zip(own, _chip_exchange(sent))]

    small = [loss_part,
             jnp.concatenate(g_n1), jnp.concatenate(g_nm), jnp.concatenate(g_n2), g_final,
             jnp.concatenate(g_cb), jnp.concatenate(g_lg), jnp.concatenate(g_lb),
             jnp.stack(g_sink)[:, :, 0], jnp.stack(g_cw)]
    small_shapes = [(1, 128), (L, D), (L, D), (L, D), (D,), (L, CC), (L, CC), (L, CC), (L, NH), (L, CW, CC)]
    tot = _unpack(_all_reduce_small(_pack(small)), small_shapes)
    loss = tot[0][0, 0]
    gr_n1, gr_nm, gr_n2, gr_final, gr_cb, gr_lg, gr_lb, gr_sink, gr_cw_full = tot[1:]
    gr_cw = lax.dynamic_slice_in_dim(gr_cw_full, dev * cw_cols, cw_cols, axis=2)

    gf = gffn_sh.reshape(2, 3, L, FF // N_DEV, D)
    grads = {
        "ffn1_norm": gr_n1, "ffn1_w_gate": t_(gf[0, 0]), "ffn1_w_up": t_(gf[0, 1]), "ffn1_w_down": gf[0, 2],
        "mix_norm": gr_nm, "w_in": t_(gin_sh), "conv_w": gr_cw, "conv_b": gr_cb, "conv_ln_g": gr_lg,
        "conv_ln_b": gr_lb, "attn_sinks": gr_sink, "w_out": gout_sh,
        "ffn2_norm": gr_n2, "ffn2_w_gate": t_(gf[1, 0]), "ffn2_w_up": t_(gf[1, 1]), "ffn2_w_down": gf[1, 2],
        "final_norm": gr_final,
    }
    weights = dict(ffn1_norm=ffn1_norm, ffn1_w_gate=ffn1_w_gate, ffn1_w_up=ffn1_w_up, ffn1_w_down=ffn1_w_down, mix_norm=mix_norm, w_in=w_in, conv_w=conv_w, conv_b=conv_b, conv_ln_g=conv_ln_g, conv_ln_b=conv_ln_b, attn_sinks=attn_sinks, w_out=w_out, ffn2_norm=ffn2_norm, ffn2_w_gate=ffn2_w_gate, ffn2_w_up=ffn2_w_up, ffn2_w_down=ffn2_w_down, final_norm=final_norm)
    moms = dict(ffn1_norm=m_ffn1_norm, ffn1_w_gate=m_ffn1_w_gate, ffn1_w_up=m_ffn1_w_up, ffn1_w_down=m_ffn1_w_down, mix_norm=m_mix_norm, w_in=m_w_in, conv_w=m_conv_w, conv_b=m_conv_b, conv_ln_g=m_conv_ln_g, conv_ln_b=m_conv_ln_b, attn_sinks=m_attn_sinks, w_out=m_w_out, ffn2_norm=m_ffn2_norm, ffn2_w_gate=m_ffn2_w_gate, ffn2_w_up=m_ffn2_w_up, ffn2_w_down=m_ffn2_w_down, final_norm=m_final_norm)
    vels = dict(ffn1_norm=v_ffn1_norm, ffn1_w_gate=v_ffn1_w_gate, ffn1_w_up=v_ffn1_w_up, ffn1_w_down=v_ffn1_w_down, mix_norm=v_mix_norm, w_in=v_w_in, conv_w=v_conv_w, conv_b=v_conv_b, conv_ln_g=v_conv_ln_g, conv_ln_b=v_conv_ln_b, attn_sinks=v_attn_sinks, w_out=v_w_out, ffn2_norm=v_ffn2_norm, ffn2_w_gate=v_ffn2_w_gate, ffn2_w_up=v_ffn2_w_up, ffn2_w_down=v_ffn2_w_down, final_norm=v_final_norm)

    names = list(weights)
    big = ("ffn1_w_gate", "ffn1_w_up", "ffn1_w_down", "w_in", "w_out", "ffn2_w_gate", "ffn2_w_up", "ffn2_w_down")
    delta, new_m, new_v = {}, {}, {}
    for k in big:
        delta[k], new_m[k], new_v[k] = _adamw(weights[k], grads[k], moms[k], vels[k])
    rest = [k for k in names if k not in big]
    rest_shapes = [weights[k].shape for k in rest]
    packed = _adamw(*[_pack([t[k] for k in rest]) for t in (weights, grads, moms, vels)])
    for res, packed_out in zip((delta, new_m, new_v), packed):
        for k, val in zip(rest, _unpack(packed_out, rest_shapes)):
            res[k] = val

    return (loss, grad_x, *[grads[k] for k in names], *[delta[k] for k in names],
            *[new_m[k] for k in names], *[new_v[k] for k in names])
```

```python
import functools

import jax
import jax.numpy as jnp
from jax import lax
from jax.experimental import pallas as pl
from jax.experimental.pallas import tpu as pltpu

F32 = jnp.float32
BF16 = jnp.bfloat16
MESH = pl.DeviceIdType.MESH

N_DEV = 8
N_CHIP = 4
D = 1024
FF = 2816
HD = 64
NH = 8
NKV = 2
GROUP = NH // NKV
AW = NH * HD
KVW = NKV * HD
QKW = AW + KVW
QKVW = AW + 2 * KVW
CC = 512
CW = 31
DIN = QKVW + 2 * CC
BLK = 128
HALO = 32
EPS = 1e-5
SCALE = HD ** -0.5
NEG = float(jnp.finfo(jnp.float32).min)

LR, B1, B2, ADAM_EPS, WD, STEP = 0.001, 0.9, 0.999, 1e-08, 0.01, 10

TM = 256
TK = 512
FC = 256
ATT_BLOCKS = 4
VMEM_LIMIT = 56 * 1024 * 1024


def _cp(*sem):
    return pltpu.CompilerParams(dimension_semantics=sem, vmem_limit_bytes=VMEM_LIMIT)


def _row(tm, c):
    return pl.BlockSpec((tm, c), lambda i: (i, 0))


def _slab(shape, k, single=False):
    zeros = (0,) * len(shape)
    kw = dict(pipeline_mode=pl.Buffered(1)) if single else {}
    return pl.BlockSpec((None, *shape), lambda i: (k, *zeros), **kw)


def _acc(shape):
    return pl.BlockSpec(shape, lambda i: (0,) * len(shape))


def _nt(a, b):
    return lax.dot_general(a, b, (((1,), (1,)), ((), ())), preferred_element_type=F32)


def _tn(a, b):
    return lax.dot_general(a, b, (((0,), (0,)), ((), ())), preferred_element_type=F32)


def _nn(a, b):
    return jnp.dot(a, b, preferred_element_type=F32)


def _sigmoid(x):
    return jax.nn.sigmoid(x)


def _dsilu(z):
    s = _sigmoid(z)
    return s * (1.0 + z * (1.0 - s))


def _rms(x, g):
    r = lax.rsqrt(jnp.mean(x * x, axis=-1, keepdims=True) + EPS)
    xh = x * r
    return xh, r, xh * g


def _rms_bwd(dh, xh, r, g):
    dxh = dh * g
    return r * (dxh - xh * jnp.mean(dxh * xh, axis=-1, keepdims=True))


def _rope(t, c128, s128):
    w = t.shape[1]
    lane = lax.broadcasted_iota(jnp.int32, t.shape, 1)
    rot = jnp.where(lane % HD < HD // 2, pltpu.roll(t, w - HD // 2, 1), pltpu.roll(t, HD // 2, 1))
    return t * jnp.tile(c128, (1, w // 128)) + rot * jnp.tile(s128, (1, w // 128))


def _ffn_up(x, norm, wffn, sg, su, layer):
    T = x.shape[0]

    def body(x_ref, g_ref, wg_ref, wu_ref, h_ref, G_ref, U_ref, A_ref):
        _, _, hn = _rms(x_ref[...], g_ref[...])
        h = hn.astype(BF16)
        h_ref[...] = h
        for c in range(FF // FC):
            sl = slice(c * FC, (c + 1) * FC)
            g = _nt(h, wg_ref[sl, :])
            u = _nt(h, wu_ref[sl, :])
            G_ref[:, sl] = g.astype(BF16)
            U_ref[:, sl] = u.astype(BF16)
            A_ref[:, sl] = (g * _sigmoid(g) * u).astype(BF16)

    return pl.pallas_call(
        body, name="ffn_up", grid=(T // TM,),
        in_specs=[_row(TM, D), _slab((1, D), layer), _slab((FF, D), sg, True), _slab((FF, D), su, True)],
        out_specs=[_row(TM, D), _row(TM, FF), _row(TM, FF), _row(TM, FF)],
        out_shape=[jax.ShapeDtypeStruct((T, D), BF16)] + [jax.ShapeDtypeStruct((T, FF), BF16)] * 3,
        compiler_params=_cp("parallel"),
    )(x, norm, wffn, wffn)


def _ffn_down(a, x, wffn, sd):
    T = x.shape[0]

    def body(a_ref, x_ref, w_ref, o_ref):
        o_ref[...] = x_ref[...] + 0.5 * _nn(a_ref[...], w_ref[...])

    return pl.pallas_call(
        body, name="ffn_down", grid=(T // TM,),
        in_specs=[_row(TM, FF), _row(TM, D), _slab((FF, D), sd, True)],
        out_specs=_row(TM, D),
        out_shape=jax.ShapeDtypeStruct((T, D), F32),
        compiler_params=_cp("parallel"),
    )(a, x, wffn)


def _mix_in(x, norm, win, rc, rs, layer):
    T = x.shape[0]

    def body(x_ref, g_ref, w_ref, c_ref, s_ref, h_ref, qkv_ref, u_ref):
        _, _, hn = _rms(x_ref[...], g_ref[...])
        h = hn.astype(BF16)
        h_ref[...] = h
        qk = _nt(h, w_ref[0:QKW, :])
        qkv_ref[:, 0:QKW] = _rope(qk, c_ref[...], s_ref[...]).astype(BF16)
        qkv_ref[:, QKW:QKVW] = _nt(h, w_ref[QKW:QKVW, :]).astype(BF16)
        for c in range(2 * CC // FC):
            u_ref[:, c * FC:(c + 1) * FC] = _nt(h, w_ref[QKVW + c * FC:QKVW + (c + 1) * FC, :]).astype(BF16)

    return pl.pallas_call(
        body, name="mix_in", grid=(T // TM,),
        in_specs=[_row(TM, D), _slab((1, D), layer), _slab((DIN, D), 0, True), _row(TM, 128), _row(TM, 128)],
        out_specs=[_row(TM, D), _row(TM, QKVW), _row(TM, 2 * CC)],
        out_shape=[jax.ShapeDtypeStruct((T, D), BF16), jax.ShapeDtypeStruct((T, QKVW), BF16),
                   jax.ShapeDtypeStruct((T, 2 * CC), BF16)],
        compiler_params=_cp("parallel"),
    )(x, norm, win, rc, rs)


def _band_mask(has_prev):
    j = lax.broadcasted_iota(jnp.int32, (2 * BLK, BLK), 0)
    r = lax.broadcasted_iota(jnp.int32, (2 * BLK, BLK), 1) + BLK
    rel = r - j
    return jnp.tile((rel >= 0) & (rel < BLK) & (has_prev | (j >= BLK)), (1, GROUP))


def _band(prev_ref, cur_ref, b, col):
    if b == 0:
        return jnp.concatenate([prev_ref[:, col:col + HD], cur_ref[0:BLK, col:col + HD]], axis=0)
    return cur_ref[(b - 1) * BLK:(b + 1) * BLK, col:col + HD]


def _stack_heads(ref, b, kv):
    cols = [(kv * GROUP + g) * HD for g in range(GROUP)]
    return jnp.concatenate([ref[b * BLK:(b + 1) * BLK, c:c + HD] for c in cols], axis=0)


def _unstack_t(xt):
    x = xt.T
    return jnp.concatenate([x[g * BLK:(g + 1) * BLK, :] for g in range(GROUP)], axis=1)


def _sink_row(sink_ref, layer, kv):
    return jnp.concatenate([jnp.full((1, BLK), sink_ref[layer, kv * GROUP + g], F32) for g in range(GROUP)], axis=1)


def _probs_t(q4, kb, mask, sink):
    s = jnp.where(mask, _nt(kb, q4) * SCALE, NEG)
    m = jnp.maximum(jnp.max(s, axis=0, keepdims=True), sink)
    p = jnp.exp(s - m)
    e = jnp.exp(sink - m)
    inv = 1.0 / (jnp.sum(p, axis=0, keepdims=True) + e)
    return p * inv, e * inv


def _attn_fwd(qkv, sinks, layer):
    T = qkv.shape[0]
    tq = ATT_BLOCKS * BLK

    def body(sink_ref, cur_ref, prev_ref, o_ref):
        first = _band_mask(pl.program_id(0) > 0)
        later = _band_mask(True)
        for b in range(ATT_BLOCKS):
            outs = []
            for kv in range(NKV):
                kb = _band(prev_ref, cur_ref, b, AW + kv * HD)
                vb = _band(prev_ref, cur_ref, b, QKW + kv * HD)
                pt, _ = _probs_t(_stack_heads(cur_ref, b, kv), kb, first if b == 0 else later,
                                 _sink_row(sink_ref, layer, kv))
                outs.append(_unstack_t(_nn(vb.T, pt.astype(BF16))))
            o_ref[b * BLK:(b + 1) * BLK, :] = jnp.concatenate(outs, axis=1).astype(BF16)

    return pl.pallas_call(
        body, name="attn_fwd", grid=(T // tq,),
        in_specs=[pl.BlockSpec(memory_space=pltpu.SMEM), _row(tq, QKVW),
                  pl.BlockSpec((BLK, QKVW), lambda i: (jnp.maximum(i * ATT_BLOCKS - 1, 0), 0))],
        out_specs=_row(tq, AW),
        out_shape=jax.ShapeDtypeStruct((T, AW), BF16),
        compiler_params=_cp("parallel"),
    )(sinks, qkv, qkv)


def _glu(u):
    u = u.astype(F32)
    return u[:, :CC] * _sigmoid(u[:, CC:])


def _conv_fwd(u, cw, cb, lg, lb, layer):
    T = u.shape[0]

    def body(u_ref, up_ref, w_ref, b_ref, g_ref, bb_ref, y_ref, o_ref, ext_ref):
        i = pl.program_id(0)
        ext_ref[0:HALO, :] = jnp.where(i > 0, _glu(up_ref[...]), 0.0)
        ext_ref[HALO:, :] = _glu(u_ref[...])
        acc = jnp.zeros((TM, CC), F32)
        for k in range(CW):
            acc = acc + w_ref[k:k + 1, :] * ext_ref[pl.ds(HALO - (CW - 1) + k, TM), :]
        y = acc + b_ref[...]
        y_ref[...] = y
        xc = y - jnp.mean(y, axis=-1, keepdims=True)
        z = xc * lax.rsqrt(jnp.mean(xc * xc, axis=-1, keepdims=True) + EPS) * g_ref[...] + bb_ref[...]
        o_ref[...] = (z * _sigmoid(z)).astype(BF16)

    return pl.pallas_call(
        body, name="conv_fwd", grid=(T // TM,),
        in_specs=[_row(TM, 2 * CC),
                  pl.BlockSpec((HALO, 2 * CC), lambda i: (jnp.maximum(i * (TM // HALO) - 1, 0), 0)),
                  _slab((CW, CC), layer), _slab((1, CC), layer), _slab((1, CC), layer), _slab((1, CC), layer)],
        out_specs=[_row(TM, CC), _row(TM, CC)],
        out_shape=[jax.ShapeDtypeStruct((T, CC), F32), jax.ShapeDtypeStruct((T, CC), BF16)],
        scratch_shapes=[pltpu.VMEM((TM + HALO, CC), F32)],
        compiler_params=_cp("parallel"),
    )(u, u, cw, cb, lg, lb)


def _mix_out(ao, co, x, wout):
    T = x.shape[0]

    def body(ao_ref, co_ref, x_ref, w_ref, o_ref, cat_ref):
        cat = jnp.concatenate([ao_ref[...], co_ref[...]], axis=1)
        cat_ref[...] = cat
        o_ref[...] = x_ref[...] + _nn(cat, w_ref[...])

    return pl.pallas_call(
        body, name="mix_out", grid=(T // TM,),
        in_specs=[_row(TM, AW), _row(TM, CC), _row(TM, D), _slab((D, D), 0, True)],
        out_specs=[_row(TM, D), _row(TM, D)],
        out_shape=[jax.ShapeDtypeStruct((T, D), F32), jax.ShapeDtypeStruct((T, D), BF16)],
        compiler_params=_cp("parallel"),
    )(ao, co, x, wout)


def _final(x, norm, target):
    T = x.shape[0]

    def body(x_ref, g_ref, t_ref, dx_ref, loss_ref, dg_ref):
        @pl.when(pl.program_id(0) == 0)
        def _():
            loss_ref[...] = jnp.zeros_like(loss_ref)
            dg_ref[...] = jnp.zeros_like(dg_ref)

        g = g_ref[...]
        xh, r, y = _rms(x_ref[...], g)
        err = y - t_ref[...]
        loss_ref[...] += jnp.full(loss_ref.shape, (0.5 / D) * jnp.sum(err * err), F32)
        dy = err * (1.0 / D)
        dg_ref[...] += jnp.sum(dy * xh, axis=0, keepdims=True)
        dx_ref[...] = _rms_bwd(dy, xh, r, g)

    return pl.pallas_call(
        body, name="final_loss", grid=(T // TM,),
        in_specs=[_row(TM, D), _acc((1, D)), _row(TM, D)],
        out_specs=[_row(TM, D), _acc((1, 128)), _acc((1, D))],
        out_shape=[jax.ShapeDtypeStruct((T, D), F32), jax.ShapeDtypeStruct((1, 128), F32),
                   jax.ShapeDtypeStruct((1, D), F32)],
        compiler_params=_cp("arbitrary"),
    )(x, norm, target)


def _ffn_bwd_act(dx, G, U, wffn, sd):
    T = dx.shape[0]

    def body(dx_ref, G_ref, U_ref, w_ref, d_ref, dG_ref, dU_ref):
        d = (0.5 * dx_ref[...]).astype(BF16)
        d_ref[...] = d
        for c in range(FF // FC):
            sl = slice(c * FC, (c + 1) * FC)
            da = _nt(d, w_ref[sl, :])
            g = G_ref[:, sl].astype(F32)
            u = U_ref[:, sl].astype(F32)
            dU_ref[:, sl] = (da * g * _sigmoid(g)).astype(BF16)
            dG_ref[:, sl] = (da * u * _dsilu(g)).astype(BF16)

    return pl.pallas_call(
        body, name="ffn_bwd_act", grid=(T // TM,),
        in_specs=[_row(TM, D), _row(TM, FF), _row(TM, FF), _slab((FF, D), sd, True)],
        out_specs=[_row(TM, D), _row(TM, FF), _row(TM, FF)],
        out_shape=[jax.ShapeDtypeStruct((T, D), BF16)] + [jax.ShapeDtypeStruct((T, FF), BF16)] * 2,
        compiler_params=_cp("parallel"),
    )(dx, G, U, wffn)


def _ffn_bwd_in(dG, dU, x, dx, norm, wffn, sg, su, layer):
    T = x.shape[0]

    def body(dG_ref, dU_ref, x_ref, dx_ref, g_ref, wg_ref, wu_ref, o_ref, dg_ref):
        @pl.when(pl.program_id(0) == 0)
        def _():
            dg_ref[...] = jnp.zeros_like(dg_ref)

        dh = _nn(dG_ref[...], wg_ref[...]) + _nn(dU_ref[...], wu_ref[...])
        g = g_ref[...]
        xh, r, _ = _rms(x_ref[...], g)
        dg_ref[...] += jnp.sum(dh * xh, axis=0, keepdims=True)
        o_ref[...] = dx_ref[...] + _rms_bwd(dh, xh, r, g)

    return pl.pallas_call(
        body, name="ffn_bwd_in", grid=(T // TM,),
        in_specs=[_row(TM, FF), _row(TM, FF), _row(TM, D), _row(TM, D), _slab((1, D), layer),
                  _slab((FF, D), sg, True), _slab((FF, D), su, True)],
        out_specs=[_row(TM, D), _acc((1, D))],
        out_shape=[jax.ShapeDtypeStruct((T, D), F32), jax.ShapeDtypeStruct((1, D), F32)],
        compiler_params=_cp("arbitrary"),
    )(dG, dU, x, dx, norm, wffn, wffn)


def _mix_out_bwd(dx, wout):
    T = dx.shape[0]

    def body(dx_ref, w_ref, d_ref, dao_ref, dco_ref):
        d = dx_ref[...].astype(BF16)
        d_ref[...] = d
        dcat = _nt(d, w_ref[...])
        dao_ref[...] = dcat[:, :AW].astype(BF16)
        dco_ref[...] = dcat[:, AW:].astype(BF16)

    return pl.pallas_call(
        body, name="mix_out_bwd", grid=(T // TM,),
        in_specs=[_row(TM, D), _slab((D, D), 0, True)],
        out_specs=[_row(TM, D), _row(TM, AW), _row(TM, CC)],
        out_shape=[jax.ShapeDtypeStruct((T, D), BF16), jax.ShapeDtypeStruct((T, AW), BF16),
                   jax.ShapeDtypeStruct((T, CC), BF16)],
        compiler_params=_cp("parallel"),
    )(dx, wout)


def _conv_bwd_norm(dco, y, lg, lb, layer):
    T = y.shape[0]

    def body(dco_ref, y_ref, g_ref, bb_ref, dy_ref, dlg_ref, dlb_ref, dcb_ref):
        @pl.when(pl.program_id(0) == 0)
        def _():
            dlg_ref[...] = jnp.zeros_like(dlg_ref)
            dlb_ref[...] = jnp.zeros_like(dlb_ref)
            dcb_ref[...] = jnp.zeros_like(dcb_ref)

        y = y_ref[...]
        g = g_ref[...]
        xc = y - jnp.mean(y, axis=-1, keepdims=True)
        rs = lax.rsqrt(jnp.mean(xc * xc, axis=-1, keepdims=True) + EPS)
        xn = xc * rs
        z = xn * g + bb_ref[...]
        dz = dco_ref[...].astype(F32) * _dsilu(z)
        dlg_ref[...] += jnp.sum(dz * xn, axis=0, keepdims=True)
        dlb_ref[...] += jnp.sum(dz, axis=0, keepdims=True)
        dxn = dz * g
        dy = rs * (dxn - jnp.mean(dxn, axis=-1, keepdims=True) - xn * jnp.mean(dxn * xn, axis=-1, keepdims=True))
        dcb_ref[...] += jnp.sum(dy, axis=0, keepdims=True)
        dy_ref[...] = dy

    return pl.pallas_call(
        body, name="conv_bwd_norm", grid=(T // TM,),
        in_specs=[_row(TM, CC), _row(TM, CC), _slab((1, CC), layer), _slab((1, CC), layer)],
        out_specs=[_row(TM, CC), _acc((1, CC)), _acc((1, CC)), _acc((1, CC))],
        out_shape=[jax.ShapeDtypeStruct((T, CC), F32)] + [jax.ShapeDtypeStruct((1, CC), F32)] * 3,
        compiler_params=_cp("arbitrary"),
    )(dco, y, lg, lb)


def _conv_bwd_taps(dy, u, cw, layer):
    T = u.shape[0]
    n_halo = T // HALO

    def body(dy_ref, dyn_ref, u_ref, up_ref, w_ref, du_ref, dw_ref, hext_ref, dext_ref):
        i = pl.program_id(0)

        @pl.when(i == 0)
        def _():
            dw_ref[...] = jnp.zeros_like(dw_ref)

        hext_ref[0:HALO, :] = jnp.where(i > 0, _glu(up_ref[...]), 0.0)
        hext_ref[HALO:, :] = _glu(u_ref[...])
        dy = dy_ref[...]
        dext_ref[0:TM, :] = dy
        dext_ref[TM:, :] = jnp.where(i < pl.num_programs(0) - 1, dyn_ref[...], 0.0)
        dh = jnp.zeros((TM, CC), F32)
        for k in range(CW):
            dh = dh + w_ref[k:k + 1, :] * dext_ref[pl.ds(CW - 1 - k, TM), :]
            dw_ref[k:k + 1, :] += jnp.sum(dy * hext_ref[pl.ds(HALO - (CW - 1) + k, TM), :], axis=0, keepdims=True)
        uu = u_ref[...].astype(F32)
        a = uu[:, :CC]
        sg = _sigmoid(uu[:, CC:])
        du_ref[:, :CC] = (dh * sg).astype(BF16)
        du_ref[:, CC:] = (dh * a * sg * (1.0 - sg)).astype(BF16)

    return pl.pallas_call(
        body, name="conv_bwd_taps", grid=(T // TM,),
        in_specs=[_row(TM, CC),
                  pl.BlockSpec((HALO, CC), lambda i: (jnp.minimum((i + 1) * (TM // HALO), n_halo - 1), 0)),
                  _row(TM, 2 * CC),
                  pl.BlockSpec((HALO, 2 * CC), lambda i: (jnp.maximum(i * (TM // HALO) - 1, 0), 0)),
                  _slab((CW, CC), layer)],
        out_specs=[_row(TM, 2 * CC), _acc((CW, CC))],
        out_shape=[jax.ShapeDtypeStruct((T, 2 * CC), BF16), jax.ShapeDtypeStruct((CW, CC), F32)],
        scratch_shapes=[pltpu.VMEM((TM + HALO, CC), F32), pltpu.VMEM((TM + HALO, CC), F32)],
        compiler_params=_cp("arbitrary"),
    )(dy, dy, u, u, cw)


def _attn_bwd(qkv, dao, sinks, layer):
    T = qkv.shape[0]
    tq = ATT_BLOCKS * BLK

    def body(sink_ref, cur_ref, prev_ref, do_ref, dq_ref, dk_ref, dv_ref, ds_ref):
        i = pl.program_id(0)

        @pl.when(i == 0)
        def _():
            dk_ref[...] = jnp.zeros_like(dk_ref)
            dv_ref[...] = jnp.zeros_like(dv_ref)
            ds_ref[...] = jnp.zeros_like(ds_ref)

        first = _band_mask(i > 0)
        later = _band_mask(True)
        base = pl.multiple_of(i * tq, tq)
        before = pl.multiple_of(jnp.maximum(i * ATT_BLOCKS - 1, 0) * BLK, BLK)
        for b in range(ATT_BLOCKS):
            dqs, dks, dvs = [], [], []
            for kv in range(NKV):
                kb = _band(prev_ref, cur_ref, b, AW + kv * HD)
                vb = _band(prev_ref, cur_ref, b, QKW + kv * HD)
                q4 = _stack_heads(cur_ref, b, kv)
                do4 = _stack_heads(do_ref, b, kv)
                pt, psink = _probs_t(q4, kb, first if b == 0 else later, _sink_row(sink_ref, layer, kv))
                dpt = _nt(vb, do4)
                dd = jnp.sum(pt * dpt, axis=0, keepdims=True)
                dst = (pt * (dpt - dd) * SCALE).astype(BF16)
                sd = psink * dd
                for g in range(GROUP):
                    hh = kv * GROUP + g
                    ds_ref[hh:hh + 1, :] += jnp.full((1, 128), -jnp.sum(sd[:, g * BLK:(g + 1) * BLK]), F32)
                dqs.append(_unstack_t(_nn(kb.T, dst)))
                dks.append(_nn(dst, q4))
                dvs.append(_nn(pt.astype(BF16), do4))
            dq_ref[b * BLK:(b + 1) * BLK, :] = jnp.concatenate(dqs, axis=1).astype(BF16)
            dkband = jnp.concatenate(dks, axis=1)
            dvband = jnp.concatenate(dvs, axis=1)
            if b == 0:
                dk_ref[pl.ds(before, BLK), :] += dkband[:BLK]
                dv_ref[pl.ds(before, BLK), :] += dvband[:BLK]
                dk_ref[pl.ds(base, BLK), :] += dkband[BLK:]
                dv_ref[pl.ds(base, BLK), :] += dvband[BLK:]
            else:
                r0 = pl.multiple_of(base + (b - 1) * BLK, BLK)
                dk_ref[pl.ds(r0, 2 * BLK), :] += dkband
                dv_ref[pl.ds(r0, 2 * BLK), :] += dvband

    return pl.pallas_call(
        body, name="attn_bwd", grid=(T // tq,),
        in_specs=[pl.BlockSpec(memory_space=pltpu.SMEM), _row(tq, QKVW),
                  pl.BlockSpec((BLK, QKVW), lambda i: (jnp.maximum(i * ATT_BLOCKS - 1, 0), 0)), _row(tq, AW)],
        out_specs=[_row(tq, AW), _acc((T, KVW)), _acc((T, KVW)), _acc((NH, 128))],
        out_shape=[jax.ShapeDtypeStruct((T, AW), BF16), jax.ShapeDtypeStruct((T, KVW), F32),
                   jax.ShapeDtypeStruct((T, KVW), F32), jax.ShapeDtypeStruct((NH, 128), F32)],
        compiler_params=_cp("arbitrary"),
    )(sinks, qkv, qkv, dao)


def _mix_in_bwd(dq, dk, dv, du, rc, rs, x, dx, norm, win, layer):
    T = x.shape[0]

    def body(dq_ref, dk_ref, dv_ref, du_ref, c_ref, s_ref, x_ref, dx_ref, g_ref, w_ref, dp_ref, o_ref, dg_ref):
        @pl.when(pl.program_id(0) == 0)
        def _():
            dg_ref[...] = jnp.zeros_like(dg_ref)

        dqk = jnp.concatenate([dq_ref[...].astype(F32), dk_ref[...]], axis=1)
        dqk = _rope(dqk, c_ref[...], -s_ref[...])
        dp = jnp.concatenate([dqk.astype(BF16), dv_ref[...].astype(BF16), du_ref[...]], axis=1)
        dp_ref[...] = dp
        dh = _nn(dp, w_ref[...])
        g = g_ref[...]
        xh, r, _ = _rms(x_ref[...], g)
        dg_ref[...] += jnp.sum(dh * xh, axis=0, keepdims=True)
        o_ref[...] = dx_ref[...] + _rms_bwd(dh, xh, r, g)

    return pl.pallas_call(
        body, name="mix_in_bwd", grid=(T // TM,),
        in_specs=[_row(TM, AW), _row(TM, KVW), _row(TM, KVW), _row(TM, 2 * CC), _row(TM, 128), _row(TM, 128),
                  _row(TM, D), _row(TM, D), _slab((1, D), layer), _slab((DIN, D), 0, True)],
        out_specs=[_row(TM, DIN), _row(TM, D), _acc((1, D))],
        out_shape=[jax.ShapeDtypeStruct((T, DIN), BF16), jax.ShapeDtypeStruct((T, D), F32),
                   jax.ShapeDtypeStruct((1, D), F32)],
        compiler_params=_cp("arbitrary"),
    )(dq, dk, dv, du, rc, rs, x, dx, norm, win)


def _wgrad(buf, slab, a, b):
    T, M = a.shape
    N = b.shape[1]
    tmm = M // 2 if M > 1024 else M

    def body(buf_ref, a_ref, b_ref, o_ref):
        @pl.when(pl.program_id(1) == 0)
        def _():
            o_ref[...] = jnp.zeros_like(o_ref)

        o_ref[...] += _tn(a_ref[...], b_ref[...])

    return pl.pallas_call(
        body, name="wgrad", grid=(M // tmm, T // TK),
        in_specs=[pl.BlockSpec(memory_space=pl.ANY),
                  pl.BlockSpec((TK, tmm), lambda i, k: (k, i)), pl.BlockSpec((TK, N), lambda i, k: (k, 0))],
        out_specs=pl.BlockSpec((None, tmm, N), lambda i, k: (slab, i, 0)),
        out_shape=jax.ShapeDtypeStruct(buf.shape, F32),
        input_output_aliases={0: 0},
        compiler_params=_cp("parallel", "arbitrary"),
    )(buf, a, b)


HBM = pl.BlockSpec(memory_space=pl.ANY)


def _coords():
    return lax.axis_index("x"), lax.axis_index("y"), lax.axis_index("c")


def _other_chips(x, y):
    return [(1 - x, y), (x, 1 - y), (1 - x, 1 - y)]


def _all_gather(shards):
    n = len(shards)

    def body(*refs):
        ins, outs = refs[:n], refs[n:2 * n]
        send_sems, recv_sems, local_sems = refs[2 * n:]
        x, y, c = _coords()
        me, sibling = (x, y, c), (x, y, 1 - c)
        chips = _other_chips(x, y)

        def rows(a, dev):
            r = ins[a].shape[1]
            return outs[a].at[:, pl.ds(pl.multiple_of((4 * dev[0] + 2 * dev[1] + dev[2]) * r, r), r), :]

        def copy(a, k, block, to, src=None):
            return pltpu.make_async_remote_copy(
                src_ref=rows(a, block) if src is None else src, dst_ref=rows(a, block),
                send_sem=send_sems.at[a * 7 + k], recv_sem=recv_sems.at[a * 7 + k],
                device_id=to, device_id_type=MESH)

        mine = [pltpu.make_async_copy(ins[a], rows(a, me), local_sems.at[a]) for a in range(n)]
        for cp in mine:
            cp.start()
        first = []
        for a in range(n):
            first.append(copy(a, 0, me, sibling, src=ins[a]))
            first += [copy(a, 1 + j, me, (*chip, c), src=ins[a]) for j, chip in enumerate(chips)]
        for cp in first:
            cp.start()
        passed = []
        for j, chip in enumerate(chips):
            for a in range(n):
                copy(a, 1 + j, (*chip, c), me).wait_recv()
                fwd = copy(a, 4 + j, (*chip, c), sibling)
                fwd.start()
                passed.append(fwd)
        for a in range(n):
            copy(a, 0, sibling, me).wait_recv()
            for j, chip in enumerate(chips):
                copy(a, 4 + j, (*chip, 1 - c), me).wait_recv()
        for cp in first + passed:
            cp.wait_send()
        for cp in mine:
            cp.wait()

    return pl.pallas_call(
        body, name="all_gather_weights",
        in_specs=[HBM] * n, out_specs=[HBM] * n,
        out_shape=[jax.ShapeDtypeStruct((s.shape[0], N_DEV * s.shape[1], s.shape[2]), s.dtype) for s in shards],
        scratch_shapes=[pltpu.SemaphoreType.DMA((7 * n,)), pltpu.SemaphoreType.DMA((7 * n,)),
                        pltpu.SemaphoreType.DMA((n,))],
    )(*shards)


def _pair_exchange(grads):
    n = len(grads)

    def body(*refs):
        ins, got = refs[:n], refs[n:2 * n]
        send_sems, recv_sems = refs[2 * n:]
        x, y, c = _coords()
        sibling = (x, y, 1 - c)

        def remote(a, q):
            r = ins[a].shape[1] // N_DEV
            src = ins[a].at[:, pl.ds(pl.multiple_of((2 * q + 1 - c) * r, r), r), :]
            return pltpu.make_async_remote_copy(
                src_ref=src, dst_ref=got[a].at[q],
                send_sem=send_sems.at[a * N_CHIP + q], recv_sem=recv_sems.at[a * N_CHIP + q],
                device_id=sibling, device_id_type=MESH)

        sends = [remote(a, q) for a in range(n) for q in range(N_CHIP)]
        for cp in sends:
            cp.start()
        for cp in sends:
            cp.wait_recv()
        for cp in sends:
            cp.wait_send()

    return pl.pallas_call(
        body, name="grad_pair_exchange",
        in_specs=[HBM] * n, out_specs=[HBM] * n,
        out_shape=[jax.ShapeDtypeStruct((N_CHIP, g.shape[0], g.shape[1] // N_DEV, g.shape[2]), g.dtype) for g in grads],
        scratch_shapes=[pltpu.SemaphoreType.DMA((N_CHIP * n,)), pltpu.SemaphoreType.DMA((N_CHIP * n,))],
    )(*grads)


def _chip_exchange(parts):
    n = len(parts)

    def body(*refs):
        ins, outs = refs[:n], refs[n:2 * n]
        send_sems, recv_sems = refs[2 * n:]
        x, y, c = _coords()
        chips = _other_chips(x, y)

        def remote(a, j):
            return pltpu.make_async_remote_copy(
                src_ref=ins[a].at[j], dst_ref=outs[a].at[j],
                send_sem=send_sems.at[a * 3 + j], recv_sem=recv_sems.at[a * 3 + j],
                device_id=(*chips[j], c), device_id_type=MESH)

        sends = [remote(a, j) for a in range(n) for j in range(3)]
        for cp in sends:
            cp.start()
        for cp in sends:
            cp.wait_recv()
        for cp in sends:
            cp.wait_send()

    return pl.pallas_call(
        body, name="grad_chip_exchange",
        in_specs=[HBM] * n, out_specs=[HBM] * n,
        out_shape=[jax.ShapeDtypeStruct(p.shape, p.dtype) for p in parts],
        scratch_shapes=[pltpu.SemaphoreType.DMA((3 * n,)), pltpu.SemaphoreType.DMA((3 * n,))],
    )(*parts)


def _all_reduce_small(pack):
    R = pack.shape[0]

    def body(p_ref, tot_ref, all_ref, send_sems, recv_sems):
        x, y, c = _coords()
        me = 4 * x + 2 * y + c
        all_ref[me] = p_ref[...]
        peers = []
        for k in range(1, N_DEV):
            bx, by, bc = (k >> 2) & 1, (k >> 1) & 1, k & 1
            peers.append((x ^ bx, y ^ by, c ^ bc))

        def copy(k, slot, to):
            return pltpu.make_async_remote_copy(
                src_ref=p_ref, dst_ref=all_ref.at[slot], send_sem=send_sems.at[k], recv_sem=recv_sems.at[k],
                device_id=to, device_id_type=MESH)

        sends = [copy(k, me, peer) for k, peer in enumerate(peers)]
        for cp in sends:
            cp.start()
        for k, peer in enumerate(peers):
            copy(k, 4 * peer[0] + 2 * peer[1] + peer[2], peer).wait_recv()
        for cp in sends:
            cp.wait_send()
        tot = all_ref[0]
        for d in range(1, N_DEV):
            tot = tot + all_ref[d]
        tot_ref[...] = tot

    vmem = pl.BlockSpec(memory_space=pltpu.VMEM)
    return pl.pallas_call(
        body, name="all_reduce_small",
        in_specs=[vmem], out_specs=vmem,
        out_shape=jax.ShapeDtypeStruct((R, 128), F32),
        scratch_shapes=[pltpu.VMEM((N_DEV, R, 128), F32), pltpu.SemaphoreType.DMA((N_DEV - 1,)),
                        pltpu.SemaphoreType.DMA((N_DEV - 1,))],
    )(pack)


HBM_ONLY = pl.BlockSpec(memory_space=pltpu.HBM)
SEM = pl.BlockSpec(memory_space=pltpu.SEMAPHORE)
DATAFLOW = pltpu.SideEffectType.DATAFLOW_SIDE_EFFECTING


def _peers(x, y, c):
    return [(x ^ ((k >> 2) & 1), y ^ ((k >> 1) & 1), c ^ (k & 1)) for k in range(1, N_DEV)]


def _gather_plan(ins, lands):
    x, y, c = _coords()

    def rows(a, dev):
        r = ins[a].shape[1]
        return lands[a].at[:, pl.ds(pl.multiple_of((4 * dev[0] + 2 * dev[1] + dev[2]) * r, r), r), :]

    return [(ins[a], rows(a, (x, y, c)), peer, rows(a, peer)) for a in range(len(ins)) for peer in _peers(x, y, c)]


def _chip_plan(ins, lands):
    x, y, c = _coords()
    chips = _other_chips(x, y)
    return [(ins[a].at[j], lands[a].at[j], (*chips[j], c), lands[a].at[j]) for a in range(len(ins)) for j in range(3)]


def _exchange_start(name, plan, copies_per_array, srcs, lands):
    n = len(srcs)
    count = copies_per_array * n

    def body(*refs):
        ins, land = refs[:n], refs[n:2 * n]
        send_sems, recv_sems = refs[2 * n], refs[2 * n + 1]
        token = refs[-1]
        for k, (src, dst, peer, _) in enumerate(plan(ins, land)):
            pltpu.make_async_remote_copy(src_ref=src, dst_ref=dst, send_sem=send_sems.at[k], recv_sem=recv_sems.at[k],
                                         device_id=peer, device_id_type=MESH).start()
        token[...] = jnp.zeros_like(token)

    thru = [pltpu.HBM(v.shape, v.dtype) for v in list(srcs) + list(lands)]
    outs = pl.pallas_call(
        body, name=name,
        in_specs=[HBM_ONLY] * (2 * n),
        out_specs=[SEM, SEM] + [HBM_ONLY] * (2 * n) + [pl.BlockSpec(memory_space=pltpu.VMEM)],
        out_shape=[pltpu.SemaphoreType.DMA((count,)), pltpu.SemaphoreType.DMA((count,))] + thru
        + [jax.ShapeDtypeStruct((8, 128), F32)],
        input_output_aliases={i: 2 + i for i in range(2 * n)},
        compiler_params=pltpu.CompilerParams(has_side_effects=DATAFLOW),
    )(*[pltpu.with_memory_space_constraint(v, pltpu.HBM) for v in list(srcs) + list(lands)])
    return outs[0], outs[1], outs[2:2 + n], outs[2 + n:2 + 2 * n], outs[-1]


def _exchange_wait(name, plan, send_sems, recv_sems, srcs, lands, after):
    n = len(srcs)

    def body(*refs):
        ins, land = refs[:n], refs[n:2 * n]
        send, recv = refs[2 * n], refs[2 * n + 1]
        for k, (src, _, peer, here) in enumerate(plan(ins, land)):
            cp = pltpu.make_async_remote_copy(src_ref=src, dst_ref=here, send_sem=send.at[k], recv_sem=recv.at[k],
                                              device_id=peer, device_id_type=MESH)
            cp.wait_send()
            cp.wait_recv()

    thru = [pltpu.HBM(v.shape, v.dtype) for v in list(srcs) + list(lands)]
    outs = pl.pallas_call(
        body, name=name,
        in_specs=[HBM_ONLY] * (2 * n) + [SEM, SEM, pl.BlockSpec(memory_space=pl.ANY)],
        out_specs=[HBM_ONLY] * (2 * n),
        out_shape=thru,
        input_output_aliases={i: i for i in range(2 * n)},
        compiler_params=pltpu.CompilerParams(has_side_effects=DATAFLOW),
    )(*srcs, *lands, send_sems, recv_sems, after)
    return outs[n:]


def _place_own(shard, dev):
    s, r, c = shard.shape

    def body(dev_ref, i_ref, o_ref):
        o_ref[...] = i_ref[...]

    return pl.pallas_call(
        body, name="place_own_shard",
        grid_spec=pltpu.PrefetchScalarGridSpec(
            num_scalar_prefetch=1, grid=(s,),
            in_specs=[pl.BlockSpec((None, r, c), lambda i, d: (i, 0, 0))],
            out_specs=pl.BlockSpec((None, r, c), lambda i, d: (i, d[0], 0))),
        out_shape=jax.ShapeDtypeStruct((s, N_DEV * r, c), shard.dtype),
        compiler_params=_cp("arbitrary"),
    )(dev, shard)


def _after(x, token):
    return lax.optimization_barrier((x, token))[0]


def _tile_rows(n, cap=512):
    t = min(n, cap)
    while n % t or t % 8:
        t -= 1
        if t < 8:
            return n
    return t


def _pair_sum(g, got, owner_dev, owner_chip, dtype):
    s, r8, c = g.shape
    r = r8 // N_DEV
    n = owner_dev.shape[0]

    def body(dev_ref, chip_ref, g_ref, got_ref, o_ref):
        o_ref[...] = (g_ref[...] + got_ref[...]).astype(dtype)

    return pl.pallas_call(
        body, name="pair_sum",
        grid_spec=pltpu.PrefetchScalarGridSpec(
            num_scalar_prefetch=2, grid=(n, s),
            in_specs=[pl.BlockSpec((None, r, c), lambda j, i, dev, chip: (i, dev[j], 0)),
                      pl.BlockSpec((None, None, r, c), lambda j, i, dev, chip: (chip[j], i, 0, 0))],
            out_specs=pl.BlockSpec((None, None, r, c), lambda j, i, dev, chip: (j, i, 0, 0))),
        out_shape=jax.ShapeDtypeStruct((n, s, r, c), dtype),
        compiler_params=_cp("parallel", "parallel"),
    )(owner_dev, owner_chip, g, got)


def _sum_chips(own, parts):
    _, s, r, c = parts.shape

    def body(o, p0, p1, p2, o_ref):
        o_ref[...] = ((o[...] + p0[...].astype(F32)) + p1[...].astype(F32)) + p2[...].astype(F32)

    def spec(q):
        return pl.BlockSpec((None, None, r, c), lambda i: (q, i, 0, 0))

    return pl.pallas_call(
        body, name="chip_sum", grid=(s,),
        in_specs=[spec(0), spec(0), spec(1), spec(2)], out_specs=pl.BlockSpec((None, r, c), lambda i: (i, 0, 0)),
        out_shape=jax.ShapeDtypeStruct((s, r, c), F32),
        compiler_params=_cp("parallel"),
    )(own, parts, parts, parts)


def _adamw(w, g, m, v):
    shape = w.shape
    c = shape[-1] if w.ndim > 1 else w.shape[0]
    args = [t.reshape(-1, c) for t in (w, g, m, v)]
    n = args[0].shape[0]
    tr = _tile_rows(n)

    def body(w_ref, g_ref, m_ref, v_ref, d_ref, mo_ref, vo_ref):
        g = g_ref[...]
        m = B1 * m_ref[...] + (1.0 - B1) * g
        v = B2 * v_ref[...] + (1.0 - B2) * jnp.square(g)
        m_hat = m / (1.0 - B1 ** STEP)
        v_hat = v / (1.0 - B2 ** STEP)
        d_ref[...] = -LR * (m_hat / (jnp.sqrt(v_hat) + ADAM_EPS) + WD * w_ref[...])
        mo_ref[...] = m
        vo_ref[...] = v

    outs = pl.pallas_call(
        body, name="adamw", grid=(n // tr,),
        in_specs=[_row(tr, c)] * 4, out_specs=[_row(tr, c)] * 3,
        out_shape=[jax.ShapeDtypeStruct((n, c), F32)] * 3,
        compiler_params=_cp("parallel"),
    )(*args)
    return [o.reshape(shape) for o in outs]


def _ffn_slab(ffn, which):
    return ffn * 3 + which


def _pack(pieces):
    flat = []
    for p in pieces:
        f = p.reshape(-1)
        flat.append(jnp.pad(f, (0, (-f.shape[0]) % 1024)))
    return jnp.concatenate(flat).reshape(-1, 128)


def _unpack(pack, shapes):
    flat = pack.reshape(-1)
    out, off = [], 0
    for s in shapes:
        size = 1
        for d in s:
            size *= d
        out.append(flat[off:off + size].reshape(s))
        off += size + (-size) % 1024
    return out


def kernel(x, positions, ffn1_norm, ffn1_w_gate, ffn1_w_up, ffn1_w_down, mix_norm, w_in, conv_w, conv_b, conv_ln_g, conv_ln_b, attn_sinks, w_out, ffn2_norm, ffn2_w_gate, ffn2_w_up, ffn2_w_down, final_norm, loss_target, m_ffn1_norm, m_ffn1_w_gate, m_ffn1_w_up, m_ffn1_w_down, m_mix_norm, m_w_in, m_conv_w, m_conv_b, m_conv_ln_g, m_conv_ln_b, m_attn_sinks, m_w_out, m_ffn2_norm, m_ffn2_w_gate, m_ffn2_w_up, m_ffn2_w_down, m_final_norm, v_ffn1_norm, v_ffn1_w_gate, v_ffn1_w_up, v_ffn1_w_down, v_mix_norm, v_w_in, v_conv_w, v_conv_b, v_conv_ln_g, v_conv_ln_b, v_attn_sinks, v_w_out, v_ffn2_norm, v_ffn2_w_gate, v_ffn2_w_up, v_ffn2_w_down, v_final_norm):
    L = ffn1_norm.shape[0]
    T = x.shape[1]
    x0 = x.reshape(T, D)
    target = loss_target.reshape(T, D)
    dev = 4 * lax.axis_index("x") + 2 * lax.axis_index("y") + lax.axis_index("c")

    def t_(w):
        return jnp.swapaxes(w, 1, 2)

    def layer_shards(l):
        ffn = jnp.stack([t_(ffn1_w_gate)[l], t_(ffn1_w_up)[l], ffn1_w_down[l],
                         t_(ffn2_w_gate)[l], t_(ffn2_w_up)[l], ffn2_w_down[l]]).astype(BF16)
        return [ffn, t_(w_in)[l:l + 1].astype(BF16), w_out[l:l + 1].astype(BF16)]

    shards = [layer_shards(l) for l in range(L)]
    cw_cols = CC // N_DEV
    cw_sh = jnp.pad(conv_w.reshape(-1), (0, (-L * CW * cw_cols) % 1024)).reshape(1, -1, 128)

    *weights0, cw_all = _all_gather(shards[0] + [cw_sh])
    weights = [weights0] + [None] * (L - 1)
    dev1 = dev.reshape(1).astype(jnp.int32)
    pending, dep = {}, cw_all
    for l in range(1, L):
        srcs = [_after(s, dep) for s in shards[l]]
        lands = [_place_own(s, dev1) for s in srcs]
        *pending[l], dep = _exchange_start(f"gather_start_{l}", _gather_plan, N_DEV - 1, srcs, lands)

    cw_rows = cw_sh.shape[1]
    cw_full = cw_all.reshape(N_DEV, cw_rows * 128)[:, :L * CW * cw_cols].reshape(N_DEV, L, CW, cw_cols)
    cw_full = jnp.transpose(cw_full, (1, 2, 0, 3)).reshape(L, CW, CC)

    inv_freq = 1.0 / (10000.0 ** (jnp.arange(0, HD, 2, dtype=F32) / HD))
    ang = positions.reshape(T).astype(F32)[:, None] * inv_freq
    cos, sin = jnp.cos(ang), jnp.sin(ang)
    rc = jnp.concatenate([cos, cos, cos, cos], axis=1)
    rs = jnp.concatenate([-sin, sin, -sin, sin], axis=1)

    n1 = ffn1_norm.reshape(L, 1, D)
    nm = mix_norm.reshape(L, 1, D)
    n2 = ffn2_norm.reshape(L, 1, D)
    cb = conv_b.reshape(L, 1, CC)
    lg = conv_ln_g.reshape(L, 1, CC)
    lb = conv_ln_b.reshape(L, 1, CC)

    saved = []
    xa = _after(x0, dep)
    for l in range(L):
        if l > 0:
            weights[l] = _exchange_wait(f"gather_wait_{l}", _gather_plan, *pending[l], after=xa)
        wffn, win, wout = weights[l]
        h1, G1, U1, A1 = _ffn_up(xa, n1, wffn, _ffn_slab(0, 0), _ffn_slab(0, 1), l)
        xb = _ffn_down(A1, xa, wffn, _ffn_slab(0, 2))
        hm, qkv, u = _mix_in(xb, nm, win, rc, rs, l)
        ao = _attn_fwd(qkv, attn_sinks, l)
        y, co = _conv_fwd(u, cw_full, cb, lg, lb, l)
        xc, cat = _mix_out(ao, co, xb, wout)
        h2, G2, U2, A2 = _ffn_up(xc, n2, wffn, _ffn_slab(1, 0), _ffn_slab(1, 1), l)
        xd = _ffn_down(A2, xc, wffn, _ffn_slab(1, 2))
        saved.append((xa, h1, G1, U1, A1, xb, hm, qkv, u, y, cat, xc, h2, G2, U2, A2))
        xa = xd

    dx, loss_part, g_final = _final(xa, final_norm.reshape(1, D), target)

    cx, cy, cc = _coords()
    chip_of = [2 * cx + cy] + [2 * px + py for px, py in _other_chips(cx, cy)]
    own_chip = jnp.stack(chip_of[:1]).astype(jnp.int32)
    other_chips = jnp.stack(chip_of[1:]).astype(jnp.int32)

    g_n1, g_nm, g_n2 = [None] * L, [None] * L, [None] * L
    g_cb, g_lg, g_lb, g_sink, g_cw = [None] * L, [None] * L, [None] * L, [None] * L, [None] * L
    own, arriving, reduced = [None] * L, {}, [None] * L
    for l in reversed(range(L)):
        xa, h1, G1, U1, A1, xb, hm, qkv, u, y, cat, xc, h2, G2, U2, A2 = saved[l]
        wffn, win, wout = weights[l]
        gffn = lax.empty((6, FF, D), F32)
        gin = lax.empty((1, DIN, D), F32)
        gout = lax.empty((1, D, D), F32)
        d, dG, dU = _ffn_bwd_act(dx, G2, U2, wffn, _ffn_slab(1, 2))
        gffn = _wgrad(gffn, _ffn_slab(1, 2), A2, d)
        gffn = _wgrad(gffn, _ffn_slab(1, 0), dG, h2)
        gffn = _wgrad(gffn, _ffn_slab(1, 1), dU, h2)
        dx, g_n2[l] = _ffn_bwd_in(dG, dU, xc, dx, n2, wffn, _ffn_slab(1, 0), _ffn_slab(1, 1), l)
        d, dao, dco = _mix_out_bwd(dx, wout)
        gout = _wgrad(gout, 0, cat, d)
        dy, g_lg[l], g_lb[l], g_cb[l] = _conv_bwd_norm(dco, y, lg, lb, l)
        du, g_cw[l] = _conv_bwd_taps(dy, u, cw_full, l)
        dq, dk, dv, g_sink[l] = _attn_bwd(qkv, dao, attn_sinks, l)
        dp, dx, g_nm[l] = _mix_in_bwd(dq, dk, dv, du, rc, rs, xb, dx, nm, win, l)
        gin = _wgrad(gin, 0, dp, hm)
        d, dG, dU = _ffn_bwd_act(dx, G1, U1, wffn, _ffn_slab(0, 2))
        gffn = _wgrad(gffn, _ffn_slab(0, 2), A1, d)
        gffn = _wgrad(gffn, _ffn_slab(0, 0), dG, h1)
        gffn = _wgrad(gffn, _ffn_slab(0, 1), dU, h1)
        dx, g_n1[l] = _ffn_bwd_in(dG, dU, xa, dx, n1, wffn, _ffn_slab(0, 0), _ffn_slab(0, 1), l)
        layer_grads = [gffn, gin, gout]
        got = _pair_exchange(layer_grads)
        own[l] = [_pair_sum(g, r, 2 * own_chip + cc, own_chip, F32) for g, r in zip(layer_grads, got)]
        sent = [_pair_sum(g, r, 2 * other_chips + cc, other_chips, BF16) for g, r in zip(layer_grads, got)]
        if l > 0:
            lands = [lax.empty(p.shape, p.dtype) for p in sent]
            *arriving[l], token = _exchange_start(f"chip_start_{l}", _chip_plan, 3, sent, lands)
            dx = _after(dx, token)
        else:
            reduced[l] = [_sum_chips(o, p) for o, p in zip(own[l], _chip_exchange(sent))]

    grad_x = dx.reshape(1, T, D)
    for l in arriving:
        parts = _exchange_wait(f"chip_wait_{l}", _chip_plan, *arriving[l], after=dx)
        reduced[l] = [_sum_chips(o, p) for o, p in zip(own[l], parts)]
    gffn_sh = jnp.stack([r[0] for r in reduced])
    gin_sh = jnp.concatenate([r[1] for r in reduced])
    gout_sh = jnp.concatenate([r[2] for r in reduced])

    small = [loss_part,
             jnp.concatenate(g_n1), jnp.concatenate(g_nm), jnp.concatenate(g_n2), g_final,
             jnp.concatenate(g_cb), jnp.concatenate(g_lg), jnp.concatenate(g_lb),
             jnp.stack(g_sink)[:, :, 0], jnp.stack(g_cw)]
    small_shapes = [(1, 128), (L, D), (L, D), (L, D), (D,), (L, CC), (L, CC), (L, CC), (L, NH), (L, CW, CC)]
    tot = _unpack(_all_reduce_small(_pack(small)), small_shapes)
    loss = tot[0][0, 0]
    gr_n1, gr_nm, gr_n2, gr_final, gr_cb, gr_lg, gr_lb, gr_sink, gr_cw_full = tot[1:]
    gr_cw = lax.dynamic_slice_in_dim(gr_cw_full, dev * cw_cols, cw_cols, axis=2)

    gf = gffn_sh.reshape(L, 2, 3, FF // N_DEV, D)
    grads = {
        "ffn1_norm": gr_n1, "ffn1_w_gate": t_(gf[:, 0, 0]), "ffn1_w_up": t_(gf[:, 0, 1]), "ffn1_w_down": gf[:, 0, 2],
        "mix_norm": gr_nm, "w_in": t_(gin_sh), "conv_w": gr_cw, "conv_b": gr_cb, "conv_ln_g": gr_lg,
        "conv_ln_b": gr_lb, "attn_sinks": gr_sink, "w_out": gout_sh,
        "ffn2_norm": gr_n2, "ffn2_w_gate": t_(gf[:, 1, 0]), "ffn2_w_up": t_(gf[:, 1, 1]), "ffn2_w_down": gf[:, 1, 2],
        "final_norm": gr_final,
    }
    weights = dict(ffn1_norm=ffn1_norm, ffn1_w_gate=ffn1_w_gate, ffn1_w_up=ffn1_w_up, ffn1_w_down=ffn1_w_down, mix_norm=mix_norm, w_in=w_in, conv_w=conv_w, conv_b=conv_b, conv_ln_g=conv_ln_g, conv_ln_b=conv_ln_b, attn_sinks=attn_sinks, w_out=w_out, ffn2_norm=ffn2_norm, ffn2_w_gate=ffn2_w_gate, ffn2_w_up=ffn2_w_up, ffn2_w_down=ffn2_w_down, final_norm=final_norm)
    moms = dict(ffn1_norm=m_ffn1_norm, ffn1_w_gate=m_ffn1_w_gate, ffn1_w_up=m_ffn1_w_up, ffn1_w_down=m_ffn1_w_down, mix_norm=m_mix_norm, w_in=m_w_in, conv_w=m_conv_w, conv_b=m_conv_b, conv_ln_g=m_conv_ln_g, conv_ln_b=m_conv_ln_b, attn_sinks=m_attn_sinks, w_out=m_w_out, ffn2_norm=m_ffn2_norm, ffn2_w_gate=m_ffn2_w_gate, ffn2_w_up=m_ffn2_w_up, ffn2_w_down=m_ffn2_w_down, final_norm=m_final_norm)
    vels = dict(ffn1_norm=v_ffn1_norm, ffn1_w_gate=v_ffn1_w_gate, ffn1_w_up=v_ffn1_w_up, ffn1_w_down=v_ffn1_w_down, mix_norm=v_mix_norm, w_in=v_w_in, conv_w=v_conv_w, conv_b=v_conv_b, conv_ln_g=v_conv_ln_g, conv_ln_b=v_conv_ln_b, attn_sinks=v_attn_sinks, w_out=v_w_out, ffn2_norm=v_ffn2_norm, ffn2_w_gate=v_ffn2_w_gate, ffn2_w_up=v_ffn2_w_up, ffn2_w_down=v_ffn2_w_down, final_norm=v_final_norm)

    names = list(weights)
    big = ("ffn1_w_gate", "ffn1_w_up", "ffn1_w_down", "w_in", "w_out", "ffn2_w_gate", "ffn2_w_up", "ffn2_w_down")
    delta, new_m, new_v = {}, {}, {}
    for k in big:
        delta[k], new_m[k], new_v[k] = _adamw(weights[k], grads[k], moms[k], vels[k])
    rest = [k for k in names if k not in big]
    rest_shapes = [weights[k].shape for k in rest]
    packed = _adamw(*[_pack([t[k] for k in rest]) for t in (weights, grads, moms, vels)])
    for res, packed_out in zip((delta, new_m, new_v), packed):
        for k, val in zip(rest, _unpack(packed_out, rest_shapes)):
            res[k] = val

    return (loss, grad_x, *[grads[k] for k in names], *[delta[k] for k in names],
            *[new_m[k] for k in names], *[new_v[k] for k in names])
```

```python
import functools

import jax
import jax.numpy as jnp
from jax import lax
from jax.experimental import pallas as pl
from jax.experimental.pallas import tpu as pltpu

F32 = jnp.float32
BF16 = jnp.bfloat16
MESH = pl.DeviceIdType.MESH

N_DEV = 8
N_CHIP = 4
D = 1024
FF = 2816
HD = 64
NH = 8
NKV = 2
GROUP = NH // NKV
AW = NH * HD
KVW = NKV * HD
QKW = AW + KVW
QKVW = AW + 2 * KVW
CC = 512
CW = 31
DIN = QKVW + 2 * CC
BLK = 128
HALO = 32
EPS = 1e-5
SCALE = HD ** -0.5
NEG = float(jnp.finfo(jnp.float32).min)

LR, B1, B2, ADAM_EPS, WD, STEP = 0.001, 0.9, 0.999, 1e-08, 0.01, 10

TM = 256
TK = 512
FC = 256
ATT_BLOCKS = 4
VMEM_LIMIT = 56 * 1024 * 1024


def _cp(*sem):
    return pltpu.CompilerParams(dimension_semantics=sem, vmem_limit_bytes=VMEM_LIMIT)


def _row(tm, c):
    return pl.BlockSpec((tm, c), lambda i: (i, 0))


def _slab(shape, k, single=False):
    zeros = (0,) * len(shape)
    kw = dict(pipeline_mode=pl.Buffered(1)) if single else {}
    return pl.BlockSpec((None, *shape), lambda i: (k, *zeros), **kw)


def _acc(shape):
    return pl.BlockSpec(shape, lambda i: (0,) * len(shape))


def _nt(a, b):
    return lax.dot_general(a, b, (((1,), (1,)), ((), ())), preferred_element_type=F32)


def _tn(a, b):
    return lax.dot_general(a, b, (((0,), (0,)), ((), ())), preferred_element_type=F32)


def _nn(a, b):
    return jnp.dot(a, b, preferred_element_type=F32)


def _sigmoid(x):
    return jax.nn.sigmoid(x)


def _dsilu(z):
    s = _sigmoid(z)
    return s * (1.0 + z * (1.0 - s))


def _rms(x, g):
    r = lax.rsqrt(jnp.mean(x * x, axis=-1, keepdims=True) + EPS)
    xh = x * r
    return xh, r, xh * g


def _rms_bwd(dh, xh, r, g):
    dxh = dh * g
    return r * (dxh - xh * jnp.mean(dxh * xh, axis=-1, keepdims=True))


def _rope(t, c128, s128):
    w = t.shape[1]
    lane = lax.broadcasted_iota(jnp.int32, t.shape, 1)
    rot = jnp.where(lane % HD < HD // 2, pltpu.roll(t, w - HD // 2, 1), pltpu.roll(t, HD // 2, 1))
    return t * jnp.tile(c128, (1, w // 128)) + rot * jnp.tile(s128, (1, w // 128))


def _ffn_up(x, norm, wffn, sg, su, layer, token):
    T = x.shape[0]

    def body(x_ref, g_ref, wg_ref, wu_ref, token_ref, h_ref, G_ref, U_ref, A_ref):
        _, _, hn = _rms(x_ref[...], g_ref[...])
        h = hn.astype(BF16)
        h_ref[...] = h
        for c in range(FF // FC):
            sl = slice(c * FC, (c + 1) * FC)
            g = _nt(h, wg_ref[sl, :])
            u = _nt(h, wu_ref[sl, :])
            G_ref[:, sl] = g.astype(BF16)
            U_ref[:, sl] = u.astype(BF16)
            A_ref[:, sl] = (g * _sigmoid(g) * u).astype(BF16)

    return pl.pallas_call(
        body, name="ffn_up", grid=(T // TM,),
        in_specs=[_row(TM, D), _slab((1, D), layer), _slab((FF, D), sg, True), _slab((FF, D), su, True), HBM],
        out_specs=[_row(TM, D), _row(TM, FF), _row(TM, FF), _row(TM, FF)],
        out_shape=[jax.ShapeDtypeStruct((T, D), BF16)] + [jax.ShapeDtypeStruct((T, FF), BF16)] * 3,
        compiler_params=_cp("parallel"),
    )(x, norm, wffn, wffn, token)


def _ffn_down(a, x, wffn, sd):
    T = x.shape[0]

    def body(a_ref, x_ref, w_ref, o_ref):
        o_ref[...] = x_ref[...] + 0.5 * _nn(a_ref[...], w_ref[...])

    return pl.pallas_call(
        body, name="ffn_down", grid=(T // TM,),
        in_specs=[_row(TM, FF), _row(TM, D), _slab((FF, D), sd, True)],
        out_specs=_row(TM, D),
        out_shape=jax.ShapeDtypeStruct((T, D), F32),
        compiler_params=_cp("parallel"),
    )(a, x, wffn)


def _mix_in(x, norm, win, rc, rs, layer, token):
    T = x.shape[0]

    def body(x_ref, g_ref, w_ref, c_ref, s_ref, token_ref, h_ref, qkv_ref, u_ref):
        _, _, hn = _rms(x_ref[...], g_ref[...])
        h = hn.astype(BF16)
        h_ref[...] = h
        qk = _nt(h, w_ref[0:QKW, :])
        qkv_ref[:, 0:QKW] = _rope(qk, c_ref[...], s_ref[...]).astype(BF16)
        qkv_ref[:, QKW:QKVW] = _nt(h, w_ref[QKW:QKVW, :]).astype(BF16)
        for c in range(2 * CC // FC):
            u_ref[:, c * FC:(c + 1) * FC] = _nt(h, w_ref[QKVW + c * FC:QKVW + (c + 1) * FC, :]).astype(BF16)

    return pl.pallas_call(
        body, name="mix_in", grid=(T // TM,),
        in_specs=[_row(TM, D), _slab((1, D), layer), _slab((DIN, D), 0, True), _row(TM, 128), _row(TM, 128), HBM],
        out_specs=[_row(TM, D), _row(TM, QKVW), _row(TM, 2 * CC)],
        out_shape=[jax.ShapeDtypeStruct((T, D), BF16), jax.ShapeDtypeStruct((T, QKVW), BF16),
                   jax.ShapeDtypeStruct((T, 2 * CC), BF16)],
        compiler_params=_cp("parallel"),
    )(x, norm, win, rc, rs, token)


def _band_mask(has_prev):
    j = lax.broadcasted_iota(jnp.int32, (2 * BLK, BLK), 0)
    r = lax.broadcasted_iota(jnp.int32, (2 * BLK, BLK), 1) + BLK
    rel = r - j
    return jnp.tile((rel >= 0) & (rel < BLK) & (has_prev | (j >= BLK)), (1, GROUP))


def _band(prev_ref, cur_ref, b, col):
    if b == 0:
        return jnp.concatenate([prev_ref[:, col:col + HD], cur_ref[0:BLK, col:col + HD]], axis=0)
    return cur_ref[(b - 1) * BLK:(b + 1) * BLK, col:col + HD]


def _stack_heads(ref, b, kv):
    cols = [(kv * GROUP + g) * HD for g in range(GROUP)]
    return jnp.concatenate([ref[b * BLK:(b + 1) * BLK, c:c + HD] for c in cols], axis=0)


def _unstack_t(xt):
    x = xt.T
    return jnp.concatenate([x[g * BLK:(g + 1) * BLK, :] for g in range(GROUP)], axis=1)


def _sink_row(sink_ref, layer, kv):
    return jnp.concatenate([jnp.full((1, BLK), sink_ref[layer, kv * GROUP + g], F32) for g in range(GROUP)], axis=1)


def _probs_t(q4, kb, mask, sink):
    s = jnp.where(mask, _nt(kb, q4) * SCALE, NEG)
    m = jnp.maximum(jnp.max(s, axis=0, keepdims=True), sink)
    p = jnp.exp(s - m)
    e = jnp.exp(sink - m)
    inv = 1.0 / (jnp.sum(p, axis=0, keepdims=True) + e)
    return p * inv, e * inv


def _attn_fwd(qkv, sinks, layer):
    T = qkv.shape[0]
    tq = ATT_BLOCKS * BLK

    def body(sink_ref, cur_ref, prev_ref, o_ref):
        first = _band_mask(pl.program_id(0) > 0)
        later = _band_mask(True)
        for b in range(ATT_BLOCKS):
            outs = []
            for kv in range(NKV):
                kb = _band(prev_ref, cur_ref, b, AW + kv * HD)
                vb = _band(prev_ref, cur_ref, b, QKW + kv * HD)
                pt, _ = _probs_t(_stack_heads(cur_ref, b, kv), kb, first if b == 0 else later,
                                 _sink_row(sink_ref, layer, kv))
                outs.append(_unstack_t(_nn(vb.T, pt.astype(BF16))))
            o_ref[b * BLK:(b + 1) * BLK, :] = jnp.concatenate(outs, axis=1).astype(BF16)

    return pl.pallas_call(
        body, name="attn_fwd", grid=(T // tq,),
        in_specs=[pl.BlockSpec(memory_space=pltpu.SMEM), _row(tq, QKVW),
                  pl.BlockSpec((BLK, QKVW), lambda i: (jnp.maximum(i * ATT_BLOCKS - 1, 0), 0))],
        out_specs=_row(tq, AW),
        out_shape=jax.ShapeDtypeStruct((T, AW), BF16),
        compiler_params=_cp("parallel"),
    )(sinks, qkv, qkv)


def _glu(u):
    u = u.astype(F32)
    return u[:, :CC] * _sigmoid(u[:, CC:])


def _conv_fwd(u, cw, cb, lg, lb, layer):
    T = u.shape[0]

    def body(u_ref, up_ref, w_ref, b_ref, g_ref, bb_ref, y_ref, o_ref, ext_ref):
        i = pl.program_id(0)
        ext_ref[0:HALO, :] = jnp.where(i > 0, _glu(up_ref[...]), 0.0)
        ext_ref[HALO:, :] = _glu(u_ref[...])
        acc = jnp.zeros((TM, CC), F32)
        for k in range(CW):
            acc = acc + w_ref[k:k + 1, :] * ext_ref[pl.ds(HALO - (CW - 1) + k, TM), :]
        y = acc + b_ref[...]
        y_ref[...] = y
        xc = y - jnp.mean(y, axis=-1, keepdims=True)
        z = xc * lax.rsqrt(jnp.mean(xc * xc, axis=-1, keepdims=True) + EPS) * g_ref[...] + bb_ref[...]
        o_ref[...] = (z * _sigmoid(z)).astype(BF16)

    return pl.pallas_call(
        body, name="conv_fwd", grid=(T // TM,),
        in_specs=[_row(TM, 2 * CC),
                  pl.BlockSpec((HALO, 2 * CC), lambda i: (jnp.maximum(i * (TM // HALO) - 1, 0), 0)),
                  _slab((CW, CC), layer), _slab((1, CC), layer), _slab((1, CC), layer), _slab((1, CC), layer)],
        out_specs=[_row(TM, CC), _row(TM, CC)],
        out_shape=[jax.ShapeDtypeStruct((T, CC), F32), jax.ShapeDtypeStruct((T, CC), BF16)],
        scratch_shapes=[pltpu.VMEM((TM + HALO, CC), F32)],
        compiler_params=_cp("parallel"),
    )(u, u, cw, cb, lg, lb)


def _mix_out(ao, co, x, wout):
    T = x.shape[0]

    def body(ao_ref, co_ref, x_ref, w_ref, o_ref, cat_ref):
        cat = jnp.concatenate([ao_ref[...], co_ref[...]], axis=1)
        cat_ref[...] = cat
        o_ref[...] = x_ref[...] + _nn(cat, w_ref[...])

    return pl.pallas_call(
        body, name="mix_out", grid=(T // TM,),
        in_specs=[_row(TM, AW), _row(TM, CC), _row(TM, D), _slab((D, D), 0, True)],
        out_specs=[_row(TM, D), _row(TM, D)],
        out_shape=[jax.ShapeDtypeStruct((T, D), F32), jax.ShapeDtypeStruct((T, D), BF16)],
        compiler_params=_cp("parallel"),
    )(ao, co, x, wout)


def _final(x, norm, target):
    T = x.shape[0]

    def body(x_ref, g_ref, t_ref, dx_ref, loss_ref, dg_ref):
        @pl.when(pl.program_id(0) == 0)
        def _():
            loss_ref[...] = jnp.zeros_like(loss_ref)
            dg_ref[...] = jnp.zeros_like(dg_ref)

        g = g_ref[...]
        xh, r, y = _rms(x_ref[...], g)
        err = y - t_ref[...]
        loss_ref[...] += jnp.full(loss_ref.shape, (0.5 / D) * jnp.sum(err * err), F32)
        dy = err * (1.0 / D)
        dg_ref[...] += jnp.sum(dy * xh, axis=0, keepdims=True)
        dx_ref[...] = _rms_bwd(dy, xh, r, g)

    return pl.pallas_call(
        body, name="final_loss", grid=(T // TM,),
        in_specs=[_row(TM, D), _acc((1, D)), _row(TM, D)],
        out_specs=[_row(TM, D), _acc((1, 128)), _acc((1, D))],
        out_shape=[jax.ShapeDtypeStruct((T, D), F32), jax.ShapeDtypeStruct((1, 128), F32),
                   jax.ShapeDtypeStruct((1, D), F32)],
        compiler_params=_cp("arbitrary"),
    )(x, norm, target)


def _ffn_bwd_act(dx, G, U, wffn, sd, token):
    T = dx.shape[0]

    def body(dx_ref, G_ref, U_ref, w_ref, token_ref, d_ref, dG_ref, dU_ref):
        d = (0.5 * dx_ref[...]).astype(BF16)
        d_ref[...] = d
        for c in range(FF // FC):
            sl = slice(c * FC, (c + 1) * FC)
            da = _nt(d, w_ref[sl, :])
            g = G_ref[:, sl].astype(F32)
            u = U_ref[:, sl].astype(F32)
            dU_ref[:, sl] = (da * g * _sigmoid(g)).astype(BF16)
            dG_ref[:, sl] = (da * u * _dsilu(g)).astype(BF16)

    return pl.pallas_call(
        body, name="ffn_bwd_act", grid=(T // TM,),
        in_specs=[_row(TM, D), _row(TM, FF), _row(TM, FF), _slab((FF, D), sd, True), HBM],
        out_specs=[_row(TM, D), _row(TM, FF), _row(TM, FF)],
        out_shape=[jax.ShapeDtypeStruct((T, D), BF16)] + [jax.ShapeDtypeStruct((T, FF), BF16)] * 2,
        compiler_params=_cp("parallel"),
    )(dx, G, U, wffn, token)


def _ffn_bwd_in(dG, dU, x, dx, norm, wffn, sg, su, layer):
    T = x.shape[0]

    def body(dG_ref, dU_ref, x_ref, dx_ref, g_ref, wg_ref, wu_ref, o_ref, dg_ref):
        @pl.when(pl.program_id(0) == 0)
        def _():
            dg_ref[...] = jnp.zeros_like(dg_ref)

        dh = _nn(dG_ref[...], wg_ref[...]) + _nn(dU_ref[...], wu_ref[...])
        g = g_ref[...]
        xh, r, _ = _rms(x_ref[...], g)
        dg_ref[...] += jnp.sum(dh * xh, axis=0, keepdims=True)
        o_ref[...] = dx_ref[...] + _rms_bwd(dh, xh, r, g)

    return pl.pallas_call(
        body, name="ffn_bwd_in", grid=(T // TM,),
        in_specs=[_row(TM, FF), _row(TM, FF), _row(TM, D), _row(TM, D), _slab((1, D), layer),
                  _slab((FF, D), sg, True), _slab((FF, D), su, True)],
        out_specs=[_row(TM, D), _acc((1, D))],
        out_shape=[jax.ShapeDtypeStruct((T, D), F32), jax.ShapeDtypeStruct((1, D), F32)],
        compiler_params=_cp("arbitrary"),
    )(dG, dU, x, dx, norm, wffn, wffn)


def _mix_out_bwd(dx, wout):
    T = dx.shape[0]

    def body(dx_ref, w_ref, d_ref, dao_ref, dco_ref):
        d = dx_ref[...].astype(BF16)
        d_ref[...] = d
        dcat = _nt(d, w_ref[...])
        dao_ref[...] = dcat[:, :AW].astype(BF16)
        dco_ref[...] = dcat[:, AW:].astype(BF16)

    return pl.pallas_call(
        body, name="mix_out_bwd", grid=(T // TM,),
        in_specs=[_row(TM, D), _slab((D, D), 0, True)],
        out_specs=[_row(TM, D), _row(TM, AW), _row(TM, CC)],
        out_shape=[jax.ShapeDtypeStruct((T, D), BF16), jax.ShapeDtypeStruct((T, AW), BF16),
                   jax.ShapeDtypeStruct((T, CC), BF16)],
        compiler_params=_cp("parallel"),
    )(dx, wout)


def _conv_bwd_norm(dco, y, lg, lb, layer):
    T = y.shape[0]

    def body(dco_ref, y_ref, g_ref, bb_ref, dy_ref, dlg_ref, dlb_ref, dcb_ref):
        @pl.when(pl.program_id(0) == 0)
        def _():
            dlg_ref[...] = jnp.zeros_like(dlg_ref)
            dlb_ref[...] = jnp.zeros_like(dlb_ref)
            dcb_ref[...] = jnp.zeros_like(dcb_ref)

        y = y_ref[...]
        g = g_ref[...]
        xc = y - jnp.mean(y, axis=-1, keepdims=True)
        rs = lax.rsqrt(jnp.mean(xc * xc, axis=-1, keepdims=True) + EPS)
        xn = xc * rs
        z = xn * g + bb_ref[...]
        dz = dco_ref[...].astype(F32) * _dsilu(z)
        dlg_ref[...] += jnp.sum(dz * xn, axis=0, keepdims=True)
        dlb_ref[...] += jnp.sum(dz, axis=0, keepdims=True)
        dxn = dz * g
        dy = rs * (dxn - jnp.mean(dxn, axis=-1, keepdims=True) - xn * jnp.mean(dxn * xn, axis=-1, keepdims=True))
        dcb_ref[...] += jnp.sum(dy, axis=0, keepdims=True)
        dy_ref[...] = dy

    return pl.pallas_call(
        body, name="conv_bwd_norm", grid=(T // TM,),
        in_specs=[_row(TM, CC), _row(TM, CC), _slab((1, CC), layer), _slab((1, CC), layer)],
        out_specs=[_row(TM, CC), _acc((1, CC)), _acc((1, CC)), _acc((1, CC))],
        out_shape=[jax.ShapeDtypeStruct((T, CC), F32)] + [jax.ShapeDtypeStruct((1, CC), F32)] * 3,
        compiler_params=_cp("arbitrary"),
    )(dco, y, lg, lb)


def _conv_bwd_taps(dy, u, cw, layer):
    T = u.shape[0]
    n_halo = T // HALO

    def body(dy_ref, dyn_ref, u_ref, up_ref, w_ref, du_ref, dw_ref, hext_ref, dext_ref):
        i = pl.program_id(0)

        @pl.when(i == 0)
        def _():
            dw_ref[...] = jnp.zeros_like(dw_ref)

        hext_ref[0:HALO, :] = jnp.where(i > 0, _glu(up_ref[...]), 0.0)
        hext_ref[HALO:, :] = _glu(u_ref[...])
        dy = dy_ref[...]
        dext_ref[0:TM, :] = dy
        dext_ref[TM:, :] = jnp.where(i < pl.num_programs(0) - 1, dyn_ref[...], 0.0)
        dh = jnp.zeros((TM, CC), F32)
        for k in range(CW):
            dh = dh + w_ref[k:k + 1, :] * dext_ref[pl.ds(CW - 1 - k, TM), :]
            dw_ref[k:k + 1, :] += jnp.sum(dy * hext_ref[pl.ds(HALO - (CW - 1) + k, TM), :], axis=0, keepdims=True)
        uu = u_ref[...].astype(F32)
        a = uu[:, :CC]
        sg = _sigmoid(uu[:, CC:])
        du_ref[:, :CC] = (dh * sg).astype(BF16)
        du_ref[:, CC:] = (dh * a * sg * (1.0 - sg)).astype(BF16)

    return pl.pallas_call(
        body, name="conv_bwd_taps", grid=(T // TM,),
        in_specs=[_row(TM, CC),
                  pl.BlockSpec((HALO, CC), lambda i: (jnp.minimum((i + 1) * (TM // HALO), n_halo - 1), 0)),
                  _row(TM, 2 * CC),
                  pl.BlockSpec((HALO, 2 * CC), lambda i: (jnp.maximum(i * (TM // HALO) - 1, 0), 0)),
                  _slab((CW, CC), layer)],
        out_specs=[_row(TM, 2 * CC), _acc((CW, CC))],
        out_shape=[jax.ShapeDtypeStruct((T, 2 * CC), BF16), jax.ShapeDtypeStruct((CW, CC), F32)],
        scratch_shapes=[pltpu.VMEM((TM + HALO, CC), F32), pltpu.VMEM((TM + HALO, CC), F32)],
        compiler_params=_cp("arbitrary"),
    )(dy, dy, u, u, cw)


def _attn_bwd(qkv, dao, sinks, layer):
    T = qkv.shape[0]
    tq = ATT_BLOCKS * BLK

    def body(sink_ref, cur_ref, prev_ref, do_ref, dq_ref, dk_ref, dv_ref, ds_ref):
        i = pl.program_id(0)

        @pl.when(i == 0)
        def _():
            dk_ref[...] = jnp.zeros_like(dk_ref)
            dv_ref[...] = jnp.zeros_like(dv_ref)
            ds_ref[...] = jnp.zeros_like(ds_ref)

        first = _band_mask(i > 0)
        later = _band_mask(True)
        base = pl.multiple_of(i * tq, tq)
        before = pl.multiple_of(jnp.maximum(i * ATT_BLOCKS - 1, 0) * BLK, BLK)
        for b in range(ATT_BLOCKS):
            dqs, dks, dvs = [], [], []
            for kv in range(NKV):
                kb = _band(prev_ref, cur_ref, b, AW + kv * HD)
                vb = _band(prev_ref, cur_ref, b, QKW + kv * HD)
                q4 = _stack_heads(cur_ref, b, kv)
                do4 = _stack_heads(do_ref, b, kv)
                pt, psink = _probs_t(q4, kb, first if b == 0 else later, _sink_row(sink_ref, layer, kv))
                dpt = _nt(vb, do4)
                dd = jnp.sum(pt * dpt, axis=0, keepdims=True)
                dst = (pt * (dpt - dd) * SCALE).astype(BF16)
                sd = psink * dd
                for g in range(GROUP):
                    hh = kv * GROUP + g
                    ds_ref[hh:hh + 1, :] += jnp.full((1, 128), -jnp.sum(sd[:, g * BLK:(g + 1) * BLK]), F32)
                dqs.append(_unstack_t(_nn(kb.T, dst)))
                dks.append(_nn(dst, q4))
                dvs.append(_nn(pt.astype(BF16), do4))
            dq_ref[b * BLK:(b + 1) * BLK, :] = jnp.concatenate(dqs, axis=1).astype(BF16)
            dkband = jnp.concatenate(dks, axis=1)
            dvband = jnp.concatenate(dvs, axis=1)
            if b == 0:
                dk_ref[pl.ds(before, BLK), :] += dkband[:BLK]
                dv_ref[pl.ds(before, BLK), :] += dvband[:BLK]
                dk_ref[pl.ds(base, BLK), :] += dkband[BLK:]
                dv_ref[pl.ds(base, BLK), :] += dvband[BLK:]
            else:
                r0 = pl.multiple_of(base + (b - 1) * BLK, BLK)
                dk_ref[pl.ds(r0, 2 * BLK), :] += dkband
                dv_ref[pl.ds(r0, 2 * BLK), :] += dvband

    return pl.pallas_call(
        body, name="attn_bwd", grid=(T // tq,),
        in_specs=[pl.BlockSpec(memory_space=pltpu.SMEM), _row(tq, QKVW),
                  pl.BlockSpec((BLK, QKVW), lambda i: (jnp.maximum(i * ATT_BLOCKS - 1, 0), 0)), _row(tq, AW)],
        out_specs=[_row(tq, AW), _acc((T, KVW)), _acc((T, KVW)), _acc((NH, 128))],
        out_shape=[jax.ShapeDtypeStruct((T, AW), BF16), jax.ShapeDtypeStruct((T, KVW), F32),
                   jax.ShapeDtypeStruct((T, KVW), F32), jax.ShapeDtypeStruct((NH, 128), F32)],
        compiler_params=_cp("arbitrary"),
    )(sinks, qkv, qkv, dao)


def _mix_in_bwd(dq, dk, dv, du, rc, rs, x, dx, norm, win, layer):
    T = x.shape[0]

    def body(dq_ref, dk_ref, dv_ref, du_ref, c_ref, s_ref, x_ref, dx_ref, g_ref, w_ref, dp_ref, o_ref, dg_ref):
        @pl.when(pl.program_id(0) == 0)
        def _():
            dg_ref[...] = jnp.zeros_like(dg_ref)

        dqk = jnp.concatenate([dq_ref[...].astype(F32), dk_ref[...]], axis=1)
        dqk = _rope(dqk, c_ref[...], -s_ref[...])
        dp = jnp.concatenate([dqk.astype(BF16), dv_ref[...].astype(BF16), du_ref[...]], axis=1)
        dp_ref[...] = dp
        dh = _nn(dp, w_ref[...])
        g = g_ref[...]
        xh, r, _ = _rms(x_ref[...], g)
        dg_ref[...] += jnp.sum(dh * xh, axis=0, keepdims=True)
        o_ref[...] = dx_ref[...] + _rms_bwd(dh, xh, r, g)

    return pl.pallas_call(
        body, name="mix_in_bwd", grid=(T // TM,),
        in_specs=[_row(TM, AW), _row(TM, KVW), _row(TM, KVW), _row(TM, 2 * CC), _row(TM, 128), _row(TM, 128),
                  _row(TM, D), _row(TM, D), _slab((1, D), layer), _slab((DIN, D), 0, True)],
        out_specs=[_row(TM, DIN), _row(TM, D), _acc((1, D))],
        out_shape=[jax.ShapeDtypeStruct((T, DIN), BF16), jax.ShapeDtypeStruct((T, D), F32),
                   jax.ShapeDtypeStruct((1, D), F32)],
        compiler_params=_cp("arbitrary"),
    )(dq, dk, dv, du, rc, rs, x, dx, norm, win)


def _wgrad(buf, slab, a, b):
    T, M = a.shape
    N = b.shape[1]
    tmm = M // 2 if M > 1024 else M

    def body(buf_ref, a_ref, b_ref, o_ref):
        @pl.when(pl.program_id(1) == 0)
        def _():
            o_ref[...] = jnp.zeros_like(o_ref)

        o_ref[...] += _tn(a_ref[...], b_ref[...])

    return pl.pallas_call(
        body, name="wgrad", grid=(M // tmm, T // TK),
        in_specs=[pl.BlockSpec(memory_space=pl.ANY),
                  pl.BlockSpec((TK, tmm), lambda i, k: (k, i)), pl.BlockSpec((TK, N), lambda i, k: (k, 0))],
        out_specs=pl.BlockSpec((None, tmm, N), lambda i, k: (slab, i, 0)),
        out_shape=jax.ShapeDtypeStruct(buf.shape, F32),
        input_output_aliases={0: 0},
        compiler_params=_cp("parallel", "arbitrary"),
    )(buf, a, b)


HBM = pl.BlockSpec(memory_space=pl.ANY)


def _coords():
    return lax.axis_index("x"), lax.axis_index("y"), lax.axis_index("c")


def _other_chips(x, y):
    return [(1 - x, y), (x, 1 - y), (1 - x, 1 - y)]


def _all_gather(shards):
    n = len(shards)

    def body(*refs):
        ins, outs = refs[:n], refs[n:2 * n]
        send_sems, recv_sems, local_sems = refs[2 * n:]
        x, y, c = _coords()
        me, sibling = (x, y, c), (x, y, 1 - c)
        chips = _other_chips(x, y)

        def rows(a, dev):
            r = ins[a].shape[1]
            return outs[a].at[:, pl.ds(pl.multiple_of((4 * dev[0] + 2 * dev[1] + dev[2]) * r, r), r), :]

        def copy(a, k, block, to, src=None):
            return pltpu.make_async_remote_copy(
                src_ref=rows(a, block) if src is None else src, dst_ref=rows(a, block),
                send_sem=send_sems.at[a * 7 + k], recv_sem=recv_sems.at[a * 7 + k],
                device_id=to, device_id_type=MESH)

        mine = [pltpu.make_async_copy(ins[a], rows(a, me), local_sems.at[a]) for a in range(n)]
        for cp in mine:
            cp.start()
        first = []
        for a in range(n):
            first.append(copy(a, 0, me, sibling, src=ins[a]))
            first += [copy(a, 1 + j, me, (*chip, c), src=ins[a]) for j, chip in enumerate(chips)]
        for cp in first:
            cp.start()
        passed = []
        for j, chip in enumerate(chips):
            for a in range(n):
                copy(a, 1 + j, (*chip, c), me).wait_recv()
                fwd = copy(a, 4 + j, (*chip, c), sibling)
                fwd.start()
                passed.append(fwd)
        for a in range(n):
            copy(a, 0, sibling, me).wait_recv()
            for j, chip in enumerate(chips):
                copy(a, 4 + j, (*chip, 1 - c), me).wait_recv()
        for cp in first + passed:
            cp.wait_send()
        for cp in mine:
            cp.wait()

    return pl.pallas_call(
        body, name="all_gather_weights",
        in_specs=[HBM] * n, out_specs=[HBM] * n,
        out_shape=[jax.ShapeDtypeStruct((s.shape[0], N_DEV * s.shape[1], s.shape[2]), s.dtype) for s in shards],
        scratch_shapes=[pltpu.SemaphoreType.DMA((7 * n,)), pltpu.SemaphoreType.DMA((7 * n,)),
                        pltpu.SemaphoreType.DMA((n,))],
    )(*shards)


def _pair_exchange(grads):
    n = len(grads)

    def body(*refs):
        ins, got = refs[:n], refs[n:2 * n]
        send_sems, recv_sems = refs[2 * n:]
        x, y, c = _coords()
        sibling = (x, y, 1 - c)

        def remote(a, q):
            r = ins[a].shape[1] // N_DEV
            src = ins[a].at[:, pl.ds(pl.multiple_of((2 * q + 1 - c) * r, r), r), :]
            return pltpu.make_async_remote_copy(
                src_ref=src, dst_ref=got[a].at[q],
                send_sem=send_sems.at[a * N_CHIP + q], recv_sem=recv_sems.at[a * N_CHIP + q],
                device_id=sibling, device_id_type=MESH)

        sends = [remote(a, q) for a in range(n) for q in range(N_CHIP)]
        for cp in sends:
            cp.start()
        for cp in sends:
            cp.wait_recv()
        for cp in sends:
            cp.wait_send()

    return pl.pallas_call(
        body, name="grad_pair_exchange",
        in_specs=[HBM] * n, out_specs=[HBM] * n,
        out_shape=[jax.ShapeDtypeStruct((N_CHIP, g.shape[0], g.shape[1] // N_DEV, g.shape[2]), g.dtype) for g in grads],
        scratch_shapes=[pltpu.SemaphoreType.DMA((N_CHIP * n,)), pltpu.SemaphoreType.DMA((N_CHIP * n,))],
    )(*grads)


def _chip_exchange(parts):
    n = len(parts)

    def body(*refs):
        ins, outs = refs[:n], refs[n:2 * n]
        send_sems, recv_sems = refs[2 * n:]
        x, y, c = _coords()
        chips = _other_chips(x, y)

        def remote(a, j):
            return pltpu.make_async_remote_copy(
                src_ref=ins[a].at[j], dst_ref=outs[a].at[j],
                send_sem=send_sems.at[a * 3 + j], recv_sem=recv_sems.at[a * 3 + j],
                device_id=(*chips[j], c), device_id_type=MESH)

        sends = [remote(a, j) for a in range(n) for j in range(3)]
        for cp in sends:
            cp.start()
        for cp in sends:
            cp.wait_recv()
        for cp in sends:
            cp.wait_send()

    return pl.pallas_call(
        body, name="grad_chip_exchange",
        in_specs=[HBM] * n, out_specs=[HBM] * n,
        out_shape=[jax.ShapeDtypeStruct(p.shape, p.dtype) for p in parts],
        scratch_shapes=[pltpu.SemaphoreType.DMA((3 * n,)), pltpu.SemaphoreType.DMA((3 * n,))],
    )(*parts)


def _all_reduce_small(pack):
    R = pack.shape[0]

    def body(p_ref, tot_ref, all_ref, send_sems, recv_sems):
        x, y, c = _coords()
        me = 4 * x + 2 * y + c
        all_ref[me] = p_ref[...]
        peers = []
        for k in range(1, N_DEV):
            bx, by, bc = (k >> 2) & 1, (k >> 1) & 1, k & 1
            peers.append((x ^ bx, y ^ by, c ^ bc))

        def copy(k, slot, to):
            return pltpu.make_async_remote_copy(
                src_ref=p_ref, dst_ref=all_ref.at[slot], send_sem=send_sems.at[k], recv_sem=recv_sems.at[k],
                device_id=to, device_id_type=MESH)

        sends = [copy(k, me, peer) for k, peer in enumerate(peers)]
        for cp in sends:
            cp.start()
        for k, peer in enumerate(peers):
            copy(k, 4 * peer[0] + 2 * peer[1] + peer[2], peer).wait_recv()
        for cp in sends:
            cp.wait_send()
        tot = all_ref[0]
        for d in range(1, N_DEV):
            tot = tot + all_ref[d]
        tot_ref[...] = tot

    vmem = pl.BlockSpec(memory_space=pltpu.VMEM)
    return pl.pallas_call(
        body, name="all_reduce_small",
        in_specs=[vmem], out_specs=vmem,
        out_shape=jax.ShapeDtypeStruct((R, 128), F32),
        scratch_shapes=[pltpu.VMEM((N_DEV, R, 128), F32), pltpu.SemaphoreType.DMA((N_DEV - 1,)),
                        pltpu.SemaphoreType.DMA((N_DEV - 1,))],
    )(pack)


HBM_ONLY = pl.BlockSpec(memory_space=pltpu.HBM)
SEM = pl.BlockSpec(memory_space=pltpu.SEMAPHORE)
DATAFLOW = pltpu.SideEffectType.DATAFLOW_SIDE_EFFECTING


def _peers(x, y, c):
    return [(x ^ ((k >> 2) & 1), y ^ ((k >> 1) & 1), c ^ (k & 1)) for k in range(1, N_DEV)]


def _gather_plan(ins, lands):
    x, y, c = _coords()

    def rows(a, dev):
        r = ins[a].shape[1]
        return lands[a].at[:, pl.ds(pl.multiple_of((4 * dev[0] + 2 * dev[1] + dev[2]) * r, r), r), :]

    return [(ins[a], rows(a, (x, y, c)), peer, rows(a, peer)) for a in range(len(ins)) for peer in _peers(x, y, c)]


def _chip_plan(ins, lands):
    x, y, c = _coords()
    chips = _other_chips(x, y)
    return [(ins[a].at[j], lands[a].at[j], (*chips[j], c), lands[a].at[j]) for a in range(len(ins)) for j in range(3)]


def _exchange_start(name, plan, copies_per_array, srcs, lands, after):
    n = len(srcs)
    count = copies_per_array * n

    def body(*refs):
        ins, land = refs[:n], refs[n:2 * n]
        send_sems, recv_sems = refs[2 * n + 1], refs[2 * n + 2]
        token = refs[-1]
        for k, (src, dst, peer, _) in enumerate(plan(ins, land)):
            pltpu.make_async_remote_copy(src_ref=src, dst_ref=dst, send_sem=send_sems.at[k], recv_sem=recv_sems.at[k],
                                         device_id=peer, device_id_type=MESH).start()
        token[...] = jnp.zeros_like(token)

    thru = [pltpu.HBM(v.shape, v.dtype) for v in list(srcs) + list(lands)]
    outs = pl.pallas_call(
        body, name=name,
        in_specs=[HBM_ONLY] * (2 * n) + [HBM],
        out_specs=[SEM, SEM] + [HBM_ONLY] * (2 * n) + [pl.BlockSpec(memory_space=pltpu.VMEM)],
        out_shape=[pltpu.SemaphoreType.DMA((count,)), pltpu.SemaphoreType.DMA((count,))] + thru
        + [jax.ShapeDtypeStruct((8, 128), F32)],
        input_output_aliases={i: 2 + i for i in range(2 * n)},
        compiler_params=pltpu.CompilerParams(has_side_effects=DATAFLOW),
    )(*[pltpu.with_memory_space_constraint(v, pltpu.HBM) for v in list(srcs) + list(lands)], after)
    return outs[0], outs[1], outs[2:2 + n], outs[2 + n:2 + 2 * n], outs[-1]


def _exchange_wait(name, plan, send_sems, recv_sems, srcs, lands, after):
    n = len(srcs)

    def body(*refs):
        ins, land = refs[:n], refs[n:2 * n]
        send, recv = refs[2 * n], refs[2 * n + 1]
        for k, (src, _, peer, here) in enumerate(plan(ins, land)):
            cp = pltpu.make_async_remote_copy(src_ref=src, dst_ref=here, send_sem=send.at[k], recv_sem=recv.at[k],
                                              device_id=peer, device_id_type=MESH)
            cp.wait_send()
            cp.wait_recv()

    thru = [pltpu.HBM(v.shape, v.dtype) for v in list(srcs) + list(lands)]
    outs = pl.pallas_call(
        body, name=name,
        in_specs=[HBM_ONLY] * (2 * n) + [SEM, SEM, pl.BlockSpec(memory_space=pl.ANY)],
        out_specs=[HBM_ONLY] * (2 * n),
        out_shape=thru,
        input_output_aliases={i: i for i in range(2 * n)},
        compiler_params=pltpu.CompilerParams(has_side_effects=DATAFLOW),
    )(*srcs, *lands, send_sems, recv_sems, after)
    return outs[n:]


def _place_own(shard, dev):
    s, r, c = shard.shape

    def body(dev_ref, i_ref, o_ref):
        o_ref[...] = i_ref[...]

    return pl.pallas_call(
        body, name="place_own_shard",
        grid_spec=pltpu.PrefetchScalarGridSpec(
            num_scalar_prefetch=1, grid=(s,),
            in_specs=[pl.BlockSpec((None, r, c), lambda i, d: (i, 0, 0))],
            out_specs=pl.BlockSpec((None, r, c), lambda i, d: (i, d[0], 0))),
        out_shape=jax.ShapeDtypeStruct((s, N_DEV * r, c), shard.dtype),
        compiler_params=_cp("arbitrary"),
    )(dev, shard)


def _tile_rows(n, cap=512):
    t = min(n, cap)
    while n % t or t % 8:
        t -= 1
        if t < 8:
            return n
    return t


def _pair_sum(g, got, owner_dev, owner_chip, dtype):
    s, r8, c = g.shape
    r = r8 // N_DEV
    n = owner_dev.shape[0]

    def body(dev_ref, chip_ref, g_ref, got_ref, o_ref):
        o_ref[...] = (g_ref[...] + got_ref[...]).astype(dtype)

    return pl.pallas_call(
        body, name="pair_sum",
        grid_spec=pltpu.PrefetchScalarGridSpec(
            num_scalar_prefetch=2, grid=(n, s),
            in_specs=[pl.BlockSpec((None, r, c), lambda j, i, dev, chip: (i, dev[j], 0)),
                      pl.BlockSpec((None, None, r, c), lambda j, i, dev, chip: (chip[j], i, 0, 0))],
            out_specs=pl.BlockSpec((None, None, r, c), lambda j, i, dev, chip: (j, i, 0, 0))),
        out_shape=jax.ShapeDtypeStruct((n, s, r, c), dtype),
        compiler_params=_cp("parallel", "parallel"),
    )(owner_dev, owner_chip, g, got)


def _sum_chips(own, parts):
    _, s, r, c = parts.shape

    def body(o, p0, p1, p2, o_ref):
        o_ref[...] = ((o[...] + p0[...].astype(F32)) + p1[...].astype(F32)) + p2[...].astype(F32)

    def spec(q):
        return pl.BlockSpec((None, None, r, c), lambda i: (q, i, 0, 0))

    return pl.pallas_call(
        body, name="chip_sum", grid=(s,),
        in_specs=[spec(0), spec(0), spec(1), spec(2)], out_specs=pl.BlockSpec((None, r, c), lambda i: (i, 0, 0)),
        out_shape=jax.ShapeDtypeStruct((s, r, c), F32),
        compiler_params=_cp("parallel"),
    )(own, parts, parts, parts)


def _adamw(w, g, m, v):
    shape = w.shape
    c = shape[-1] if w.ndim > 1 else w.shape[0]
    args = [t.reshape(-1, c) for t in (w, g, m, v)]
    n = args[0].shape[0]
    tr = _tile_rows(n)

    def body(w_ref, g_ref, m_ref, v_ref, d_ref, mo_ref, vo_ref):
        g = g_ref[...]
        m = B1 * m_ref[...] + (1.0 - B1) * g
        v = B2 * v_ref[...] + (1.0 - B2) * jnp.square(g)
        m_hat = m / (1.0 - B1 ** STEP)
        v_hat = v / (1.0 - B2 ** STEP)
        d_ref[...] = -LR * (m_hat / (jnp.sqrt(v_hat) + ADAM_EPS) + WD * w_ref[...])
        mo_ref[...] = m
        vo_ref[...] = v

    outs = pl.pallas_call(
        body, name="adamw", grid=(n // tr,),
        in_specs=[_row(tr, c)] * 4, out_specs=[_row(tr, c)] * 3,
        out_shape=[jax.ShapeDtypeStruct((n, c), F32)] * 3,
        compiler_params=_cp("parallel"),
    )(*args)
    return [o.reshape(shape) for o in outs]


def _pack(pieces):
    flat = []
    for p in pieces:
        f = p.reshape(-1)
        flat.append(jnp.pad(f, (0, (-f.shape[0]) % 1024)))
    return jnp.concatenate(flat).reshape(-1, 128)


def _unpack(pack, shapes):
    flat = pack.reshape(-1)
    out, off = [], 0
    for s in shapes:
        size = 1
        for d in s:
            size *= d
        out.append(flat[off:off + size].reshape(s))
        off += size + (-size) % 1024
    return out


def kernel(x, positions, ffn1_norm, ffn1_w_gate, ffn1_w_up, ffn1_w_down, mix_norm, w_in, conv_w, conv_b, conv_ln_g, conv_ln_b, attn_sinks, w_out, ffn2_norm, ffn2_w_gate, ffn2_w_up, ffn2_w_down, final_norm, loss_target, m_ffn1_norm, m_ffn1_w_gate, m_ffn1_w_up, m_ffn1_w_down, m_mix_norm, m_w_in, m_conv_w, m_conv_b, m_conv_ln_g, m_conv_ln_b, m_attn_sinks, m_w_out, m_ffn2_norm, m_ffn2_w_gate, m_ffn2_w_up, m_ffn2_w_down, m_final_norm, v_ffn1_norm, v_ffn1_w_gate, v_ffn1_w_up, v_ffn1_w_down, v_mix_norm, v_w_in, v_conv_w, v_conv_b, v_conv_ln_g, v_conv_ln_b, v_attn_sinks, v_w_out, v_ffn2_norm, v_ffn2_w_gate, v_ffn2_w_up, v_ffn2_w_down, v_final_norm):
    L = ffn1_norm.shape[0]
    T = x.shape[1]
    x0 = x.reshape(T, D)
    target = loss_target.reshape(T, D)
    dev = 4 * lax.axis_index("x") + 2 * lax.axis_index("y") + lax.axis_index("c")

    def t_(w):
        return jnp.swapaxes(w, 1, 2)

    def ffn_shards(gate, up, down, l):
        return jnp.stack([t_(gate)[l], t_(up)[l], down[l]]).astype(BF16)

    shards = [[ffn_shards(ffn1_w_gate, ffn1_w_up, ffn1_w_down, l), ffn_shards(ffn2_w_gate, ffn2_w_up, ffn2_w_down, l),
               t_(w_in)[l:l + 1].astype(BF16), w_out[l:l + 1].astype(BF16)] for l in range(L)]
    cw_cols = CC // N_DEV
    cw_sh = jnp.pad(conv_w.reshape(-1), (0, (-L * CW * cw_cols) % 1024)).reshape(1, -1, 128)
    dev1 = dev.reshape(1).astype(jnp.int32)
    no_token = jnp.zeros((8, 128), F32)

    def gather_start(name, parts, after):
        lands = [_place_own(s, dev1) for s in parts]
        *handles, token = _exchange_start(name, _gather_plan, N_DEV - 1, parts, lands, after)
        return handles, token

    wffn1_0, cw_all = _all_gather([shards[0][0], cw_sh])
    rest0, token = gather_start("gather_start_0", shards[0][1:], cw_all)
    weights = [None] * L

    cw_rows = cw_sh.shape[1]
    cw_full = cw_all.reshape(N_DEV, cw_rows * 128)[:, :L * CW * cw_cols].reshape(N_DEV, L, CW, cw_cols)
    cw_full = jnp.transpose(cw_full, (1, 2, 0, 3)).reshape(L, CW, CC)

    inv_freq = 1.0 / (10000.0 ** (jnp.arange(0, HD, 2, dtype=F32) / HD))
    ang = positions.reshape(T).astype(F32)[:, None] * inv_freq
    cos, sin = jnp.cos(ang), jnp.sin(ang)
    rc = jnp.concatenate([cos, cos, cos, cos], axis=1)
    rs = jnp.concatenate([-sin, sin, -sin, sin], axis=1)

    n1 = ffn1_norm.reshape(L, 1, D)
    nm = mix_norm.reshape(L, 1, D)
    n2 = ffn2_norm.reshape(L, 1, D)
    cb = conv_b.reshape(L, 1, CC)
    lg = conv_ln_g.reshape(L, 1, CC)
    lb = conv_ln_b.reshape(L, 1, CC)

    saved = []
    xa = x0
    for l in range(L):
        if l == 0:
            wffn1 = wffn1_0
        else:
            wffn1, wffn2, win, wout = _exchange_wait(f"gather_wait_{l}", _gather_plan, *pending, after=xa)
        h1, G1, U1, A1 = _ffn_up(xa, n1, wffn1, 0, 1, l, token)
        xb = _ffn_down(A1, xa, wffn1, 2)
        if l == 0:
            wffn2, win, wout = _exchange_wait("gather_wait_0", _gather_plan, *rest0, after=xb)
        weights[l] = (wffn1, wffn2, win, wout)
        token = no_token
        if l + 1 < L:
            pending, token = gather_start(f"gather_start_{l + 1}", shards[l + 1], win)
        hm, qkv, u = _mix_in(xb, nm, win, rc, rs, l, token)
        ao = _attn_fwd(qkv, attn_sinks, l)
        y, co = _conv_fwd(u, cw_full, cb, lg, lb, l)
        xc, cat = _mix_out(ao, co, xb, wout)
        h2, G2, U2, A2 = _ffn_up(xc, n2, wffn2, 0, 1, l, no_token)
        xd = _ffn_down(A2, xc, wffn2, 2)
        saved.append((xa, h1, G1, U1, A1, xb, hm, qkv, u, y, cat, xc, h2, G2, U2, A2))
        xa = xd
        token = no_token

    dx, loss_part, g_final = _final(xa, final_norm.reshape(1, D), target)

    cx, cy, cc = _coords()
    chip_of = [2 * cx + cy] + [2 * px + py for px, py in _other_chips(cx, cy)]
    own_chip = jnp.stack(chip_of[:1]).astype(jnp.int32)
    other_chips = jnp.stack(chip_of[1:]).astype(jnp.int32)

    def pair_sums(group):
        got = _pair_exchange(group)
        own = [_pair_sum(g, r, 2 * own_chip + cc, own_chip, F32) for g, r in zip(group, got)]
        sent = [_pair_sum(g, r, 2 * other_chips + cc, other_chips, BF16) for g, r in zip(group, got)]
        return own, sent

    g_n1, g_nm, g_n2 = [None] * L, [None] * L, [None] * L
    g_cb, g_lg, g_lb, g_sink, g_cw = [None] * L, [None] * L, [None] * L, [None] * L, [None] * L
    in_flight, reduced = [], {}
    for l in reversed(range(L)):
        xa, h1, G1, U1, A1, xb, hm, qkv, u, y, cat, xc, h2, G2, U2, A2 = saved[l]
        wffn1, wffn2, win, wout = weights[l]
        gffn1 = lax.empty((3, FF, D), F32)
        gffn2 = lax.empty((3, FF, D), F32)
        gin = lax.empty((1, DIN, D), F32)
        gout = lax.empty((1, D, D), F32)
        d, dG, dU = _ffn_bwd_act(dx, G2, U2, wffn2, 2, token)
        gffn2 = _wgrad(gffn2, 2, A2, d)
        gffn2 = _wgrad(gffn2, 0, dG, h2)
        gffn2 = _wgrad(gffn2, 1, dU, h2)
        dx, g_n2[l] = _ffn_bwd_in(dG, dU, xc, dx, n2, wffn2, 0, 1, l)
        d, dao, dco = _mix_out_bwd(dx, wout)
        gout = _wgrad(gout, 0, cat, d)
        dy, g_lg[l], g_lb[l], g_cb[l] = _conv_bwd_norm(dco, y, lg, lb, l)
        du, g_cw[l] = _conv_bwd_taps(dy, u, cw_full, l)
        dq, dk, dv, g_sink[l] = _attn_bwd(qkv, dao, attn_sinks, l)
        dp, dx, g_nm[l] = _mix_in_bwd(dq, dk, dv, du, rc, rs, xb, dx, nm, win, l)
        gin = _wgrad(gin, 0, dp, hm)
        own, sent = pair_sums([gffn2, gin, gout])
        *handles, token = _exchange_start(f"chip_start_{l}a", _chip_plan, 3, sent,
                                          [lax.empty(p.shape, p.dtype) for p in sent], dx)
        in_flight.append(((l, "a"), own, handles))
        d, dG, dU = _ffn_bwd_act(dx, G1, U1, wffn1, 2, token)
        gffn1 = _wgrad(gffn1, 2, A1, d)
        gffn1 = _wgrad(gffn1, 0, dG, h1)
        gffn1 = _wgrad(gffn1, 1, dU, h1)
        dx, g_n1[l] = _ffn_bwd_in(dG, dU, xa, dx, n1, wffn1, 0, 1, l)
        own, sent = pair_sums([gffn1])
        if l > 0:
            *handles, token = _exchange_start(f"chip_start_{l}b", _chip_plan, 3, sent,
                                              [lax.empty(p.shape, p.dtype) for p in sent], dx)
            in_flight.append(((l, "b"), own, handles))
        else:
            reduced[(l, "b")] = [_sum_chips(o, p) for o, p in zip(own, _chip_exchange(sent))]

    grad_x = dx.reshape(1, T, D)
    for (l, part), own, handles in in_flight:
        parts = _exchange_wait(f"chip_wait_{l}{part}", _chip_plan, *handles, after=dx)
        reduced[(l, part)] = [_sum_chips(o, p) for o, p in zip(own, parts)]
    g1 = jnp.stack([reduced[(l, "b")][0] for l in range(L)])
    g2 = jnp.stack([reduced[(l, "a")][0] for l in range(L)])
    gin_sh = jnp.concatenate([reduced[(l, "a")][1] for l in range(L)])
    gout_sh = jnp.concatenate([reduced[(l, "a")][2] for l in range(L)])

    small = [loss_part,
             jnp.concatenate(g_n1), jnp.concatenate(g_nm), jnp.concatenate(g_n2), g_final,
             jnp.concatenate(g_cb), jnp.concatenate(g_lg), jnp.concatenate(g_lb),
             jnp.stack(g_sink)[:, :, 0], jnp.stack(g_cw)]
    small_shapes = [(1, 128), (L, D), (L, D), (L, D), (D,), (L, CC), (L, CC), (L, CC), (L, NH), (L, CW, CC)]
    tot = _unpack(_all_reduce_small(_pack(small)), small_shapes)
    loss = tot[0][0, 0]
    gr_n1, gr_nm, gr_n2, gr_final, gr_cb, gr_lg, gr_lb, gr_sink, gr_cw_full = tot[1:]
    gr_cw = lax.dynamic_slice_in_dim(gr_cw_full, dev * cw_cols, cw_cols, axis=2)

    grads = {
        "ffn1_norm": gr_n1, "ffn1_w_gate": t_(g1[:, 0]), "ffn1_w_up": t_(g1[:, 1]), "ffn1_w_down": g1[:, 2],
        "mix_norm": gr_nm, "w_in": t_(gin_sh), "conv_w": gr_cw, "conv_b": gr_cb, "conv_ln_g": gr_lg,
        "conv_ln_b": gr_lb, "attn_sinks": gr_sink, "w_out": gout_sh,
        "ffn2_norm": gr_n2, "ffn2_w_gate": t_(g2[:, 0]), "ffn2_w_up": t_(g2[:, 1]), "ffn2_w_down": g2[:, 2],
        "final_norm": gr_final,
    }
    weights = dict(ffn1_norm=ffn1_norm, ffn1_w_gate=ffn1_w_gate, ffn1_w_up=ffn1_w_up, ffn1_w_down=ffn1_w_down, mix_norm=mix_norm, w_in=w_in, conv_w=conv_w, conv_b=conv_b, conv_ln_g=conv_ln_g, conv_ln_b=conv_ln_b, attn_sinks=attn_sinks, w_out=w_out, ffn2_norm=ffn2_norm, ffn2_w_gate=ffn2_w_gate, ffn2_w_up=ffn2_w_up, ffn2_w_down=ffn2_w_down, final_norm=final_norm)
    moms = dict(ffn1_norm=m_ffn1_norm, ffn1_w_gate=m_ffn1_w_gate, ffn1_w_up=m_ffn1_w_up, ffn1_w_down=m_ffn1_w_down, mix_norm=m_mix_norm, w_in=m_w_in, conv_w=m_conv_w, conv_b=m_conv_b, conv_ln_g=m_conv_ln_g, conv_ln_b=m_conv_ln_b, attn_sinks=m_attn_sinks, w_out=m_w_out, ffn2_norm=m_ffn2_norm, ffn2_w_gate=m_ffn2_w_gate, ffn2_w_up=m_ffn2_w_up, ffn2_w_down=m_ffn2_w_down, final_norm=m_final_norm)
    vels = dict(ffn1_norm=v_ffn1_norm, ffn1_w_gate=v_ffn1_w_gate, ffn1_w_up=v_ffn1_w_up, ffn1_w_down=v_ffn1_w_down, mix_norm=v_mix_norm, w_in=v_w_in, conv_w=v_conv_w, conv_b=v_conv_b, conv_ln_g=v_conv_ln_g, conv_ln_b=v_conv_ln_b, attn_sinks=v_attn_sinks, w_out=v_w_out, ffn2_norm=v_ffn2_norm, ffn2_w_gate=v_ffn2_w_gate, ffn2_w_up=v_ffn2_w_up, ffn2_w_down=v_ffn2_w_down, final_norm=v_final_norm)

    names = list(weights)
    big = ("ffn1_w_gate", "ffn1_w_up", "ffn1_w_down", "w_in", "w_out", "ffn2_w_gate", "ffn2_w_up", "ffn2_w_down")
    delta, new_m, new_v = {}, {}, {}
    for k in big:
        delta[k], new_m[k], new_v[k] = _adamw(weights[k], grads[k], moms[k], vels[k])
    rest = [k for k in names if k not in big]
    rest_shapes = [weights[k].shape for k in rest]
    packed = _adamw(*[_pack([t[k] for k in rest]) for t in (weights, grads, moms, vels)])
    for res, packed_out in zip((delta, new_m, new_v), packed):
        for k, val in zip(rest, _unpack(packed_out, rest_shapes)):
            res[k] = val

    return (loss, grad_x, *[grads[k] for k in names], *[delta[k] for k in names],
            *[new_m[k] for k in names], *[new_v[k] for k in names])
```

```python
import functools

import jax
import jax.numpy as jnp
from jax import lax
from jax.experimental import pallas as pl
from jax.experimental.pallas import tpu as pltpu

F32 = jnp.float32
BF16 = jnp.bfloat16
MESH = pl.DeviceIdType.MESH

N_DEV = 8
N_CHIP = 4
D = 1024
FF = 2816
HD = 64
NH = 8
NKV = 2
GROUP = NH // NKV
AW = NH * HD
KVW = NKV * HD
QKW = AW + KVW
QKVW = AW + 2 * KVW
CC = 512
CW = 31
DIN = QKVW + 2 * CC
BLK = 128
HALO = 32
EPS = 1e-5
SCALE = HD ** -0.5
NEG = float(jnp.finfo(jnp.float32).min)

LR, B1, B2, ADAM_EPS, WD, STEP = 0.001, 0.9, 0.999, 1e-08, 0.01, 10

TM = 256
TK = 1024
FC = 256
ATT_BLOCKS = 4
VMEM_LIMIT = 56 * 1024 * 1024


def _cp(*sem):
    return pltpu.CompilerParams(dimension_semantics=sem, vmem_limit_bytes=VMEM_LIMIT)


def _row(tm, c):
    return pl.BlockSpec((tm, c), lambda i: (i, 0))


def _slab(shape, k, single=False):
    zeros = (0,) * len(shape)
    kw = dict(pipeline_mode=pl.Buffered(1)) if single else {}
    return pl.BlockSpec((None, *shape), lambda i: (k, *zeros), **kw)


def _acc(shape):
    return pl.BlockSpec(shape, lambda i: (0,) * len(shape))


def _nt(a, b):
    return lax.dot_general(a, b, (((1,), (1,)), ((), ())), preferred_element_type=F32)


def _tn(a, b):
    return lax.dot_general(a, b, (((0,), (0,)), ((), ())), preferred_element_type=F32)


def _nn(a, b):
    return jnp.dot(a, b, preferred_element_type=F32)


def _sigmoid(x):
    return jax.nn.sigmoid(x)


def _dsilu(z):
    s = _sigmoid(z)
    return s * (1.0 + z * (1.0 - s))


def _rms(x, g):
    r = lax.rsqrt(jnp.mean(x * x, axis=-1, keepdims=True) + EPS)
    xh = x * r
    return xh, r, xh * g


def _rms_bwd(dh, xh, r, g):
    dxh = dh * g
    return r * (dxh - xh * jnp.mean(dxh * xh, axis=-1, keepdims=True))


def _rope(t, c128, s128):
    w = t.shape[1]
    lane = lax.broadcasted_iota(jnp.int32, t.shape, 1)
    rot = jnp.where(lane % HD < HD // 2, pltpu.roll(t, w - HD // 2, 1), pltpu.roll(t, HD // 2, 1))
    return t * jnp.tile(c128, (1, w // 128)) + rot * jnp.tile(s128, (1, w // 128))


def _ffn_up(x, norm, wffn, sg, su, layer, token):
    T = x.shape[0]

    def body(x_ref, g_ref, wg_ref, wu_ref, token_ref, h_ref, P_ref, Q_ref, A_ref):
        _, _, hn = _rms(x_ref[...], g_ref[...])
        h = hn.astype(BF16)
        h_ref[...] = h
        for c in range(FF // FC):
            sl = slice(c * FC, (c + 1) * FC)
            g = _nt(h, wg_ref[sl, :])
            u = _nt(h, wu_ref[sl, :])
            s = _sigmoid(g)
            p = g * s
            P_ref[:, sl] = p.astype(BF16)
            Q_ref[:, sl] = (u * (s + p - p * s)).astype(BF16)
            A_ref[:, sl] = (p * u).astype(BF16)

    return pl.pallas_call(
        body, name="ffn_up", grid=(T // TM,),
        in_specs=[_row(TM, D), _slab((1, D), layer), _slab((FF, D), sg, True), _slab((FF, D), su, True), HBM],
        out_specs=[_row(TM, D), _row(TM, FF), _row(TM, FF), _row(TM, FF)],
        out_shape=[jax.ShapeDtypeStruct((T, D), BF16)] + [jax.ShapeDtypeStruct((T, FF), BF16)] * 3,
        compiler_params=_cp("parallel"),
    )(x, norm, wffn, wffn, token)


def _ffn_down(a, x, wffn, sd):
    T = x.shape[0]

    def body(a_ref, x_ref, w_ref, o_ref):
        o_ref[...] = x_ref[...] + 0.5 * _nn(a_ref[...], w_ref[...])

    return pl.pallas_call(
        body, name="ffn_down", grid=(T // TM,),
        in_specs=[_row(TM, FF), _row(TM, D), _slab((FF, D), sd, True)],
        out_specs=_row(TM, D),
        out_shape=jax.ShapeDtypeStruct((T, D), F32),
        compiler_params=_cp("parallel"),
    )(a, x, wffn)


def _mix_in(x, norm, win, rc, rs, layer, token):
    T = x.shape[0]

    def body(x_ref, g_ref, w_ref, c_ref, s_ref, token_ref, h_ref, qkv_ref, u_ref):
        _, _, hn = _rms(x_ref[...], g_ref[...])
        h = hn.astype(BF16)
        h_ref[...] = h
        qk = _nt(h, w_ref[0:QKW, :])
        qkv_ref[:, 0:QKW] = _rope(qk, c_ref[...], s_ref[...]).astype(BF16)
        qkv_ref[:, QKW:QKVW] = _nt(h, w_ref[QKW:QKVW, :]).astype(BF16)
        for c in range(2 * CC // FC):
            u_ref[:, c * FC:(c + 1) * FC] = _nt(h, w_ref[QKVW + c * FC:QKVW + (c + 1) * FC, :]).astype(BF16)

    return pl.pallas_call(
        body, name="mix_in", grid=(T // TM,),
        in_specs=[_row(TM, D), _slab((1, D), layer), _slab((DIN, D), 0, True), _row(TM, 128), _row(TM, 128), HBM],
        out_specs=[_row(TM, D), _row(TM, QKVW), _row(TM, 2 * CC)],
        out_shape=[jax.ShapeDtypeStruct((T, D), BF16), jax.ShapeDtypeStruct((T, QKVW), BF16),
                   jax.ShapeDtypeStruct((T, 2 * CC), BF16)],
        compiler_params=_cp("parallel"),
    )(x, norm, win, rc, rs, token)


def _band_mask(has_prev):
    j = lax.broadcasted_iota(jnp.int32, (2 * BLK, BLK), 0)
    r = lax.broadcasted_iota(jnp.int32, (2 * BLK, BLK), 1) + BLK
    rel = r - j
    return jnp.tile((rel >= 0) & (rel < BLK) & (has_prev | (j >= BLK)), (1, GROUP))


def _band(prev_ref, cur_ref, b, col):
    if b == 0:
        return jnp.concatenate([prev_ref[:, col:col + HD], cur_ref[0:BLK, col:col + HD]], axis=0)
    return cur_ref[(b - 1) * BLK:(b + 1) * BLK, col:col + HD]


def _stack_heads(ref, b, kv):
    cols = [(kv * GROUP + g) * HD for g in range(GROUP)]
    return jnp.concatenate([ref[b * BLK:(b + 1) * BLK, c:c + HD] for c in cols], axis=0)


def _unstack_t(xt):
    x = xt.T
    return jnp.concatenate([x[g * BLK:(g + 1) * BLK, :] for g in range(GROUP)], axis=1)


def _sink_row(sink_ref, layer, kv):
    return jnp.concatenate([jnp.full((1, BLK), sink_ref[layer, kv * GROUP + g], F32) for g in range(GROUP)], axis=1)


def _probs_t(q4, kb, mask, sink):
    s = jnp.where(mask, _nt(kb, q4) * SCALE, NEG)
    m = jnp.maximum(jnp.max(s, axis=0, keepdims=True), sink)
    p = jnp.exp(s - m)
    e = jnp.exp(sink - m)
    inv = 1.0 / (jnp.sum(p, axis=0, keepdims=True) + e)
    return p * inv, e * inv


def _attn_fwd(qkv, sinks, layer):
    T = qkv.shape[0]
    tq = ATT_BLOCKS * BLK

    def body(sink_ref, cur_ref, prev_ref, o_ref):
        first = _band_mask(pl.program_id(0) > 0)
        later = _band_mask(True)
        for b in range(ATT_BLOCKS):
            outs = []
            for kv in range(NKV):
                kb = _band(prev_ref, cur_ref, b, AW + kv * HD)
                vb = _band(prev_ref, cur_ref, b, QKW + kv * HD)
                pt, _ = _probs_t(_stack_heads(cur_ref, b, kv), kb, first if b == 0 else later,
                                 _sink_row(sink_ref, layer, kv))
                outs.append(_unstack_t(_nn(vb.T, pt.astype(BF16))))
            o_ref[b * BLK:(b + 1) * BLK, :] = jnp.concatenate(outs, axis=1).astype(BF16)

    return pl.pallas_call(
        body, name="attn_fwd", grid=(T // tq,),
        in_specs=[pl.BlockSpec(memory_space=pltpu.SMEM), _row(tq, QKVW),
                  pl.BlockSpec((BLK, QKVW), lambda i: (jnp.maximum(i * ATT_BLOCKS - 1, 0), 0))],
        out_specs=_row(tq, AW),
        out_shape=jax.ShapeDtypeStruct((T, AW), BF16),
        compiler_params=_cp("parallel"),
    )(sinks, qkv, qkv)


def _glu(u):
    u = u.astype(F32)
    return u[:, :CC] * _sigmoid(u[:, CC:])


def _conv_fwd(u, cw, cb, lg, lb, layer):
    T = u.shape[0]

    def body(u_ref, up_ref, w_ref, b_ref, g_ref, bb_ref, y_ref, o_ref, ext_ref):
        i = pl.program_id(0)
        ext_ref[0:HALO, :] = jnp.where(i > 0, _glu(up_ref[...]), 0.0)
        ext_ref[HALO:, :] = _glu(u_ref[...])
        acc = jnp.zeros((TM, CC), F32)
        for k in range(CW):
            acc = acc + w_ref[k:k + 1, :] * ext_ref[pl.ds(HALO - (CW - 1) + k, TM), :]
        y = acc + b_ref[...]
        y_ref[...] = y
        xc = y - jnp.mean(y, axis=-1, keepdims=True)
        z = xc * lax.rsqrt(jnp.mean(xc * xc, axis=-1, keepdims=True) + EPS) * g_ref[...] + bb_ref[...]
        o_ref[...] = (z * _sigmoid(z)).astype(BF16)

    return pl.pallas_call(
        body, name="conv_fwd", grid=(T // TM,),
        in_specs=[_row(TM, 2 * CC),
                  pl.BlockSpec((HALO, 2 * CC), lambda i: (jnp.maximum(i * (TM // HALO) - 1, 0), 0)),
                  _slab((CW, CC), layer), _slab((1, CC), layer), _slab((1, CC), layer), _slab((1, CC), layer)],
        out_specs=[_row(TM, CC), _row(TM, CC)],
        out_shape=[jax.ShapeDtypeStruct((T, CC), F32), jax.ShapeDtypeStruct((T, CC), BF16)],
        scratch_shapes=[pltpu.VMEM((TM + HALO, CC), F32)],
        compiler_params=_cp("parallel"),
    )(u, u, cw, cb, lg, lb)


def _mix_out(ao, co, x, wout):
    T = x.shape[0]

    def body(ao_ref, co_ref, x_ref, w_ref, o_ref, cat_ref):
        cat = jnp.concatenate([ao_ref[...], co_ref[...]], axis=1)
        cat_ref[...] = cat
        o_ref[...] = x_ref[...] + _nn(cat, w_ref[...])

    return pl.pallas_call(
        body, name="mix_out", grid=(T // TM,),
        in_specs=[_row(TM, AW), _row(TM, CC), _row(TM, D), _slab((D, D), 0, True)],
        out_specs=[_row(TM, D), _row(TM, D)],
        out_shape=[jax.ShapeDtypeStruct((T, D), F32), jax.ShapeDtypeStruct((T, D), BF16)],
        compiler_params=_cp("parallel"),
    )(ao, co, x, wout)


def _final(x, norm, target):
    T = x.shape[0]

    def body(x_ref, g_ref, t_ref, dx_ref, loss_ref, dg_ref):
        @pl.when(pl.program_id(0) == 0)
        def _():
            loss_ref[...] = jnp.zeros_like(loss_ref)
            dg_ref[...] = jnp.zeros_like(dg_ref)

        g = g_ref[...]
        xh, r, y = _rms(x_ref[...], g)
        err = y - t_ref[...]
        loss_ref[...] += jnp.full(loss_ref.shape, (0.5 / D) * jnp.sum(err * err), F32)
        dy = err * (1.0 / D)
        dg_ref[...] += jnp.sum(dy * xh, axis=0, keepdims=True)
        dx_ref[...] = _rms_bwd(dy, xh, r, g)

    return pl.pallas_call(
        body, name="final_loss", grid=(T // TM,),
        in_specs=[_row(TM, D), _acc((1, D)), _row(TM, D)],
        out_specs=[_row(TM, D), _acc((1, 128)), _acc((1, D))],
        out_shape=[jax.ShapeDtypeStruct((T, D), F32), jax.ShapeDtypeStruct((1, 128), F32),
                   jax.ShapeDtypeStruct((1, D), F32)],
        compiler_params=_cp("arbitrary"),
    )(x, norm, target)


def _ffn_bwd_act(dx, P, Q, wffn, sd, token):
    T = dx.shape[0]

    def body(dx_ref, P_ref, Q_ref, w_ref, token_ref, d_ref, dG_ref, dU_ref):
        d = (0.5 * dx_ref[...]).astype(BF16)
        d_ref[...] = d
        for c in range(FF // FC):
            sl = slice(c * FC, (c + 1) * FC)
            da = _nt(d, w_ref[sl, :])
            dU_ref[:, sl] = (da * P_ref[:, sl].astype(F32)).astype(BF16)
            dG_ref[:, sl] = (da * Q_ref[:, sl].astype(F32)).astype(BF16)

    return pl.pallas_call(
        body, name="ffn_bwd_act", grid=(T // TM,),
        in_specs=[_row(TM, D), _row(TM, FF), _row(TM, FF), _slab((FF, D), sd, True), HBM],
        out_specs=[_row(TM, D), _row(TM, FF), _row(TM, FF)],
        out_shape=[jax.ShapeDtypeStruct((T, D), BF16)] + [jax.ShapeDtypeStruct((T, FF), BF16)] * 2,
        compiler_params=_cp("parallel"),
    )(dx, P, Q, wffn, token)


def _ffn_bwd_in(dG, dU, x, dx, norm, wffn, sg, su, layer, token):
    T = x.shape[0]

    def body(dG_ref, dU_ref, x_ref, dx_ref, g_ref, wg_ref, wu_ref, token_ref, o_ref, dg_ref):
        @pl.when(pl.program_id(0) == 0)
        def _():
            dg_ref[...] = jnp.zeros_like(dg_ref)

        dh = _nn(dG_ref[...], wg_ref[...]) + _nn(dU_ref[...], wu_ref[...])
        g = g_ref[...]
        xh, r, _ = _rms(x_ref[...], g)
        dg_ref[...] += jnp.sum(dh * xh, axis=0, keepdims=True)
        o_ref[...] = dx_ref[...] + _rms_bwd(dh, xh, r, g)

    return pl.pallas_call(
        body, name="ffn_bwd_in", grid=(T // TM,),
        in_specs=[_row(TM, FF), _row(TM, FF), _row(TM, D), _row(TM, D), _slab((1, D), layer),
                  _slab((FF, D), sg, True), _slab((FF, D), su, True), HBM],
        out_specs=[_row(TM, D), _acc((1, D))],
        out_shape=[jax.ShapeDtypeStruct((T, D), F32), jax.ShapeDtypeStruct((1, D), F32)],
        compiler_params=_cp("arbitrary"),
    )(dG, dU, x, dx, norm, wffn, wffn, token)


def _mix_out_bwd(dx, wout):
    T = dx.shape[0]

    def body(dx_ref, w_ref, d_ref, dao_ref, dco_ref):
        d = dx_ref[...].astype(BF16)
        d_ref[...] = d
        dcat = _nt(d, w_ref[...])
        dao_ref[...] = dcat[:, :AW].astype(BF16)
        dco_ref[...] = dcat[:, AW:].astype(BF16)

    return pl.pallas_call(
        body, name="mix_out_bwd", grid=(T // TM,),
        in_specs=[_row(TM, D), _slab((D, D), 0, True)],
        out_specs=[_row(TM, D), _row(TM, AW), _row(TM, CC)],
        out_shape=[jax.ShapeDtypeStruct((T, D), BF16), jax.ShapeDtypeStruct((T, AW), BF16),
                   jax.ShapeDtypeStruct((T, CC), BF16)],
        compiler_params=_cp("parallel"),
    )(dx, wout)


def _conv_bwd_norm(dco, y, lg, lb, layer):
    T = y.shape[0]

    def body(dco_ref, y_ref, g_ref, bb_ref, dy_ref, dlg_ref, dlb_ref, dcb_ref):
        @pl.when(pl.program_id(0) == 0)
        def _():
            dlg_ref[...] = jnp.zeros_like(dlg_ref)
            dlb_ref[...] = jnp.zeros_like(dlb_ref)
            dcb_ref[...] = jnp.zeros_like(dcb_ref)

        y = y_ref[...]
        g = g_ref[...]
        xc = y - jnp.mean(y, axis=-1, keepdims=True)
        rs = lax.rsqrt(jnp.mean(xc * xc, axis=-1, keepdims=True) + EPS)
        xn = xc * rs
        z = xn * g + bb_ref[...]
        dz = dco_ref[...].astype(F32) * _dsilu(z)
        dlg_ref[...] += jnp.sum(dz * xn, axis=0, keepdims=True)
        dlb_ref[...] += jnp.sum(dz, axis=0, keepdims=True)
        dxn = dz * g
        dy = rs * (dxn - jnp.mean(dxn, axis=-1, keepdims=True) - xn * jnp.mean(dxn * xn, axis=-1, keepdims=True))
        dcb_ref[...] += jnp.sum(dy, axis=0, keepdims=True)
        dy_ref[...] = dy

    return pl.pallas_call(
        body, name="conv_bwd_norm", grid=(T // TM,),
        in_specs=[_row(TM, CC), _row(TM, CC), _slab((1, CC), layer), _slab((1, CC), layer)],
        out_specs=[_row(TM, CC), _acc((1, CC)), _acc((1, CC)), _acc((1, CC))],
        out_shape=[jax.ShapeDtypeStruct((T, CC), F32)] + [jax.ShapeDtypeStruct((1, CC), F32)] * 3,
        compiler_params=_cp("arbitrary"),
    )(dco, y, lg, lb)


def _conv_bwd_taps(dy, u, cw, layer):
    T = u.shape[0]
    n_halo = T // HALO

    def body(dy_ref, dyn_ref, u_ref, up_ref, w_ref, du_ref, dw_ref, hext_ref, dext_ref):
        i = pl.program_id(0)

        @pl.when(i == 0)
        def _():
            dw_ref[...] = jnp.zeros_like(dw_ref)

        hext_ref[0:HALO, :] = jnp.where(i > 0, _glu(up_ref[...]), 0.0)
        hext_ref[HALO:, :] = _glu(u_ref[...])
        dy = dy_ref[...]
        dext_ref[0:TM, :] = dy
        dext_ref[TM:, :] = jnp.where(i < pl.num_programs(0) - 1, dyn_ref[...], 0.0)
        dh = jnp.zeros((TM, CC), F32)
        for k in range(CW):
            dh = dh + w_ref[k:k + 1, :] * dext_ref[pl.ds(CW - 1 - k, TM), :]
            dw_ref[k:k + 1, :] += jnp.sum(dy * hext_ref[pl.ds(HALO - (CW - 1) + k, TM), :], axis=0, keepdims=True)
        uu = u_ref[...].astype(F32)
        a = uu[:, :CC]
        sg = _sigmoid(uu[:, CC:])
        du_ref[:, :CC] = (dh * sg).astype(BF16)
        du_ref[:, CC:] = (dh * a * sg * (1.0 - sg)).astype(BF16)

    return pl.pallas_call(
        body, name="conv_bwd_taps", grid=(T // TM,),
        in_specs=[_row(TM, CC),
                  pl.BlockSpec((HALO, CC), lambda i: (jnp.minimum((i + 1) * (TM // HALO), n_halo - 1), 0)),
                  _row(TM, 2 * CC),
                  pl.BlockSpec((HALO, 2 * CC), lambda i: (jnp.maximum(i * (TM // HALO) - 1, 0), 0)),
                  _slab((CW, CC), layer)],
        out_specs=[_row(TM, 2 * CC), _acc((CW, CC))],
        out_shape=[jax.ShapeDtypeStruct((T, 2 * CC), BF16), jax.ShapeDtypeStruct((CW, CC), F32)],
        scratch_shapes=[pltpu.VMEM((TM + HALO, CC), F32), pltpu.VMEM((TM + HALO, CC), F32)],
        compiler_params=_cp("arbitrary"),
    )(dy, dy, u, u, cw)


def _attn_bwd(qkv, dao, sinks, layer):
    T = qkv.shape[0]
    tq = ATT_BLOCKS * BLK

    def body(sink_ref, cur_ref, prev_ref, do_ref, dq_ref, dk_ref, dv_ref, ds_ref):
        i = pl.program_id(0)

        @pl.when(i == 0)
        def _():
            dk_ref[...] = jnp.zeros_like(dk_ref)
            dv_ref[...] = jnp.zeros_like(dv_ref)
            ds_ref[...] = jnp.zeros_like(ds_ref)

        first = _band_mask(i > 0)
        later = _band_mask(True)
        base = pl.multiple_of(i * tq, tq)
        before = pl.multiple_of(jnp.maximum(i * ATT_BLOCKS - 1, 0) * BLK, BLK)
        for b in range(ATT_BLOCKS):
            dqs, dks, dvs = [], [], []
            for kv in range(NKV):
                kb = _band(prev_ref, cur_ref, b, AW + kv * HD)
                vb = _band(prev_ref, cur_ref, b, QKW + kv * HD)
                q4 = _stack_heads(cur_ref, b, kv)
                do4 = _stack_heads(do_ref, b, kv)
                pt, psink = _probs_t(q4, kb, first if b == 0 else later, _sink_row(sink_ref, layer, kv))
                dpt = _nt(vb, do4)
                dd = jnp.sum(pt * dpt, axis=0, keepdims=True)
                dst = (pt * (dpt - dd) * SCALE).astype(BF16)
                sd = psink * dd
                for g in range(GROUP):
                    hh = kv * GROUP + g
                    ds_ref[hh:hh + 1, :] += jnp.full((1, 128), -jnp.sum(sd[:, g * BLK:(g + 1) * BLK]), F32)
                dqs.append(_unstack_t(_nn(kb.T, dst)))
                dks.append(_nn(dst, q4))
                dvs.append(_nn(pt.astype(BF16), do4))
            dq_ref[b * BLK:(b + 1) * BLK, :] = jnp.concatenate(dqs, axis=1).astype(BF16)
            dkband = jnp.concatenate(dks, axis=1)
            dvband = jnp.concatenate(dvs, axis=1)
            if b == 0:
                dk_ref[pl.ds(before, BLK), :] += dkband[:BLK]
                dv_ref[pl.ds(before, BLK), :] += dvband[:BLK]
                dk_ref[pl.ds(base, BLK), :] += dkband[BLK:]
                dv_ref[pl.ds(base, BLK), :] += dvband[BLK:]
            else:
                r0 = pl.multiple_of(base + (b - 1) * BLK, BLK)
                dk_ref[pl.ds(r0, 2 * BLK), :] += dkband
                dv_ref[pl.ds(r0, 2 * BLK), :] += dvband

    return pl.pallas_call(
        body, name="attn_bwd", grid=(T // tq,),
        in_specs=[pl.BlockSpec(memory_space=pltpu.SMEM), _row(tq, QKVW),
                  pl.BlockSpec((BLK, QKVW), lambda i: (jnp.maximum(i * ATT_BLOCKS - 1, 0), 0)), _row(tq, AW)],
        out_specs=[_row(tq, AW), _acc((T, KVW)), _acc((T, KVW)), _acc((NH, 128))],
        out_shape=[jax.ShapeDtypeStruct((T, AW), BF16), jax.ShapeDtypeStruct((T, KVW), F32),
                   jax.ShapeDtypeStruct((T, KVW), F32), jax.ShapeDtypeStruct((NH, 128), F32)],
        compiler_params=_cp("arbitrary"),
    )(sinks, qkv, qkv, dao)


def _mix_in_bwd(dq, dk, dv, du, rc, rs, x, dx, norm, win, layer):
    T = x.shape[0]

    def body(dq_ref, dk_ref, dv_ref, du_ref, c_ref, s_ref, x_ref, dx_ref, g_ref, w_ref, dp_ref, o_ref, dg_ref):
        @pl.when(pl.program_id(0) == 0)
        def _():
            dg_ref[...] = jnp.zeros_like(dg_ref)

        dqk = jnp.concatenate([dq_ref[...].astype(F32), dk_ref[...]], axis=1)
        dqk = _rope(dqk, c_ref[...], -s_ref[...])
        dp = jnp.concatenate([dqk.astype(BF16), dv_ref[...].astype(BF16), du_ref[...]], axis=1)
        dp_ref[...] = dp
        dh = _nn(dp, w_ref[...])
        g = g_ref[...]
        xh, r, _ = _rms(x_ref[...], g)
        dg_ref[...] += jnp.sum(dh * xh, axis=0, keepdims=True)
        o_ref[...] = dx_ref[...] + _rms_bwd(dh, xh, r, g)

    return pl.pallas_call(
        body, name="mix_in_bwd", grid=(T // TM,),
        in_specs=[_row(TM, AW), _row(TM, KVW), _row(TM, KVW), _row(TM, 2 * CC), _row(TM, 128), _row(TM, 128),
                  _row(TM, D), _row(TM, D), _slab((1, D), layer), _slab((DIN, D), 0, True)],
        out_specs=[_row(TM, DIN), _row(TM, D), _acc((1, D))],
        out_shape=[jax.ShapeDtypeStruct((T, DIN), BF16), jax.ShapeDtypeStruct((T, D), F32),
                   jax.ShapeDtypeStruct((1, D), F32)],
        compiler_params=_cp("arbitrary"),
    )(dq, dk, dv, du, rc, rs, x, dx, norm, win)


def _wgrad(buf, slab, a, b):
    T, M = a.shape
    N = b.shape[1]
    tmm = M // 2 if M > 1024 else M

    def body(buf_ref, a_ref, b_ref, o_ref):
        @pl.when(pl.program_id(1) == 0)
        def _():
            o_ref[...] = jnp.zeros_like(o_ref)

        o_ref[...] += _tn(a_ref[...], b_ref[...])

    return pl.pallas_call(
        body, name="wgrad", grid=(M // tmm, T // TK),
        in_specs=[pl.BlockSpec(memory_space=pl.ANY),
                  pl.BlockSpec((TK, tmm), lambda i, k: (k, i)), pl.BlockSpec((TK, N), lambda i, k: (k, 0))],
        out_specs=pl.BlockSpec((None, tmm, N), lambda i, k: (slab, i, 0)),
        out_shape=jax.ShapeDtypeStruct(buf.shape, F32),
        input_output_aliases={0: 0},
        compiler_params=_cp("parallel", "arbitrary"),
    )(buf, a, b)


HBM = pl.BlockSpec(memory_space=pl.ANY)


def _coords():
    return lax.axis_index("x"), lax.axis_index("y"), lax.axis_index("c")


def _other_chips(x, y):
    return [(1 - x, y), (x, 1 - y), (1 - x, 1 - y)]


def _all_gather(shards):
    n = len(shards)

    def body(*refs):
        ins, outs = refs[:n], refs[n:2 * n]
        send_sems, recv_sems, local_sems = refs[2 * n:]
        x, y, c = _coords()
        me, sibling = (x, y, c), (x, y, 1 - c)
        chips = _other_chips(x, y)

        def rows(a, dev):
            r = ins[a].shape[1]
            return outs[a].at[:, pl.ds(pl.multiple_of((4 * dev[0] + 2 * dev[1] + dev[2]) * r, r), r), :]

        def copy(a, k, block, to, src=None):
            return pltpu.make_async_remote_copy(
                src_ref=rows(a, block) if src is None else src, dst_ref=rows(a, block),
                send_sem=send_sems.at[a * 7 + k], recv_sem=recv_sems.at[a * 7 + k],
                device_id=to, device_id_type=MESH)

        mine = [pltpu.make_async_copy(ins[a], rows(a, me), local_sems.at[a]) for a in range(n)]
        for cp in mine:
            cp.start()
        first = []
        for a in range(n):
            first.append(copy(a, 0, me, sibling, src=ins[a]))
            first += [copy(a, 1 + j, me, (*chip, c), src=ins[a]) for j, chip in enumerate(chips)]
        for cp in first:
            cp.start()
        passed = []
        for j, chip in enumerate(chips):
            for a in range(n):
                copy(a, 1 + j, (*chip, c), me).wait_recv()
                fwd = copy(a, 4 + j, (*chip, c), sibling)
                fwd.start()
                passed.append(fwd)
        for a in range(n):
            copy(a, 0, sibling, me).wait_recv()
            for j, chip in enumerate(chips):
                copy(a, 4 + j, (*chip, 1 - c), me).wait_recv()
        for cp in first + passed:
            cp.wait_send()
        for cp in mine:
            cp.wait()

    return pl.pallas_call(
        body, name="all_gather_weights",
        in_specs=[HBM] * n, out_specs=[HBM] * n,
        out_shape=[jax.ShapeDtypeStruct((s.shape[0], N_DEV * s.shape[1], s.shape[2]), s.dtype) for s in shards],
        scratch_shapes=[pltpu.SemaphoreType.DMA((7 * n,)), pltpu.SemaphoreType.DMA((7 * n,)),
                        pltpu.SemaphoreType.DMA((n,))],
    )(*shards)


def _pair_exchange(grads):
    n = len(grads)

    def body(*refs):
        ins, got = refs[:n], refs[n:2 * n]
        send_sems, recv_sems = refs[2 * n:]
        x, y, c = _coords()
        sibling = (x, y, 1 - c)

        def remote(a, q):
            r = ins[a].shape[1] // N_DEV
            src = ins[a].at[:, pl.ds(pl.multiple_of((2 * q + 1 - c) * r, r), r), :]
            return pltpu.make_async_remote_copy(
                src_ref=src, dst_ref=got[a].at[q],
                send_sem=send_sems.at[a * N_CHIP + q], recv_sem=recv_sems.at[a * N_CHIP + q],
                device_id=sibling, device_id_type=MESH)

        sends = [remote(a, q) for a in range(n) for q in range(N_CHIP)]
        for cp in sends:
            cp.start()
        for cp in sends:
            cp.wait_recv()
        for cp in sends:
            cp.wait_send()

    return pl.pallas_call(
        body, name="grad_pair_exchange",
        in_specs=[HBM] * n, out_specs=[HBM] * n,
        out_shape=[jax.ShapeDtypeStruct((N_CHIP, g.shape[0], g.shape[1] // N_DEV, g.shape[2]), g.dtype) for g in grads],
        scratch_shapes=[pltpu.SemaphoreType.DMA((N_CHIP * n,)), pltpu.SemaphoreType.DMA((N_CHIP * n,))],
    )(*grads)


def _all_reduce_small(pack):
    R = pack.shape[0]

    def body(p_ref, tot_ref, all_ref, send_sems, recv_sems):
        x, y, c = _coords()
        me = 4 * x + 2 * y + c
        all_ref[me] = p_ref[...]
        peers = []
        for k in range(1, N_DEV):
            bx, by, bc = (k >> 2) & 1, (k >> 1) & 1, k & 1
            peers.append((x ^ bx, y ^ by, c ^ bc))

        def copy(k, slot, to):
            return pltpu.make_async_remote_copy(
                src_ref=p_ref, dst_ref=all_ref.at[slot], send_sem=send_sems.at[k], recv_sem=recv_sems.at[k],
                device_id=to, device_id_type=MESH)

        sends = [copy(k, me, peer) for k, peer in enumerate(peers)]
        for cp in sends:
            cp.start()
        for k, peer in enumerate(peers):
            copy(k, 4 * peer[0] + 2 * peer[1] + peer[2], peer).wait_recv()
        for cp in sends:
            cp.wait_send()
        tot = all_ref[0]
        for d in range(1, N_DEV):
            tot = tot + all_ref[d]
        tot_ref[...] = tot

    vmem = pl.BlockSpec(memory_space=pltpu.VMEM)
    return pl.pallas_call(
        body, name="all_reduce_small",
        in_specs=[vmem], out_specs=vmem,
        out_shape=jax.ShapeDtypeStruct((R, 128), F32),
        scratch_shapes=[pltpu.VMEM((N_DEV, R, 128), F32), pltpu.SemaphoreType.DMA((N_DEV - 1,)),
                        pltpu.SemaphoreType.DMA((N_DEV - 1,))],
    )(pack)


HBM_ONLY = pl.BlockSpec(memory_space=pltpu.HBM)
SEM = pl.BlockSpec(memory_space=pltpu.SEMAPHORE)
DATAFLOW = pltpu.SideEffectType.DATAFLOW_SIDE_EFFECTING


def _peers(x, y, c):
    return [(x ^ ((k >> 2) & 1), y ^ ((k >> 1) & 1), c ^ (k & 1)) for k in range(1, N_DEV)]


def _gather_plan(ins, lands):
    x, y, c = _coords()

    def rows(a, dev):
        r = ins[a].shape[1]
        return lands[a].at[:, pl.ds(pl.multiple_of((4 * dev[0] + 2 * dev[1] + dev[2]) * r, r), r), :]

    return [(ins[a], rows(a, (x, y, c)), peer, rows(a, peer)) for a in range(len(ins)) for peer in _peers(x, y, c)]


def _chip_plan(ins, lands):
    x, y, c = _coords()
    chips = _other_chips(x, y)
    return [(ins[a].at[j], lands[a].at[j], (*chips[j], c), lands[a].at[j]) for a in range(len(ins)) for j in range(3)]


def _exchange_start(name, plan, copies_per_array, srcs, lands, after):
    n = len(srcs)
    count = copies_per_array * n

    def body(*refs):
        ins, land = refs[:n], refs[n:2 * n]
        send_sems, recv_sems = refs[2 * n + 1], refs[2 * n + 2]
        token = refs[-1]
        for k, (src, dst, peer, _) in enumerate(plan(ins, land)):
            pltpu.make_async_remote_copy(src_ref=src, dst_ref=dst, send_sem=send_sems.at[k], recv_sem=recv_sems.at[k],
                                         device_id=peer, device_id_type=MESH).start()
        token[...] = jnp.zeros_like(token)

    thru = [pltpu.HBM(v.shape, v.dtype) for v in list(srcs) + list(lands)]
    outs = pl.pallas_call(
        body, name=name,
        in_specs=[HBM_ONLY] * (2 * n) + [HBM],
        out_specs=[SEM, SEM] + [HBM_ONLY] * (2 * n) + [pl.BlockSpec(memory_space=pltpu.VMEM)],
        out_shape=[pltpu.SemaphoreType.DMA((count,)), pltpu.SemaphoreType.DMA((count,))] + thru
        + [jax.ShapeDtypeStruct((8, 128), F32)],
        input_output_aliases={i: 2 + i for i in range(2 * n)},
        compiler_params=pltpu.CompilerParams(has_side_effects=DATAFLOW),
    )(*[pltpu.with_memory_space_constraint(v, pltpu.HBM) for v in list(srcs) + list(lands)], after)
    return outs[0], outs[1], outs[2:2 + n], outs[2 + n:2 + 2 * n], outs[-1]


def _exchange_wait(name, plan, send_sems, recv_sems, srcs, lands, after):
    n = len(srcs)

    def body(*refs):
        ins, land = refs[:n], refs[n:2 * n]
        send, recv = refs[2 * n], refs[2 * n + 1]
        for k, (src, _, peer, here) in enumerate(plan(ins, land)):
            cp = pltpu.make_async_remote_copy(src_ref=src, dst_ref=here, send_sem=send.at[k], recv_sem=recv.at[k],
                                              device_id=peer, device_id_type=MESH)
            cp.wait_send()
            cp.wait_recv()

    thru = [pltpu.HBM(v.shape, v.dtype) for v in list(srcs) + list(lands)]
    outs = pl.pallas_call(
        body, name=name,
        in_specs=[HBM_ONLY] * (2 * n) + [SEM, SEM, pl.BlockSpec(memory_space=pl.ANY)],
        out_specs=[HBM_ONLY] * (2 * n),
        out_shape=thru,
        input_output_aliases={i: i for i in range(2 * n)},
        compiler_params=pltpu.CompilerParams(has_side_effects=DATAFLOW),
    )(*srcs, *lands, send_sems, recv_sems, after)
    return outs[n:]


def _place_own(shard, dev):
    s, r, c = shard.shape

    def body(dev_ref, i_ref, o_ref):
        o_ref[...] = i_ref[...]

    return pl.pallas_call(
        body, name="place_own_shard",
        grid_spec=pltpu.PrefetchScalarGridSpec(
            num_scalar_prefetch=1, grid=(s,),
            in_specs=[pl.BlockSpec((None, r, c), lambda i, d: (i, 0, 0))],
            out_specs=pl.BlockSpec((None, r, c), lambda i, d: (i, d[0], 0))),
        out_shape=jax.ShapeDtypeStruct((s, N_DEV * r, c), shard.dtype),
        compiler_params=_cp("arbitrary"),
    )(dev, shard)


def _tile_rows(n, cap=512):
    t = min(n, cap)
    while n % t or t % 8:
        t -= 1
        if t < 8:
            return n
    return t


def _pair_sum(g, got, owner_dev, owner_chip, dtype):
    s, r8, c = g.shape
    r = r8 // N_DEV
    n = owner_dev.shape[0]

    def body(dev_ref, chip_ref, g_ref, got_ref, o_ref):
        o_ref[...] = (g_ref[...] + got_ref[...]).astype(dtype)

    return pl.pallas_call(
        body, name="pair_sum",
        grid_spec=pltpu.PrefetchScalarGridSpec(
            num_scalar_prefetch=2, grid=(n, s),
            in_specs=[pl.BlockSpec((None, r, c), lambda j, i, dev, chip: (i, dev[j], 0)),
                      pl.BlockSpec((None, None, r, c), lambda j, i, dev, chip: (chip[j], i, 0, 0))],
            out_specs=pl.BlockSpec((None, None, r, c), lambda j, i, dev, chip: (j, i, 0, 0))),
        out_shape=jax.ShapeDtypeStruct((n, s, r, c), dtype),
        compiler_params=_cp("parallel", "parallel"),
    )(owner_dev, owner_chip, g, got)


def _sum_chips(g, got, owner_dev, owner_chip, parts):
    _, s, r, c = parts.shape

    def body(dev_ref, chip_ref, g_ref, got_ref, p0, p1, p2, o_ref):
        own = g_ref[...] + got_ref[...]
        o_ref[...] = ((own + p0[...].astype(F32)) + p1[...].astype(F32)) + p2[...].astype(F32)

    def part(q):
        return pl.BlockSpec((None, None, r, c), lambda i, dev, chip: (q, i, 0, 0))

    return pl.pallas_call(
        body, name="chip_sum",
        grid_spec=pltpu.PrefetchScalarGridSpec(
            num_scalar_prefetch=2, grid=(s,),
            in_specs=[pl.BlockSpec((None, r, c), lambda i, dev, chip: (i, dev[0], 0)),
                      pl.BlockSpec((None, None, r, c), lambda i, dev, chip: (chip[0], i, 0, 0)),
                      part(0), part(1), part(2)],
            out_specs=pl.BlockSpec((None, r, c), lambda i, dev, chip: (i, 0, 0))),
        out_shape=jax.ShapeDtypeStruct((s, r, c), F32),
        compiler_params=_cp("parallel"),
    )(owner_dev, owner_chip, g, got, parts, parts, parts)


def _adamw(w, g, m, v):
    shape = w.shape
    c = shape[-1] if w.ndim > 1 else w.shape[0]
    args = [t.reshape(-1, c) for t in (w, g, m, v)]
    n = args[0].shape[0]
    tr = _tile_rows(n)

    def body(w_ref, g_ref, m_ref, v_ref, d_ref, mo_ref, vo_ref):
        g = g_ref[...]
        m = B1 * m_ref[...] + (1.0 - B1) * g
        v = B2 * v_ref[...] + (1.0 - B2) * jnp.square(g)
        m_hat = m / (1.0 - B1 ** STEP)
        v_hat = v / (1.0 - B2 ** STEP)
        d_ref[...] = -LR * (m_hat / (jnp.sqrt(v_hat) + ADAM_EPS) + WD * w_ref[...])
        mo_ref[...] = m
        vo_ref[...] = v

    outs = pl.pallas_call(
        body, name="adamw", grid=(n // tr,),
        in_specs=[_row(tr, c)] * 4, out_specs=[_row(tr, c)] * 3,
        out_shape=[jax.ShapeDtypeStruct((n, c), F32)] * 3,
        compiler_params=_cp("parallel"),
    )(*args)
    return [o.reshape(shape) for o in outs]


def _pack(pieces):
    flat = []
    for p in pieces:
        f = p.reshape(-1)
        flat.append(jnp.pad(f, (0, (-f.shape[0]) % 1024)))
    return jnp.concatenate(flat).reshape(-1, 128)


def _unpack(pack, shapes):
    flat = pack.reshape(-1)
    out, off = [], 0
    for s in shapes:
        size = 1
        for d in s:
            size *= d
        out.append(flat[off:off + size].reshape(s))
        off += size + (-size) % 1024
    return out


def kernel(x, positions, ffn1_norm, ffn1_w_gate, ffn1_w_up, ffn1_w_down, mix_norm, w_in, conv_w, conv_b, conv_ln_g, conv_ln_b, attn_sinks, w_out, ffn2_norm, ffn2_w_gate, ffn2_w_up, ffn2_w_down, final_norm, loss_target, m_ffn1_norm, m_ffn1_w_gate, m_ffn1_w_up, m_ffn1_w_down, m_mix_norm, m_w_in, m_conv_w, m_conv_b, m_conv_ln_g, m_conv_ln_b, m_attn_sinks, m_w_out, m_ffn2_norm, m_ffn2_w_gate, m_ffn2_w_up, m_ffn2_w_down, m_final_norm, v_ffn1_norm, v_ffn1_w_gate, v_ffn1_w_up, v_ffn1_w_down, v_mix_norm, v_w_in, v_conv_w, v_conv_b, v_conv_ln_g, v_conv_ln_b, v_attn_sinks, v_w_out, v_ffn2_norm, v_ffn2_w_gate, v_ffn2_w_up, v_ffn2_w_down, v_final_norm):
    L = ffn1_norm.shape[0]
    T = x.shape[1]
    x0 = x.reshape(T, D)
    target = loss_target.reshape(T, D)
    dev = 4 * lax.axis_index("x") + 2 * lax.axis_index("y") + lax.axis_index("c")

    def t_(w):
        return jnp.swapaxes(w, 1, 2)

    def ffn_shards(gate, up, down, l):
        return jnp.stack([t_(gate)[l], t_(up)[l], down[l]]).astype(BF16)

    shards = [[ffn_shards(ffn1_w_gate, ffn1_w_up, ffn1_w_down, l), ffn_shards(ffn2_w_gate, ffn2_w_up, ffn2_w_down, l),
               t_(w_in)[l:l + 1].astype(BF16), w_out[l:l + 1].astype(BF16)] for l in range(L)]
    cw_cols = CC // N_DEV
    cw_sh = jnp.pad(conv_w.reshape(-1), (0, (-L * CW * cw_cols) % 1024)).reshape(1, -1, 128)
    dev1 = dev.reshape(1).astype(jnp.int32)
    no_token = jnp.zeros((8, 128), F32)

    def gather_start(name, parts, after):
        lands = [_place_own(s, dev1) for s in parts]
        *handles, token = _exchange_start(name, _gather_plan, N_DEV - 1, parts, lands, after)
        return handles, token

    wffn1_0, cw_all = _all_gather([shards[0][0], cw_sh])
    rest0, token = gather_start("gather_start_0", shards[0][1:], cw_all)
    weights = [None] * L

    cw_rows = cw_sh.shape[1]
    cw_full = cw_all.reshape(N_DEV, cw_rows * 128)[:, :L * CW * cw_cols].reshape(N_DEV, L, CW, cw_cols)
    cw_full = jnp.transpose(cw_full, (1, 2, 0, 3)).reshape(L, CW, CC)

    inv_freq = 1.0 / (10000.0 ** (jnp.arange(0, HD, 2, dtype=F32) / HD))
    ang = positions.reshape(T).astype(F32)[:, None] * inv_freq
    cos, sin = jnp.cos(ang), jnp.sin(ang)
    rc = jnp.concatenate([cos, cos, cos, cos], axis=1)
    rs = jnp.concatenate([-sin, sin, -sin, sin], axis=1)

    n1 = ffn1_norm.reshape(L, 1, D)
    nm = mix_norm.reshape(L, 1, D)
    n2 = ffn2_norm.reshape(L, 1, D)
    cb = conv_b.reshape(L, 1, CC)
    lg = conv_ln_g.reshape(L, 1, CC)
    lb = conv_ln_b.reshape(L, 1, CC)

    saved = []
    xa = x0
    for l in range(L):
        if l == 0:
            wffn1 = wffn1_0
        else:
            wffn1, wffn2, win, wout = _exchange_wait(f"gather_wait_{l}", _gather_plan, *pending, after=xa)
        h1, G1, U1, A1 = _ffn_up(xa, n1, wffn1, 0, 1, l, token)
        xb = _ffn_down(A1, xa, wffn1, 2)
        if l == 0:
            wffn2, win, wout = _exchange_wait("gather_wait_0", _gather_plan, *rest0, after=xb)
        weights[l] = (wffn1, wffn2, win, wout)
        token = no_token
        if l + 1 < L:
            pending, token = gather_start(f"gather_start_{l + 1}", shards[l + 1], win)
        hm, qkv, u = _mix_in(xb, nm, win, rc, rs, l, token)
        ao = _attn_fwd(qkv, attn_sinks, l)
        y, co = _conv_fwd(u, cw_full, cb, lg, lb, l)
        xc, cat = _mix_out(ao, co, xb, wout)
        h2, G2, U2, A2 = _ffn_up(xc, n2, wffn2, 0, 1, l, no_token)
        xd = _ffn_down(A2, xc, wffn2, 2)
        saved.append((xa, h1, G1, U1, A1, xb, hm, qkv, u, y, cat, xc, h2, G2, U2, A2))
        xa = xd
        token = no_token

    dx, loss_part, g_final = _final(xa, final_norm.reshape(1, D), target)

    cx, cy, cc = _coords()
    chip_of = [2 * cx + cy] + [2 * px + py for px, py in _other_chips(cx, cy)]
    own_chip = jnp.stack(chip_of[:1]).astype(jnp.int32)
    other_chips = jnp.stack(chip_of[1:]).astype(jnp.int32)

    g_n1, g_nm, g_n2 = [None] * L, [None] * L, [None] * L
    g_cb, g_lg, g_lb, g_sink, g_cw = [None] * L, [None] * L, [None] * L, [None] * L, [None] * L
    in_flight, reduced = [], {}

    def reduce_start(tag, group, after):
        got = _pair_exchange(group)
        sent = [_pair_sum(g, r, 2 * other_chips + cc, other_chips, BF16) for g, r in zip(group, got)]
        *handles, token = _exchange_start(f"chip_start_{tag}", _chip_plan, 3, sent,
                                          [lax.empty(p.shape, p.dtype) for p in sent], after)
        in_flight.append((tag, group, got, handles))
        return token

    for l in reversed(range(L)):
        xa, h1, G1, U1, A1, xb, hm, qkv, u, y, cat, xc, h2, G2, U2, A2 = saved[l]
        wffn1, wffn2, win, wout = weights[l]
        gffn1 = lax.empty((3, FF, D), F32)
        gffn2 = lax.empty((3, FF, D), F32)
        gin = lax.empty((1, DIN, D), F32)
        gout = lax.empty((1, D, D), F32)
        d, dG, dU = _ffn_bwd_act(dx, G2, U2, wffn2, 2, token)
        gffn2 = _wgrad(gffn2, 2, A2, d)
        gffn2 = _wgrad(gffn2, 0, dG, h2)
        gffn2 = _wgrad(gffn2, 1, dU, h2)
        token = reduce_start(f"{l}c", [gffn2], d)
        dx, g_n2[l] = _ffn_bwd_in(dG, dU, xc, dx, n2, wffn2, 0, 1, l, token)
        d, dao, dco = _mix_out_bwd(dx, wout)
        gout = _wgrad(gout, 0, cat, d)
        dy, g_lg[l], g_lb[l], g_cb[l] = _conv_bwd_norm(dco, y, lg, lb, l)
        du, g_cw[l] = _conv_bwd_taps(dy, u, cw_full, l)
        dq, dk, dv, g_sink[l] = _attn_bwd(qkv, dao, attn_sinks, l)
        dp, dx, g_nm[l] = _mix_in_bwd(dq, dk, dv, du, rc, rs, xb, dx, nm, win, l)
        gin = _wgrad(gin, 0, dp, hm)
        token = reduce_start(f"{l}a", [gin, gout], dx)
        d, dG, dU = _ffn_bwd_act(dx, G1, U1, wffn1, 2, token)
        gffn1 = _wgrad(gffn1, 2, A1, d)
        gffn1 = _wgrad(gffn1, 0, dG, h1)
        gffn1 = _wgrad(gffn1, 1, dU, h1)
        token = reduce_start(f"{l}b", [gffn1], d)
        dx, g_n1[l] = _ffn_bwd_in(dG, dU, xa, dx, n1, wffn1, 0, 1, l, token)
        token = no_token

    grad_x = dx.reshape(1, T, D)
    for tag, group, got, handles in in_flight:
        parts = _exchange_wait(f"chip_wait_{tag}", _chip_plan, *handles, after=dx)
        reduced[tag] = [_sum_chips(g, r, 2 * own_chip + cc, own_chip, p) for g, r, p in zip(group, got, parts)]
    g1 = jnp.stack([reduced[f"{l}b"][0] for l in range(L)])
    g2 = jnp.stack([reduced[f"{l}c"][0] for l in range(L)])
    gin_t = jnp.concatenate([reduced[f"{l}a"][0] for l in range(L)])
    gout_sh = jnp.concatenate([reduced[f"{l}a"][1] for l in range(L)])

    small = [loss_part,
             jnp.concatenate(g_n1), jnp.concatenate(g_nm), jnp.concatenate(g_n2), g_final,
             jnp.concatenate(g_cb), jnp.concatenate(g_lg), jnp.concatenate(g_lb),
             jnp.stack(g_sink)[:, :, 0], jnp.stack(g_cw)]
    small_shapes = [(1, 128), (L, D), (L, D), (L, D), (D,), (L, CC), (L, CC), (L, CC), (L, NH), (L, CW, CC)]
    tot = _unpack(_all_reduce_small(_pack(small)), small_shapes)
    loss = tot[0][0, 0]
    gr_n1, gr_nm, gr_n2, gr_final, gr_cb, gr_lg, gr_lb, gr_sink, gr_cw_full = tot[1:]
    gr_cw = lax.dynamic_slice_in_dim(gr_cw_full, dev * cw_cols, cw_cols, axis=2)

    grads_t = {"ffn1_w_gate": g1[:, 0], "ffn1_w_up": g1[:, 1], "ffn2_w_gate": g2[:, 0], "ffn2_w_up": g2[:, 1],
               "w_in": gin_t}
    grads = {
        "ffn1_norm": gr_n1, "ffn1_w_down": g1[:, 2],
        "mix_norm": gr_nm, "conv_w": gr_cw, "conv_b": gr_cb, "conv_ln_g": gr_lg,
        "conv_ln_b": gr_lb, "attn_sinks": gr_sink, "w_out": gout_sh,
        "ffn2_norm": gr_n2, "ffn2_w_down": g2[:, 2],
        "final_norm": gr_final,
    }
    weights = dict(ffn1_norm=ffn1_norm, ffn1_w_gate=ffn1_w_gate, ffn1_w_up=ffn1_w_up, ffn1_w_down=ffn1_w_down, mix_norm=mix_norm, w_in=w_in, conv_w=conv_w, conv_b=conv_b, conv_ln_g=conv_ln_g, conv_ln_b=conv_ln_b, attn_sinks=attn_sinks, w_out=w_out, ffn2_norm=ffn2_norm, ffn2_w_gate=ffn2_w_gate, ffn2_w_up=ffn2_w_up, ffn2_w_down=ffn2_w_down, final_norm=final_norm)
    moms = dict(ffn1_norm=m_ffn1_norm, ffn1_w_gate=m_ffn1_w_gate, ffn1_w_up=m_ffn1_w_up, ffn1_w_down=m_ffn1_w_down, mix_norm=m_mix_norm, w_in=m_w_in, conv_w=m_conv_w, conv_b=m_conv_b, conv_ln_g=m_conv_ln_g, conv_ln_b=m_conv_ln_b, attn_sinks=m_attn_sinks, w_out=m_w_out, ffn2_norm=m_ffn2_norm, ffn2_w_gate=m_ffn2_w_gate, ffn2_w_up=m_ffn2_w_up, ffn2_w_down=m_ffn2_w_down, final_norm=m_final_norm)
    vels = dict(ffn1_norm=v_ffn1_norm, ffn1_w_gate=v_ffn1_w_gate, ffn1_w_up=v_ffn1_w_up, ffn1_w_down=v_ffn1_w_down, mix_norm=v_mix_norm, w_in=v_w_in, conv_w=v_conv_w, conv_b=v_conv_b, conv_ln_g=v_conv_ln_g, conv_ln_b=v_conv_ln_b, attn_sinks=v_attn_sinks, w_out=v_w_out, ffn2_norm=v_ffn2_norm, ffn2_w_gate=v_ffn2_w_gate, ffn2_w_up=v_ffn2_w_up, ffn2_w_down=v_ffn2_w_down, final_norm=v_final_norm)

    names = list(weights)
    big = ("ffn1_w_gate", "ffn1_w_up", "ffn1_w_down", "w_in", "w_out", "ffn2_w_gate", "ffn2_w_up", "ffn2_w_down")
    delta, new_m, new_v = {}, {}, {}
    for k in big:
        if k in grads_t:
            outs = _adamw(t_(weights[k]), grads_t[k], t_(moms[k]), t_(vels[k]))
            grads[k], delta[k], new_m[k], new_v[k] = [t_(o) for o in [grads_t[k]] + outs]
        else:
            delta[k], new_m[k], new_v[k] = _adamw(weights[k], grads[k], moms[k], vels[k])
    rest = [k for k in names if k not in big]
    rest_shapes = [weights[k].shape for k in rest]
    packed = _adamw(*[_pack([t[k] for k in rest]) for t in (weights, grads, moms, vels)])
    for res, packed_out in zip((delta, new_m, new_v), packed):
        for k, val in zip(rest, _unpack(packed_out, rest_shapes)):
            res[k] = val

    return (loss, grad_x, *[grads[k] for k in names], *[delta[k] for k in names],
            *[new_m[k] for k in names], *[new_v[k] for k in names])
```

```python
import functools

import jax
import jax.numpy as jnp
from jax import lax
from jax.experimental import pallas as pl
from jax.experimental.pallas import tpu as pltpu

F32 = jnp.float32
BF16 = jnp.bfloat16
MESH = pl.DeviceIdType.MESH

N_DEV = 8
N_CHIP = 4
D = 1024
FF = 2816
HD = 64
NH = 8
NKV = 2
GROUP = NH // NKV
AW = NH * HD
KVW = NKV * HD
QKW = AW + KVW
QKVW = AW + 2 * KVW
CC = 512
CW = 31
DIN = QKVW + 2 * CC
BLK = 128
HALO = 32
EPS = 1e-5
SCALE = HD ** -0.5
NEG = float(jnp.finfo(jnp.float32).min)

LR, B1, B2, ADAM_EPS, WD, STEP = 0.001, 0.9, 0.999, 1e-08, 0.01, 10

TM = 256
TK = 1024
FC = 256
ATT_BLOCKS = 4
VMEM_LIMIT = 56 * 1024 * 1024


def _cp(*sem):
    return pltpu.CompilerParams(dimension_semantics=sem, vmem_limit_bytes=VMEM_LIMIT)


def _row(tm, c):
    return pl.BlockSpec((tm, c), lambda i: (i, 0))


def _slab(shape, k, single=False):
    zeros = (0,) * len(shape)
    kw = dict(pipeline_mode=pl.Buffered(1)) if single else {}
    return pl.BlockSpec((None, *shape), lambda i: (k, *zeros), **kw)


def _acc(shape):
    return pl.BlockSpec(shape, lambda i: (0,) * len(shape))


def _nt(a, b):
    return lax.dot_general(a, b, (((1,), (1,)), ((), ())), preferred_element_type=F32)


def _tn(a, b):
    return lax.dot_general(a, b, (((0,), (0,)), ((), ())), preferred_element_type=F32)


def _nn(a, b):
    return jnp.dot(a, b, preferred_element_type=F32)


def _sigmoid(x):
    return jax.nn.sigmoid(x)


def _dsilu(z):
    s = _sigmoid(z)
    return s * (1.0 + z * (1.0 - s))


def _rms(x, g):
    r = lax.rsqrt(jnp.mean(x * x, axis=-1, keepdims=True) + EPS)
    xh = x * r
    return xh, r, xh * g


def _rms_bwd(dh, xh, r, g):
    dxh = dh * g
    return r * (dxh - xh * jnp.mean(dxh * xh, axis=-1, keepdims=True))


def _rope(t, c128, s128):
    w = t.shape[1]
    lane = lax.broadcasted_iota(jnp.int32, t.shape, 1)
    rot = jnp.where(lane % HD < HD // 2, pltpu.roll(t, w - HD // 2, 1), pltpu.roll(t, HD // 2, 1))
    return t * jnp.tile(c128, (1, w // 128)) + rot * jnp.tile(s128, (1, w // 128))


def _ffn_up(x, norm, wffn, sg, su, layer, token):
    T = x.shape[0]

    def body(x_ref, g_ref, wg_ref, wu_ref, token_ref, h_ref, P_ref, Q_ref, A_ref):
        _, _, hn = _rms(x_ref[...], g_ref[...])
        h = hn.astype(BF16)
        h_ref[...] = h
        for c in range(FF // FC):
            sl = slice(c * FC, (c + 1) * FC)
            g = _nt(h, wg_ref[sl, :])
            u = _nt(h, wu_ref[sl, :])
            s = _sigmoid(g)
            p = g * s
            P_ref[:, sl] = p.astype(BF16)
            Q_ref[:, sl] = (u * (s + p - p * s)).astype(BF16)
            A_ref[:, sl] = (p * u).astype(BF16)

    return pl.pallas_call(
        body, name="ffn_up", grid=(T // TM,),
        in_specs=[_row(TM, D), _slab((1, D), layer), _slab((FF, D), sg, True), _slab((FF, D), su, True), HBM],
        out_specs=[_row(TM, D), _row(TM, FF), _row(TM, FF), _row(TM, FF)],
        out_shape=[jax.ShapeDtypeStruct((T, D), BF16)] + [jax.ShapeDtypeStruct((T, FF), BF16)] * 3,
        compiler_params=_cp("parallel"),
    )(x, norm, wffn, wffn, token)


def _ffn_down(a, x, wffn, sd):
    T = x.shape[0]

    def body(a_ref, x_ref, w_ref, o_ref):
        o_ref[...] = x_ref[...] + 0.5 * _nn(a_ref[...], w_ref[...])

    return pl.pallas_call(
        body, name="ffn_down", grid=(T // TM,),
        in_specs=[_row(TM, FF), _row(TM, D), _slab((FF, D), sd, True)],
        out_specs=_row(TM, D),
        out_shape=jax.ShapeDtypeStruct((T, D), F32),
        compiler_params=_cp("parallel"),
    )(a, x, wffn)


def _mix_in(x, norm, win, rc, rs, layer, token):
    T = x.shape[0]

    def body(x_ref, g_ref, w_ref, c_ref, s_ref, token_ref, h_ref, qkv_ref, u_ref):
        _, _, hn = _rms(x_ref[...], g_ref[...])
        h = hn.astype(BF16)
        h_ref[...] = h
        qk = _nt(h, w_ref[0:QKW, :])
        qkv_ref[:, 0:QKW] = _rope(qk, c_ref[...], s_ref[...]).astype(BF16)
        qkv_ref[:, QKW:QKVW] = _nt(h, w_ref[QKW:QKVW, :]).astype(BF16)
        for c in range(2 * CC // FC):
            u_ref[:, c * FC:(c + 1) * FC] = _nt(h, w_ref[QKVW + c * FC:QKVW + (c + 1) * FC, :]).astype(BF16)

    return pl.pallas_call(
        body, name="mix_in", grid=(T // TM,),
        in_specs=[_row(TM, D), _slab((1, D), layer), _slab((DIN, D), 0, True), _row(TM, 128), _row(TM, 128), HBM],
        out_specs=[_row(TM, D), _row(TM, QKVW), _row(TM, 2 * CC)],
        out_shape=[jax.ShapeDtypeStruct((T, D), BF16), jax.ShapeDtypeStruct((T, QKVW), BF16),
                   jax.ShapeDtypeStruct((T, 2 * CC), BF16)],
        compiler_params=_cp("parallel"),
    )(x, norm, win, rc, rs, token)


def _band_mask(has_prev):
    j = lax.broadcasted_iota(jnp.int32, (2 * BLK, BLK), 0)
    r = lax.broadcasted_iota(jnp.int32, (2 * BLK, BLK), 1) + BLK
    rel = r - j
    return jnp.tile((rel >= 0) & (rel < BLK) & (has_prev | (j >= BLK)), (1, GROUP))


def _band(prev_ref, cur_ref, b, col):
    if b == 0:
        return jnp.concatenate([prev_ref[:, col:col + HD], cur_ref[0:BLK, col:col + HD]], axis=0)
    return cur_ref[(b - 1) * BLK:(b + 1) * BLK, col:col + HD]


def _stack_heads(ref, b, kv):
    cols = [(kv * GROUP + g) * HD for g in range(GROUP)]
    return jnp.concatenate([ref[b * BLK:(b + 1) * BLK, c:c + HD] for c in cols], axis=0)


def _unstack_t(xt):
    x = xt.T
    return jnp.concatenate([x[g * BLK:(g + 1) * BLK, :] for g in range(GROUP)], axis=1)


def _sink_row(sink_ref, layer, kv):
    return jnp.concatenate([jnp.full((1, BLK), sink_ref[layer, kv * GROUP + g], F32) for g in range(GROUP)], axis=1)


def _probs_t(q4, kb, mask, sink):
    s = jnp.where(mask, _nt(kb, q4) * SCALE, NEG)
    m = jnp.maximum(jnp.max(s, axis=0, keepdims=True), sink)
    p = jnp.exp(s - m)
    e = jnp.exp(sink - m)
    inv = 1.0 / (jnp.sum(p, axis=0, keepdims=True) + e)
    return p * inv, e * inv


def _attn_fwd(qkv, sinks, layer):
    T = qkv.shape[0]
    tq = ATT_BLOCKS * BLK

    def body(sink_ref, cur_ref, prev_ref, o_ref):
        first = _band_mask(pl.program_id(0) > 0)
        later = _band_mask(True)
        for b in range(ATT_BLOCKS):
            outs = []
            for kv in range(NKV):
                kb = _band(prev_ref, cur_ref, b, AW + kv * HD)
                vb = _band(prev_ref, cur_ref, b, QKW + kv * HD)
                pt, _ = _probs_t(_stack_heads(cur_ref, b, kv), kb, first if b == 0 else later,
                                 _sink_row(sink_ref, layer, kv))
                outs.append(_unstack_t(_nn(vb.T, pt.astype(BF16))))
            o_ref[b * BLK:(b + 1) * BLK, :] = jnp.concatenate(outs, axis=1).astype(BF16)

    return pl.pallas_call(
        body, name="attn_fwd", grid=(T // tq,),
        in_specs=[pl.BlockSpec(memory_space=pltpu.SMEM), _row(tq, QKVW),
                  pl.BlockSpec((BLK, QKVW), lambda i: (jnp.maximum(i * ATT_BLOCKS - 1, 0), 0))],
        out_specs=_row(tq, AW),
        out_shape=jax.ShapeDtypeStruct((T, AW), BF16),
        compiler_params=_cp("parallel"),
    )(sinks, qkv, qkv)


def _glu(u):
    u = u.astype(F32)
    return u[:, :CC] * _sigmoid(u[:, CC:])


def _fill_ext(ext_ref, first, second):
    n = ext_ref.shape[0] - 8
    ext_ref[0:first.shape[0], :] = first
    ext_ref[first.shape[0]:n, :] = second
    ext_ref[n:, :] = jnp.zeros((8, ext_ref.shape[1]), F32)


def _taps(ext_ref, w_ref, offsets, cols):
    y = None
    for b in range(8):
        z = None
        for k, off in enumerate(offsets):
            if off % 8 == b:
                term = w_ref[k:k + 1, cols] * ext_ref[pl.ds(off - b, TM + 8), cols]
                z = term if z is None else z + term
        if z is not None:
            y = z[b:b + TM, :] if y is None else y + z[b:b + TM, :]
    return y


def _conv_fwd(u, cw, cb, lg, lb, layer):
    T = u.shape[0]

    def body(u_ref, up_ref, w_ref, b_ref, g_ref, bb_ref, y_ref, o_ref, ext_ref):
        i = pl.program_id(0)
        _fill_ext(ext_ref, jnp.where(i > 0, _glu(up_ref[...]), 0.0), _glu(u_ref[...]))
        for c in range(CC // 128):
            cols = slice(c * 128, (c + 1) * 128)
            y_ref[:, cols] = _taps(ext_ref, w_ref, [HALO - (CW - 1) + k for k in range(CW)], cols) + b_ref[:, cols]
        y = y_ref[...]
        xc = y - jnp.mean(y, axis=-1, keepdims=True)
        z = xc * lax.rsqrt(jnp.mean(xc * xc, axis=-1, keepdims=True) + EPS) * g_ref[...] + bb_ref[...]
        o_ref[...] = (z * _sigmoid(z)).astype(BF16)

    return pl.pallas_call(
        body, name="conv_fwd", grid=(T // TM,),
        in_specs=[_row(TM, 2 * CC),
                  pl.BlockSpec((HALO, 2 * CC), lambda i: (jnp.maximum(i * (TM // HALO) - 1, 0), 0)),
                  _slab((CW, CC), layer), _slab((1, CC), layer), _slab((1, CC), layer), _slab((1, CC), layer)],
        out_specs=[_row(TM, CC), _row(TM, CC)],
        out_shape=[jax.ShapeDtypeStruct((T, CC), F32), jax.ShapeDtypeStruct((T, CC), BF16)],
        scratch_shapes=[pltpu.VMEM((TM + HALO + 8, CC), F32)],
        compiler_params=_cp("parallel"),
    )(u, u, cw, cb, lg, lb)


def _mix_out(ao, co, x, wout):
    T = x.shape[0]

    def body(ao_ref, co_ref, x_ref, w_ref, o_ref, cat_ref):
        cat = jnp.concatenate([ao_ref[...], co_ref[...]], axis=1)
        cat_ref[...] = cat
        o_ref[...] = x_ref[...] + _nn(cat, w_ref[...])

    return pl.pallas_call(
        body, name="mix_out", grid=(T // TM,),
        in_specs=[_row(TM, AW), _row(TM, CC), _row(TM, D), _slab((D, D), 0, True)],
        out_specs=[_row(TM, D), _row(TM, D)],
        out_shape=[jax.ShapeDtypeStruct((T, D), F32), jax.ShapeDtypeStruct((T, D), BF16)],
        compiler_params=_cp("parallel"),
    )(ao, co, x, wout)


def _final(x, norm, target):
    T = x.shape[0]

    def body(x_ref, g_ref, t_ref, dx_ref, loss_ref, dg_ref):
        @pl.when(pl.program_id(0) == 0)
        def _():
            loss_ref[...] = jnp.zeros_like(loss_ref)
            dg_ref[...] = jnp.zeros_like(dg_ref)

        g = g_ref[...]
        xh, r, y = _rms(x_ref[...], g)
        err = y - t_ref[...]
        loss_ref[...] += jnp.full(loss_ref.shape, (0.5 / D) * jnp.sum(err * err), F32)
        dy = err * (1.0 / D)
        dg_ref[...] += jnp.sum(dy * xh, axis=0, keepdims=True)
        dx_ref[...] = _rms_bwd(dy, xh, r, g)

    return pl.pallas_call(
        body, name="final_loss", grid=(T // TM,),
        in_specs=[_row(TM, D), _acc((1, D)), _row(TM, D)],
        out_specs=[_row(TM, D), _acc((1, 128)), _acc((1, D))],
        out_shape=[jax.ShapeDtypeStruct((T, D), F32), jax.ShapeDtypeStruct((1, 128), F32),
                   jax.ShapeDtypeStruct((1, D), F32)],
        compiler_params=_cp("arbitrary"),
    )(x, norm, target)


def _ffn_bwd_act(dx, P, Q, wffn, sd, token):
    T = dx.shape[0]

    def body(dx_ref, P_ref, Q_ref, w_ref, token_ref, d_ref, dG_ref, dU_ref):
        d = (0.5 * dx_ref[...]).astype(BF16)
        d_ref[...] = d
        for c in range(FF // FC):
            sl = slice(c * FC, (c + 1) * FC)
            da = _nt(d, w_ref[sl, :])
            dU_ref[:, sl] = (da * P_ref[:, sl].astype(F32)).astype(BF16)
            dG_ref[:, sl] = (da * Q_ref[:, sl].astype(F32)).astype(BF16)

    return pl.pallas_call(
        body, name="ffn_bwd_act", grid=(T // TM,),
        in_specs=[_row(TM, D), _row(TM, FF), _row(TM, FF), _slab((FF, D), sd, True), HBM],
        out_specs=[_row(TM, D), _row(TM, FF), _row(TM, FF)],
        out_shape=[jax.ShapeDtypeStruct((T, D), BF16)] + [jax.ShapeDtypeStruct((T, FF), BF16)] * 2,
        compiler_params=_cp("parallel"),
    )(dx, P, Q, wffn, token)


def _ffn_bwd_in(dG, dU, x, dx, norm, wffn, sg, su, layer, token):
    T = x.shape[0]

    def body(dG_ref, dU_ref, x_ref, dx_ref, g_ref, wg_ref, wu_ref, token_ref, o_ref, dg_ref):
        @pl.when(pl.program_id(0) == 0)
        def _():
            dg_ref[...] = jnp.zeros_like(dg_ref)

        dh = _nn(dG_ref[...], wg_ref[...]) + _nn(dU_ref[...], wu_ref[...])
        g = g_ref[...]
        xh, r, _ = _rms(x_ref[...], g)
        dg_ref[...] += jnp.sum(dh * xh, axis=0, keepdims=True)
        o_ref[...] = dx_ref[...] + _rms_bwd(dh, xh, r, g)

    return pl.pallas_call(
        body, name="ffn_bwd_in", grid=(T // TM,),
        in_specs=[_row(TM, FF), _row(TM, FF), _row(TM, D), _row(TM, D), _slab((1, D), layer),
                  _slab((FF, D), sg, True), _slab((FF, D), su, True), HBM],
        out_specs=[_row(TM, D), _acc((1, D))],
        out_shape=[jax.ShapeDtypeStruct((T, D), F32), jax.ShapeDtypeStruct((1, D), F32)],
        compiler_params=_cp("arbitrary"),
    )(dG, dU, x, dx, norm, wffn, wffn, token)


def _mix_out_bwd(dx, wout, token):
    T = dx.shape[0]

    def body(dx_ref, w_ref, token_ref, d_ref, dao_ref, dco_ref):
        d = dx_ref[...].astype(BF16)
        d_ref[...] = d
        dcat = _nt(d, w_ref[...])
        dao_ref[...] = dcat[:, :AW].astype(BF16)
        dco_ref[...] = dcat[:, AW:].astype(BF16)

    return pl.pallas_call(
        body, name="mix_out_bwd", grid=(T // TM,),
        in_specs=[_row(TM, D), _slab((D, D), 0, True), HBM],
        out_specs=[_row(TM, D), _row(TM, AW), _row(TM, CC)],
        out_shape=[jax.ShapeDtypeStruct((T, D), BF16), jax.ShapeDtypeStruct((T, AW), BF16),
                   jax.ShapeDtypeStruct((T, CC), BF16)],
        compiler_params=_cp("parallel"),
    )(dx, wout, token)


def _conv_bwd_norm(dco, y, lg, lb, layer):
    T = y.shape[0]

    def body(dco_ref, y_ref, g_ref, bb_ref, dy_ref, dlg_ref, dlb_ref, dcb_ref):
        @pl.when(pl.program_id(0) == 0)
        def _():
            dlg_ref[...] = jnp.zeros_like(dlg_ref)
            dlb_ref[...] = jnp.zeros_like(dlb_ref)
            dcb_ref[...] = jnp.zeros_like(dcb_ref)

        y = y_ref[...]
        g = g_ref[...]
        xc = y - jnp.mean(y, axis=-1, keepdims=True)
        rs = lax.rsqrt(jnp.mean(xc * xc, axis=-1, keepdims=True) + EPS)
        xn = xc * rs
        z = xn * g + bb_ref[...]
        dz = dco_ref[...].astype(F32) * _dsilu(z)
        dlg_ref[...] += jnp.sum(dz * xn, axis=0, keepdims=True)
        dlb_ref[...] += jnp.sum(dz, axis=0, keepdims=True)
        dxn = dz * g
        dy = rs * (dxn - jnp.mean(dxn, axis=-1, keepdims=True) - xn * jnp.mean(dxn * xn, axis=-1, keepdims=True))
        dcb_ref[...] += jnp.sum(dy, axis=0, keepdims=True)
        dy_ref[...] = dy

    return pl.pallas_call(
        body, name="conv_bwd_norm", grid=(T // TM,),
        in_specs=[_row(TM, CC), _row(TM, CC), _slab((1, CC), layer), _slab((1, CC), layer)],
        out_specs=[_row(TM, CC), _acc((1, CC)), _acc((1, CC)), _acc((1, CC))],
        out_shape=[jax.ShapeDtypeStruct((T, CC), F32)] + [jax.ShapeDtypeStruct((1, CC), F32)] * 3,
        compiler_params=_cp("arbitrary"),
    )(dco, y, lg, lb)


def _conv_bwd_taps(dy, u, cw, layer):
    T = u.shape[0]
    n_halo = T // HALO

    def body(dy_ref, dyn_ref, u_ref, up_ref, w_ref, du_ref, dw_ref, hext_ref, dext_ref, dz_ref, dsh_ref, dh_ref):
        i = pl.program_id(0)

        @pl.when(i == 0)
        def _():
            dw_ref[...] = jnp.zeros_like(dw_ref)

        _fill_ext(hext_ref, jnp.where(i > 0, _glu(up_ref[...]), 0.0), _glu(u_ref[...]))
        _fill_ext(dext_ref, dy_ref[...], jnp.where(i < pl.num_programs(0) - 1, dyn_ref[...], 0.0))
        _fill_ext(dz_ref, jnp.zeros((8, CC), F32), dy_ref[...])
        for b in range(8):
            dsh_ref[b] = dz_ref[pl.ds(8 - b, TM + 8), :]
        h_offsets = [HALO - (CW - 1) + k for k in range(CW)]
        for c in range(CC // 128):
            cols = slice(c * 128, (c + 1) * 128)
            dh_ref[:, cols] = _taps(dext_ref, w_ref, [CW - 1 - k for k in range(CW)], cols)
            for k, off in enumerate(h_offsets):
                b = off % 8
                prod = dsh_ref[b, :, cols] * hext_ref[pl.ds(off - b, TM + 8), cols]
                dw_ref[8 * k:8 * k + 8, cols] += jnp.sum(prod.reshape((TM + 8) // 8, 8, 128), axis=0)
        dh = dh_ref[...]
        uu = u_ref[...].astype(F32)
        a = uu[:, :CC]
        sg = _sigmoid(uu[:, CC:])
        du_ref[:, :CC] = (dh * sg).astype(BF16)
        du_ref[:, CC:] = (dh * a * sg * (1.0 - sg)).astype(BF16)

    return pl.pallas_call(
        body, name="conv_bwd_taps", grid=(T // TM,),
        in_specs=[_row(TM, CC),
                  pl.BlockSpec((HALO, CC), lambda i: (jnp.minimum((i + 1) * (TM // HALO), n_halo - 1), 0)),
                  _row(TM, 2 * CC),
                  pl.BlockSpec((HALO, 2 * CC), lambda i: (jnp.maximum(i * (TM // HALO) - 1, 0), 0)),
                  _slab((CW, CC), layer)],
        out_specs=[_row(TM, 2 * CC), _acc((CW * 8, CC))],
        out_shape=[jax.ShapeDtypeStruct((T, 2 * CC), BF16), jax.ShapeDtypeStruct((CW * 8, CC), F32)],
        scratch_shapes=[pltpu.VMEM((TM + HALO + 8, CC), F32), pltpu.VMEM((TM + HALO + 8, CC), F32),
                        pltpu.VMEM((TM + 16, CC), F32), pltpu.VMEM((8, TM + 8, CC), F32), pltpu.VMEM((TM, CC), F32)],
        compiler_params=_cp("arbitrary"),
    )(dy, dy, u, u, cw)


def _attn_bwd(qkv, dao, sinks, layer):
    T = qkv.shape[0]
    tq = ATT_BLOCKS * BLK

    def body(sink_ref, cur_ref, prev_ref, do_ref, dq_ref, dk_ref, dv_ref, ds_ref):
        i = pl.program_id(0)

        @pl.when(i == 0)
        def _():
            dk_ref[...] = jnp.zeros_like(dk_ref)
            dv_ref[...] = jnp.zeros_like(dv_ref)
            ds_ref[...] = jnp.zeros_like(ds_ref)

        first = _band_mask(i > 0)
        later = _band_mask(True)
        base = pl.multiple_of(i * tq, tq)
        before = pl.multiple_of(jnp.maximum(i * ATT_BLOCKS - 1, 0) * BLK, BLK)
        for b in range(ATT_BLOCKS):
            dqs, dks, dvs = [], [], []
            for kv in range(NKV):
                kb = _band(prev_ref, cur_ref, b, AW + kv * HD)
                vb = _band(prev_ref, cur_ref, b, QKW + kv * HD)
                q4 = _stack_heads(cur_ref, b, kv)
                do4 = _stack_heads(do_ref, b, kv)
                pt, psink = _probs_t(q4, kb, first if b == 0 else later, _sink_row(sink_ref, layer, kv))
                dpt = _nt(vb, do4)
                dd = jnp.sum(pt * dpt, axis=0, keepdims=True)
                dst = (pt * (dpt - dd) * SCALE).astype(BF16)
                sd = psink * dd
                for g in range(GROUP):
                    hh = kv * GROUP + g
                    ds_ref[hh:hh + 1, :] += jnp.full((1, 128), -jnp.sum(sd[:, g * BLK:(g + 1) * BLK]), F32)
                dqs.append(_unstack_t(_nn(kb.T, dst)))
                dks.append(_nn(dst, q4))
                dvs.append(_nn(pt.astype(BF16), do4))
            dq_ref[b * BLK:(b + 1) * BLK, :] = jnp.concatenate(dqs, axis=1).astype(BF16)
            dkband = jnp.concatenate(dks, axis=1)
            dvband = jnp.concatenate(dvs, axis=1)
            if b == 0:
                dk_ref[pl.ds(before, BLK), :] += dkband[:BLK]
                dv_ref[pl.ds(before, BLK), :] += dvband[:BLK]
                dk_ref[pl.ds(base, BLK), :] += dkband[BLK:]
                dv_ref[pl.ds(base, BLK), :] += dvband[BLK:]
            else:
                r0 = pl.multiple_of(base + (b - 1) * BLK, BLK)
                dk_ref[pl.ds(r0, 2 * BLK), :] += dkband
                dv_ref[pl.ds(r0, 2 * BLK), :] += dvband

    return pl.pallas_call(
        body, name="attn_bwd", grid=(T // tq,),
        in_specs=[pl.BlockSpec(memory_space=pltpu.SMEM), _row(tq, QKVW),
                  pl.BlockSpec((BLK, QKVW), lambda i: (jnp.maximum(i * ATT_BLOCKS - 1, 0), 0)), _row(tq, AW)],
        out_specs=[_row(tq, AW), _acc((T, KVW)), _acc((T, KVW)), _acc((NH, 128))],
        out_shape=[jax.ShapeDtypeStruct((T, AW), BF16), jax.ShapeDtypeStruct((T, KVW), F32),
                   jax.ShapeDtypeStruct((T, KVW), F32), jax.ShapeDtypeStruct((NH, 128), F32)],
        compiler_params=_cp("arbitrary"),
    )(sinks, qkv, qkv, dao)


def _mix_in_bwd(dq, dk, dv, du, rc, rs, x, dx, norm, win, layer):
    T = x.shape[0]

    def body(dq_ref, dk_ref, dv_ref, du_ref, c_ref, s_ref, x_ref, dx_ref, g_ref, w_ref, dp_ref, o_ref, dg_ref):
        @pl.when(pl.program_id(0) == 0)
        def _():
            dg_ref[...] = jnp.zeros_like(dg_ref)

        dqk = jnp.concatenate([dq_ref[...].astype(F32), dk_ref[...]], axis=1)
        dqk = _rope(dqk, c_ref[...], -s_ref[...])
        dp = jnp.concatenate([dqk.astype(BF16), dv_ref[...].astype(BF16), du_ref[...]], axis=1)
        dp_ref[...] = dp
        dh = _nn(dp, w_ref[...])
        g = g_ref[...]
        xh, r, _ = _rms(x_ref[...], g)
        dg_ref[...] += jnp.sum(dh * xh, axis=0, keepdims=True)
        o_ref[...] = dx_ref[...] + _rms_bwd(dh, xh, r, g)

    return pl.pallas_call(
        body, name="mix_in_bwd", grid=(T // TM,),
        in_specs=[_row(TM, AW), _row(TM, KVW), _row(TM, KVW), _row(TM, 2 * CC), _row(TM, 128), _row(TM, 128),
                  _row(TM, D), _row(TM, D), _slab((1, D), layer), _slab((DIN, D), 0, True)],
        out_specs=[_row(TM, DIN), _row(TM, D), _acc((1, D))],
        out_shape=[jax.ShapeDtypeStruct((T, DIN), BF16), jax.ShapeDtypeStruct((T, D), F32),
                   jax.ShapeDtypeStruct((1, D), F32)],
        compiler_params=_cp("arbitrary"),
    )(dq, dk, dv, du, rc, rs, x, dx, norm, win)


def _wgrad(buf, slab, a, b):
    T, M = a.shape
    N = b.shape[1]
    tmm = M // 2 if M > 1024 else M

    def body(buf_ref, a_ref, b_ref, o_ref):
        @pl.when(pl.program_id(1) == 0)
        def _():
            o_ref[...] = jnp.zeros_like(o_ref)

        o_ref[...] += _tn(a_ref[...], b_ref[...])

    return pl.pallas_call(
        body, name="wgrad", grid=(M // tmm, T // TK),
        in_specs=[pl.BlockSpec(memory_space=pl.ANY),
                  pl.BlockSpec((TK, tmm), lambda i, k: (k, i)), pl.BlockSpec((TK, N), lambda i, k: (k, 0))],
        out_specs=pl.BlockSpec((None, tmm, N), lambda i, k: (slab, i, 0)),
        out_shape=jax.ShapeDtypeStruct(buf.shape, F32),
        input_output_aliases={0: 0},
        compiler_params=_cp("parallel", "arbitrary"),
    )(buf, a, b)


HBM = pl.BlockSpec(memory_space=pl.ANY)


def _coords():
    return lax.axis_index("x"), lax.axis_index("y"), lax.axis_index("c")


def _other_chips(x, y):
    return [(1 - x, y), (x, 1 - y), (1 - x, 1 - y)]


def _all_gather(shards):
    n = len(shards)

    def body(*refs):
        ins, outs = refs[:n], refs[n:2 * n]
        send_sems, recv_sems, local_sems = refs[2 * n:]
        x, y, c = _coords()
        me, sibling = (x, y, c), (x, y, 1 - c)
        chips = _other_chips(x, y)

        def rows(a, dev):
            r = ins[a].shape[1]
            return outs[a].at[:, pl.ds(pl.multiple_of((4 * dev[0] + 2 * dev[1] + dev[2]) * r, r), r), :]

        def copy(a, k, block, to, src=None):
            return pltpu.make_async_remote_copy(
                src_ref=rows(a, block) if src is None else src, dst_ref=rows(a, block),
                send_sem=send_sems.at[a * 7 + k], recv_sem=recv_sems.at[a * 7 + k],
                device_id=to, device_id_type=MESH)

        mine = [pltpu.make_async_copy(ins[a], rows(a, me), local_sems.at[a]) for a in range(n)]
        for cp in mine:
            cp.start()
        first = []
        for a in range(n):
            first.append(copy(a, 0, me, sibling, src=ins[a]))
            first += [copy(a, 1 + j, me, (*chip, c), src=ins[a]) for j, chip in enumerate(chips)]
        for cp in first:
            cp.start()
        passed = []
        for j, chip in enumerate(chips):
            for a in range(n):
                copy(a, 1 + j, (*chip, c), me).wait_recv()
                fwd = copy(a, 4 + j, (*chip, c), sibling)
                fwd.start()
                passed.append(fwd)
        for a in range(n):
            copy(a, 0, sibling, me).wait_recv()
            for j, chip in enumerate(chips):
                copy(a, 4 + j, (*chip, 1 - c), me).wait_recv()
        for cp in first + passed:
            cp.wait_send()
        for cp in mine:
            cp.wait()

    return pl.pallas_call(
        body, name="all_gather_weights",
        in_specs=[HBM] * n, out_specs=[HBM] * n,
        out_shape=[jax.ShapeDtypeStruct((s.shape[0], N_DEV * s.shape[1], s.shape[2]), s.dtype) for s in shards],
        scratch_shapes=[pltpu.SemaphoreType.DMA((7 * n,)), pltpu.SemaphoreType.DMA((7 * n,)),
                        pltpu.SemaphoreType.DMA((n,))],
    )(*shards)


def _pair_exchange(grads):
    n = len(grads)

    def body(*refs):
        ins, got = refs[:n], refs[n:2 * n]
        send_sems, recv_sems = refs[2 * n:]
        x, y, c = _coords()
        sibling = (x, y, 1 - c)

        def remote(a, q):
            r = ins[a].shape[1] // N_DEV
            src = ins[a].at[:, pl.ds(pl.multiple_of((2 * q + 1 - c) * r, r), r), :]
            return pltpu.make_async_remote_copy(
                src_ref=src, dst_ref=got[a].at[q],
                send_sem=send_sems.at[a * N_CHIP + q], recv_sem=recv_sems.at[a * N_CHIP + q],
                device_id=sibling, device_id_type=MESH)

        sends = [remote(a, q) for a in range(n) for q in range(N_CHIP)]
        for cp in sends:
            cp.start()
        for cp in sends:
            cp.wait_recv()
        for cp in sends:
            cp.wait_send()

    return pl.pallas_call(
        body, name="grad_pair_exchange",
        in_specs=[HBM] * n, out_specs=[HBM] * n,
        out_shape=[jax.ShapeDtypeStruct((N_CHIP, g.shape[0], g.shape[1] // N_DEV, g.shape[2]), g.dtype) for g in grads],
        scratch_shapes=[pltpu.SemaphoreType.DMA((N_CHIP * n,)), pltpu.SemaphoreType.DMA((N_CHIP * n,))],
    )(*grads)


def _all_reduce_small(pack):
    R = pack.shape[0]

    def body(p_ref, tot_ref, all_ref, send_sems, recv_sems):
        x, y, c = _coords()
        me = 4 * x + 2 * y + c
        all_ref[me] = p_ref[...]
        peers = []
        for k in range(1, N_DEV):
            bx, by, bc = (k >> 2) & 1, (k >> 1) & 1, k & 1
            peers.append((x ^ bx, y ^ by, c ^ bc))

        def copy(k, slot, to):
            return pltpu.make_async_remote_copy(
                src_ref=p_ref, dst_ref=all_ref.at[slot], send_sem=send_sems.at[k], recv_sem=recv_sems.at[k],
                device_id=to, device_id_type=MESH)

        sends = [copy(k, me, peer) for k, peer in enumerate(peers)]
        for cp in sends:
            cp.start()
        for k, peer in enumerate(peers):
            copy(k, 4 * peer[0] + 2 * peer[1] + peer[2], peer).wait_recv()
        for cp in sends:
            cp.wait_send()
        tot = all_ref[0]
        for d in range(1, N_DEV):
            tot = tot + all_ref[d]
        tot_ref[...] = tot

    vmem = pl.BlockSpec(memory_space=pltpu.VMEM)
    return pl.pallas_call(
        body, name="all_reduce_small",
        in_specs=[vmem], out_specs=vmem,
        out_shape=jax.ShapeDtypeStruct((R, 128), F32),
        scratch_shapes=[pltpu.VMEM((N_DEV, R, 128), F32), pltpu.SemaphoreType.DMA((N_DEV - 1,)),
                        pltpu.SemaphoreType.DMA((N_DEV - 1,))],
    )(pack)


HBM_ONLY = pl.BlockSpec(memory_space=pltpu.HBM)
SEM = pl.BlockSpec(memory_space=pltpu.SEMAPHORE)
DATAFLOW = pltpu.SideEffectType.DATAFLOW_SIDE_EFFECTING


def _peers(x, y, c):
    return [(x ^ ((k >> 2) & 1), y ^ ((k >> 1) & 1), c ^ (k & 1)) for k in range(1, N_DEV)]


def _gather_plan(ins, lands):
    x, y, c = _coords()

    def rows(a, dev):
        r = ins[a].shape[1]
        return lands[a].at[:, pl.ds(pl.multiple_of((4 * dev[0] + 2 * dev[1] + dev[2]) * r, r), r), :]

    return [(ins[a], rows(a, (x, y, c)), peer, rows(a, peer)) for a in range(len(ins)) for peer in _peers(x, y, c)]


def _pair_plan(ins, lands):
    x, y, c = _coords()
    plan = []
    for a in range(len(ins)):
        r = ins[a].shape[1] // N_DEV
        for q in range(N_CHIP):
            src = ins[a].at[:, pl.ds(pl.multiple_of((2 * q + 1 - c) * r, r), r), :]
            plan.append((src, lands[a].at[q], (x, y, 1 - c), lands[a].at[q]))
    return plan


def _chip_plan(ins, lands):
    x, y, c = _coords()
    chips = _other_chips(x, y)
    return [(ins[a].at[j], lands[a].at[j], (*chips[j], c), lands[a].at[j]) for a in range(len(ins)) for j in range(3)]


def _exchange_start(name, plan, copies_per_array, srcs, lands, after):
    n = len(srcs)
    count = copies_per_array * n

    def body(*refs):
        ins, land = refs[:n], refs[n:2 * n]
        send_sems, recv_sems = refs[2 * n + 1], refs[2 * n + 2]
        token = refs[-1]
        for k, (src, dst, peer, _) in enumerate(plan(ins, land)):
            pltpu.make_async_remote_copy(src_ref=src, dst_ref=dst, send_sem=send_sems.at[k], recv_sem=recv_sems.at[k],
                                         device_id=peer, device_id_type=MESH).start()
        token[...] = jnp.zeros_like(token)

    thru = [pltpu.HBM(v.shape, v.dtype) for v in list(srcs) + list(lands)]
    outs = pl.pallas_call(
        body, name=name,
        in_specs=[HBM_ONLY] * (2 * n) + [HBM],
        out_specs=[SEM, SEM] + [HBM_ONLY] * (2 * n) + [pl.BlockSpec(memory_space=pltpu.VMEM)],
        out_shape=[pltpu.SemaphoreType.DMA((count,)), pltpu.SemaphoreType.DMA((count,))] + thru
        + [jax.ShapeDtypeStruct((8, 128), F32)],
        input_output_aliases={i: 2 + i for i in range(2 * n)},
        compiler_params=pltpu.CompilerParams(has_side_effects=DATAFLOW),
    )(*[pltpu.with_memory_space_constraint(v, pltpu.HBM) for v in list(srcs) + list(lands)], after)
    return outs[0], outs[1], outs[2:2 + n], outs[2 + n:2 + 2 * n], outs[-1]


def _exchange_wait(name, plan, send_sems, recv_sems, srcs, lands, after):
    n = len(srcs)

    def body(*refs):
        ins, land = refs[:n], refs[n:2 * n]
        send, recv = refs[2 * n], refs[2 * n + 1]
        for k, (src, _, peer, here) in enumerate(plan(ins, land)):
            cp = pltpu.make_async_remote_copy(src_ref=src, dst_ref=here, send_sem=send.at[k], recv_sem=recv.at[k],
                                              device_id=peer, device_id_type=MESH)
            cp.wait_send()
            cp.wait_recv()

    thru = [pltpu.HBM(v.shape, v.dtype) for v in list(srcs) + list(lands)]
    outs = pl.pallas_call(
        body, name=name,
        in_specs=[HBM_ONLY] * (2 * n) + [SEM, SEM, pl.BlockSpec(memory_space=pl.ANY)],
        out_specs=[HBM_ONLY] * (2 * n),
        out_shape=thru,
        input_output_aliases={i: i for i in range(2 * n)},
        compiler_params=pltpu.CompilerParams(has_side_effects=DATAFLOW),
    )(*srcs, *lands, send_sems, recv_sems, after)
    return outs[:n], outs[n:]


def _place_own(shard, dev):
    s, r, c = shard.shape

    def body(dev_ref, i_ref, o_ref):
        o_ref[...] = i_ref[...]

    return pl.pallas_call(
        body, name="place_own_shard",
        grid_spec=pltpu.PrefetchScalarGridSpec(
            num_scalar_prefetch=1, grid=(s,),
            in_specs=[pl.BlockSpec((None, r, c), lambda i, d: (i, 0, 0))],
            out_specs=pl.BlockSpec((None, r, c), lambda i, d: (i, d[0], 0))),
        out_shape=jax.ShapeDtypeStruct((s, N_DEV * r, c), shard.dtype),
        compiler_params=_cp("arbitrary"),
    )(dev, shard)


def _tile_rows(n, cap=512):
    t = min(n, cap)
    while n % t or t % 8:
        t -= 1
        if t < 8:
            return n
    return t


def _pair_sum(g, got, owner_dev, owner_chip, dtype):
    s, r8, c = g.shape
    r = r8 // N_DEV
    n = owner_dev.shape[0]

    def body(dev_ref, chip_ref, g_ref, got_ref, o_ref):
        o_ref[...] = (g_ref[...] + got_ref[...]).astype(dtype)

    return pl.pallas_call(
        body, name="pair_sum",
        grid_spec=pltpu.PrefetchScalarGridSpec(
            num_scalar_prefetch=2, grid=(n, s),
            in_specs=[pl.BlockSpec((None, r, c), lambda j, i, dev, chip: (i, dev[j], 0)),
                      pl.BlockSpec((None, None, r, c), lambda j, i, dev, chip: (chip[j], i, 0, 0))],
            out_specs=pl.BlockSpec((None, None, r, c), lambda j, i, dev, chip: (j, i, 0, 0))),
        out_shape=jax.ShapeDtypeStruct((n, s, r, c), dtype),
        compiler_params=_cp("parallel", "parallel"),
    )(owner_dev, owner_chip, g, got)


def _sum_chips(g, got, owner_dev, owner_chip, parts):
    _, s, r, c = parts.shape

    def body(dev_ref, chip_ref, g_ref, got_ref, p0, p1, p2, o_ref):
        own = g_ref[...] + got_ref[...]
        o_ref[...] = ((own + p0[...].astype(F32)) + p1[...].astype(F32)) + p2[...].astype(F32)

    def part(q):
        return pl.BlockSpec((None, None, r, c), lambda i, dev, chip: (q, i, 0, 0))

    return pl.pallas_call(
        body, name="chip_sum",
        grid_spec=pltpu.PrefetchScalarGridSpec(
            num_scalar_prefetch=2, grid=(s,),
            in_specs=[pl.BlockSpec((None, r, c), lambda i, dev, chip: (i, dev[0], 0)),
                      pl.BlockSpec((None, None, r, c), lambda i, dev, chip: (chip[0], i, 0, 0)),
                      part(0), part(1), part(2)],
            out_specs=pl.BlockSpec((None, r, c), lambda i, dev, chip: (i, 0, 0))),
        out_shape=jax.ShapeDtypeStruct((s, r, c), F32),
        compiler_params=_cp("parallel"),
    )(owner_dev, owner_chip, g, got, parts, parts, parts)


def _adamw(w, g, m, v):
    shape = w.shape
    c = shape[-1] if w.ndim > 1 else w.shape[0]
    args = [t.reshape(-1, c) for t in (w, g, m, v)]
    n = args[0].shape[0]
    tr = _tile_rows(n)

    def body(w_ref, g_ref, m_ref, v_ref, d_ref, mo_ref, vo_ref):
        g = g_ref[...]
        m = B1 * m_ref[...] + (1.0 - B1) * g
        v = B2 * v_ref[...] + (1.0 - B2) * jnp.square(g)
        m_hat = m / (1.0 - B1 ** STEP)
        v_hat = v / (1.0 - B2 ** STEP)
        d_ref[...] = -LR * (m_hat / (jnp.sqrt(v_hat) + ADAM_EPS) + WD * w_ref[...])
        mo_ref[...] = m
        vo_ref[...] = v

    outs = pl.pallas_call(
        body, name="adamw", grid=(n // tr,),
        in_specs=[_row(tr, c)] * 4, out_specs=[_row(tr, c)] * 3,
        out_shape=[jax.ShapeDtypeStruct((n, c), F32)] * 3,
        compiler_params=_cp("parallel"),
    )(*args)
    return [o.reshape(shape) for o in outs]


def _pack(pieces):
    flat = []
    for p in pieces:
        f = p.reshape(-1)
        flat.append(jnp.pad(f, (0, (-f.shape[0]) % 1024)))
    return jnp.concatenate(flat).reshape(-1, 128)


def _unpack(pack, shapes):
    flat = pack.reshape(-1)
    out, off = [], 0
    for s in shapes:
        size = 1
        for d in s:
            size *= d
        out.append(flat[off:off + size].reshape(s))
        off += size + (-size) % 1024
    return out


def kernel(x, positions, ffn1_norm, ffn1_w_gate, ffn1_w_up, ffn1_w_down, mix_norm, w_in, conv_w, conv_b, conv_ln_g, conv_ln_b, attn_sinks, w_out, ffn2_norm, ffn2_w_gate, ffn2_w_up, ffn2_w_down, final_norm, loss_target, m_ffn1_norm, m_ffn1_w_gate, m_ffn1_w_up, m_ffn1_w_down, m_mix_norm, m_w_in, m_conv_w, m_conv_b, m_conv_ln_g, m_conv_ln_b, m_attn_sinks, m_w_out, m_ffn2_norm, m_ffn2_w_gate, m_ffn2_w_up, m_ffn2_w_down, m_final_norm, v_ffn1_norm, v_ffn1_w_gate, v_ffn1_w_up, v_ffn1_w_down, v_mix_norm, v_w_in, v_conv_w, v_conv_b, v_conv_ln_g, v_conv_ln_b, v_attn_sinks, v_w_out, v_ffn2_norm, v_ffn2_w_gate, v_ffn2_w_up, v_ffn2_w_down, v_final_norm):
    L = ffn1_norm.shape[0]
    T = x.shape[1]
    x0 = x.reshape(T, D)
    target = loss_target.reshape(T, D)
    dev = 4 * lax.axis_index("x") + 2 * lax.axis_index("y") + lax.axis_index("c")

    def t_(w):
        return jnp.swapaxes(w, 1, 2)

    def ffn_shards(gate, up, down, l):
        return jnp.stack([t_(gate)[l], t_(up)[l], down[l]]).astype(BF16)

    shards = [[ffn_shards(ffn1_w_gate, ffn1_w_up, ffn1_w_down, l), ffn_shards(ffn2_w_gate, ffn2_w_up, ffn2_w_down, l),
               t_(w_in)[l:l + 1].astype(BF16), w_out[l:l + 1].astype(BF16)] for l in range(L)]
    cw_cols = CC // N_DEV
    cw_sh = jnp.pad(conv_w.reshape(-1), (0, (-L * CW * cw_cols) % 1024)).reshape(1, -1, 128)
    dev1 = dev.reshape(1).astype(jnp.int32)
    no_token = jnp.zeros((8, 128), F32)

    def gather_start(name, parts, after):
        lands = [_place_own(s, dev1) for s in parts]
        *handles, token = _exchange_start(name, _gather_plan, N_DEV - 1, parts, lands, after)
        return handles, token

    wffn1_0, cw_all = _all_gather([shards[0][0], cw_sh])
    rest0, token = gather_start("gather_start_0", shards[0][1:], cw_all)
    weights = [None] * L

    cw_rows = cw_sh.shape[1]
    cw_full = cw_all.reshape(N_DEV, cw_rows * 128)[:, :L * CW * cw_cols].reshape(N_DEV, L, CW, cw_cols)
    cw_full = jnp.transpose(cw_full, (1, 2, 0, 3)).reshape(L, CW, CC)

    inv_freq = 1.0 / (10000.0 ** (jnp.arange(0, HD, 2, dtype=F32) / HD))
    ang = positions.reshape(T).astype(F32)[:, None] * inv_freq
    cos, sin = jnp.cos(ang), jnp.sin(ang)
    rc = jnp.concatenate([cos, cos, cos, cos], axis=1)
    rs = jnp.concatenate([-sin, sin, -sin, sin], axis=1)

    n1 = ffn1_norm.reshape(L, 1, D)
    nm = mix_norm.reshape(L, 1, D)
    n2 = ffn2_norm.reshape(L, 1, D)
    cb = conv_b.reshape(L, 1, CC)
    lg = conv_ln_g.reshape(L, 1, CC)
    lb = conv_ln_b.reshape(L, 1, CC)

    saved = []
    xa = x0
    for l in range(L):
        if l == 0:
            wffn1 = wffn1_0
        else:
            wffn1, wffn2, win, wout = _exchange_wait(f"gather_wait_{l}", _gather_plan, *pending, after=xa)[1]
        h1, G1, U1, A1 = _ffn_up(xa, n1, wffn1, 0, 1, l, token)
        xb = _ffn_down(A1, xa, wffn1, 2)
        if l == 0:
            wffn2, win, wout = _exchange_wait("gather_wait_0", _gather_plan, *rest0, after=xb)[1]
        weights[l] = (wffn1, wffn2, win, wout)
        token = no_token
        if l + 1 < L:
            pending, token = gather_start(f"gather_start_{l + 1}", shards[l + 1], win)
        hm, qkv, u = _mix_in(xb, nm, win, rc, rs, l, token)
        ao = _attn_fwd(qkv, attn_sinks, l)
        y, co = _conv_fwd(u, cw_full, cb, lg, lb, l)
        xc, cat = _mix_out(ao, co, xb, wout)
        h2, G2, U2, A2 = _ffn_up(xc, n2, wffn2, 0, 1, l, no_token)
        xd = _ffn_down(A2, xc, wffn2, 2)
        saved.append((xa, h1, G1, U1, A1, xb, hm, qkv, u, y, cat, xc, h2, G2, U2, A2))
        xa = xd
        token = no_token

    dx, loss_part, g_final = _final(xa, final_norm.reshape(1, D), target)

    cx, cy, cc = _coords()
    chip_of = [2 * cx + cy] + [2 * px + py for px, py in _other_chips(cx, cy)]
    own_chip = jnp.stack(chip_of[:1]).astype(jnp.int32)
    other_chips = jnp.stack(chip_of[1:]).astype(jnp.int32)

    g_n1, g_nm, g_n2 = [None] * L, [None] * L, [None] * L
    g_cb, g_lg, g_lb, g_sink, g_cw = [None] * L, [None] * L, [None] * L, [None] * L, [None] * L
    in_flight, reduced = [], {}

    def pair_begin(tag, group, after):
        lands = [lax.empty((N_CHIP, g.shape[0], g.shape[1] // N_DEV, g.shape[2]), F32) for g in group]
        *handles, token = _exchange_start(f"pair_start_{tag}", _pair_plan, N_CHIP, group, lands, after)
        return handles, token

    def chip_begin(tag, group, got, after):
        sent = [_pair_sum(g, r, 2 * other_chips + cc, other_chips, BF16) for g, r in zip(group, got)]
        *handles, token = _exchange_start(f"chip_start_{tag}", _chip_plan, 3, sent,
                                          [lax.empty(p.shape, p.dtype) for p in sent], after)
        in_flight.append((tag, group, got, handles))
        return token

    def pair_end_chip_begin(tag, handles, after):
        group, got = _exchange_wait(f"pair_wait_{tag}", _pair_plan, *handles, after=after)
        return chip_begin(tag, group, got, after)

    for l in reversed(range(L)):
        xa, h1, G1, U1, A1, xb, hm, qkv, u, y, cat, xc, h2, G2, U2, A2 = saved[l]
        wffn1, wffn2, win, wout = weights[l]
        gffn1 = lax.empty((3, FF, D), F32)
        gffn2 = lax.empty((3, FF, D), F32)
        gin = lax.empty((1, DIN, D), F32)
        gout = lax.empty((1, D, D), F32)
        d, dG, dU = _ffn_bwd_act(dx, G2, U2, wffn2, 2, token)
        gffn2 = _wgrad(gffn2, 2, A2, d)
        gffn2 = _wgrad(gffn2, 0, dG, h2)
        gffn2 = _wgrad(gffn2, 1, dU, h2)
        handles, token = pair_begin(f"{l}c", [gffn2], d)
        dx, g_n2[l] = _ffn_bwd_in(dG, dU, xc, dx, n2, wffn2, 0, 1, l, token)
        token = pair_end_chip_begin(f"{l}c", handles, dx)
        d, dao, dco = _mix_out_bwd(dx, wout, token)
        gout = _wgrad(gout, 0, cat, d)
        dy, g_lg[l], g_lb[l], g_cb[l] = _conv_bwd_norm(dco, y, lg, lb, l)
        du, g_cw8 = _conv_bwd_taps(dy, u, cw_full, l)
        g_cw[l] = g_cw8.reshape(CW, 8, CC).sum(axis=1)
        dq, dk, dv, g_sink[l] = _attn_bwd(qkv, dao, attn_sinks, l)
        dp, dx, g_nm[l] = _mix_in_bwd(dq, dk, dv, du, rc, rs, xb, dx, nm, win, l)
        gin = _wgrad(gin, 0, dp, hm)
        handles, token = pair_begin(f"{l}a", [gin, gout], dx)
        d, dG, dU = _ffn_bwd_act(dx, G1, U1, wffn1, 2, token)
        token = pair_end_chip_begin(f"{l}a", handles, d)
        gffn1 = _wgrad(gffn1, 2, A1, d)
        gffn1 = _wgrad(gffn1, 0, dG, h1)
        gffn1 = _wgrad(gffn1, 1, dU, h1)
        if l > 0:
            handles, token = pair_begin(f"{l}b", [gffn1], token)
            dx, g_n1[l] = _ffn_bwd_in(dG, dU, xa, dx, n1, wffn1, 0, 1, l, token)
            token = pair_end_chip_begin(f"{l}b", handles, dx)
        else:
            token = chip_begin(f"{l}b", [gffn1], _pair_exchange([gffn1]), token)
            dx, g_n1[l] = _ffn_bwd_in(dG, dU, xa, dx, n1, wffn1, 0, 1, l, token)

    grad_x = dx.reshape(1, T, D)
    for tag, group, got, handles in in_flight:
        parts = _exchange_wait(f"chip_wait_{tag}", _chip_plan, *handles, after=dx)[1]
        reduced[tag] = [_sum_chips(g, r, 2 * own_chip + cc, own_chip, p) for g, r, p in zip(group, got, parts)]
    g1 = jnp.stack([reduced[f"{l}b"][0] for l in range(L)])
    g2 = jnp.stack([reduced[f"{l}c"][0] for l in range(L)])
    gin_t = jnp.concatenate([reduced[f"{l}a"][0] for l in range(L)])
    gout_sh = jnp.concatenate([reduced[f"{l}a"][1] for l in range(L)])

    small = [loss_part,
             jnp.concatenate(g_n1), jnp.concatenate(g_nm), jnp.concatenate(g_n2), g_final,
             jnp.concatenate(g_cb), jnp.concatenate(g_lg), jnp.concatenate(g_lb),
             jnp.stack(g_sink)[:, :, 0], jnp.stack(g_cw)]
    small_shapes = [(1, 128), (L, D), (L, D), (L, D), (D,), (L, CC), (L, CC), (L, CC), (L, NH), (L, CW, CC)]
    tot = _unpack(_all_reduce_small(_pack(small)), small_shapes)
    loss = tot[0][0, 0]
    gr_n1, gr_nm, gr_n2, gr_final, gr_cb, gr_lg, gr_lb, gr_sink, gr_cw_full = tot[1:]
    gr_cw = lax.dynamic_slice_in_dim(gr_cw_full, dev * cw_cols, cw_cols, axis=2)

    grads_t = {"ffn1_w_gate": g1[:, 0], "ffn1_w_up": g1[:, 1], "ffn2_w_gate": g2[:, 0], "ffn2_w_up": g2[:, 1],
               "w_in": gin_t}
    grads = {
        "ffn1_norm": gr_n1, "ffn1_w_down": g1[:, 2],
        "mix_norm": gr_nm, "conv_w": gr_cw, "conv_b": gr_cb, "conv_ln_g": gr_lg,
        "conv_ln_b": gr_lb, "attn_sinks": gr_sink, "w_out": gout_sh,
        "ffn2_norm": gr_n2, "ffn2_w_down": g2[:, 2],
        "final_norm": gr_final,
    }
    weights = dict(ffn1_norm=ffn1_norm, ffn1_w_gate=ffn1_w_gate, ffn1_w_up=ffn1_w_up, ffn1_w_down=ffn1_w_down, mix_norm=mix_norm, w_in=w_in, conv_w=conv_w, conv_b=conv_b, conv_ln_g=conv_ln_g, conv_ln_b=conv_ln_b, attn_sinks=attn_sinks, w_out=w_out, ffn2_norm=ffn2_norm, ffn2_w_gate=ffn2_w_gate, ffn2_w_up=ffn2_w_up, ffn2_w_down=ffn2_w_down, final_norm=final_norm)
    moms = dict(ffn1_norm=m_ffn1_norm, ffn1_w_gate=m_ffn1_w_gate, ffn1_w_up=m_ffn1_w_up, ffn1_w_down=m_ffn1_w_down, mix_norm=m_mix_norm, w_in=m_w_in, conv_w=m_conv_w, conv_b=m_conv_b, conv_ln_g=m_conv_ln_g, conv_ln_b=m_conv_ln_b, attn_sinks=m_attn_sinks, w_out=m_w_out, ffn2_norm=m_ffn2_norm, ffn2_w_gate=m_ffn2_w_gate, ffn2_w_up=m_ffn2_w_up, ffn2_w_down=m_ffn2_w_down, final_norm=m_final_norm)
    vels = dict(ffn1_norm=v_ffn1_norm, ffn1_w_gate=v_ffn1_w_gate, ffn1_w_up=v_ffn1_w_up, ffn1_w_down=v_ffn1_w_down, mix_norm=v_mix_norm, w_in=v_w_in, conv_w=v_conv_w, conv_b=v_conv_b, conv_ln_g=v_conv_ln_g, conv_ln_b=v_conv_ln_b, attn_sinks=v_attn_sinks, w_out=v_w_out, ffn2_norm=v_ffn2_norm, ffn2_w_gate=v_ffn2_w_gate, ffn2_w_up=v_ffn2_w_up, ffn2_w_down=v_ffn2_w_down, final_norm=v_final_norm)

    names = list(weights)
    big = ("ffn1_w_gate", "ffn1_w_up", "ffn1_w_down", "w_in", "w_out", "ffn2_w_gate", "ffn2_w_up", "ffn2_w_down")
    delta, new_m, new_v = {}, {}, {}
    for k in big:
        if k in grads_t:
            outs = _adamw(t_(weights[k]), grads_t[k], t_(moms[k]), t_(vels[k]))
            grads[k], delta[k], new_m[k], new_v[k] = [t_(o) for o in [grads_t[k]] + outs]
        else:
            delta[k], new_m[k], new_v[k] = _adamw(weights[k], grads[k], moms[k], vels[k])
    rest = [k for k in names if k not in big]
    rest_shapes = [weights[k].shape for k in rest]
    packed = _adamw(*[_pack([t[k] for k in rest]) for t in (weights, grads, moms, vels)])
    for res, packed_out in zip((delta, new_m, new_v), packed):
        for k, val in zip(rest, _unpack(packed_out, rest_shapes)):
            res[k] = val

    return (loss, grad_x, *[grads[k] for k in names], *[delta[k] for k in names],
            *[new_m[k] for k in names], *[new_v[k] for k in names])
```

```python
import functools

import jax
import jax.numpy as jnp
from jax import lax
from jax.experimental import pallas as pl
from jax.experimental.pallas import tpu as pltpu

F32 = jnp.float32
BF16 = jnp.bfloat16
MESH = pl.DeviceIdType.MESH

N_DEV = 8
N_CHIP = 4
D = 1024
FF = 2816
HD = 64
NH = 8
NKV = 2
GROUP = NH // NKV
AW = NH * HD
KVW = NKV * HD
QKW = AW + KVW
QKVW = AW + 2 * KVW
CC = 512
CW = 31
DIN = QKVW + 2 * CC
BLK = 128
HALO = 32
EPS = 1e-5
SCALE = HD ** -0.5
NEG = float(jnp.finfo(jnp.float32).min)

LR, B1, B2, ADAM_EPS, WD, STEP = 0.001, 0.9, 0.999, 1e-08, 0.01, 10

TM = 512
TM_FFN_UP = 256
TM_CONV_BWD = 256
TK = 1024
FC = 256
ATT_BLOCKS = 8
VMEM_LIMIT = 56 * 1024 * 1024


def _cp(*sem):
    return pltpu.CompilerParams(dimension_semantics=sem, vmem_limit_bytes=VMEM_LIMIT)


def _row(tm, c):
    return pl.BlockSpec((tm, c), lambda i: (i, 0))


def _slab(shape, k, single=False):
    zeros = (0,) * len(shape)
    kw = dict(pipeline_mode=pl.Buffered(1)) if single else {}
    return pl.BlockSpec((None, *shape), lambda i: (k, *zeros), **kw)


def _acc(shape):
    return pl.BlockSpec(shape, lambda i: (0,) * len(shape))


def _nt(a, b):
    return lax.dot_general(a, b, (((1,), (1,)), ((), ())), preferred_element_type=F32)


def _tn(a, b):
    return lax.dot_general(a, b, (((0,), (0,)), ((), ())), preferred_element_type=F32)


def _nn(a, b):
    return jnp.dot(a, b, preferred_element_type=F32)


def _sigmoid(x):
    return jax.nn.sigmoid(x)


def _dsilu(z):
    s = _sigmoid(z)
    return s * (1.0 + z * (1.0 - s))


def _rms(x, g):
    r = lax.rsqrt(jnp.mean(x * x, axis=-1, keepdims=True) + EPS)
    xh = x * r
    return xh, r, xh * g


def _rms_bwd(dh, xh, r, g):
    dxh = dh * g
    return r * (dxh - xh * jnp.mean(dxh * xh, axis=-1, keepdims=True))


def _rope(t, c128, s128):
    w = t.shape[1]
    lane = lax.broadcasted_iota(jnp.int32, t.shape, 1)
    rot = jnp.where(lane % HD < HD // 2, pltpu.roll(t, w - HD // 2, 1), pltpu.roll(t, HD // 2, 1))
    return t * jnp.tile(c128, (1, w // 128)) + rot * jnp.tile(s128, (1, w // 128))


def _ffn_up(x, norm, wffn, sg, su, layer, token):
    T = x.shape[0]
    tm = TM_FFN_UP

    def body(x_ref, g_ref, wg_ref, wu_ref, token_ref, h_ref, P_ref, Q_ref, A_ref):
        _, _, hn = _rms(x_ref[...], g_ref[...])
        h = hn.astype(BF16)
        h_ref[...] = h
        for c in range(FF // FC):
            sl = slice(c * FC, (c + 1) * FC)
            g = _nt(h, wg_ref[sl, :])
            u = _nt(h, wu_ref[sl, :])
            s = _sigmoid(g)
            p = g * s
            P_ref[:, sl] = p.astype(BF16)
            Q_ref[:, sl] = (u * (s + p - p * s)).astype(BF16)
            A_ref[:, sl] = (p * u).astype(BF16)

    return pl.pallas_call(
        body, name="ffn_up", grid=(T // tm,),
        in_specs=[_row(tm, D), _slab((1, D), layer), _slab((FF, D), sg, True), _slab((FF, D), su, True), HBM],
        out_specs=[_row(tm, D), _row(tm, FF), _row(tm, FF), _row(tm, FF)],
        out_shape=[jax.ShapeDtypeStruct((T, D), BF16)] + [jax.ShapeDtypeStruct((T, FF), BF16)] * 3,
        compiler_params=_cp("parallel"),
    )(x, norm, wffn, wffn, token)


def _ffn_down(a, x, wffn, sd):
    T = x.shape[0]

    def body(a_ref, x_ref, w_ref, o_ref):
        o_ref[...] = x_ref[...] + 0.5 * _nn(a_ref[...], w_ref[...])

    return pl.pallas_call(
        body, name="ffn_down", grid=(T // TM,),
        in_specs=[_row(TM, FF), _row(TM, D), _slab((FF, D), sd, True)],
        out_specs=_row(TM, D),
        out_shape=jax.ShapeDtypeStruct((T, D), F32),
        compiler_params=_cp("parallel"),
    )(a, x, wffn)


def _mix_in(x, norm, win, rc, rs, layer, token):
    T = x.shape[0]

    def body(x_ref, g_ref, w_ref, c_ref, s_ref, token_ref, h_ref, qkv_ref, u_ref):
        _, _, hn = _rms(x_ref[...], g_ref[...])
        h = hn.astype(BF16)
        h_ref[...] = h
        qk = _nt(h, w_ref[0:QKW, :])
        qkv_ref[:, 0:QKW] = _rope(qk, c_ref[...], s_ref[...]).astype(BF16)
        qkv_ref[:, QKW:QKVW] = _nt(h, w_ref[QKW:QKVW, :]).astype(BF16)
        for c in range(2 * CC // FC):
            u_ref[:, c * FC:(c + 1) * FC] = _nt(h, w_ref[QKVW + c * FC:QKVW + (c + 1) * FC, :]).astype(BF16)

    return pl.pallas_call(
        body, name="mix_in", grid=(T // TM,),
        in_specs=[_row(TM, D), _slab((1, D), layer), _slab((DIN, D), 0, True), _row(TM, 128), _row(TM, 128), HBM],
        out_specs=[_row(TM, D), _row(TM, QKVW), _row(TM, 2 * CC)],
        out_shape=[jax.ShapeDtypeStruct((T, D), BF16), jax.ShapeDtypeStruct((T, QKVW), BF16),
                   jax.ShapeDtypeStruct((T, 2 * CC), BF16)],
        compiler_params=_cp("parallel"),
    )(x, norm, win, rc, rs, token)


def _band_mask(has_prev):
    j = lax.broadcasted_iota(jnp.int32, (2 * BLK, BLK), 0)
    r = lax.broadcasted_iota(jnp.int32, (2 * BLK, BLK), 1) + BLK
    rel = r - j
    return jnp.tile((rel >= 0) & (rel < BLK) & (has_prev | (j >= BLK)), (1, GROUP))


def _band(prev_ref, cur_ref, b, col):
    if b == 0:
        return jnp.concatenate([prev_ref[:, col:col + HD], cur_ref[0:BLK, col:col + HD]], axis=0)
    return cur_ref[(b - 1) * BLK:(b + 1) * BLK, col:col + HD]


def _stack_heads(ref, b, kv):
    cols = [(kv * GROUP + g) * HD for g in range(GROUP)]
    return jnp.concatenate([ref[b * BLK:(b + 1) * BLK, c:c + HD] for c in cols], axis=0)


def _unstack_t(xt):
    x = xt.T
    return jnp.concatenate([x[g * BLK:(g + 1) * BLK, :] for g in range(GROUP)], axis=1)


def _sink_row(sink_ref, layer, kv):
    return jnp.concatenate([jnp.full((1, BLK), sink_ref[layer, kv * GROUP + g], F32) for g in range(GROUP)], axis=1)


def _probs_t(q4, kb, mask, sink):
    s = jnp.where(mask, _nt(kb, q4) * SCALE, NEG)
    m = jnp.maximum(jnp.max(s, axis=0, keepdims=True), sink)
    p = jnp.exp(s - m)
    e = jnp.exp(sink - m)
    inv = 1.0 / (jnp.sum(p, axis=0, keepdims=True) + e)
    return p * inv, e * inv


def _attn_fwd(qkv, sinks, layer):
    T = qkv.shape[0]
    tq = ATT_BLOCKS * BLK

    def body(sink_ref, cur_ref, prev_ref, o_ref):
        first = _band_mask(pl.program_id(0) > 0)
        later = _band_mask(True)
        for b in range(ATT_BLOCKS):
            outs = []
            for kv in range(NKV):
                kb = _band(prev_ref, cur_ref, b, AW + kv * HD)
                vb = _band(prev_ref, cur_ref, b, QKW + kv * HD)
                pt, _ = _probs_t(_stack_heads(cur_ref, b, kv), kb, first if b == 0 else later,
                                 _sink_row(sink_ref, layer, kv))
                outs.append(_unstack_t(_nn(vb.T, pt.astype(BF16))))
            o_ref[b * BLK:(b + 1) * BLK, :] = jnp.concatenate(outs, axis=1).astype(BF16)

    return pl.pallas_call(
        body, name="attn_fwd", grid=(T // tq,),
        in_specs=[pl.BlockSpec(memory_space=pltpu.SMEM), _row(tq, QKVW),
                  pl.BlockSpec((BLK, QKVW), lambda i: (jnp.maximum(i * ATT_BLOCKS - 1, 0), 0))],
        out_specs=_row(tq, AW),
        out_shape=jax.ShapeDtypeStruct((T, AW), BF16),
        compiler_params=_cp("parallel"),
    )(sinks, qkv, qkv)


def _glu(u):
    u = u.astype(F32)
    return u[:, :CC] * _sigmoid(u[:, CC:])


def _fill_ext(ext_ref, first, second):
    n = ext_ref.shape[0] - 8
    ext_ref[0:first.shape[0], :] = first
    ext_ref[first.shape[0]:n, :] = second
    ext_ref[n:, :] = jnp.zeros((8, ext_ref.shape[1]), F32)


def _taps(ext_ref, w_ref, offsets, cols, tm):
    y = None
    for b in range(8):
        z = None
        for k, off in enumerate(offsets):
            if off % 8 == b:
                term = w_ref[k:k + 1, cols] * ext_ref[pl.ds(off - b, tm + 8), cols]
                z = term if z is None else z + term
        if z is not None:
            y = z[b:b + tm, :] if y is None else y + z[b:b + tm, :]
    return y


def _conv_fwd(u, cw, cb, lg, lb, layer):
    T = u.shape[0]

    def body(u_ref, up_ref, w_ref, b_ref, g_ref, bb_ref, y_ref, o_ref, ext_ref):
        i = pl.program_id(0)
        _fill_ext(ext_ref, jnp.where(i > 0, _glu(up_ref[...]), 0.0), _glu(u_ref[...]))
        for c in range(CC // 128):
            cols = slice(c * 128, (c + 1) * 128)
            y_ref[:, cols] = _taps(ext_ref, w_ref, [HALO - (CW - 1) + k for k in range(CW)], cols, TM) + b_ref[:, cols]
        y = y_ref[...]
        xc = y - jnp.mean(y, axis=-1, keepdims=True)
        z = xc * lax.rsqrt(jnp.mean(xc * xc, axis=-1, keepdims=True) + EPS) * g_ref[...] + bb_ref[...]
        o_ref[...] = (z * _sigmoid(z)).astype(BF16)

    return pl.pallas_call(
        body, name="conv_fwd", grid=(T // TM,),
        in_specs=[_row(TM, 2 * CC),
                  pl.BlockSpec((HALO, 2 * CC), lambda i: (jnp.maximum(i * (TM // HALO) - 1, 0), 0)),
                  _slab((CW, CC), layer), _slab((1, CC), layer), _slab((1, CC), layer), _slab((1, CC), layer)],
        out_specs=[_row(TM, CC), _row(TM, CC)],
        out_shape=[jax.ShapeDtypeStruct((T, CC), F32), jax.ShapeDtypeStruct((T, CC), BF16)],
        scratch_shapes=[pltpu.VMEM((TM + HALO + 8, CC), F32)],
        compiler_params=_cp("parallel"),
    )(u, u, cw, cb, lg, lb)


def _mix_out(ao, co, x, wout):
    T = x.shape[0]

    def body(ao_ref, co_ref, x_ref, w_ref, o_ref, cat_ref):
        cat = jnp.concatenate([ao_ref[...], co_ref[...]], axis=1)
        cat_ref[...] = cat
        o_ref[...] = x_ref[...] + _nn(cat, w_ref[...])

    return pl.pallas_call(
        body, name="mix_out", grid=(T // TM,),
        in_specs=[_row(TM, AW), _row(TM, CC), _row(TM, D), _slab((D, D), 0, True)],
        out_specs=[_row(TM, D), _row(TM, D)],
        out_shape=[jax.ShapeDtypeStruct((T, D), F32), jax.ShapeDtypeStruct((T, D), BF16)],
        compiler_params=_cp("parallel"),
    )(ao, co, x, wout)


def _final(x, norm, target):
    T = x.shape[0]

    def body(x_ref, g_ref, t_ref, dx_ref, loss_ref, dg_ref):
        @pl.when(pl.program_id(0) == 0)
        def _():
            loss_ref[...] = jnp.zeros_like(loss_ref)
            dg_ref[...] = jnp.zeros_like(dg_ref)

        g = g_ref[...]
        xh, r, y = _rms(x_ref[...], g)
        err = y - t_ref[...]
        loss_ref[...] += jnp.full(loss_ref.shape, (0.5 / D) * jnp.sum(err * err), F32)
        dy = err * (1.0 / D)
        dg_ref[...] += jnp.sum(dy * xh, axis=0, keepdims=True)
        dx_ref[...] = _rms_bwd(dy, xh, r, g)

    return pl.pallas_call(
        body, name="final_loss", grid=(T // TM,),
        in_specs=[_row(TM, D), _acc((1, D)), _row(TM, D)],
        out_specs=[_row(TM, D), _acc((1, 128)), _acc((1, D))],
        out_shape=[jax.ShapeDtypeStruct((T, D), F32), jax.ShapeDtypeStruct((1, 128), F32),
                   jax.ShapeDtypeStruct((1, D), F32)],
        compiler_params=_cp("arbitrary"),
    )(x, norm, target)


def _ffn_bwd_act(dx, P, Q, wffn, sd, token):
    T = dx.shape[0]

    def body(dx_ref, P_ref, Q_ref, w_ref, token_ref, d_ref, dG_ref, dU_ref):
        d = (0.5 * dx_ref[...]).astype(BF16)
        d_ref[...] = d
        for c in range(FF // FC):
            sl = slice(c * FC, (c + 1) * FC)
            da = _nt(d, w_ref[sl, :])
            dU_ref[:, sl] = (da * P_ref[:, sl].astype(F32)).astype(BF16)
            dG_ref[:, sl] = (da * Q_ref[:, sl].astype(F32)).astype(BF16)

    return pl.pallas_call(
        body, name="ffn_bwd_act", grid=(T // TM,),
        in_specs=[_row(TM, D), _row(TM, FF), _row(TM, FF), _slab((FF, D), sd, True), HBM],
        out_specs=[_row(TM, D), _row(TM, FF), _row(TM, FF)],
        out_shape=[jax.ShapeDtypeStruct((T, D), BF16)] + [jax.ShapeDtypeStruct((T, FF), BF16)] * 2,
        compiler_params=_cp("parallel"),
    )(dx, P, Q, wffn, token)


def _ffn_bwd_in(dG, dU, x, dx, norm, wffn, sg, su, layer, token):
    T = x.shape[0]

    def body(dG_ref, dU_ref, x_ref, dx_ref, g_ref, wg_ref, wu_ref, token_ref, o_ref, dg_ref):
        @pl.when(pl.program_id(0) == 0)
        def _():
            dg_ref[...] = jnp.zeros_like(dg_ref)

        dh = _nn(dG_ref[...], wg_ref[...]) + _nn(dU_ref[...], wu_ref[...])
        g = g_ref[...]
        xh, r, _ = _rms(x_ref[...], g)
        dg_ref[...] += jnp.sum(dh * xh, axis=0, keepdims=True)
        o_ref[...] = dx_ref[...] + _rms_bwd(dh, xh, r, g)

    return pl.pallas_call(
        body, name="ffn_bwd_in", grid=(T // TM,),
        in_specs=[_row(TM, FF), _row(TM, FF), _row(TM, D), _row(TM, D), _slab((1, D), layer),
                  _slab((FF, D), sg, True), _slab((FF, D), su, True), HBM],
        out_specs=[_row(TM, D), _acc((1, D))],
        out_shape=[jax.ShapeDtypeStruct((T, D), F32), jax.ShapeDtypeStruct((1, D), F32)],
        compiler_params=_cp("arbitrary"),
    )(dG, dU, x, dx, norm, wffn, wffn, token)


def _mix_out_bwd(dx, wout, token):
    T = dx.shape[0]

    def body(dx_ref, w_ref, token_ref, d_ref, dao_ref, dco_ref):
        d = dx_ref[...].astype(BF16)
        d_ref[...] = d
        dcat = _nt(d, w_ref[...])
        dao_ref[...] = dcat[:, :AW].astype(BF16)
        dco_ref[...] = dcat[:, AW:].astype(BF16)

    return pl.pallas_call(
        body, name="mix_out_bwd", grid=(T // TM,),
        in_specs=[_row(TM, D), _slab((D, D), 0, True), HBM],
        out_specs=[_row(TM, D), _row(TM, AW), _row(TM, CC)],
        out_shape=[jax.ShapeDtypeStruct((T, D), BF16), jax.ShapeDtypeStruct((T, AW), BF16),
                   jax.ShapeDtypeStruct((T, CC), BF16)],
        compiler_params=_cp("parallel"),
    )(dx, wout, token)


def _conv_bwd_norm(dco, y, lg, lb, layer):
    T = y.shape[0]

    def body(dco_ref, y_ref, g_ref, bb_ref, dy_ref, dlg_ref, dlb_ref, dcb_ref):
        @pl.when(pl.program_id(0) == 0)
        def _():
            dlg_ref[...] = jnp.zeros_like(dlg_ref)
            dlb_ref[...] = jnp.zeros_like(dlb_ref)
            dcb_ref[...] = jnp.zeros_like(dcb_ref)

        y = y_ref[...]
        g = g_ref[...]
        xc = y - jnp.mean(y, axis=-1, keepdims=True)
        rs = lax.rsqrt(jnp.mean(xc * xc, axis=-1, keepdims=True) + EPS)
        xn = xc * rs
        z = xn * g + bb_ref[...]
        dz = dco_ref[...].astype(F32) * _dsilu(z)
        dlg_ref[...] += jnp.sum(dz * xn, axis=0, keepdims=True)
        dlb_ref[...] += jnp.sum(dz, axis=0, keepdims=True)
        dxn = dz * g
        dy = rs * (dxn - jnp.mean(dxn, axis=-1, keepdims=True) - xn * jnp.mean(dxn * xn, axis=-1, keepdims=True))
        dcb_ref[...] += jnp.sum(dy, axis=0, keepdims=True)
        dy_ref[...] = dy

    return pl.pallas_call(
        body, name="conv_bwd_norm", grid=(T // TM,),
        in_specs=[_row(TM, CC), _row(TM, CC), _slab((1, CC), layer), _slab((1, CC), layer)],
        out_specs=[_row(TM, CC), _acc((1, CC)), _acc((1, CC)), _acc((1, CC))],
        out_shape=[jax.ShapeDtypeStruct((T, CC), F32)] + [jax.ShapeDtypeStruct((1, CC), F32)] * 3,
        compiler_params=_cp("arbitrary"),
    )(dco, y, lg, lb)


def _conv_bwd_taps(dy, u, cw, layer):
    T = u.shape[0]
    tm = TM_CONV_BWD
    n_halo = T // HALO

    def body(dy_ref, dyn_ref, u_ref, up_ref, w_ref, du_ref, dw_ref, hext_ref, dext_ref, dz_ref, dsh_ref, dh_ref):
        i = pl.program_id(0)

        @pl.when(i == 0)
        def _():
            dw_ref[...] = jnp.zeros_like(dw_ref)

        _fill_ext(hext_ref, jnp.where(i > 0, _glu(up_ref[...]), 0.0), _glu(u_ref[...]))
        _fill_ext(dext_ref, dy_ref[...], jnp.where(i < pl.num_programs(0) - 1, dyn_ref[...], 0.0))
        _fill_ext(dz_ref, jnp.zeros((8, CC), F32), dy_ref[...])
        for b in range(8):
            dsh_ref[b] = dz_ref[pl.ds(8 - b, tm + 8), :]
        h_offsets = [HALO - (CW - 1) + k for k in range(CW)]
        for c in range(CC // 128):
            cols = slice(c * 128, (c + 1) * 128)
            dh_ref[:, cols] = _taps(dext_ref, w_ref, [CW - 1 - k for k in range(CW)], cols, tm)
            for k, off in enumerate(h_offsets):
                b = off % 8
                prod = dsh_ref[b, :, cols] * hext_ref[pl.ds(off - b, tm + 8), cols]
                dw_ref[8 * k:8 * k + 8, cols] += jnp.sum(prod.reshape((tm + 8) // 8, 8, 128), axis=0)
        dh = dh_ref[...]
        uu = u_ref[...].astype(F32)
        a = uu[:, :CC]
        sg = _sigmoid(uu[:, CC:])
        du_ref[:, :CC] = (dh * sg).astype(BF16)
        du_ref[:, CC:] = (dh * a * sg * (1.0 - sg)).astype(BF16)

    return pl.pallas_call(
        body, name="conv_bwd_taps", grid=(T // tm,),
        in_specs=[_row(tm, CC),
                  pl.BlockSpec((HALO, CC), lambda i: (jnp.minimum((i + 1) * (tm // HALO), n_halo - 1), 0)),
                  _row(tm, 2 * CC),
                  pl.BlockSpec((HALO, 2 * CC), lambda i: (jnp.maximum(i * (tm // HALO) - 1, 0), 0)),
                  _slab((CW, CC), layer)],
        out_specs=[_row(tm, 2 * CC), _acc((CW * 8, CC))],
        out_shape=[jax.ShapeDtypeStruct((T, 2 * CC), BF16), jax.ShapeDtypeStruct((CW * 8, CC), F32)],
        scratch_shapes=[pltpu.VMEM((tm + HALO + 8, CC), F32), pltpu.VMEM((tm + HALO + 8, CC), F32),
                        pltpu.VMEM((tm + 16, CC), F32), pltpu.VMEM((8, tm + 8, CC), F32), pltpu.VMEM((tm, CC), F32)],
        compiler_params=_cp("arbitrary"),
    )(dy, dy, u, u, cw)


def _attn_bwd(qkv, dao, sinks, layer):
    T = qkv.shape[0]
    tq = ATT_BLOCKS * BLK

    def body(sink_ref, cur_ref, prev_ref, do_ref, dq_ref, dk_ref, dv_ref, ds_ref):
        i = pl.program_id(0)

        @pl.when(i == 0)
        def _():
            dk_ref[...] = jnp.zeros_like(dk_ref)
            dv_ref[...] = jnp.zeros_like(dv_ref)
            ds_ref[...] = jnp.zeros_like(ds_ref)

        first = _band_mask(i > 0)
        later = _band_mask(True)
        base = pl.multiple_of(i * tq, tq)
        before = pl.multiple_of(jnp.maximum(i * ATT_BLOCKS - 1, 0) * BLK, BLK)
        for b in range(ATT_BLOCKS):
            dqs, dks, dvs = [], [], []
            for kv in range(NKV):
                kb = _band(prev_ref, cur_ref, b, AW + kv * HD)
                vb = _band(prev_ref, cur_ref, b, QKW + kv * HD)
                q4 = _stack_heads(cur_ref, b, kv)
                do4 = _stack_heads(do_ref, b, kv)
                pt, psink = _probs_t(q4, kb, first if b == 0 else later, _sink_row(sink_ref, layer, kv))
                dpt = _nt(vb, do4)
                dd = jnp.sum(pt * dpt, axis=0, keepdims=True)
                dst = (pt * (dpt - dd) * SCALE).astype(BF16)
                sd = psink * dd
                for g in range(GROUP):
                    hh = kv * GROUP + g
                    ds_ref[hh:hh + 1, :] += jnp.full((1, 128), -jnp.sum(sd[:, g * BLK:(g + 1) * BLK]), F32)
                dqs.append(_unstack_t(_nn(kb.T, dst)))
                dks.append(_nn(dst, q4))
                dvs.append(_nn(pt.astype(BF16), do4))
            dq_ref[b * BLK:(b + 1) * BLK, :] = jnp.concatenate(dqs, axis=1).astype(BF16)
            dkband = jnp.concatenate(dks, axis=1)
            dvband = jnp.concatenate(dvs, axis=1)
            if b == 0:
                dk_ref[pl.ds(before, BLK), :] += dkband[:BLK]
                dv_ref[pl.ds(before, BLK), :] += dvband[:BLK]
                dk_ref[pl.ds(base, BLK), :] += dkband[BLK:]
                dv_ref[pl.ds(base, BLK), :] += dvband[BLK:]
            else:
                r0 = pl.multiple_of(base + (b - 1) * BLK, BLK)
                dk_ref[pl.ds(r0, 2 * BLK), :] += dkband
                dv_ref[pl.ds(r0, 2 * BLK), :] += dvband

    return pl.pallas_call(
        body, name="attn_bwd", grid=(T // tq,),
        in_specs=[pl.BlockSpec(memory_space=pltpu.SMEM), _row(tq, QKVW),
                  pl.BlockSpec((BLK, QKVW), lambda i: (jnp.maximum(i * ATT_BLOCKS - 1, 0), 0)), _row(tq, AW)],
        out_specs=[_row(tq, AW), _acc((T, KVW)), _acc((T, KVW)), _acc((NH, 128))],
        out_shape=[jax.ShapeDtypeStruct((T, AW), BF16), jax.ShapeDtypeStruct((T, KVW), F32),
                   jax.ShapeDtypeStruct((T, KVW), F32), jax.ShapeDtypeStruct((NH, 128), F32)],
        compiler_params=_cp("arbitrary"),
    )(sinks, qkv, qkv, dao)


def _mix_in_bwd(dq, dk, dv, du, rc, rs, x, dx, norm, win, layer):
    T = x.shape[0]

    def body(dq_ref, dk_ref, dv_ref, du_ref, c_ref, s_ref, x_ref, dx_ref, g_ref, w_ref, dp_ref, o_ref, dg_ref):
        @pl.when(pl.program_id(0) == 0)
        def _():
            dg_ref[...] = jnp.zeros_like(dg_ref)

        dqk = jnp.concatenate([dq_ref[...].astype(F32), dk_ref[...]], axis=1)
        dqk = _rope(dqk, c_ref[...], -s_ref[...])
        dp = jnp.concatenate([dqk.astype(BF16), dv_ref[...].astype(BF16), du_ref[...]], axis=1)
        dp_ref[...] = dp
        dh = _nn(dp, w_ref[...])
        g = g_ref[...]
        xh, r, _ = _rms(x_ref[...], g)
        dg_ref[...] += jnp.sum(dh * xh, axis=0, keepdims=True)
        o_ref[...] = dx_ref[...] + _rms_bwd(dh, xh, r, g)

    return pl.pallas_call(
        body, name="mix_in_bwd", grid=(T // TM,),
        in_specs=[_row(TM, AW), _row(TM, KVW), _row(TM, KVW), _row(TM, 2 * CC), _row(TM, 128), _row(TM, 128),
                  _row(TM, D), _row(TM, D), _slab((1, D), layer), _slab((DIN, D), 0, True)],
        out_specs=[_row(TM, DIN), _row(TM, D), _acc((1, D))],
        out_shape=[jax.ShapeDtypeStruct((T, DIN), BF16), jax.ShapeDtypeStruct((T, D), F32),
                   jax.ShapeDtypeStruct((1, D), F32)],
        compiler_params=_cp("arbitrary"),
    )(dq, dk, dv, du, rc, rs, x, dx, norm, win)


def _wgrad(buf, slab, a, b):
    T, M = a.shape
    N = b.shape[1]
    tmm = M // 2 if M > 1024 else M

    def body(buf_ref, a_ref, b_ref, o_ref):
        @pl.when(pl.program_id(1) == 0)
        def _():
            o_ref[...] = jnp.zeros_like(o_ref)

        o_ref[...] += _tn(a_ref[...], b_ref[...])

    return pl.pallas_call(
        body, name="wgrad", grid=(M // tmm, T // TK),
        in_specs=[pl.BlockSpec(memory_space=pl.ANY),
                  pl.BlockSpec((TK, tmm), lambda i, k: (k, i)), pl.BlockSpec((TK, N), lambda i, k: (k, 0))],
        out_specs=pl.BlockSpec((None, tmm, N), lambda i, k: (slab, i, 0)),
        out_shape=jax.ShapeDtypeStruct(buf.shape, F32),
        input_output_aliases={0: 0},
        compiler_params=_cp("parallel", "arbitrary"),
    )(buf, a, b)


HBM = pl.BlockSpec(memory_space=pl.ANY)


def _coords():
    return lax.axis_index("x"), lax.axis_index("y"), lax.axis_index("c")


def _other_chips(x, y):
    return [(1 - x, y), (x, 1 - y), (1 - x, 1 - y)]


def _all_gather(shards):
    n = len(shards)

    def body(*refs):
        ins, outs = refs[:n], refs[n:2 * n]
        send_sems, recv_sems, local_sems = refs[2 * n:]
        x, y, c = _coords()
        me, sibling = (x, y, c), (x, y, 1 - c)
        chips = _other_chips(x, y)

        def rows(a, dev):
            r = ins[a].shape[1]
            return outs[a].at[:, pl.ds(pl.multiple_of((4 * dev[0] + 2 * dev[1] + dev[2]) * r, r), r), :]

        def copy(a, k, block, to, src=None):
            return pltpu.make_async_remote_copy(
                src_ref=rows(a, block) if src is None else src, dst_ref=rows(a, block),
                send_sem=send_sems.at[a * 7 + k], recv_sem=recv_sems.at[a * 7 + k],
                device_id=to, device_id_type=MESH)

        mine = [pltpu.make_async_copy(ins[a], rows(a, me), local_sems.at[a]) for a in range(n)]
        for cp in mine:
            cp.start()
        first = []
        for a in range(n):
            first.append(copy(a, 0, me, sibling, src=ins[a]))
            first += [copy(a, 1 + j, me, (*chip, c), src=ins[a]) for j, chip in enumerate(chips)]
        for cp in first:
            cp.start()
        passed = []
        for j, chip in enumerate(chips):
            for a in range(n):
                copy(a, 1 + j, (*chip, c), me).wait_recv()
                fwd = copy(a, 4 + j, (*chip, c), sibling)
                fwd.start()
                passed.append(fwd)
        for a in range(n):
            copy(a, 0, sibling, me).wait_recv()
            for j, chip in enumerate(chips):
                copy(a, 4 + j, (*chip, 1 - c), me).wait_recv()
        for cp in first + passed:
            cp.wait_send()
        for cp in mine:
            cp.wait()

    return pl.pallas_call(
        body, name="all_gather_weights",
        in_specs=[HBM] * n, out_specs=[HBM] * n,
        out_shape=[jax.ShapeDtypeStruct((s.shape[0], N_DEV * s.shape[1], s.shape[2]), s.dtype) for s in shards],
        scratch_shapes=[pltpu.SemaphoreType.DMA((7 * n,)), pltpu.SemaphoreType.DMA((7 * n,)),
                        pltpu.SemaphoreType.DMA((n,))],
    )(*shards)


def _pair_exchange(grads):
    n = len(grads)

    def body(*refs):
        ins, got = refs[:n], refs[n:2 * n]
        send_sems, recv_sems = refs[2 * n:]
        x, y, c = _coords()
        sibling = (x, y, 1 - c)

        def remote(a, q):
            r = ins[a].shape[1] // N_DEV
            src = ins[a].at[:, pl.ds(pl.multiple_of((2 * q + 1 - c) * r, r), r), :]
            return pltpu.make_async_remote_copy(
                src_ref=src, dst_ref=got[a].at[q],
                send_sem=send_sems.at[a * N_CHIP + q], recv_sem=recv_sems.at[a * N_CHIP + q],
                device_id=sibling, device_id_type=MESH)

        sends = [remote(a, q) for a in range(n) for q in range(N_CHIP)]
        for cp in sends:
            cp.start()
        for cp in sends:
            cp.wait_recv()
        for cp in sends:
            cp.wait_send()

    return pl.pallas_call(
        body, name="grad_pair_exchange",
        in_specs=[HBM] * n, out_specs=[HBM] * n,
        out_shape=[jax.ShapeDtypeStruct((N_CHIP, g.shape[0], g.shape[1] // N_DEV, g.shape[2]), g.dtype) for g in grads],
        scratch_shapes=[pltpu.SemaphoreType.DMA((N_CHIP * n,)), pltpu.SemaphoreType.DMA((N_CHIP * n,))],
    )(*grads)


def _all_reduce_small(pack):
    R = pack.shape[0]

    def body(p_ref, tot_ref, all_ref, send_sems, recv_sems):
        x, y, c = _coords()
        me = 4 * x + 2 * y + c
        all_ref[me] = p_ref[...]
        peers = []
        for k in range(1, N_DEV):
            bx, by, bc = (k >> 2) & 1, (k >> 1) & 1, k & 1
            peers.append((x ^ bx, y ^ by, c ^ bc))

        def copy(k, slot, to):
            return pltpu.make_async_remote_copy(
                src_ref=p_ref, dst_ref=all_ref.at[slot], send_sem=send_sems.at[k], recv_sem=recv_sems.at[k],
                device_id=to, device_id_type=MESH)

        sends = [copy(k, me, peer) for k, peer in enumerate(peers)]
        for cp in sends:
            cp.start()
        for k, peer in enumerate(peers):
            copy(k, 4 * peer[0] + 2 * peer[1] + peer[2], peer).wait_recv()
        for cp in sends:
            cp.wait_send()
        tot = all_ref[0]
        for d in range(1, N_DEV):
            tot = tot + all_ref[d]
        tot_ref[...] = tot

    vmem = pl.BlockSpec(memory_space=pltpu.VMEM)
    return pl.pallas_call(
        body, name="all_reduce_small",
        in_specs=[vmem], out_specs=vmem,
        out_shape=jax.ShapeDtypeStruct((R, 128), F32),
        scratch_shapes=[pltpu.VMEM((N_DEV, R, 128), F32), pltpu.SemaphoreType.DMA((N_DEV - 1,)),
                        pltpu.SemaphoreType.DMA((N_DEV - 1,))],
    )(pack)


HBM_ONLY = pl.BlockSpec(memory_space=pltpu.HBM)
SEM = pl.BlockSpec(memory_space=pltpu.SEMAPHORE)
DATAFLOW = pltpu.SideEffectType.DATAFLOW_SIDE_EFFECTING


def _peers(x, y, c):
    return [(x ^ ((k >> 2) & 1), y ^ ((k >> 1) & 1), c ^ (k & 1)) for k in range(1, N_DEV)]


def _gather_plan(ins, lands):
    x, y, c = _coords()

    def rows(a, dev):
        r = ins[a].shape[1]
        return lands[a].at[:, pl.ds(pl.multiple_of((4 * dev[0] + 2 * dev[1] + dev[2]) * r, r), r), :]

    return [(ins[a], rows(a, (x, y, c)), peer, rows(a, peer)) for a in range(len(ins)) for peer in _peers(x, y, c)]


def _pair_plan(ins, lands):
    x, y, c = _coords()
    plan = []
    for a in range(len(ins)):
        r = ins[a].shape[1] // N_DEV
        for q in range(N_CHIP):
            src = ins[a].at[:, pl.ds(pl.multiple_of((2 * q + 1 - c) * r, r), r), :]
            plan.append((src, lands[a].at[q], (x, y, 1 - c), lands[a].at[q]))
    return plan


def _chip_plan(ins, lands):
    x, y, c = _coords()
    chips = _other_chips(x, y)
    return [(ins[a].at[j], lands[a].at[j], (*chips[j], c), lands[a].at[j]) for a in range(len(ins)) for j in range(3)]


def _exchange_start(name, plan, copies_per_array, srcs, lands, after):
    n = len(srcs)
    count = copies_per_array * n

    def body(*refs):
        ins, land = refs[:n], refs[n:2 * n]
        send_sems, recv_sems = refs[2 * n + 1], refs[2 * n + 2]
        token = refs[-1]
        for k, (src, dst, peer, _) in enumerate(plan(ins, land)):
            pltpu.make_async_remote_copy(src_ref=src, dst_ref=dst, send_sem=send_sems.at[k], recv_sem=recv_sems.at[k],
                                         device_id=peer, device_id_type=MESH).start()
        token[...] = jnp.zeros_like(token)

    thru = [pltpu.HBM(v.shape, v.dtype) for v in list(srcs) + list(lands)]
    outs = pl.pallas_call(
        body, name=name,
        in_specs=[HBM_ONLY] * (2 * n) + [HBM],
        out_specs=[SEM, SEM] + [HBM_ONLY] * (2 * n) + [pl.BlockSpec(memory_space=pltpu.VMEM)],
        out_shape=[pltpu.SemaphoreType.DMA((count,)), pltpu.SemaphoreType.DMA((count,))] + thru
        + [jax.ShapeDtypeStruct((8, 128), F32)],
        input_output_aliases={i: 2 + i for i in range(2 * n)},
        compiler_params=pltpu.CompilerParams(has_side_effects=DATAFLOW),
    )(*[pltpu.with_memory_space_constraint(v, pltpu.HBM) for v in list(srcs) + list(lands)], after)
    return outs[0], outs[1], outs[2:2 + n], outs[2 + n:2 + 2 * n], outs[-1]


def _exchange_wait(name, plan, send_sems, recv_sems, srcs, lands, after):
    n = len(srcs)

    def body(*refs):
        ins, land = refs[:n], refs[n:2 * n]
        send, recv = refs[2 * n], refs[2 * n + 1]
        for k, (src, _, peer, here) in enumerate(plan(ins, land)):
            cp = pltpu.make_async_remote_copy(src_ref=src, dst_ref=here, send_sem=send.at[k], recv_sem=recv.at[k],
                                              device_id=peer, device_id_type=MESH)
            cp.wait_send()
            cp.wait_recv()

    thru = [pltpu.HBM(v.shape, v.dtype) for v in list(srcs) + list(lands)]
    outs = pl.pallas_call(
        body, name=name,
        in_specs=[HBM_ONLY] * (2 * n) + [SEM, SEM, pl.BlockSpec(memory_space=pl.ANY)],
        out_specs=[HBM_ONLY] * (2 * n),
        out_shape=thru,
        input_output_aliases={i: i for i in range(2 * n)},
        compiler_params=pltpu.CompilerParams(has_side_effects=DATAFLOW),
    )(*srcs, *lands, send_sems, recv_sems, after)
    return outs[:n], outs[n:]


def _place_own(shard, dev):
    s, r, c = shard.shape

    def body(dev_ref, i_ref, o_ref):
        o_ref[...] = i_ref[...]

    return pl.pallas_call(
        body, name="place_own_shard",
        grid_spec=pltpu.PrefetchScalarGridSpec(
            num_scalar_prefetch=1, grid=(s,),
            in_specs=[pl.BlockSpec((None, r, c), lambda i, d: (i, 0, 0))],
            out_specs=pl.BlockSpec((None, r, c), lambda i, d: (i, d[0], 0))),
        out_shape=jax.ShapeDtypeStruct((s, N_DEV * r, c), shard.dtype),
        compiler_params=_cp("arbitrary"),
    )(dev, shard)


def _tile_rows(n, cap=512):
    t = min(n, cap)
    while n % t or t % 8:
        t -= 1
        if t < 8:
            return n
    return t


def _pair_sum(g, got, owner_dev, owner_chip, dtype):
    s, r8, c = g.shape
    r = r8 // N_DEV
    n = owner_dev.shape[0]

    def body(dev_ref, chip_ref, g_ref, got_ref, o_ref):
        o_ref[...] = (g_ref[...] + got_ref[...]).astype(dtype)

    return pl.pallas_call(
        body, name="pair_sum",
        grid_spec=pltpu.PrefetchScalarGridSpec(
            num_scalar_prefetch=2, grid=(n, s),
            in_specs=[pl.BlockSpec((None, r, c), lambda j, i, dev, chip: (i, dev[j], 0)),
                      pl.BlockSpec((None, None, r, c), lambda j, i, dev, chip: (chip[j], i, 0, 0))],
            out_specs=pl.BlockSpec((None, None, r, c), lambda j, i, dev, chip: (j, i, 0, 0))),
        out_shape=jax.ShapeDtypeStruct((n, s, r, c), dtype),
        compiler_params=_cp("parallel", "parallel"),
    )(owner_dev, owner_chip, g, got)


def _sum_chips(g, got, owner_dev, owner_chip, parts):
    _, s, r, c = parts.shape

    def body(dev_ref, chip_ref, g_ref, got_ref, p0, p1, p2, o_ref):
        own = g_ref[...] + got_ref[...]
        o_ref[...] = ((own + p0[...].astype(F32)) + p1[...].astype(F32)) + p2[...].astype(F32)

    def part(q):
        return pl.BlockSpec((None, None, r, c), lambda i, dev, chip: (q, i, 0, 0))

    return pl.pallas_call(
        body, name="chip_sum",
        grid_spec=pltpu.PrefetchScalarGridSpec(
            num_scalar_prefetch=2, grid=(s,),
            in_specs=[pl.BlockSpec((None, r, c), lambda i, dev, chip: (i, dev[0], 0)),
                      pl.BlockSpec((None, None, r, c), lambda i, dev, chip: (chip[0], i, 0, 0)),
                      part(0), part(1), part(2)],
            out_specs=pl.BlockSpec((None, r, c), lambda i, dev, chip: (i, 0, 0))),
        out_shape=jax.ShapeDtypeStruct((s, r, c), F32),
        compiler_params=_cp("parallel"),
    )(owner_dev, owner_chip, g, got, parts, parts, parts)


def _adamw(w, g, m, v):
    shape = w.shape
    c = shape[-1] if w.ndim > 1 else w.shape[0]
    args = [t.reshape(-1, c) for t in (w, g, m, v)]
    n = args[0].shape[0]
    tr = _tile_rows(n)

    def body(w_ref, g_ref, m_ref, v_ref, d_ref, mo_ref, vo_ref):
        g = g_ref[...]
        m = B1 * m_ref[...] + (1.0 - B1) * g
        v = B2 * v_ref[...] + (1.0 - B2) * jnp.square(g)
        m_hat = m / (1.0 - B1 ** STEP)
        v_hat = v / (1.0 - B2 ** STEP)
        d_ref[...] = -LR * (m_hat / (jnp.sqrt(v_hat) + ADAM_EPS) + WD * w_ref[...])
        mo_ref[...] = m
        vo_ref[...] = v

    outs = pl.pallas_call(
        body, name="adamw", grid=(n // tr,),
        in_specs=[_row(tr, c)] * 4, out_specs=[_row(tr, c)] * 3,
        out_shape=[jax.ShapeDtypeStruct((n, c), F32)] * 3,
        compiler_params=_cp("parallel"),
    )(*args)
    return [o.reshape(shape) for o in outs]


def _pack(pieces):
    flat = []
    for p in pieces:
        f = p.reshape(-1)
        flat.append(jnp.pad(f, (0, (-f.shape[0]) % 1024)))
    return jnp.concatenate(flat).reshape(-1, 128)


def _unpack(pack, shapes):
    flat = pack.reshape(-1)
    out, off = [], 0
    for s in shapes:
        size = 1
        for d in s:
            size *= d
        out.append(flat[off:off + size].reshape(s))
        off += size + (-size) % 1024
    return out


def kernel(x, positions, ffn1_norm, ffn1_w_gate, ffn1_w_up, ffn1_w_down, mix_norm, w_in, conv_w, conv_b, conv_ln_g, conv_ln_b, attn_sinks, w_out, ffn2_norm, ffn2_w_gate, ffn2_w_up, ffn2_w_down, final_norm, loss_target, m_ffn1_norm, m_ffn1_w_gate, m_ffn1_w_up, m_ffn1_w_down, m_mix_norm, m_w_in, m_conv_w, m_conv_b, m_conv_ln_g, m_conv_ln_b, m_attn_sinks, m_w_out, m_ffn2_norm, m_ffn2_w_gate, m_ffn2_w_up, m_ffn2_w_down, m_final_norm, v_ffn1_norm, v_ffn1_w_gate, v_ffn1_w_up, v_ffn1_w_down, v_mix_norm, v_w_in, v_conv_w, v_conv_b, v_conv_ln_g, v_conv_ln_b, v_attn_sinks, v_w_out, v_ffn2_norm, v_ffn2_w_gate, v_ffn2_w_up, v_ffn2_w_down, v_final_norm):
    L = ffn1_norm.shape[0]
    T = x.shape[1]
    x0 = x.reshape(T, D)
    target = loss_target.reshape(T, D)
    dev = 4 * lax.axis_index("x") + 2 * lax.axis_index("y") + lax.axis_index("c")

    def t_(w):
        return jnp.swapaxes(w, 1, 2)

    def ffn_shards(gate, up, down, l):
        return jnp.stack([t_(gate)[l], t_(up)[l], down[l]]).astype(BF16)

    shards = [[ffn_shards(ffn1_w_gate, ffn1_w_up, ffn1_w_down, l), ffn_shards(ffn2_w_gate, ffn2_w_up, ffn2_w_down, l),
               t_(w_in)[l:l + 1].astype(BF16), w_out[l:l + 1].astype(BF16)] for l in range(L)]
    cw_cols = CC // N_DEV
    cw_sh = jnp.pad(conv_w.reshape(-1), (0, (-L * CW * cw_cols) % 1024)).reshape(1, -1, 128)
    dev1 = dev.reshape(1).astype(jnp.int32)
    no_token = jnp.zeros((8, 128), F32)

    def gather_start(name, parts, after):
        lands = [_place_own(s, dev1) for s in parts]
        *handles, token = _exchange_start(name, _gather_plan, N_DEV - 1, parts, lands, after)
        return handles, token

    wffn1_0, cw_all = _all_gather([shards[0][0], cw_sh])
    rest0, token = gather_start("gather_start_0", shards[0][1:], cw_all)
    weights = [None] * L

    cw_rows = cw_sh.shape[1]
    cw_full = cw_all.reshape(N_DEV, cw_rows * 128)[:, :L * CW * cw_cols].reshape(N_DEV, L, CW, cw_cols)
    cw_full = jnp.transpose(cw_full, (1, 2, 0, 3)).reshape(L, CW, CC)

    inv_freq = 1.0 / (10000.0 ** (jnp.arange(0, HD, 2, dtype=F32) / HD))
    ang = positions.reshape(T).astype(F32)[:, None] * inv_freq
    cos, sin = jnp.cos(ang), jnp.sin(ang)
    rc = jnp.concatenate([cos, cos, cos, cos], axis=1)
    rs = jnp.concatenate([-sin, sin, -sin, sin], axis=1)

    n1 = ffn1_norm.reshape(L, 1, D)
    nm = mix_norm.reshape(L, 1, D)
    n2 = ffn2_norm.reshape(L, 1, D)
    cb = conv_b.reshape(L, 1, CC)
    lg = conv_ln_g.reshape(L, 1, CC)
    lb = conv_ln_b.reshape(L, 1, CC)

    saved = []
    xa = x0
    for l in range(L):
        if l == 0:
            wffn1 = wffn1_0
        else:
            wffn1, wffn2, win, wout = _exchange_wait(f"gather_wait_{l}", _gather_plan, *pending, after=xa)[1]
        h1, G1, U1, A1 = _ffn_up(xa, n1, wffn1, 0, 1, l, token)
        xb = _ffn_down(A1, xa, wffn1, 2)
        if l == 0:
            wffn2, win, wout = _exchange_wait("gather_wait_0", _gather_plan, *rest0, after=xb)[1]
        weights[l] = (wffn1, wffn2, win, wout)
        token = no_token
        if l + 1 < L:
            pending, token = gather_start(f"gather_start_{l + 1}", shards[l + 1], win)
        hm, qkv, u = _mix_in(xb, nm, win, rc, rs, l, token)
        ao = _attn_fwd(qkv, attn_sinks, l)
        y, co = _conv_fwd(u, cw_full, cb, lg, lb, l)
        xc, cat = _mix_out(ao, co, xb, wout)
        h2, G2, U2, A2 = _ffn_up(xc, n2, wffn2, 0, 1, l, no_token)
        xd = _ffn_down(A2, xc, wffn2, 2)
        saved.append((xa, h1, G1, U1, A1, xb, hm, qkv, u, y, cat, xc, h2, G2, U2, A2))
        xa = xd
        token = no_token

    dx, loss_part, g_final = _final(xa, final_norm.reshape(1, D), target)

    cx, cy, cc = _coords()
    chip_of = [2 * cx + cy] + [2 * px + py for px, py in _other_chips(cx, cy)]
    own_chip = jnp.stack(chip_of[:1]).astype(jnp.int32)
    other_chips = jnp.stack(chip_of[1:]).astype(jnp.int32)

    g_n1, g_nm, g_n2 = [None] * L, [None] * L, [None] * L
    g_cb, g_lg, g_lb, g_sink, g_cw = [None] * L, [None] * L, [None] * L, [None] * L, [None] * L
    in_flight, reduced = [], {}

    def pair_begin(tag, group, after):
        lands = [lax.empty((N_CHIP, g.shape[0], g.shape[1] // N_DEV, g.shape[2]), F32) for g in group]
        *handles, token = _exchange_start(f"pair_start_{tag}", _pair_plan, N_CHIP, group, lands, after)
        return handles, token

    def chip_begin(tag, group, got, after):
        sent = [_pair_sum(g, r, 2 * other_chips + cc, other_chips, BF16) for g, r in zip(group, got)]
        *handles, token = _exchange_start(f"chip_start_{tag}", _chip_plan, 3, sent,
                                          [lax.empty(p.shape, p.dtype) for p in sent], after)
        in_flight.append((tag, group, got, handles))
        return token

    def pair_end_chip_begin(tag, handles, after):
        group, got = _exchange_wait(f"pair_wait_{tag}", _pair_plan, *handles, after=after)
        return chip_begin(tag, group, got, after)

    for l in reversed(range(L)):
        xa, h1, G1, U1, A1, xb, hm, qkv, u, y, cat, xc, h2, G2, U2, A2 = saved[l]
        wffn1, wffn2, win, wout = weights[l]
        gffn1 = lax.empty((3, FF, D), F32)
        gffn2 = lax.empty((3, FF, D), F32)
        gin = lax.empty((1, DIN, D), F32)
        gout = lax.empty((1, D, D), F32)
        d, dG, dU = _ffn_bwd_act(dx, G2, U2, wffn2, 2, token)
        gffn2 = _wgrad(gffn2, 2, A2, d)
        gffn2 = _wgrad(gffn2, 0, dG, h2)
        gffn2 = _wgrad(gffn2, 1, dU, h2)
        handles, token = pair_begin(f"{l}c", [gffn2], d)
        dx, g_n2[l] = _ffn_bwd_in(dG, dU, xc, dx, n2, wffn2, 0, 1, l, token)
        token = pair_end_chip_begin(f"{l}c", handles, dx)
        d, dao, dco = _mix_out_bwd(dx, wout, token)
        gout = _wgrad(gout, 0, cat, d)
        dy, g_lg[l], g_lb[l], g_cb[l] = _conv_bwd_norm(dco, y, lg, lb, l)
        du, g_cw8 = _conv_bwd_taps(dy, u, cw_full, l)
        g_cw[l] = g_cw8.reshape(CW, 8, CC).sum(axis=1)
        dq, dk, dv, g_sink[l] = _attn_bwd(qkv, dao, attn_sinks, l)
        dp, dx, g_nm[l] = _mix_in_bwd(dq, dk, dv, du, rc, rs, xb, dx, nm, win, l)
        gin = _wgrad(gin, 0, dp, hm)
        handles, token = pair_begin(f"{l}a", [gin, gout], dx)
        d, dG, dU = _ffn_bwd_act(dx, G1, U1, wffn1, 2, token)
        token = pair_end_chip_begin(f"{l}a", handles, d)
        gffn1 = _wgrad(gffn1, 2, A1, d)
        gffn1 = _wgrad(gffn1, 0, dG, h1)
        gffn1 = _wgrad(gffn1, 1, dU, h1)
        if l > 0:
            handles, token = pair_begin(f"{l}b", [gffn1], token)
            dx, g_n1[l] = _ffn_bwd_in(dG, dU, xa, dx, n1, wffn1, 0, 1, l, token)
            token = pair_end_chip_begin(f"{l}b", handles, dx)
        else:
            token = chip_begin(f"{l}b", [gffn1], _pair_exchange([gffn1]), token)
            dx, g_n1[l] = _ffn_bwd_in(dG, dU, xa, dx, n1, wffn1, 0, 1, l, token)

    grad_x = dx.reshape(1, T, D)
    for tag, group, got, handles in in_flight:
        parts = _exchange_wait(f"chip_wait_{tag}", _chip_plan, *handles, after=dx)[1]
        reduced[tag] = [_sum_chips(g, r, 2 * own_chip + cc, own_chip, p) for g, r, p in zip(group, got, parts)]
    g1 = jnp.stack([reduced[f"{l}b"][0] for l in range(L)])
    g2 = jnp.stack([reduced[f"{l}c"][0] for l in range(L)])
    gin_t = jnp.concatenate([reduced[f"{l}a"][0] for l in range(L)])
    gout_sh = jnp.concatenate([reduced[f"{l}a"][1] for l in range(L)])

    small = [loss_part,
             jnp.concatenate(g_n1), jnp.concatenate(g_nm), jnp.concatenate(g_n2), g_final,
             jnp.concatenate(g_cb), jnp.concatenate(g_lg), jnp.concatenate(g_lb),
             jnp.stack(g_sink)[:, :, 0], jnp.stack(g_cw)]
    small_shapes = [(1, 128), (L, D), (L, D), (L, D), (D,), (L, CC), (L, CC), (L, CC), (L, NH), (L, CW, CC)]
    tot = _unpack(_all_reduce_small(_pack(small)), small_shapes)
    loss = tot[0][0, 0]
    gr_n1, gr_nm, gr_n2, gr_final, gr_cb, gr_lg, gr_lb, gr_sink, gr_cw_full = tot[1:]
    gr_cw = lax.dynamic_slice_in_dim(gr_cw_full, dev * cw_cols, cw_cols, axis=2)

    grads_t = {"ffn1_w_gate": g1[:, 0], "ffn1_w_up": g1[:, 1], "ffn2_w_gate": g2[:, 0], "ffn2_w_up": g2[:, 1],
               "w_in": gin_t}
    grads = {
        "ffn1_norm": gr_n1, "ffn1_w_down": g1[:, 2],
        "mix_norm": gr_nm, "conv_w": gr_cw, "conv_b": gr_cb, "conv_ln_g": gr_lg,
        "conv_ln_b": gr_lb, "attn_sinks": gr_sink, "w_out": gout_sh,
        "ffn2_norm": gr_n2, "ffn2_w_down": g2[:, 2],
        "final_norm": gr_final,
    }
    weights = dict(ffn1_norm=ffn1_norm, ffn1_w_gate=ffn1_w_gate, ffn1_w_up=ffn1_w_up, ffn1_w_down=ffn1_w_down, mix_norm=mix_norm, w_in=w_in, conv_w=conv_w, conv_b=conv_b, conv_ln_g=conv_ln_g, conv_ln_b=conv_ln_b, attn_sinks=attn_sinks, w_out=w_out, ffn2_norm=ffn2_norm, ffn2_w_gate=ffn2_w_gate, ffn2_w_up=ffn2_w_up, ffn2_w_down=ffn2_w_down, final_norm=final_norm)
    moms = dict(ffn1_norm=m_ffn1_norm, ffn1_w_gate=m_ffn1_w_gate, ffn1_w_up=m_ffn1_w_up, ffn1_w_down=m_ffn1_w_down, mix_norm=m_mix_norm, w_in=m_w_in, conv_w=m_conv_w, conv_b=m_conv_b, conv_ln_g=m_conv_ln_g, conv_ln_b=m_conv_ln_b, attn_sinks=m_attn_sinks, w_out=m_w_out, ffn2_norm=m_ffn2_norm, ffn2_w_gate=m_ffn2_w_gate, ffn2_w_up=m_ffn2_w_up, ffn2_w_down=m_ffn2_w_down, final_norm=m_final_norm)
    vels = dict(ffn1_norm=v_ffn1_norm, ffn1_w_gate=v_ffn1_w_gate, ffn1_w_up=v_ffn1_w_up, ffn1_w_down=v_ffn1_w_down, mix_norm=v_mix_norm, w_in=v_w_in, conv_w=v_conv_w, conv_b=v_conv_b, conv_ln_g=v_conv_ln_g, conv_ln_b=v_conv_ln_b, attn_sinks=v_attn_sinks, w_out=v_w_out, ffn2_norm=v_ffn2_norm, ffn2_w_gate=v_ffn2_w_gate, ffn2_w_up=v_ffn2_w_up, ffn2_w_down=v_ffn2_w_down, final_norm=v_final_norm)

    names = list(weights)
    big = ("ffn1_w_gate", "ffn1_w_up", "ffn1_w_down", "w_in", "w_out", "ffn2_w_gate", "ffn2_w_up", "ffn2_w_down")
    delta, new_m, new_v = {}, {}, {}
    for k in big:
        if k in grads_t:
            outs = _adamw(t_(weights[k]), grads_t[k], t_(moms[k]), t_(vels[k]))
            grads[k], delta[k], new_m[k], new_v[k] = [t_(o) for o in [grads_t[k]] + outs]
        else:
            delta[k], new_m[k], new_v[k] = _adamw(weights[k], grads[k], moms[k], vels[k])
    rest = [k for k in names if k not in big]
    rest_shapes = [weights[k].shape for k in rest]
    packed = _adamw(*[_pack([t[k] for k in rest]) for t in (weights, grads, moms, vels)])
    for res, packed_out in zip((delta, new_m, new_v), packed):
        for k, val in zip(rest, _unpack(packed_out, rest_shapes)):
            res[k] = val

    return (loss, grad_x, *[grads[k] for k in names], *[delta[k] for k in names],
            *[new_m[k] for k in names], *[new_v[k] for k in names])
```

```python
import functools

import jax
import jax.numpy as jnp
from jax import lax
from jax.experimental import pallas as pl
from jax.experimental.pallas import tpu as pltpu

F32 = jnp.float32
BF16 = jnp.bfloat16
MESH = pl.DeviceIdType.MESH

N_DEV = 8
N_CHIP = 4
D = 1024
FF = 2816
HD = 64
NH = 8
NKV = 2
GROUP = NH // NKV
AW = NH * HD
KVW = NKV * HD
QKW = AW + KVW
QKVW = AW + 2 * KVW
CC = 512
CW = 31
DIN = QKVW + 2 * CC
BLK = 128
HALO = 32
EPS = 1e-5
SCALE = HD ** -0.5
NEG = float(jnp.finfo(jnp.float32).min)

LR, B1, B2, ADAM_EPS, WD, STEP = 0.001, 0.9, 0.999, 1e-08, 0.01, 10

TM = 512
TM_FFN_UP = 256
TM_CONV_BWD = 64
TK = 1024
FC = 256
ATT_BLOCKS = 8
VMEM_LIMIT = 56 * 1024 * 1024


def _cp(*sem):
    return pltpu.CompilerParams(dimension_semantics=sem, vmem_limit_bytes=VMEM_LIMIT)


def _row(tm, c):
    return pl.BlockSpec((tm, c), lambda i: (i, 0))


def _slab(shape, k, single=False):
    zeros = (0,) * len(shape)
    kw = dict(pipeline_mode=pl.Buffered(1)) if single else {}
    return pl.BlockSpec((None, *shape), lambda i: (k, *zeros), **kw)


def _acc(shape):
    return pl.BlockSpec(shape, lambda i: (0,) * len(shape))


def _nt(a, b):
    return lax.dot_general(a, b, (((1,), (1,)), ((), ())), preferred_element_type=F32)


def _tn(a, b):
    return lax.dot_general(a, b, (((0,), (0,)), ((), ())), preferred_element_type=F32)


def _nn(a, b):
    return jnp.dot(a, b, preferred_element_type=F32)


def _sigmoid(x):
    return jax.nn.sigmoid(x)


def _dsilu(z):
    s = _sigmoid(z)
    return s * (1.0 + z * (1.0 - s))


def _rms(x, g):
    r = lax.rsqrt(jnp.mean(x * x, axis=-1, keepdims=True) + EPS)
    xh = x * r
    return xh, r, xh * g


def _rms_bwd(dh, xh, r, g):
    dxh = dh * g
    return r * (dxh - xh * jnp.mean(dxh * xh, axis=-1, keepdims=True))


def _rope(t, c128, s128):
    w = t.shape[1]
    lane = lax.broadcasted_iota(jnp.int32, t.shape, 1)
    rot = jnp.where(lane % HD < HD // 2, pltpu.roll(t, w - HD // 2, 1), pltpu.roll(t, HD // 2, 1))
    return t * jnp.tile(c128, (1, w // 128)) + rot * jnp.tile(s128, (1, w // 128))


def _ffn_up(x, norm, wffn, sg, su, layer, token):
    T = x.shape[0]
    tm = TM_FFN_UP

    def body(x_ref, g_ref, wg_ref, wu_ref, token_ref, h_ref, P_ref, Q_ref, A_ref):
        _, _, hn = _rms(x_ref[...], g_ref[...])
        h = hn.astype(BF16)
        h_ref[...] = h
        for c in range(FF // FC):
            sl = slice(c * FC, (c + 1) * FC)
            g = _nt(h, wg_ref[sl, :])
            u = _nt(h, wu_ref[sl, :])
            s = _sigmoid(g)
            p = g * s
            P_ref[:, sl] = p.astype(BF16)
            Q_ref[:, sl] = (u * (s + p - p * s)).astype(BF16)
            A_ref[:, sl] = (p * u).astype(BF16)

    return pl.pallas_call(
        body, name="ffn_up", grid=(T // tm,),
        in_specs=[_row(tm, D), _slab((1, D), layer), _slab((FF, D), sg, True), _slab((FF, D), su, True), HBM],
        out_specs=[_row(tm, D), _row(tm, FF), _row(tm, FF), _row(tm, FF)],
        out_shape=[jax.ShapeDtypeStruct((T, D), BF16)] + [jax.ShapeDtypeStruct((T, FF), BF16)] * 3,
        compiler_params=_cp("parallel"),
    )(x, norm, wffn, wffn, token)


def _ffn_down(a, x, wffn, sd, token):
    T = x.shape[0]

    def body(a_ref, x_ref, w_ref, token_ref, o_ref):
        o_ref[...] = x_ref[...] + 0.5 * _nn(a_ref[...], w_ref[...])

    return pl.pallas_call(
        body, name="ffn_down", grid=(T // TM,),
        in_specs=[_row(TM, FF), _row(TM, D), _slab((FF, D), sd, True), HBM],
        out_specs=_row(TM, D),
        out_shape=jax.ShapeDtypeStruct((T, D), F32),
        compiler_params=_cp("parallel"),
    )(a, x, wffn, token)


def _mix_in(x, norm, win, rc, rs, layer, token):
    T = x.shape[0]

    def body(x_ref, g_ref, w_ref, c_ref, s_ref, token_ref, h_ref, qkv_ref, u_ref):
        _, _, hn = _rms(x_ref[...], g_ref[...])
        h = hn.astype(BF16)
        h_ref[...] = h
        qk = _nt(h, w_ref[0:QKW, :])
        qkv_ref[:, 0:QKW] = _rope(qk, c_ref[...], s_ref[...]).astype(BF16)
        qkv_ref[:, QKW:QKVW] = _nt(h, w_ref[QKW:QKVW, :]).astype(BF16)
        for c in range(2 * CC // FC):
            u_ref[:, c * FC:(c + 1) * FC] = _nt(h, w_ref[QKVW + c * FC:QKVW + (c + 1) * FC, :]).astype(BF16)

    return pl.pallas_call(
        body, name="mix_in", grid=(T // TM,),
        in_specs=[_row(TM, D), _slab((1, D), layer), _slab((DIN, D), 0, True), _row(TM, 128), _row(TM, 128), HBM],
        out_specs=[_row(TM, D), _row(TM, QKVW), _row(TM, 2 * CC)],
        out_shape=[jax.ShapeDtypeStruct((T, D), BF16), jax.ShapeDtypeStruct((T, QKVW), BF16),
                   jax.ShapeDtypeStruct((T, 2 * CC), BF16)],
        compiler_params=_cp("parallel"),
    )(x, norm, win, rc, rs, token)


def _band_mask(has_prev):
    j = lax.broadcasted_iota(jnp.int32, (2 * BLK, BLK), 0)
    r = lax.broadcasted_iota(jnp.int32, (2 * BLK, BLK), 1) + BLK
    rel = r - j
    return jnp.tile((rel >= 0) & (rel < BLK) & (has_prev | (j >= BLK)), (1, GROUP))


def _band(prev_ref, cur_ref, b, col):
    if b == 0:
        return jnp.concatenate([prev_ref[:, col:col + HD], cur_ref[0:BLK, col:col + HD]], axis=0)
    return cur_ref[(b - 1) * BLK:(b + 1) * BLK, col:col + HD]


def _stack_heads(ref, b, kv):
    cols = [(kv * GROUP + g) * HD for g in range(GROUP)]
    return jnp.concatenate([ref[b * BLK:(b + 1) * BLK, c:c + HD] for c in cols], axis=0)


def _unstack_t(xt):
    x = xt.T
    return jnp.concatenate([x[g * BLK:(g + 1) * BLK, :] for g in range(GROUP)], axis=1)


def _sink_row(sink_ref, layer, kv):
    return jnp.concatenate([jnp.full((1, BLK), sink_ref[layer, kv * GROUP + g], F32) for g in range(GROUP)], axis=1)


def _probs_t(q4, kb, mask, sink):
    s = jnp.where(mask, _nt(kb, q4) * SCALE, NEG)
    m = jnp.maximum(jnp.max(s, axis=0, keepdims=True), sink)
    p = jnp.exp(s - m)
    e = jnp.exp(sink - m)
    inv = 1.0 / (jnp.sum(p, axis=0, keepdims=True) + e)
    return p * inv, e * inv


def _attn_fwd(qkv, sinks, layer):
    T = qkv.shape[0]
    tq = ATT_BLOCKS * BLK

    def body(sink_ref, cur_ref, prev_ref, o_ref):
        first = _band_mask(pl.program_id(0) > 0)
        later = _band_mask(True)
        for b in range(ATT_BLOCKS):
            outs = []
            for kv in range(NKV):
                kb = _band(prev_ref, cur_ref, b, AW + kv * HD)
                vb = _band(prev_ref, cur_ref, b, QKW + kv * HD)
                pt, _ = _probs_t(_stack_heads(cur_ref, b, kv), kb, first if b == 0 else later,
                                 _sink_row(sink_ref, layer, kv))
                outs.append(_unstack_t(_nn(vb.T, pt.astype(BF16))))
            o_ref[b * BLK:(b + 1) * BLK, :] = jnp.concatenate(outs, axis=1).astype(BF16)

    return pl.pallas_call(
        body, name="attn_fwd", grid=(T // tq,),
        in_specs=[pl.BlockSpec(memory_space=pltpu.SMEM), _row(tq, QKVW),
                  pl.BlockSpec((BLK, QKVW), lambda i: (jnp.maximum(i * ATT_BLOCKS - 1, 0), 0))],
        out_specs=_row(tq, AW),
        out_shape=jax.ShapeDtypeStruct((T, AW), BF16),
        compiler_params=_cp("parallel"),
    )(sinks, qkv, qkv)


def _glu(u):
    u = u.astype(F32)
    return u[:, :CC] * _sigmoid(u[:, CC:])


def _fill_ext(ext_ref, first, second):
    n = ext_ref.shape[0] - 8
    ext_ref[0:first.shape[0], :] = first
    ext_ref[first.shape[0]:n, :] = second
    ext_ref[n:, :] = jnp.zeros((8, ext_ref.shape[1]), F32)


def _taps(ext_ref, w_ref, offsets, cols, tm):
    y = None
    for b in range(8):
        z = None
        for k, off in enumerate(offsets):
            if off % 8 == b:
                term = w_ref[k:k + 1, cols] * ext_ref[pl.ds(off - b, tm + 8), cols]
                z = term if z is None else z + term
        if z is not None:
            y = z[b:b + tm, :] if y is None else y + z[b:b + tm, :]
    return y


def _conv_fwd(u, cw, cb, lg, lb, layer):
    T = u.shape[0]
    tm = TM

    def body(u_ref, up_ref, w_ref, b_ref, g_ref, bb_ref, y_ref, o_ref, ext_ref):
        i = pl.program_id(0)
        _fill_ext(ext_ref, jnp.where(i > 0, _glu(up_ref[...]), 0.0), _glu(u_ref[...]))
        for c in range(CC // 128):
            cols = slice(c * 128, (c + 1) * 128)
            y_ref[:, cols] = _taps(ext_ref, w_ref, [HALO - (CW - 1) + k for k in range(CW)], cols, tm) + b_ref[:, cols]
        y = y_ref[...]
        xc = y - jnp.mean(y, axis=-1, keepdims=True)
        z = xc * lax.rsqrt(jnp.mean(xc * xc, axis=-1, keepdims=True) + EPS) * g_ref[...] + bb_ref[...]
        o_ref[...] = (z * _sigmoid(z)).astype(BF16)

    return pl.pallas_call(
        body, name="conv_fwd", grid=(T // tm,),
        in_specs=[_row(tm, 2 * CC),
                  pl.BlockSpec((HALO, 2 * CC), lambda i: (jnp.maximum(i * (tm // HALO) - 1, 0), 0)),
                  _slab((CW, CC), layer), _slab((1, CC), layer), _slab((1, CC), layer), _slab((1, CC), layer)],
        out_specs=[_row(tm, CC), _row(tm, CC)],
        out_shape=[jax.ShapeDtypeStruct((T, CC), F32), jax.ShapeDtypeStruct((T, CC), BF16)],
        scratch_shapes=[pltpu.VMEM((tm + HALO + 8, CC), F32)],
        compiler_params=_cp("parallel"),
    )(u, u, cw, cb, lg, lb)


def _mix_out(ao, co, x, wout):
    T = x.shape[0]

    def body(ao_ref, co_ref, x_ref, w_ref, o_ref, cat_ref):
        cat = jnp.concatenate([ao_ref[...], co_ref[...]], axis=1)
        cat_ref[...] = cat
        o_ref[...] = x_ref[...] + _nn(cat, w_ref[...])

    return pl.pallas_call(
        body, name="mix_out", grid=(T // TM,),
        in_specs=[_row(TM, AW), _row(TM, CC), _row(TM, D), _slab((D, D), 0, True)],
        out_specs=[_row(TM, D), _row(TM, D)],
        out_shape=[jax.ShapeDtypeStruct((T, D), F32), jax.ShapeDtypeStruct((T, D), BF16)],
        compiler_params=_cp("parallel"),
    )(ao, co, x, wout)


def _final(x, norm, target):
    T = x.shape[0]

    def body(x_ref, g_ref, t_ref, dx_ref, loss_ref, dg_ref):
        @pl.when(pl.program_id(0) == 0)
        def _():
            loss_ref[...] = jnp.zeros_like(loss_ref)
            dg_ref[...] = jnp.zeros_like(dg_ref)

        g = g_ref[...]
        xh, r, y = _rms(x_ref[...], g)
        err = y - t_ref[...]
        loss_ref[...] += jnp.full(loss_ref.shape, (0.5 / D) * jnp.sum(err * err), F32)
        dy = err * (1.0 / D)
        dg_ref[...] += jnp.sum(dy * xh, axis=0, keepdims=True)
        dx_ref[...] = _rms_bwd(dy, xh, r, g)

    return pl.pallas_call(
        body, name="final_loss", grid=(T // TM,),
        in_specs=[_row(TM, D), _acc((1, D)), _row(TM, D)],
        out_specs=[_row(TM, D), _acc((1, 128)), _acc((1, D))],
        out_shape=[jax.ShapeDtypeStruct((T, D), F32), jax.ShapeDtypeStruct((1, 128), F32),
                   jax.ShapeDtypeStruct((1, D), F32)],
        compiler_params=_cp("arbitrary"),
    )(x, norm, target)


def _ffn_bwd_act(dx, P, Q, wffn, sd, token):
    T = dx.shape[0]

    def body(dx_ref, P_ref, Q_ref, w_ref, token_ref, d_ref, dG_ref, dU_ref):
        d = (0.5 * dx_ref[...]).astype(BF16)
        d_ref[...] = d
        for c in range(FF // FC):
            sl = slice(c * FC, (c + 1) * FC)
            da = _nt(d, w_ref[sl, :])
            dU_ref[:, sl] = (da * P_ref[:, sl].astype(F32)).astype(BF16)
            dG_ref[:, sl] = (da * Q_ref[:, sl].astype(F32)).astype(BF16)

    return pl.pallas_call(
        body, name="ffn_bwd_act", grid=(T // TM,),
        in_specs=[_row(TM, D), _row(TM, FF), _row(TM, FF), _slab((FF, D), sd, True), HBM],
        out_specs=[_row(TM, D), _row(TM, FF), _row(TM, FF)],
        out_shape=[jax.ShapeDtypeStruct((T, D), BF16)] + [jax.ShapeDtypeStruct((T, FF), BF16)] * 2,
        compiler_params=_cp("parallel"),
    )(dx, P, Q, wffn, token)


def _ffn_bwd_in(dG, dU, x, dx, norm, wffn, sg, su, layer, token):
    T = x.shape[0]

    def body(dG_ref, dU_ref, x_ref, dx_ref, g_ref, wg_ref, wu_ref, token_ref, o_ref, dg_ref):
        @pl.when(pl.program_id(0) == 0)
        def _():
            dg_ref[...] = jnp.zeros_like(dg_ref)

        dh = _nn(dG_ref[...], wg_ref[...]) + _nn(dU_ref[...], wu_ref[...])
        g = g_ref[...]
        xh, r, _ = _rms(x_ref[...], g)
        dg_ref[...] += jnp.sum(dh * xh, axis=0, keepdims=True)
        o_ref[...] = dx_ref[...] + _rms_bwd(dh, xh, r, g)

    return pl.pallas_call(
        body, name="ffn_bwd_in", grid=(T // TM,),
        in_specs=[_row(TM, FF), _row(TM, FF), _row(TM, D), _row(TM, D), _slab((1, D), layer),
                  _slab((FF, D), sg, True), _slab((FF, D), su, True), HBM],
        out_specs=[_row(TM, D), _acc((1, D))],
        out_shape=[jax.ShapeDtypeStruct((T, D), F32), jax.ShapeDtypeStruct((1, D), F32)],
        compiler_params=_cp("arbitrary"),
    )(dG, dU, x, dx, norm, wffn, wffn, token)


def _mix_out_bwd(dx, wout, token):
    T = dx.shape[0]

    def body(dx_ref, w_ref, token_ref, d_ref, dao_ref, dco_ref):
        d = dx_ref[...].astype(BF16)
        d_ref[...] = d
        dcat = _nt(d, w_ref[...])
        dao_ref[...] = dcat[:, :AW].astype(BF16)
        dco_ref[...] = dcat[:, AW:].astype(BF16)

    return pl.pallas_call(
        body, name="mix_out_bwd", grid=(T // TM,),
        in_specs=[_row(TM, D), _slab((D, D), 0, True), HBM],
        out_specs=[_row(TM, D), _row(TM, AW), _row(TM, CC)],
        out_shape=[jax.ShapeDtypeStruct((T, D), BF16), jax.ShapeDtypeStruct((T, AW), BF16),
                   jax.ShapeDtypeStruct((T, CC), BF16)],
        compiler_params=_cp("parallel"),
    )(dx, wout, token)


def _conv_bwd_norm(dco, y, lg, lb, layer):
    T = y.shape[0]

    def body(dco_ref, y_ref, g_ref, bb_ref, dy_ref, dlg_ref, dlb_ref, dcb_ref):
        @pl.when(pl.program_id(0) == 0)
        def _():
            dlg_ref[...] = jnp.zeros_like(dlg_ref)
            dlb_ref[...] = jnp.zeros_like(dlb_ref)
            dcb_ref[...] = jnp.zeros_like(dcb_ref)

        y = y_ref[...]
        g = g_ref[...]
        xc = y - jnp.mean(y, axis=-1, keepdims=True)
        rs = lax.rsqrt(jnp.mean(xc * xc, axis=-1, keepdims=True) + EPS)
        xn = xc * rs
        z = xn * g + bb_ref[...]
        dz = dco_ref[...].astype(F32) * _dsilu(z)
        dlg_ref[...] += jnp.sum(dz * xn, axis=0, keepdims=True)
        dlb_ref[...] += jnp.sum(dz, axis=0, keepdims=True)
        dxn = dz * g
        dy = rs * (dxn - jnp.mean(dxn, axis=-1, keepdims=True) - xn * jnp.mean(dxn * xn, axis=-1, keepdims=True))
        dcb_ref[...] += jnp.sum(dy, axis=0, keepdims=True)
        dy_ref[...] = dy

    return pl.pallas_call(
        body, name="conv_bwd_norm", grid=(T // TM,),
        in_specs=[_row(TM, CC), _row(TM, CC), _slab((1, CC), layer), _slab((1, CC), layer)],
        out_specs=[_row(TM, CC), _acc((1, CC)), _acc((1, CC)), _acc((1, CC))],
        out_shape=[jax.ShapeDtypeStruct((T, CC), F32)] + [jax.ShapeDtypeStruct((1, CC), F32)] * 3,
        compiler_params=_cp("arbitrary"),
    )(dco, y, lg, lb)


def _conv_bwd_taps(dy, u, cw, layer):
    T = u.shape[0]
    tm = TM_CONV_BWD
    n_halo = T // HALO

    def body(dy_ref, dyn_ref, u_ref, up_ref, w_ref, du_ref, dw_ref, hext_ref, dext_ref, dz_ref, dsh_ref, dh_ref):
        i = pl.program_id(0)

        @pl.when(i == 0)
        def _():
            dw_ref[...] = jnp.zeros_like(dw_ref)

        _fill_ext(hext_ref, jnp.where(i > 0, _glu(up_ref[...]), 0.0), _glu(u_ref[...]))
        _fill_ext(dext_ref, dy_ref[...], jnp.where(i < pl.num_programs(0) - 1, dyn_ref[...], 0.0))
        _fill_ext(dz_ref, jnp.zeros((8, CC), F32), dy_ref[...])
        for b in range(8):
            dsh_ref[b] = dz_ref[pl.ds(8 - b, tm + 8), :]
        h_offsets = [HALO - (CW - 1) + k for k in range(CW)]
        for c in range(CC // 128):
            cols = slice(c * 128, (c + 1) * 128)
            dh_ref[:, cols] = _taps(dext_ref, w_ref, [CW - 1 - k for k in range(CW)], cols, tm)
            for k, off in enumerate(h_offsets):
                b = off % 8
                prod = dsh_ref[b, :, cols] * hext_ref[pl.ds(off - b, tm + 8), cols]
                dw_ref[8 * k:8 * k + 8, cols] += jnp.sum(prod.reshape((tm + 8) // 8, 8, 128), axis=0)
        dh = dh_ref[...]
        uu = u_ref[...].astype(F32)
        a = uu[:, :CC]
        sg = _sigmoid(uu[:, CC:])
        du_ref[:, :CC] = (dh * sg).astype(BF16)
        du_ref[:, CC:] = (dh * a * sg * (1.0 - sg)).astype(BF16)

    return pl.pallas_call(
        body, name="conv_bwd_taps", grid=(T // tm,),
        in_specs=[_row(tm, CC),
                  pl.BlockSpec((HALO, CC), lambda i: (jnp.minimum((i + 1) * (tm // HALO), n_halo - 1), 0)),
                  _row(tm, 2 * CC),
                  pl.BlockSpec((HALO, 2 * CC), lambda i: (jnp.maximum(i * (tm // HALO) - 1, 0), 0)),
                  _slab((CW, CC), layer)],
        out_specs=[_row(tm, 2 * CC), _acc((CW * 8, CC))],
        out_shape=[jax.ShapeDtypeStruct((T, 2 * CC), BF16), jax.ShapeDtypeStruct((CW * 8, CC), F32)],
        scratch_shapes=[pltpu.VMEM((tm + HALO + 8, CC), F32), pltpu.VMEM((tm + HALO + 8, CC), F32),
                        pltpu.VMEM((tm + 16, CC), F32), pltpu.VMEM((8, tm + 8, CC), F32), pltpu.VMEM((tm, CC), F32)],
        compiler_params=_cp("arbitrary"),
    )(dy, dy, u, u, cw)


def _attn_bwd(qkv, dao, sinks, layer):
    T = qkv.shape[0]
    tq = ATT_BLOCKS * BLK

    def body(sink_ref, cur_ref, prev_ref, do_ref, dq_ref, dk_ref, dv_ref, ds_ref):
        i = pl.program_id(0)

        @pl.when(i == 0)
        def _():
            dk_ref[...] = jnp.zeros_like(dk_ref)
            dv_ref[...] = jnp.zeros_like(dv_ref)
            ds_ref[...] = jnp.zeros_like(ds_ref)

        first = _band_mask(i > 0)
        later = _band_mask(True)
        base = pl.multiple_of(i * tq, tq)
        before = pl.multiple_of(jnp.maximum(i * ATT_BLOCKS - 1, 0) * BLK, BLK)
        for b in range(ATT_BLOCKS):
            dqs, dks, dvs = [], [], []
            for kv in range(NKV):
                kb = _band(prev_ref, cur_ref, b, AW + kv * HD)
                vb = _band(prev_ref, cur_ref, b, QKW + kv * HD)
                q4 = _stack_heads(cur_ref, b, kv)
                do4 = _stack_heads(do_ref, b, kv)
                pt, psink = _probs_t(q4, kb, first if b == 0 else later, _sink_row(sink_ref, layer, kv))
                dpt = _nt(vb, do4)
                dd = jnp.sum(pt * dpt, axis=0, keepdims=True)
                dst = (pt * (dpt - dd) * SCALE).astype(BF16)
                sd = psink * dd
                for g in range(GROUP):
                    hh = kv * GROUP + g
                    ds_ref[hh:hh + 1, :] += jnp.full((1, 128), -jnp.sum(sd[:, g * BLK:(g + 1) * BLK]), F32)
                dqs.append(_unstack_t(_nn(kb.T, dst)))
                dks.append(_nn(dst, q4))
                dvs.append(_nn(pt.astype(BF16), do4))
            dq_ref[b * BLK:(b + 1) * BLK, :] = jnp.concatenate(dqs, axis=1).astype(BF16)
            dkband = jnp.concatenate(dks, axis=1)
            dvband = jnp.concatenate(dvs, axis=1)
            if b == 0:
                dk_ref[pl.ds(before, BLK), :] += dkband[:BLK]
                dv_ref[pl.ds(before, BLK), :] += dvband[:BLK]
                dk_ref[pl.ds(base, BLK), :] += dkband[BLK:]
                dv_ref[pl.ds(base, BLK), :] += dvband[BLK:]
            else:
                r0 = pl.multiple_of(base + (b - 1) * BLK, BLK)
                dk_ref[pl.ds(r0, 2 * BLK), :] += dkband
                dv_ref[pl.ds(r0, 2 * BLK), :] += dvband

    return pl.pallas_call(
        body, name="attn_bwd", grid=(T // tq,),
        in_specs=[pl.BlockSpec(memory_space=pltpu.SMEM), _row(tq, QKVW),
                  pl.BlockSpec((BLK, QKVW), lambda i: (jnp.maximum(i * ATT_BLOCKS - 1, 0), 0)), _row(tq, AW)],
        out_specs=[_row(tq, AW), _acc((T, KVW)), _acc((T, KVW)), _acc((NH, 128))],
        out_shape=[jax.ShapeDtypeStruct((T, AW), BF16), jax.ShapeDtypeStruct((T, KVW), F32),
                   jax.ShapeDtypeStruct((T, KVW), F32), jax.ShapeDtypeStruct((NH, 128), F32)],
        compiler_params=_cp("arbitrary"),
    )(sinks, qkv, qkv, dao)


def _mix_in_bwd(dq, dk, dv, du, rc, rs, x, dx, norm, win, layer):
    T = x.shape[0]

    def body(dq_ref, dk_ref, dv_ref, du_ref, c_ref, s_ref, x_ref, dx_ref, g_ref, w_ref, dp_ref, o_ref, dg_ref):
        @pl.when(pl.program_id(0) == 0)
        def _():
            dg_ref[...] = jnp.zeros_like(dg_ref)

        dqk = jnp.concatenate([dq_ref[...].astype(F32), dk_ref[...]], axis=1)
        dqk = _rope(dqk, c_ref[...], -s_ref[...])
        dp = jnp.concatenate([dqk.astype(BF16), dv_ref[...].astype(BF16), du_ref[...]], axis=1)
        dp_ref[...] = dp
        dh = _nn(dp, w_ref[...])
        g = g_ref[...]
        xh, r, _ = _rms(x_ref[...], g)
        dg_ref[...] += jnp.sum(dh * xh, axis=0, keepdims=True)
        o_ref[...] = dx_ref[...] + _rms_bwd(dh, xh, r, g)

    return pl.pallas_call(
        body, name="mix_in_bwd", grid=(T // TM,),
        in_specs=[_row(TM, AW), _row(TM, KVW), _row(TM, KVW), _row(TM, 2 * CC), _row(TM, 128), _row(TM, 128),
                  _row(TM, D), _row(TM, D), _slab((1, D), layer), _slab((DIN, D), 0, True)],
        out_specs=[_row(TM, DIN), _row(TM, D), _acc((1, D))],
        out_shape=[jax.ShapeDtypeStruct((T, DIN), BF16), jax.ShapeDtypeStruct((T, D), F32),
                   jax.ShapeDtypeStruct((1, D), F32)],
        compiler_params=_cp("arbitrary"),
    )(dq, dk, dv, du, rc, rs, x, dx, norm, win)


def _wgrad(buf, slab, a, b):
    T, M = a.shape
    N = b.shape[1]
    tmm = M // 2 if M > 1024 else M

    def body(buf_ref, a_ref, b_ref, o_ref):
        @pl.when(pl.program_id(1) == 0)
        def _():
            o_ref[...] = jnp.zeros_like(o_ref)

        o_ref[...] += _tn(a_ref[...], b_ref[...])

    return pl.pallas_call(
        body, name="wgrad", grid=(M // tmm, T // TK),
        in_specs=[pl.BlockSpec(memory_space=pl.ANY),
                  pl.BlockSpec((TK, tmm), lambda i, k: (k, i)), pl.BlockSpec((TK, N), lambda i, k: (k, 0))],
        out_specs=pl.BlockSpec((None, tmm, N), lambda i, k: (slab, i, 0)),
        out_shape=jax.ShapeDtypeStruct(buf.shape, F32),
        input_output_aliases={0: 0},
        compiler_params=_cp("parallel", "arbitrary"),
    )(buf, a, b)


HBM = pl.BlockSpec(memory_space=pl.ANY)


def _coords():
    return lax.axis_index("x"), lax.axis_index("y"), lax.axis_index("c")


def _other_chips(x, y):
    return [(1 - x, y), (x, 1 - y), (1 - x, 1 - y)]


def _all_gather(shards):
    n = len(shards)

    def body(*refs):
        ins, outs = refs[:n], refs[n:2 * n]
        send_sems, recv_sems, local_sems = refs[2 * n:]
        x, y, c = _coords()
        me, sibling = (x, y, c), (x, y, 1 - c)
        chips = _other_chips(x, y)

        def rows(a, dev):
            r = ins[a].shape[1]
            return outs[a].at[:, pl.ds(pl.multiple_of((4 * dev[0] + 2 * dev[1] + dev[2]) * r, r), r), :]

        def copy(a, k, block, to, src=None):
            return pltpu.make_async_remote_copy(
                src_ref=rows(a, block) if src is None else src, dst_ref=rows(a, block),
                send_sem=send_sems.at[a * 7 + k], recv_sem=recv_sems.at[a * 7 + k],
                device_id=to, device_id_type=MESH)

        mine = [pltpu.make_async_copy(ins[a], rows(a, me), local_sems.at[a]) for a in range(n)]
        for cp in mine:
            cp.start()
        first = []
        for a in range(n):
            first.append(copy(a, 0, me, sibling, src=ins[a]))
            first += [copy(a, 1 + j, me, (*chip, c), src=ins[a]) for j, chip in enumerate(chips)]
        for cp in first:
            cp.start()
        passed = []
        for j, chip in enumerate(chips):
            for a in range(n):
                copy(a, 1 + j, (*chip, c), me).wait_recv()
                fwd = copy(a, 4 + j, (*chip, c), sibling)
                fwd.start()
                passed.append(fwd)
        for a in range(n):
            copy(a, 0, sibling, me).wait_recv()
            for j, chip in enumerate(chips):
                copy(a, 4 + j, (*chip, 1 - c), me).wait_recv()
        for cp in first + passed:
            cp.wait_send()
        for cp in mine:
            cp.wait()

    return pl.pallas_call(
        body, name="all_gather_weights",
        in_specs=[HBM] * n, out_specs=[HBM] * n,
        out_shape=[jax.ShapeDtypeStruct((s.shape[0], N_DEV * s.shape[1], s.shape[2]), s.dtype) for s in shards],
        scratch_shapes=[pltpu.SemaphoreType.DMA((7 * n,)), pltpu.SemaphoreType.DMA((7 * n,)),
                        pltpu.SemaphoreType.DMA((n,))],
    )(*shards)


def _pair_exchange(grads):
    n = len(grads)

    def body(*refs):
        ins, got = refs[:n], refs[n:2 * n]
        send_sems, recv_sems = refs[2 * n:]
        x, y, c = _coords()
        sibling = (x, y, 1 - c)

        def remote(a, q):
            r = ins[a].shape[1] // N_DEV
            src = ins[a].at[:, pl.ds(pl.multiple_of((2 * q + 1 - c) * r, r), r), :]
            return pltpu.make_async_remote_copy(
                src_ref=src, dst_ref=got[a].at[q],
                send_sem=send_sems.at[a * N_CHIP + q], recv_sem=recv_sems.at[a * N_CHIP + q],
                device_id=sibling, device_id_type=MESH)

        sends = [remote(a, q) for a in range(n) for q in range(N_CHIP)]
        for cp in sends:
            cp.start()
        for cp in sends:
            cp.wait_recv()
        for cp in sends:
            cp.wait_send()

    return pl.pallas_call(
        body, name="grad_pair_exchange",
        in_specs=[HBM] * n, out_specs=[HBM] * n,
        out_shape=[jax.ShapeDtypeStruct((N_CHIP, g.shape[0], g.shape[1] // N_DEV, g.shape[2]), g.dtype) for g in grads],
        scratch_shapes=[pltpu.SemaphoreType.DMA((N_CHIP * n,)), pltpu.SemaphoreType.DMA((N_CHIP * n,))],
    )(*grads)


def _all_reduce_small(pack):
    R = pack.shape[0]

    def body(p_ref, tot_ref, all_ref, send_sems, recv_sems):
        x, y, c = _coords()
        me = 4 * x + 2 * y + c
        all_ref[me] = p_ref[...]
        peers = []
        for k in range(1, N_DEV):
            bx, by, bc = (k >> 2) & 1, (k >> 1) & 1, k & 1
            peers.append((x ^ bx, y ^ by, c ^ bc))

        def copy(k, slot, to):
            return pltpu.make_async_remote_copy(
                src_ref=p_ref, dst_ref=all_ref.at[slot], send_sem=send_sems.at[k], recv_sem=recv_sems.at[k],
                device_id=to, device_id_type=MESH)

        sends = [copy(k, me, peer) for k, peer in enumerate(peers)]
        for cp in sends:
            cp.start()
        for k, peer in enumerate(peers):
            copy(k, 4 * peer[0] + 2 * peer[1] + peer[2], peer).wait_recv()
        for cp in sends:
            cp.wait_send()
        tot = all_ref[0]
        for d in range(1, N_DEV):
            tot = tot + all_ref[d]
        tot_ref[...] = tot

    vmem = pl.BlockSpec(memory_space=pltpu.VMEM)
    return pl.pallas_call(
        body, name="all_reduce_small",
        in_specs=[vmem], out_specs=vmem,
        out_shape=jax.ShapeDtypeStruct((R, 128), F32),
        scratch_shapes=[pltpu.VMEM((N_DEV, R, 128), F32), pltpu.SemaphoreType.DMA((N_DEV - 1,)),
                        pltpu.SemaphoreType.DMA((N_DEV - 1,))],
    )(pack)


HBM_ONLY = pl.BlockSpec(memory_space=pltpu.HBM)
SEM = pl.BlockSpec(memory_space=pltpu.SEMAPHORE)
DATAFLOW = pltpu.SideEffectType.DATAFLOW_SIDE_EFFECTING


def _shard_rows(buf, rows, dev):
    return buf.at[:, pl.ds(pl.multiple_of((4 * dev[0] + 2 * dev[1] + dev[2]) * rows, rows), rows), :]


def _gather_chips_plan(ins, lands):
    x, y, c = _coords()
    targets = [(x, y, 1 - c)] + [(*chip, c) for chip in _other_chips(x, y)]
    return [(ins[a], _shard_rows(lands[a], ins[a].shape[1], (x, y, c)), t, _shard_rows(lands[a], ins[a].shape[1], t))
            for a in range(len(ins)) for t in targets]


def _gather_pass_plan(bufs, _):
    x, y, c = _coords()
    plan = []
    for buf in bufs:
        r = buf.shape[1] // N_DEV
        for chip in _other_chips(x, y):
            mine, theirs = _shard_rows(buf, r, (*chip, c)), _shard_rows(buf, r, (*chip, 1 - c))
            plan.append((mine, mine, (x, y, 1 - c), theirs))
    return plan


def _pair_plan(ins, lands):
    x, y, c = _coords()
    plan = []
    for a in range(len(ins)):
        r = ins[a].shape[1] // N_DEV
        for q in range(N_CHIP):
            src = ins[a].at[:, pl.ds(pl.multiple_of((2 * q + 1 - c) * r, r), r), :]
            plan.append((src, lands[a].at[q], (x, y, 1 - c), lands[a].at[q]))
    return plan


def _chip_plan(ins, lands):
    x, y, c = _coords()
    chips = _other_chips(x, y)
    return [(ins[a].at[j], lands[a].at[j], (*chips[j], c), lands[a].at[j]) for a in range(len(ins)) for j in range(3)]


def _exchange_start(name, plan, copies_per_array, srcs, lands, after):
    n, m = len(srcs), len(srcs) + len(lands)
    count = copies_per_array * n

    def body(*refs):
        ins, land = refs[:n], refs[n:m]
        send_sems, recv_sems = refs[m + 1], refs[m + 2]
        token = refs[-1]
        for k, (src, dst, peer, _) in enumerate(plan(ins, land)):
            pltpu.make_async_remote_copy(src_ref=src, dst_ref=dst, send_sem=send_sems.at[k], recv_sem=recv_sems.at[k],
                                         device_id=peer, device_id_type=MESH).start()
        token[...] = jnp.zeros_like(token)

    thru = [pltpu.HBM(v.shape, v.dtype) for v in list(srcs) + list(lands)]
    outs = pl.pallas_call(
        body, name=name,
        in_specs=[HBM_ONLY] * m + [HBM],
        out_specs=[SEM, SEM] + [HBM_ONLY] * m + [pl.BlockSpec(memory_space=pltpu.VMEM)],
        out_shape=[pltpu.SemaphoreType.DMA((count,)), pltpu.SemaphoreType.DMA((count,))] + thru
        + [jax.ShapeDtypeStruct((8, 128), F32)],
        input_output_aliases={i: 2 + i for i in range(m)},
        compiler_params=pltpu.CompilerParams(has_side_effects=DATAFLOW),
    )(*[pltpu.with_memory_space_constraint(v, pltpu.HBM) for v in list(srcs) + list(lands)], after)
    return outs[0], outs[1], outs[2:2 + n], outs[2 + n:2 + m], outs[-1]


def _exchange_wait(name, plan, send_sems, recv_sems, srcs, lands, after):
    n, m = len(srcs), len(srcs) + len(lands)

    def body(*refs):
        ins, land = refs[:n], refs[n:m]
        send, recv = refs[m], refs[m + 1]
        for k, (src, _, peer, here) in enumerate(plan(ins, land)):
            cp = pltpu.make_async_remote_copy(src_ref=src, dst_ref=here, send_sem=send.at[k], recv_sem=recv.at[k],
                                              device_id=peer, device_id_type=MESH)
            cp.wait_send()
            cp.wait_recv()

    thru = [pltpu.HBM(v.shape, v.dtype) for v in list(srcs) + list(lands)]
    outs = pl.pallas_call(
        body, name=name,
        in_specs=[HBM_ONLY] * m + [SEM, SEM, pl.BlockSpec(memory_space=pl.ANY)],
        out_specs=[HBM_ONLY] * m,
        out_shape=thru,
        input_output_aliases={i: i for i in range(m)},
        compiler_params=pltpu.CompilerParams(has_side_effects=DATAFLOW),
    )(*srcs, *lands, send_sems, recv_sems, after)
    return outs[:n], outs[n:]


def _place_own(shard, dev):
    s, r, c = shard.shape

    def body(dev_ref, i_ref, o_ref):
        o_ref[...] = i_ref[...]

    return pl.pallas_call(
        body, name="place_own_shard",
        grid_spec=pltpu.PrefetchScalarGridSpec(
            num_scalar_prefetch=1, grid=(s,),
            in_specs=[pl.BlockSpec((None, r, c), lambda i, d: (i, 0, 0))],
            out_specs=pl.BlockSpec((None, r, c), lambda i, d: (i, d[0], 0))),
        out_shape=jax.ShapeDtypeStruct((s, N_DEV * r, c), shard.dtype),
        compiler_params=_cp("arbitrary"),
    )(dev, shard)


def _tile_rows(n, cap=512):
    t = min(n, cap)
    while n % t or t % 8:
        t -= 1
        if t < 8:
            return n
    return t


def _pair_sum(g, got, owner_dev, owner_chip, dtype):
    s, r8, c = g.shape
    r = r8 // N_DEV
    n = owner_dev.shape[0]

    def body(dev_ref, chip_ref, g_ref, got_ref, o_ref):
        o_ref[...] = (g_ref[...] + got_ref[...]).astype(dtype)

    return pl.pallas_call(
        body, name="pair_sum",
        grid_spec=pltpu.PrefetchScalarGridSpec(
            num_scalar_prefetch=2, grid=(n, s),
            in_specs=[pl.BlockSpec((None, r, c), lambda j, i, dev, chip: (i, dev[j], 0)),
                      pl.BlockSpec((None, None, r, c), lambda j, i, dev, chip: (chip[j], i, 0, 0))],
            out_specs=pl.BlockSpec((None, None, r, c), lambda j, i, dev, chip: (j, i, 0, 0))),
        out_shape=jax.ShapeDtypeStruct((n, s, r, c), dtype),
        compiler_params=_cp("parallel", "parallel"),
    )(owner_dev, owner_chip, g, got)


def _sum_chips(g, got, owner_dev, owner_chip, parts):
    _, s, r, c = parts.shape

    def body(dev_ref, chip_ref, g_ref, got_ref, p0, p1, p2, o_ref):
        own = g_ref[...] + got_ref[...]
        o_ref[...] = ((own + p0[...].astype(F32)) + p1[...].astype(F32)) + p2[...].astype(F32)

    def part(q):
        return pl.BlockSpec((None, None, r, c), lambda i, dev, chip: (q, i, 0, 0))

    return pl.pallas_call(
        body, name="chip_sum",
        grid_spec=pltpu.PrefetchScalarGridSpec(
            num_scalar_prefetch=2, grid=(s,),
            in_specs=[pl.BlockSpec((None, r, c), lambda i, dev, chip: (i, dev[0], 0)),
                      pl.BlockSpec((None, None, r, c), lambda i, dev, chip: (chip[0], i, 0, 0)),
                      part(0), part(1), part(2)],
            out_specs=pl.BlockSpec((None, r, c), lambda i, dev, chip: (i, 0, 0))),
        out_shape=jax.ShapeDtypeStruct((s, r, c), F32),
        compiler_params=_cp("parallel"),
    )(owner_dev, owner_chip, g, got, parts, parts, parts)


def _adamw(w, g, m, v):
    shape = w.shape
    c = shape[-1] if w.ndim > 1 else w.shape[0]
    args = [t.reshape(-1, c) for t in (w, g, m, v)]
    n = args[0].shape[0]
    tr = _tile_rows(n)

    def body(w_ref, g_ref, m_ref, v_ref, d_ref, mo_ref, vo_ref):
        g = g_ref[...]
        m = B1 * m_ref[...] + (1.0 - B1) * g
        v = B2 * v_ref[...] + (1.0 - B2) * jnp.square(g)
        m_hat = m / (1.0 - B1 ** STEP)
        v_hat = v / (1.0 - B2 ** STEP)
        d_ref[...] = -LR * (m_hat / (jnp.sqrt(v_hat) + ADAM_EPS) + WD * w_ref[...])
        mo_ref[...] = m
        vo_ref[...] = v

    outs = pl.pallas_call(
        body, name="adamw", grid=(n // tr,),
        in_specs=[_row(tr, c)] * 4, out_specs=[_row(tr, c)] * 3,
        out_shape=[jax.ShapeDtypeStruct((n, c), F32)] * 3,
        compiler_params=_cp("parallel"),
    )(*args)
    return [o.reshape(shape) for o in outs]


def _pack(pieces):
    flat = []
    for p in pieces:
        f = p.reshape(-1)
        flat.append(jnp.pad(f, (0, (-f.shape[0]) % 1024)))
    return jnp.concatenate(flat).reshape(-1, 128)


def _unpack(pack, shapes):
    flat = pack.reshape(-1)
    out, off = [], 0
    for s in shapes:
        size = 1
        for d in s:
            size *= d
        out.append(flat[off:off + size].reshape(s))
        off += size + (-size) % 1024
    return out


def kernel(x, positions, ffn1_norm, ffn1_w_gate, ffn1_w_up, ffn1_w_down, mix_norm, w_in, conv_w, conv_b, conv_ln_g, conv_ln_b, attn_sinks, w_out, ffn2_norm, ffn2_w_gate, ffn2_w_up, ffn2_w_down, final_norm, loss_target, m_ffn1_norm, m_ffn1_w_gate, m_ffn1_w_up, m_ffn1_w_down, m_mix_norm, m_w_in, m_conv_w, m_conv_b, m_conv_ln_g, m_conv_ln_b, m_attn_sinks, m_w_out, m_ffn2_norm, m_ffn2_w_gate, m_ffn2_w_up, m_ffn2_w_down, m_final_norm, v_ffn1_norm, v_ffn1_w_gate, v_ffn1_w_up, v_ffn1_w_down, v_mix_norm, v_w_in, v_conv_w, v_conv_b, v_conv_ln_g, v_conv_ln_b, v_attn_sinks, v_w_out, v_ffn2_norm, v_ffn2_w_gate, v_ffn2_w_up, v_ffn2_w_down, v_final_norm):
    L = ffn1_norm.shape[0]
    T = x.shape[1]
    x0 = x.reshape(T, D)
    target = loss_target.reshape(T, D)
    dev = 4 * lax.axis_index("x") + 2 * lax.axis_index("y") + lax.axis_index("c")

    def t_(w):
        return jnp.swapaxes(w, 1, 2)

    def ffn_shards(gate, up, down, l):
        return jnp.stack([t_(gate)[l], t_(up)[l], down[l]]).astype(BF16)

    shards = [[ffn_shards(ffn1_w_gate, ffn1_w_up, ffn1_w_down, l), ffn_shards(ffn2_w_gate, ffn2_w_up, ffn2_w_down, l),
               t_(w_in)[l:l + 1].astype(BF16), w_out[l:l + 1].astype(BF16)] for l in range(L)]
    cw_cols = CC // N_DEV
    cw_sh = jnp.pad(conv_w.reshape(-1), (0, (-L * CW * cw_cols) % 1024)).reshape(1, -1, 128)
    dev1 = dev.reshape(1).astype(jnp.int32)
    no_token = jnp.zeros((8, 128), F32)

    wffn1_0, cw_all = _all_gather([shards[0][0], cw_sh])
    lands0 = [_place_own(s, dev1) for s in shards[0][1:]]
    *rest0, token = _exchange_start("gather_start_0", _gather_chips_plan, 4, shards[0][1:], lands0, cw_all)
    weights = [None] * L

    cw_rows = cw_sh.shape[1]
    cw_full = cw_all.reshape(N_DEV, cw_rows * 128)[:, :L * CW * cw_cols].reshape(N_DEV, L, CW, cw_cols)
    cw_full = jnp.transpose(cw_full, (1, 2, 0, 3)).reshape(L, CW, CC)

    inv_freq = 1.0 / (10000.0 ** (jnp.arange(0, HD, 2, dtype=F32) / HD))
    ang = positions.reshape(T).astype(F32)[:, None] * inv_freq
    cos, sin = jnp.cos(ang), jnp.sin(ang)
    rc = jnp.concatenate([cos, cos, cos, cos], axis=1)
    rs = jnp.concatenate([-sin, sin, -sin, sin], axis=1)

    n1 = ffn1_norm.reshape(L, 1, D)
    nm = mix_norm.reshape(L, 1, D)
    n2 = ffn2_norm.reshape(L, 1, D)
    cb = conv_b.reshape(L, 1, CC)
    lg = conv_ln_g.reshape(L, 1, CC)
    lb = conv_ln_b.reshape(L, 1, CC)

    saved = []
    xa = x0
    for l in range(L):
        if l == 0:
            wffn1 = wffn1_0
        else:
            wffn1, wffn2, win, wout = _exchange_wait(f"gather_passed_{l}", _gather_pass_plan, *passing, after=xa)[0]
        h1, G1, U1, A1 = _ffn_up(xa, n1, wffn1, 0, 1, l, token)
        if l == 0:
            arrived = _exchange_wait("gather_wait_0", _gather_chips_plan, *rest0, after=A1)[1]
            *passing, token = _exchange_start("gather_pass_0", _gather_pass_plan, 3, arrived, [], A1)
        xb = _ffn_down(A1, xa, wffn1, 2, token)
        if l == 0:
            wffn2, win, wout = _exchange_wait("gather_passed_0", _gather_pass_plan, *passing, after=xb)[0]
        weights[l] = (wffn1, wffn2, win, wout)
        token = no_token
        if l + 1 < L:
            lands = [_place_own(s, dev1) for s in shards[l + 1]]
            *pending, token = _exchange_start(f"gather_start_{l + 1}", _gather_chips_plan, 4, shards[l + 1], lands, win)
        hm, qkv, u = _mix_in(xb, nm, win, rc, rs, l, token)
        ao = _attn_fwd(qkv, attn_sinks, l)
        y, co = _conv_fwd(u, cw_full, cb, lg, lb, l)
        xc, cat = _mix_out(ao, co, xb, wout)
        h2, G2, U2, A2 = _ffn_up(xc, n2, wffn2, 0, 1, l, no_token)
        token = no_token
        if l + 1 < L:
            arrived = _exchange_wait(f"gather_wait_{l + 1}", _gather_chips_plan, *pending, after=A2)[1]
            *passing, token = _exchange_start(f"gather_pass_{l + 1}", _gather_pass_plan, 3, arrived, [], A2)
        xd = _ffn_down(A2, xc, wffn2, 2, token)
        saved.append((xa, h1, G1, U1, A1, xb, hm, qkv, u, y, cat, xc, h2, G2, U2, A2))
        xa = xd
        token = no_token

    dx, loss_part, g_final = _final(xa, final_norm.reshape(1, D), target)

    cx, cy, cc = _coords()
    chip_of = [2 * cx + cy] + [2 * px + py for px, py in _other_chips(cx, cy)]
    own_chip = jnp.stack(chip_of[:1]).astype(jnp.int32)
    other_chips = jnp.stack(chip_of[1:]).astype(jnp.int32)

    g_n1, g_nm, g_n2 = [None] * L, [None] * L, [None] * L
    g_cb, g_lg, g_lb, g_sink, g_cw = [None] * L, [None] * L, [None] * L, [None] * L, [None] * L
    in_flight, reduced = [], {}

    def pair_begin(tag, group, after):
        lands = [lax.empty((N_CHIP, g.shape[0], g.shape[1] // N_DEV, g.shape[2]), F32) for g in group]
        *handles, token = _exchange_start(f"pair_start_{tag}", _pair_plan, N_CHIP, group, lands, after)
        return handles, token

    def chip_begin(tag, group, got, after):
        sent = [_pair_sum(g, r, 2 * other_chips + cc, other_chips, BF16) for g, r in zip(group, got)]
        *handles, token = _exchange_start(f"chip_start_{tag}", _chip_plan, 3, sent,
                                          [lax.empty(p.shape, p.dtype) for p in sent], after)
        in_flight.append((tag, group, got, handles))
        return token

    def pair_end_chip_begin(tag, handles, after):
        group, got = _exchange_wait(f"pair_wait_{tag}", _pair_plan, *handles, after=after)
        return chip_begin(tag, group, got, after)

    for l in reversed(range(L)):
        xa, h1, G1, U1, A1, xb, hm, qkv, u, y, cat, xc, h2, G2, U2, A2 = saved[l]
        wffn1, wffn2, win, wout = weights[l]
        gffn1 = lax.empty((3, FF, D), F32)
        gffn2 = lax.empty((3, FF, D), F32)
        gin = lax.empty((1, DIN, D), F32)
        gout = lax.empty((1, D, D), F32)
        d, dG, dU = _ffn_bwd_act(dx, G2, U2, wffn2, 2, token)
        gffn2 = _wgrad(gffn2, 2, A2, d)
        gffn2 = _wgrad(gffn2, 0, dG, h2)
        gffn2 = _wgrad(gffn2, 1, dU, h2)
        handles, token = pair_begin(f"{l}c", [gffn2], d)
        dx, g_n2[l] = _ffn_bwd_in(dG, dU, xc, dx, n2, wffn2, 0, 1, l, token)
        token = pair_end_chip_begin(f"{l}c", handles, dx)
        d, dao, dco = _mix_out_bwd(dx, wout, token)
        gout = _wgrad(gout, 0, cat, d)
        dy, g_lg[l], g_lb[l], g_cb[l] = _conv_bwd_norm(dco, y, lg, lb, l)
        du, g_cw8 = _conv_bwd_taps(dy, u, cw_full, l)
        g_cw[l] = g_cw8.reshape(CW, 8, CC).sum(axis=1)
        dq, dk, dv, g_sink[l] = _attn_bwd(qkv, dao, attn_sinks, l)
        dp, dx, g_nm[l] = _mix_in_bwd(dq, dk, dv, du, rc, rs, xb, dx, nm, win, l)
        gin = _wgrad(gin, 0, dp, hm)
        handles, token = pair_begin(f"{l}a", [gin, gout], dx)
        d, dG, dU = _ffn_bwd_act(dx, G1, U1, wffn1, 2, token)
        token = pair_end_chip_begin(f"{l}a", handles, d)
        gffn1 = _wgrad(gffn1, 2, A1, d)
        gffn1 = _wgrad(gffn1, 0, dG, h1)
        gffn1 = _wgrad(gffn1, 1, dU, h1)
        if l > 0:
            handles, token = pair_begin(f"{l}b", [gffn1], token)
            dx, g_n1[l] = _ffn_bwd_in(dG, dU, xa, dx, n1, wffn1, 0, 1, l, token)
            token = pair_end_chip_begin(f"{l}b", handles, dx)
        else:
            token = chip_begin(f"{l}b", [gffn1], _pair_exchange([gffn1]), token)
            dx, g_n1[l] = _ffn_bwd_in(dG, dU, xa, dx, n1, wffn1, 0, 1, l, token)

    grad_x = dx.reshape(1, T, D)
    for tag, group, got, handles in in_flight:
        parts = _exchange_wait(f"chip_wait_{tag}", _chip_plan, *handles, after=dx)[1]
        reduced[tag] = [_sum_chips(g, r, 2 * own_chip + cc, own_chip, p) for g, r, p in zip(group, got, parts)]
    g1 = jnp.stack([reduced[f"{l}b"][0] for l in range(L)])
    g2 = jnp.stack([reduced[f"{l}c"][0] for l in range(L)])
    gin_t = jnp.concatenate([reduced[f"{l}a"][0] for l in range(L)])
    gout_sh = jnp.concatenate([reduced[f"{l}a"][1] for l in range(L)])

    small = [loss_part,
             jnp.concatenate(g_n1), jnp.concatenate(g_nm), jnp.concatenate(g_n2), g_final,
             jnp.concatenate(g_cb), jnp.concatenate(g_lg), jnp.concatenate(g_lb),
             jnp.stack(g_sink)[:, :, 0], jnp.stack(g_cw)]
    small_shapes = [(1, 128), (L, D), (L, D), (L, D), (D,), (L, CC), (L, CC), (L, CC), (L, NH), (L, CW, CC)]
    tot = _unpack(_all_reduce_small(_pack(small)), small_shapes)
    loss = tot[0][0, 0]
    gr_n1, gr_nm, gr_n2, gr_final, gr_cb, gr_lg, gr_lb, gr_sink, gr_cw_full = tot[1:]
    gr_cw = lax.dynamic_slice_in_dim(gr_cw_full, dev * cw_cols, cw_cols, axis=2)

    grads_t = {"ffn1_w_gate": g1[:, 0], "ffn1_w_up": g1[:, 1], "ffn2_w_gate": g2[:, 0], "ffn2_w_up": g2[:, 1],
               "w_in": gin_t}
    grads = {
        "ffn1_norm": gr_n1, "ffn1_w_down": g1[:, 2],
        "mix_norm": gr_nm, "conv_w": gr_cw, "conv_b": gr_cb, "conv_ln_g": gr_lg,
        "conv_ln_b": gr_lb, "attn_sinks": gr_sink, "w_out": gout_sh,
        "ffn2_norm": gr_n2, "ffn2_w_down": g2[:, 2],
        "final_norm": gr_final,
    }
    weights = dict(ffn1_norm=ffn1_norm, ffn1_w_gate=ffn1_w_gate, ffn1_w_up=ffn1_w_up, ffn1_w_down=ffn1_w_down, mix_norm=mix_norm, w_in=w_in, conv_w=conv_w, conv_b=conv_b, conv_ln_g=conv_ln_g, conv_ln_b=conv_ln_b, attn_sinks=attn_sinks, w_out=w_out, ffn2_norm=ffn2_norm, ffn2_w_gate=ffn2_w_gate, ffn2_w_up=ffn2_w_up, ffn2_w_down=ffn2_w_down, final_norm=final_norm)
    moms = dict(ffn1_norm=m_ffn1_norm, ffn1_w_gate=m_ffn1_w_gate, ffn1_w_up=m_ffn1_w_up, ffn1_w_down=m_ffn1_w_down, mix_norm=m_mix_norm, w_in=m_w_in, conv_w=m_conv_w, conv_b=m_conv_b, conv_ln_g=m_conv_ln_g, conv_ln_b=m_conv_ln_b, attn_sinks=m_attn_sinks, w_out=m_w_out, ffn2_norm=m_ffn2_norm, ffn2_w_gate=m_ffn2_w_gate, ffn2_w_up=m_ffn2_w_up, ffn2_w_down=m_ffn2_w_down, final_norm=m_final_norm)
    vels = dict(ffn1_norm=v_ffn1_norm, ffn1_w_gate=v_ffn1_w_gate, ffn1_w_up=v_ffn1_w_up, ffn1_w_down=v_ffn1_w_down, mix_norm=v_mix_norm, w_in=v_w_in, conv_w=v_conv_w, conv_b=v_conv_b, conv_ln_g=v_conv_ln_g, conv_ln_b=v_conv_ln_b, attn_sinks=v_attn_sinks, w_out=v_w_out, ffn2_norm=v_ffn2_norm, ffn2_w_gate=v_ffn2_w_gate, ffn2_w_up=v_ffn2_w_up, ffn2_w_down=v_ffn2_w_down, final_norm=v_final_norm)

    names = list(weights)
    big = ("ffn1_w_gate", "ffn1_w_up", "ffn1_w_down", "w_in", "w_out", "ffn2_w_gate", "ffn2_w_up", "ffn2_w_down")
    delta, new_m, new_v = {}, {}, {}
    for k in big:
        if k in grads_t:
            outs = _adamw(t_(weights[k]), grads_t[k], t_(moms[k]), t_(vels[k]))
            grads[k], delta[k], new_m[k], new_v[k] = [t_(o) for o in [grads_t[k]] + outs]
        else:
            delta[k], new_m[k], new_v[k] = _adamw(weights[k], grads[k], moms[k], vels[k])
    rest = [k for k in names if k not in big]
    rest_shapes = [weights[k].shape for k in rest]
    packed = _adamw(*[_pack([t[k] for k in rest]) for t in (weights, grads, moms, vels)])
    for res, packed_out in zip((delta, new_m, new_v), packed):
        for k, val in zip(rest, _unpack(packed_out, rest_shapes)):
            res[k] = val

    return (loss, grad_x, *[grads[k] for k in names], *[delta[k] for k in names],
            *[new_m[k] for k in names], *[new_v[k] for k in names])
```

```python
import functools

import jax
import jax.numpy as jnp
from jax import lax
from jax.experimental import pallas as pl
from jax.experimental.pallas import tpu as pltpu

F32 = jnp.float32
BF16 = jnp.bfloat16
MESH = pl.DeviceIdType.MESH

N_DEV = 8
N_CHIP = 4
D = 1024
FF = 2816
HD = 64
NH = 8
NKV = 2
GROUP = NH // NKV
AW = NH * HD
KVW = NKV * HD
QKW = AW + KVW
QKVW = AW + 2 * KVW
CC = 512
CW = 31
DIN = QKVW + 2 * CC
BLK = 128
HALO = 32
EPS = 1e-5
SCALE = HD ** -0.5
NEG = float(jnp.finfo(jnp.float32).min)

LR, B1, B2, ADAM_EPS, WD, STEP = 0.001, 0.9, 0.999, 1e-08, 0.01, 10

TM = 512
TM_FFN_UP = 256
TM_CONV_BWD = 64
TK = 2048
FC = 256
ATT_BLOCKS = 8
VMEM_LIMIT = 56 * 1024 * 1024


def _cp(*sem):
    return pltpu.CompilerParams(dimension_semantics=sem, vmem_limit_bytes=VMEM_LIMIT)


def _row(tm, c):
    return pl.BlockSpec((tm, c), lambda i: (i, 0))


def _slab(shape, k, single=False):
    zeros = (0,) * len(shape)
    kw = dict(pipeline_mode=pl.Buffered(1)) if single else {}
    return pl.BlockSpec((None, *shape), lambda i: (k, *zeros), **kw)


def _acc(shape):
    return pl.BlockSpec(shape, lambda i: (0,) * len(shape))


def _nt(a, b):
    return lax.dot_general(a, b, (((1,), (1,)), ((), ())), preferred_element_type=F32)


def _tn(a, b):
    return lax.dot_general(a, b, (((0,), (0,)), ((), ())), preferred_element_type=F32)


def _nn(a, b):
    return jnp.dot(a, b, preferred_element_type=F32)


def _sigmoid(x):
    return jax.nn.sigmoid(x)


def _dsilu(z):
    s = _sigmoid(z)
    return s * (1.0 + z * (1.0 - s))


def _rms(x, g):
    r = lax.rsqrt(jnp.mean(x * x, axis=-1, keepdims=True) + EPS)
    xh = x * r
    return xh, r, xh * g


def _rms_bwd(dh, xh, r, g):
    dxh = dh * g
    return r * (dxh - xh * jnp.mean(dxh * xh, axis=-1, keepdims=True))


def _rope(t, c128, s128):
    w = t.shape[1]
    lane = lax.broadcasted_iota(jnp.int32, t.shape, 1)
    rot = jnp.where(lane % HD < HD // 2, pltpu.roll(t, w - HD // 2, 1), pltpu.roll(t, HD // 2, 1))
    return t * jnp.tile(c128, (1, w // 128)) + rot * jnp.tile(s128, (1, w // 128))


def _ffn_up(x, norm, wffn, sg, su, layer, token):
    T = x.shape[0]
    tm = TM_FFN_UP

    def body(x_ref, g_ref, wg_ref, wu_ref, token_ref, h_ref, P_ref, Q_ref, A_ref):
        _, _, hn = _rms(x_ref[...], g_ref[...])
        h = hn.astype(BF16)
        h_ref[...] = h
        for c in range(FF // FC):
            sl = slice(c * FC, (c + 1) * FC)
            g = _nt(h, wg_ref[sl, :])
            u = _nt(h, wu_ref[sl, :])
            s = _sigmoid(g)
            p = g * s
            P_ref[:, sl] = p.astype(BF16)
            Q_ref[:, sl] = (u * (s + p - p * s)).astype(BF16)
            A_ref[:, sl] = (p * u).astype(BF16)

    return pl.pallas_call(
        body, name="ffn_up", grid=(T // tm,),
        in_specs=[_row(tm, D), _slab((1, D), layer), _slab((FF, D), sg, True), _slab((FF, D), su, True), HBM],
        out_specs=[_row(tm, D), _row(tm, FF), _row(tm, FF), _row(tm, FF)],
        out_shape=[jax.ShapeDtypeStruct((T, D), BF16)] + [jax.ShapeDtypeStruct((T, FF), BF16)] * 3,
        compiler_params=_cp("parallel"),
    )(x, norm, wffn, wffn, token)


def _ffn_down(a, x, wffn, sd, token):
    T = x.shape[0]

    def body(a_ref, x_ref, w_ref, token_ref, o_ref):
        o_ref[...] = x_ref[...] + 0.5 * _nn(a_ref[...], w_ref[...])

    return pl.pallas_call(
        body, name="ffn_down", grid=(T // TM,),
        in_specs=[_row(TM, FF), _row(TM, D), _slab((FF, D), sd, True), HBM],
        out_specs=_row(TM, D),
        out_shape=jax.ShapeDtypeStruct((T, D), F32),
        compiler_params=_cp("parallel"),
    )(a, x, wffn, token)


def _mix_in(x, norm, win, rc, rs, layer, token):
    T = x.shape[0]

    def body(x_ref, g_ref, w_ref, c_ref, s_ref, token_ref, h_ref, qkv_ref, u_ref):
        _, _, hn = _rms(x_ref[...], g_ref[...])
        h = hn.astype(BF16)
        h_ref[...] = h
        qk = _nt(h, w_ref[0:QKW, :])
        qkv_ref[:, 0:QKW] = _rope(qk, c_ref[...], s_ref[...]).astype(BF16)
        qkv_ref[:, QKW:QKVW] = _nt(h, w_ref[QKW:QKVW, :]).astype(BF16)
        for c in range(2 * CC // FC):
            u_ref[:, c * FC:(c + 1) * FC] = _nt(h, w_ref[QKVW + c * FC:QKVW + (c + 1) * FC, :]).astype(BF16)

    return pl.pallas_call(
        body, name="mix_in", grid=(T // TM,),
        in_specs=[_row(TM, D), _slab((1, D), layer), _slab((DIN, D), 0, True), _row(TM, 128), _row(TM, 128), HBM],
        out_specs=[_row(TM, D), _row(TM, QKVW), _row(TM, 2 * CC)],
        out_shape=[jax.ShapeDtypeStruct((T, D), BF16), jax.ShapeDtypeStruct((T, QKVW), BF16),
                   jax.ShapeDtypeStruct((T, 2 * CC), BF16)],
        compiler_params=_cp("parallel"),
    )(x, norm, win, rc, rs, token)


def _band_mask(has_prev):
    j = lax.broadcasted_iota(jnp.int32, (2 * BLK, BLK), 0)
    r = lax.broadcasted_iota(jnp.int32, (2 * BLK, BLK), 1) + BLK
    rel = r - j
    return jnp.tile((rel >= 0) & (rel < BLK) & (has_prev | (j >= BLK)), (1, GROUP))


def _band(prev_ref, cur_ref, b, col):
    if b == 0:
        return jnp.concatenate([prev_ref[:, col:col + HD], cur_ref[0:BLK, col:col + HD]], axis=0)
    return cur_ref[(b - 1) * BLK:(b + 1) * BLK, col:col + HD]


def _stack_heads(ref, b, kv):
    cols = [(kv * GROUP + g) * HD for g in range(GROUP)]
    return jnp.concatenate([ref[b * BLK:(b + 1) * BLK, c:c + HD] for c in cols], axis=0)


def _unstack_t(xt):
    x = xt.T
    return jnp.concatenate([x[g * BLK:(g + 1) * BLK, :] for g in range(GROUP)], axis=1)


def _sink_row(sink_ref, layer, kv):
    return jnp.concatenate([jnp.full((1, BLK), sink_ref[layer, kv * GROUP + g], F32) for g in range(GROUP)], axis=1)


def _probs_t(q4, kb, mask, sink):
    s = jnp.where(mask, _nt(kb, q4) * SCALE, NEG)
    m = jnp.maximum(jnp.max(s, axis=0, keepdims=True), sink)
    p = jnp.exp(s - m)
    e = jnp.exp(sink - m)
    inv = 1.0 / (jnp.sum(p, axis=0, keepdims=True) + e)
    return p * inv, e * inv


def _attn_fwd(qkv, sinks, layer):
    T = qkv.shape[0]
    tq = ATT_BLOCKS * BLK

    def body(sink_ref, cur_ref, prev_ref, o_ref):
        first = _band_mask(pl.program_id(0) > 0)
        later = _band_mask(True)
        for b in range(ATT_BLOCKS):
            outs = []
            for kv in range(NKV):
                kb = _band(prev_ref, cur_ref, b, AW + kv * HD)
                vb = _band(prev_ref, cur_ref, b, QKW + kv * HD)
                pt, _ = _probs_t(_stack_heads(cur_ref, b, kv), kb, first if b == 0 else later,
                                 _sink_row(sink_ref, layer, kv))
                outs.append(_unstack_t(_nn(vb.T, pt.astype(BF16))))
            o_ref[b * BLK:(b + 1) * BLK, :] = jnp.concatenate(outs, axis=1).astype(BF16)

    return pl.pallas_call(
        body, name="attn_fwd", grid=(T // tq,),
        in_specs=[pl.BlockSpec(memory_space=pltpu.SMEM), _row(tq, QKVW),
                  pl.BlockSpec((BLK, QKVW), lambda i: (jnp.maximum(i * ATT_BLOCKS - 1, 0), 0))],
        out_specs=_row(tq, AW),
        out_shape=jax.ShapeDtypeStruct((T, AW), BF16),
        compiler_params=_cp("parallel"),
    )(sinks, qkv, qkv)


def _glu(u):
    u = u.astype(F32)
    return u[:, :CC] * _sigmoid(u[:, CC:])


def _fill_ext(ext_ref, first, second):
    n = ext_ref.shape[0] - 8
    ext_ref[0:first.shape[0], :] = first
    ext_ref[first.shape[0]:n, :] = second
    ext_ref[n:, :] = jnp.zeros((8, ext_ref.shape[1]), F32)


def _taps(ext_ref, w_ref, offsets, cols, tm):
    y = None
    for b in range(8):
        z = None
        for k, off in enumerate(offsets):
            if off % 8 == b:
                term = w_ref[k:k + 1, cols] * ext_ref[pl.ds(off - b, tm + 8), cols]
                z = term if z is None else z + term
        if z is not None:
            y = z[b:b + tm, :] if y is None else y + z[b:b + tm, :]
    return y


def _conv_fwd(u, cw, cb, lg, lb, layer):
    T = u.shape[0]
    tm = TM

    def body(u_ref, up_ref, w_ref, b_ref, g_ref, bb_ref, y_ref, o_ref, ext_ref):
        i = pl.program_id(0)
        _fill_ext(ext_ref, jnp.where(i > 0, _glu(up_ref[...]), 0.0), _glu(u_ref[...]))
        for c in range(CC // 128):
            cols = slice(c * 128, (c + 1) * 128)
            y_ref[:, cols] = _taps(ext_ref, w_ref, [HALO - (CW - 1) + k for k in range(CW)], cols, tm) + b_ref[:, cols]
        y = y_ref[...]
        xc = y - jnp.mean(y, axis=-1, keepdims=True)
        z = xc * lax.rsqrt(jnp.mean(xc * xc, axis=-1, keepdims=True) + EPS) * g_ref[...] + bb_ref[...]
        o_ref[...] = (z * _sigmoid(z)).astype(BF16)

    return pl.pallas_call(
        body, name="conv_fwd", grid=(T // tm,),
        in_specs=[_row(tm, 2 * CC),
                  pl.BlockSpec((HALO, 2 * CC), lambda i: (jnp.maximum(i * (tm // HALO) - 1, 0), 0)),
                  _slab((CW, CC), layer), _slab((1, CC), layer), _slab((1, CC), layer), _slab((1, CC), layer)],
        out_specs=[_row(tm, CC), _row(tm, CC)],
        out_shape=[jax.ShapeDtypeStruct((T, CC), F32), jax.ShapeDtypeStruct((T, CC), BF16)],
        scratch_shapes=[pltpu.VMEM((tm + HALO + 8, CC), F32)],
        compiler_params=_cp("parallel"),
    )(u, u, cw, cb, lg, lb)


def _mix_out(ao, co, x, wout):
    T = x.shape[0]

    def body(ao_ref, co_ref, x_ref, w_ref, o_ref, cat_ref):
        cat = jnp.concatenate([ao_ref[...], co_ref[...]], axis=1)
        cat_ref[...] = cat
        o_ref[...] = x_ref[...] + _nn(cat, w_ref[...])

    return pl.pallas_call(
        body, name="mix_out", grid=(T // TM,),
        in_specs=[_row(TM, AW), _row(TM, CC), _row(TM, D), _slab((D, D), 0, True)],
        out_specs=[_row(TM, D), _row(TM, D)],
        out_shape=[jax.ShapeDtypeStruct((T, D), F32), jax.ShapeDtypeStruct((T, D), BF16)],
        compiler_params=_cp("parallel"),
    )(ao, co, x, wout)


def _final(x, norm, target):
    T = x.shape[0]

    def body(x_ref, g_ref, t_ref, dx_ref, loss_ref, dg_ref):
        @pl.when(pl.program_id(0) == 0)
        def _():
            loss_ref[...] = jnp.zeros_like(loss_ref)
            dg_ref[...] = jnp.zeros_like(dg_ref)

        g = g_ref[...]
        xh, r, y = _rms(x_ref[...], g)
        err = y - t_ref[...]
        loss_ref[...] += jnp.full(loss_ref.shape, (0.5 / D) * jnp.sum(err * err), F32)
        dy = err * (1.0 / D)
        dg_ref[...] += jnp.sum(dy * xh, axis=0, keepdims=True)
        dx_ref[...] = _rms_bwd(dy, xh, r, g)

    return pl.pallas_call(
        body, name="final_loss", grid=(T // TM,),
        in_specs=[_row(TM, D), _acc((1, D)), _row(TM, D)],
        out_specs=[_row(TM, D), _acc((1, 128)), _acc((1, D))],
        out_shape=[jax.ShapeDtypeStruct((T, D), F32), jax.ShapeDtypeStruct((1, 128), F32),
                   jax.ShapeDtypeStruct((1, D), F32)],
        compiler_params=_cp("arbitrary"),
    )(x, norm, target)


def _ffn_bwd_act(dx, P, Q, wffn, sd, token):
    T = dx.shape[0]

    def body(dx_ref, P_ref, Q_ref, w_ref, token_ref, d_ref, dG_ref, dU_ref):
        d = (0.5 * dx_ref[...]).astype(BF16)
        d_ref[...] = d
        for c in range(FF // FC):
            sl = slice(c * FC, (c + 1) * FC)
            da = _nt(d, w_ref[sl, :])
            dU_ref[:, sl] = (da * P_ref[:, sl].astype(F32)).astype(BF16)
            dG_ref[:, sl] = (da * Q_ref[:, sl].astype(F32)).astype(BF16)

    return pl.pallas_call(
        body, name="ffn_bwd_act", grid=(T // TM,),
        in_specs=[_row(TM, D), _row(TM, FF), _row(TM, FF), _slab((FF, D), sd, True), HBM],
        out_specs=[_row(TM, D), _row(TM, FF), _row(TM, FF)],
        out_shape=[jax.ShapeDtypeStruct((T, D), BF16)] + [jax.ShapeDtypeStruct((T, FF), BF16)] * 2,
        compiler_params=_cp("parallel"),
    )(dx, P, Q, wffn, token)


def _ffn_bwd_in(dG, dU, x, dx, norm, wffn, sg, su, layer, token):
    T = x.shape[0]

    def body(dG_ref, dU_ref, x_ref, dx_ref, g_ref, wg_ref, wu_ref, token_ref, o_ref, dg_ref):
        @pl.when(pl.program_id(0) == 0)
        def _():
            dg_ref[...] = jnp.zeros_like(dg_ref)

        dh = _nn(dG_ref[...], wg_ref[...]) + _nn(dU_ref[...], wu_ref[...])
        g = g_ref[...]
        xh, r, _ = _rms(x_ref[...], g)
        dg_ref[...] += jnp.sum(dh * xh, axis=0, keepdims=True)
        o_ref[...] = dx_ref[...] + _rms_bwd(dh, xh, r, g)

    return pl.pallas_call(
        body, name="ffn_bwd_in", grid=(T // TM,),
        in_specs=[_row(TM, FF), _row(TM, FF), _row(TM, D), _row(TM, D), _slab((1, D), layer),
                  _slab((FF, D), sg, True), _slab((FF, D), su, True), HBM],
        out_specs=[_row(TM, D), _acc((1, D))],
        out_shape=[jax.ShapeDtypeStruct((T, D), F32), jax.ShapeDtypeStruct((1, D), F32)],
        compiler_params=_cp("arbitrary"),
    )(dG, dU, x, dx, norm, wffn, wffn, token)


def _mix_out_bwd(dx, wout, token):
    T = dx.shape[0]

    def body(dx_ref, w_ref, token_ref, d_ref, dao_ref, dco_ref):
        d = dx_ref[...].astype(BF16)
        d_ref[...] = d
        dcat = _nt(d, w_ref[...])
        dao_ref[...] = dcat[:, :AW].astype(BF16)
        dco_ref[...] = dcat[:, AW:].astype(BF16)

    return pl.pallas_call(
        body, name="mix_out_bwd", grid=(T // TM,),
        in_specs=[_row(TM, D), _slab((D, D), 0, True), HBM],
        out_specs=[_row(TM, D), _row(TM, AW), _row(TM, CC)],
        out_shape=[jax.ShapeDtypeStruct((T, D), BF16), jax.ShapeDtypeStruct((T, AW), BF16),
                   jax.ShapeDtypeStruct((T, CC), BF16)],
        compiler_params=_cp("parallel"),
    )(dx, wout, token)


def _conv_bwd_norm(dco, y, lg, lb, layer):
    T = y.shape[0]

    def body(dco_ref, y_ref, g_ref, bb_ref, dy_ref, dlg_ref, dlb_ref, dcb_ref):
        @pl.when(pl.program_id(0) == 0)
        def _():
            dlg_ref[...] = jnp.zeros_like(dlg_ref)
            dlb_ref[...] = jnp.zeros_like(dlb_ref)
            dcb_ref[...] = jnp.zeros_like(dcb_ref)

        y = y_ref[...]
        g = g_ref[...]
        xc = y - jnp.mean(y, axis=-1, keepdims=True)
        rs = lax.rsqrt(jnp.mean(xc * xc, axis=-1, keepdims=True) + EPS)
        xn = xc * rs
        z = xn * g + bb_ref[...]
        dz = dco_ref[...].astype(F32) * _dsilu(z)
        dlg_ref[...] += jnp.sum(dz * xn, axis=0, keepdims=True)
        dlb_ref[...] += jnp.sum(dz, axis=0, keepdims=True)
        dxn = dz * g
        dy = rs * (dxn - jnp.mean(dxn, axis=-1, keepdims=True) - xn * jnp.mean(dxn * xn, axis=-1, keepdims=True))
        dcb_ref[...] += jnp.sum(dy, axis=0, keepdims=True)
        dy_ref[...] = dy

    return pl.pallas_call(
        body, name="conv_bwd_norm", grid=(T // TM,),
        in_specs=[_row(TM, CC), _row(TM, CC), _slab((1, CC), layer), _slab((1, CC), layer)],
        out_specs=[_row(TM, CC), _acc((1, CC)), _acc((1, CC)), _acc((1, CC))],
        out_shape=[jax.ShapeDtypeStruct((T, CC), F32)] + [jax.ShapeDtypeStruct((1, CC), F32)] * 3,
        compiler_params=_cp("arbitrary"),
    )(dco, y, lg, lb)


def _conv_bwd_taps(dy, u, cw, layer):
    T = u.shape[0]
    tm = TM_CONV_BWD
    n_halo = T // HALO

    def body(dy_ref, dyn_ref, u_ref, up_ref, w_ref, du_ref, dw_ref, hext_ref, dext_ref, dz_ref, dsh_ref, dh_ref):
        i = pl.program_id(0)

        @pl.when(i == 0)
        def _():
            dw_ref[...] = jnp.zeros_like(dw_ref)

        _fill_ext(hext_ref, jnp.where(i > 0, _glu(up_ref[...]), 0.0), _glu(u_ref[...]))
        _fill_ext(dext_ref, dy_ref[...], jnp.where(i < pl.num_programs(0) - 1, dyn_ref[...], 0.0))
        _fill_ext(dz_ref, jnp.zeros((8, CC), F32), dy_ref[...])
        for b in range(8):
            dsh_ref[b] = dz_ref[pl.ds(8 - b, tm + 8), :]
        h_offsets = [HALO - (CW - 1) + k for k in range(CW)]
        for c in range(CC // 128):
            cols = slice(c * 128, (c + 1) * 128)
            dh_ref[:, cols] = _taps(dext_ref, w_ref, [CW - 1 - k for k in range(CW)], cols, tm)
            for k, off in enumerate(h_offsets):
                b = off % 8
                prod = dsh_ref[b, :, cols] * hext_ref[pl.ds(off - b, tm + 8), cols]
                dw_ref[8 * k:8 * k + 8, cols] += jnp.sum(prod.reshape((tm + 8) // 8, 8, 128), axis=0)
        dh = dh_ref[...]
        uu = u_ref[...].astype(F32)
        a = uu[:, :CC]
        sg = _sigmoid(uu[:, CC:])
        du_ref[:, :CC] = (dh * sg).astype(BF16)
        du_ref[:, CC:] = (dh * a * sg * (1.0 - sg)).astype(BF16)

    return pl.pallas_call(
        body, name="conv_bwd_taps", grid=(T // tm,),
        in_specs=[_row(tm, CC),
                  pl.BlockSpec((HALO, CC), lambda i: (jnp.minimum((i + 1) * (tm // HALO), n_halo - 1), 0)),
                  _row(tm, 2 * CC),
                  pl.BlockSpec((HALO, 2 * CC), lambda i: (jnp.maximum(i * (tm // HALO) - 1, 0), 0)),
                  _slab((CW, CC), layer)],
        out_specs=[_row(tm, 2 * CC), _acc((CW * 8, CC))],
        out_shape=[jax.ShapeDtypeStruct((T, 2 * CC), BF16), jax.ShapeDtypeStruct((CW * 8, CC), F32)],
        scratch_shapes=[pltpu.VMEM((tm + HALO + 8, CC), F32), pltpu.VMEM((tm + HALO + 8, CC), F32),
                        pltpu.VMEM((tm + 16, CC), F32), pltpu.VMEM((8, tm + 8, CC), F32), pltpu.VMEM((tm, CC), F32)],
        compiler_params=_cp("arbitrary"),
    )(dy, dy, u, u, cw)


def _attn_bwd(qkv, dao, sinks, layer):
    T = qkv.shape[0]
    tq = ATT_BLOCKS * BLK

    def body(sink_ref, cur_ref, prev_ref, do_ref, dq_ref, dk_ref, dv_ref, ds_ref):
        i = pl.program_id(0)

        @pl.when(i == 0)
        def _():
            dk_ref[...] = jnp.zeros_like(dk_ref)
            dv_ref[...] = jnp.zeros_like(dv_ref)
            ds_ref[...] = jnp.zeros_like(ds_ref)

        first = _band_mask(i > 0)
        later = _band_mask(True)
        base = pl.multiple_of(i * tq, tq)
        before = pl.multiple_of(jnp.maximum(i * ATT_BLOCKS - 1, 0) * BLK, BLK)
        for b in range(ATT_BLOCKS):
            dqs, dks, dvs = [], [], []
            for kv in range(NKV):
                kb = _band(prev_ref, cur_ref, b, AW + kv * HD)
                vb = _band(prev_ref, cur_ref, b, QKW + kv * HD)
                q4 = _stack_heads(cur_ref, b, kv)
                do4 = _stack_heads(do_ref, b, kv)
                pt, psink = _probs_t(q4, kb, first if b == 0 else later, _sink_row(sink_ref, layer, kv))
                dpt = _nt(vb, do4)
                dd = jnp.sum(pt * dpt, axis=0, keepdims=True)
                dst = (pt * (dpt - dd) * SCALE).astype(BF16)
                sd = psink * dd
                for g in range(GROUP):
                    hh = kv * GROUP + g
                    ds_ref[hh:hh + 1, :] += jnp.full((1, 128), -jnp.sum(sd[:, g * BLK:(g + 1) * BLK]), F32)
                dqs.append(_unstack_t(_nn(kb.T, dst)))
                dks.append(_nn(dst, q4))
                dvs.append(_nn(pt.astype(BF16), do4))
            dq_ref[b * BLK:(b + 1) * BLK, :] = jnp.concatenate(dqs, axis=1).astype(BF16)
            dkband = jnp.concatenate(dks, axis=1)
            dvband = jnp.concatenate(dvs, axis=1)
            if b == 0:
                dk_ref[pl.ds(before, BLK), :] += dkband[:BLK]
                dv_ref[pl.ds(before, BLK), :] += dvband[:BLK]
                dk_ref[pl.ds(base, BLK), :] += dkband[BLK:]
                dv_ref[pl.ds(base, BLK), :] += dvband[BLK:]
            else:
                r0 = pl.multiple_of(base + (b - 1) * BLK, BLK)
                dk_ref[pl.ds(r0, 2 * BLK), :] += dkband
                dv_ref[pl.ds(r0, 2 * BLK), :] += dvband

    return pl.pallas_call(
        body, name="attn_bwd", grid=(T // tq,),
        in_specs=[pl.BlockSpec(memory_space=pltpu.SMEM), _row(tq, QKVW),
                  pl.BlockSpec((BLK, QKVW), lambda i: (jnp.maximum(i * ATT_BLOCKS - 1, 0), 0)), _row(tq, AW)],
        out_specs=[_row(tq, AW), _acc((T, KVW)), _acc((T, KVW)), _acc((NH, 128))],
        out_shape=[jax.ShapeDtypeStruct((T, AW), BF16), jax.ShapeDtypeStruct((T, KVW), F32),
                   jax.ShapeDtypeStruct((T, KVW), F32), jax.ShapeDtypeStruct((NH, 128), F32)],
        compiler_params=_cp("arbitrary"),
    )(sinks, qkv, qkv, dao)


def _mix_in_bwd(dq, dk, dv, du, rc, rs, x, dx, norm, win, layer):
    T = x.shape[0]

    def body(dq_ref, dk_ref, dv_ref, du_ref, c_ref, s_ref, x_ref, dx_ref, g_ref, w_ref, dp_ref, o_ref, dg_ref):
        @pl.when(pl.program_id(0) == 0)
        def _():
            dg_ref[...] = jnp.zeros_like(dg_ref)

        dqk = jnp.concatenate([dq_ref[...].astype(F32), dk_ref[...]], axis=1)
        dqk = _rope(dqk, c_ref[...], -s_ref[...])
        dp = jnp.concatenate([dqk.astype(BF16), dv_ref[...].astype(BF16), du_ref[...]], axis=1)
        dp_ref[...] = dp
        dh = _nn(dp, w_ref[...])
        g = g_ref[...]
        xh, r, _ = _rms(x_ref[...], g)
        dg_ref[...] += jnp.sum(dh * xh, axis=0, keepdims=True)
        o_ref[...] = dx_ref[...] + _rms_bwd(dh, xh, r, g)

    return pl.pallas_call(
        body, name="mix_in_bwd", grid=(T // TM,),
        in_specs=[_row(TM, AW), _row(TM, KVW), _row(TM, KVW), _row(TM, 2 * CC), _row(TM, 128), _row(TM, 128),
                  _row(TM, D), _row(TM, D), _slab((1, D), layer), _slab((DIN, D), 0, True)],
        out_specs=[_row(TM, DIN), _row(TM, D), _acc((1, D))],
        out_shape=[jax.ShapeDtypeStruct((T, DIN), BF16), jax.ShapeDtypeStruct((T, D), F32),
                   jax.ShapeDtypeStruct((1, D), F32)],
        compiler_params=_cp("arbitrary"),
    )(dq, dk, dv, du, rc, rs, x, dx, norm, win)


def _wgrad(buf, slab, a, b):
    T, M = a.shape
    N = b.shape[1]
    tmm = M // 2 if M > 1024 else M

    def body(buf_ref, a_ref, b_ref, o_ref):
        @pl.when(pl.program_id(1) == 0)
        def _():
            o_ref[...] = jnp.zeros_like(o_ref)

        o_ref[...] += _tn(a_ref[...], b_ref[...])

    return pl.pallas_call(
        body, name="wgrad", grid=(M // tmm, T // TK),
        in_specs=[pl.BlockSpec(memory_space=pl.ANY),
                  pl.BlockSpec((TK, tmm), lambda i, k: (k, i)), pl.BlockSpec((TK, N), lambda i, k: (k, 0))],
        out_specs=pl.BlockSpec((None, tmm, N), lambda i, k: (slab, i, 0)),
        out_shape=jax.ShapeDtypeStruct(buf.shape, F32),
        input_output_aliases={0: 0},
        compiler_params=_cp("parallel", "arbitrary"),
    )(buf, a, b)


HBM = pl.BlockSpec(memory_space=pl.ANY)


def _coords():
    return lax.axis_index("x"), lax.axis_index("y"), lax.axis_index("c")


def _other_chips(x, y):
    return [(1 - x, y), (x, 1 - y), (1 - x, 1 - y)]


def _all_gather(shards):
    n = len(shards)

    def body(*refs):
        ins, outs = refs[:n], refs[n:2 * n]
        send_sems, recv_sems, local_sems = refs[2 * n:]
        x, y, c = _coords()
        me, sibling = (x, y, c), (x, y, 1 - c)
        chips = _other_chips(x, y)

        def rows(a, dev):
            r = ins[a].shape[1]
            return outs[a].at[:, pl.ds(pl.multiple_of((4 * dev[0] + 2 * dev[1] + dev[2]) * r, r), r), :]

        def copy(a, k, block, to, src=None):
            return pltpu.make_async_remote_copy(
                src_ref=rows(a, block) if src is None else src, dst_ref=rows(a, block),
                send_sem=send_sems.at[a * 7 + k], recv_sem=recv_sems.at[a * 7 + k],
                device_id=to, device_id_type=MESH)

        mine = [pltpu.make_async_copy(ins[a], rows(a, me), local_sems.at[a]) for a in range(n)]
        for cp in mine:
            cp.start()
        first = []
        for a in range(n):
            first.append(copy(a, 0, me, sibling, src=ins[a]))
            first += [copy(a, 1 + j, me, (*chip, c), src=ins[a]) for j, chip in enumerate(chips)]
        for cp in first:
            cp.start()
        passed = []
        for j, chip in enumerate(chips):
            for a in range(n):
                copy(a, 1 + j, (*chip, c), me).wait_recv()
                fwd = copy(a, 4 + j, (*chip, c), sibling)
                fwd.start()
                passed.append(fwd)
        for a in range(n):
            copy(a, 0, sibling, me).wait_recv()
            for j, chip in enumerate(chips):
                copy(a, 4 + j, (*chip, 1 - c), me).wait_recv()
        for cp in first + passed:
            cp.wait_send()
        for cp in mine:
            cp.wait()

    return pl.pallas_call(
        body, name="all_gather_weights",
        in_specs=[HBM] * n, out_specs=[HBM] * n,
        out_shape=[jax.ShapeDtypeStruct((s.shape[0], N_DEV * s.shape[1], s.shape[2]), s.dtype) for s in shards],
        scratch_shapes=[pltpu.SemaphoreType.DMA((7 * n,)), pltpu.SemaphoreType.DMA((7 * n,)),
                        pltpu.SemaphoreType.DMA((n,))],
    )(*shards)


def _pair_exchange(grads):
    n = len(grads)

    def body(*refs):
        ins, got = refs[:n], refs[n:2 * n]
        send_sems, recv_sems = refs[2 * n:]
        x, y, c = _coords()
        sibling = (x, y, 1 - c)

        def remote(a, q):
            r = ins[a].shape[1] // N_DEV
            src = ins[a].at[:, pl.ds(pl.multiple_of((2 * q + 1 - c) * r, r), r), :]
            return pltpu.make_async_remote_copy(
                src_ref=src, dst_ref=got[a].at[q],
                send_sem=send_sems.at[a * N_CHIP + q], recv_sem=recv_sems.at[a * N_CHIP + q],
                device_id=sibling, device_id_type=MESH)

        sends = [remote(a, q) for a in range(n) for q in range(N_CHIP)]
        for cp in sends:
            cp.start()
        for cp in sends:
            cp.wait_recv()
        for cp in sends:
            cp.wait_send()

    return pl.pallas_call(
        body, name="grad_pair_exchange",
        in_specs=[HBM] * n, out_specs=[HBM] * n,
        out_shape=[jax.ShapeDtypeStruct((N_CHIP, g.shape[0], g.shape[1] // N_DEV, g.shape[2]), g.dtype) for g in grads],
        scratch_shapes=[pltpu.SemaphoreType.DMA((N_CHIP * n,)), pltpu.SemaphoreType.DMA((N_CHIP * n,))],
    )(*grads)


def _all_reduce_small(pack):
    R = pack.shape[0]

    def body(p_ref, tot_ref, all_ref, send_sems, recv_sems):
        x, y, c = _coords()
        me = 4 * x + 2 * y + c
        all_ref[me] = p_ref[...]
        peers = []
        for k in range(1, N_DEV):
            bx, by, bc = (k >> 2) & 1, (k >> 1) & 1, k & 1
            peers.append((x ^ bx, y ^ by, c ^ bc))

        def copy(k, slot, to):
            return pltpu.make_async_remote_copy(
                src_ref=p_ref, dst_ref=all_ref.at[slot], send_sem=send_sems.at[k], recv_sem=recv_sems.at[k],
                device_id=to, device_id_type=MESH)

        sends = [copy(k, me, peer) for k, peer in enumerate(peers)]
        for cp in sends:
            cp.start()
        for k, peer in enumerate(peers):
            copy(k, 4 * peer[0] + 2 * peer[1] + peer[2], peer).wait_recv()
        for cp in sends:
            cp.wait_send()
        tot = all_ref[0]
        for d in range(1, N_DEV):
            tot = tot + all_ref[d]
        tot_ref[...] = tot

    vmem = pl.BlockSpec(memory_space=pltpu.VMEM)
    return pl.pallas_call(
        body, name="all_reduce_small",
        in_specs=[vmem], out_specs=vmem,
        out_shape=jax.ShapeDtypeStruct((R, 128), F32),
        scratch_shapes=[pltpu.VMEM((N_DEV, R, 128), F32), pltpu.SemaphoreType.DMA((N_DEV - 1,)),
                        pltpu.SemaphoreType.DMA((N_DEV - 1,))],
    )(pack)


HBM_ONLY = pl.BlockSpec(memory_space=pltpu.HBM)
SEM = pl.BlockSpec(memory_space=pltpu.SEMAPHORE)
DATAFLOW = pltpu.SideEffectType.DATAFLOW_SIDE_EFFECTING


def _shard_rows(buf, rows, dev):
    return buf.at[:, pl.ds(pl.multiple_of((4 * dev[0] + 2 * dev[1] + dev[2]) * rows, rows), rows), :]


def _gather_chips_plan(ins, lands):
    x, y, c = _coords()
    targets = [(x, y, 1 - c)] + [(*chip, c) for chip in _other_chips(x, y)]
    return [(ins[a], _shard_rows(lands[a], ins[a].shape[1], (x, y, c)), t, _shard_rows(lands[a], ins[a].shape[1], t))
            for a in range(len(ins)) for t in targets]


def _gather_pass_plan(bufs, _):
    x, y, c = _coords()
    plan = []
    for buf in bufs:
        r = buf.shape[1] // N_DEV
        for chip in _other_chips(x, y):
            mine, theirs = _shard_rows(buf, r, (*chip, c)), _shard_rows(buf, r, (*chip, 1 - c))
            plan.append((mine, mine, (x, y, 1 - c), theirs))
    return plan


def _pair_plan(ins, lands):
    x, y, c = _coords()
    plan = []
    for a in range(len(ins)):
        r = ins[a].shape[1] // N_DEV
        for q in range(N_CHIP):
            src = ins[a].at[:, pl.ds(pl.multiple_of((2 * q + 1 - c) * r, r), r), :]
            plan.append((src, lands[a].at[q], (x, y, 1 - c), lands[a].at[q]))
    return plan


def _chip_plan(ins, lands):
    x, y, c = _coords()
    chips = _other_chips(x, y)
    return [(ins[a].at[j], lands[a].at[j], (*chips[j], c), lands[a].at[j]) for a in range(len(ins)) for j in range(3)]


def _exchange_start(name, plan, copies_per_array, srcs, lands, after):
    n, m = len(srcs), len(srcs) + len(lands)
    count = copies_per_array * n

    def body(*refs):
        ins, land = refs[:n], refs[n:m]
        send_sems, recv_sems = refs[m + 1], refs[m + 2]
        token = refs[-1]
        for k, (src, dst, peer, _) in enumerate(plan(ins, land)):
            pltpu.make_async_remote_copy(src_ref=src, dst_ref=dst, send_sem=send_sems.at[k], recv_sem=recv_sems.at[k],
                                         device_id=peer, device_id_type=MESH).start()
        token[...] = jnp.zeros_like(token)

    thru = [pltpu.HBM(v.shape, v.dtype) for v in list(srcs) + list(lands)]
    outs = pl.pallas_call(
        body, name=name,
        in_specs=[HBM_ONLY] * m + [HBM],
        out_specs=[SEM, SEM] + [HBM_ONLY] * m + [pl.BlockSpec(memory_space=pltpu.VMEM)],
        out_shape=[pltpu.SemaphoreType.DMA((count,)), pltpu.SemaphoreType.DMA((count,))] + thru
        + [jax.ShapeDtypeStruct((8, 128), F32)],
        input_output_aliases={i: 2 + i for i in range(m)},
        compiler_params=pltpu.CompilerParams(has_side_effects=DATAFLOW),
    )(*[pltpu.with_memory_space_constraint(v, pltpu.HBM) for v in list(srcs) + list(lands)], after)
    return outs[0], outs[1], outs[2:2 + n], outs[2 + n:2 + m], outs[-1]


def _exchange_wait(name, plan, send_sems, recv_sems, srcs, lands, after):
    n, m = len(srcs), len(srcs) + len(lands)

    def body(*refs):
        ins, land = refs[:n], refs[n:m]
        send, recv = refs[m], refs[m + 1]
        for k, (src, _, peer, here) in enumerate(plan(ins, land)):
            cp = pltpu.make_async_remote_copy(src_ref=src, dst_ref=here, send_sem=send.at[k], recv_sem=recv.at[k],
                                              device_id=peer, device_id_type=MESH)
            cp.wait_send()
            cp.wait_recv()

    thru = [pltpu.HBM(v.shape, v.dtype) for v in list(srcs) + list(lands)]
    outs = pl.pallas_call(
        body, name=name,
        in_specs=[HBM_ONLY] * m + [SEM, SEM, pl.BlockSpec(memory_space=pl.ANY)],
        out_specs=[HBM_ONLY] * m,
        out_shape=thru,
        input_output_aliases={i: i for i in range(m)},
        compiler_params=pltpu.CompilerParams(has_side_effects=DATAFLOW),
    )(*srcs, *lands, send_sems, recv_sems, after)
    return outs[:n], outs[n:]


def _place_own(shard, dev):
    s, r, c = shard.shape

    def body(dev_ref, i_ref, o_ref):
        o_ref[...] = i_ref[...]

    return pl.pallas_call(
        body, name="place_own_shard",
        grid_spec=pltpu.PrefetchScalarGridSpec(
            num_scalar_prefetch=1, grid=(s,),
            in_specs=[pl.BlockSpec((None, r, c), lambda i, d: (i, 0, 0))],
            out_specs=pl.BlockSpec((None, r, c), lambda i, d: (i, d[0], 0))),
        out_shape=jax.ShapeDtypeStruct((s, N_DEV * r, c), shard.dtype),
        compiler_params=_cp("arbitrary"),
    )(dev, shard)


def _tile_rows(n, cap=512):
    t = min(n, cap)
    while n % t or t % 8:
        t -= 1
        if t < 8:
            return n
    return t


def _pair_sum(g, got, owner_dev, owner_chip, dtype):
    s, r8, c = g.shape
    r = r8 // N_DEV
    n = owner_dev.shape[0]

    def body(dev_ref, chip_ref, g_ref, got_ref, o_ref):
        o_ref[...] = (g_ref[...] + got_ref[...]).astype(dtype)

    return pl.pallas_call(
        body, name="pair_sum",
        grid_spec=pltpu.PrefetchScalarGridSpec(
            num_scalar_prefetch=2, grid=(n, s),
            in_specs=[pl.BlockSpec((None, r, c), lambda j, i, dev, chip: (i, dev[j], 0)),
                      pl.BlockSpec((None, None, r, c), lambda j, i, dev, chip: (chip[j], i, 0, 0))],
            out_specs=pl.BlockSpec((None, None, r, c), lambda j, i, dev, chip: (j, i, 0, 0))),
        out_shape=jax.ShapeDtypeStruct((n, s, r, c), dtype),
        compiler_params=_cp("parallel", "parallel"),
    )(owner_dev, owner_chip, g, got)


def _sum_chips(g, got, owner_dev, owner_chip, parts):
    _, s, r, c = parts.shape

    def body(dev_ref, chip_ref, g_ref, got_ref, p0, p1, p2, o_ref):
        own = g_ref[...] + got_ref[...]
        o_ref[...] = ((own + p0[...].astype(F32)) + p1[...].astype(F32)) + p2[...].astype(F32)

    def part(q):
        return pl.BlockSpec((None, None, r, c), lambda i, dev, chip: (q, i, 0, 0))

    return pl.pallas_call(
        body, name="chip_sum",
        grid_spec=pltpu.PrefetchScalarGridSpec(
            num_scalar_prefetch=2, grid=(s,),
            in_specs=[pl.BlockSpec((None, r, c), lambda i, dev, chip: (i, dev[0], 0)),
                      pl.BlockSpec((None, None, r, c), lambda i, dev, chip: (chip[0], i, 0, 0)),
                      part(0), part(1), part(2)],
            out_specs=pl.BlockSpec((None, r, c), lambda i, dev, chip: (i, 0, 0))),
        out_shape=jax.ShapeDtypeStruct((s, r, c), F32),
        compiler_params=_cp("parallel"),
    )(owner_dev, owner_chip, g, got, parts, parts, parts)


def _adamw(w, g, m, v):
    shape = w.shape
    c = shape[-1] if w.ndim > 1 else w.shape[0]
    args = [t.reshape(-1, c) for t in (w, g, m, v)]
    n = args[0].shape[0]
    tr = _tile_rows(n)

    def body(w_ref, g_ref, m_ref, v_ref, d_ref, mo_ref, vo_ref):
        g = g_ref[...]
        m = B1 * m_ref[...] + (1.0 - B1) * g
        v = B2 * v_ref[...] + (1.0 - B2) * jnp.square(g)
        m_hat = m / (1.0 - B1 ** STEP)
        v_hat = v / (1.0 - B2 ** STEP)
        d_ref[...] = -LR * (m_hat / (jnp.sqrt(v_hat) + ADAM_EPS) + WD * w_ref[...])
        mo_ref[...] = m
        vo_ref[...] = v

    outs = pl.pallas_call(
        body, name="adamw", grid=(n // tr,),
        in_specs=[_row(tr, c)] * 4, out_specs=[_row(tr, c)] * 3,
        out_shape=[jax.ShapeDtypeStruct((n, c), F32)] * 3,
        compiler_params=_cp("parallel"),
    )(*args)
    return [o.reshape(shape) for o in outs]


def _pack(pieces):
    flat = []
    for p in pieces:
        f = p.reshape(-1)
        flat.append(jnp.pad(f, (0, (-f.shape[0]) % 1024)))
    return jnp.concatenate(flat).reshape(-1, 128)


def _unpack(pack, shapes):
    flat = pack.reshape(-1)
    out, off = [], 0
    for s in shapes:
        size = 1
        for d in s:
            size *= d
        out.append(flat[off:off + size].reshape(s))
        off += size + (-size) % 1024
    return out


def kernel(x, positions, ffn1_norm, ffn1_w_gate, ffn1_w_up, ffn1_w_down, mix_norm, w_in, conv_w, conv_b, conv_ln_g, conv_ln_b, attn_sinks, w_out, ffn2_norm, ffn2_w_gate, ffn2_w_up, ffn2_w_down, final_norm, loss_target, m_ffn1_norm, m_ffn1_w_gate, m_ffn1_w_up, m_ffn1_w_down, m_mix_norm, m_w_in, m_conv_w, m_conv_b, m_conv_ln_g, m_conv_ln_b, m_attn_sinks, m_w_out, m_ffn2_norm, m_ffn2_w_gate, m_ffn2_w_up, m_ffn2_w_down, m_final_norm, v_ffn1_norm, v_ffn1_w_gate, v_ffn1_w_up, v_ffn1_w_down, v_mix_norm, v_w_in, v_conv_w, v_conv_b, v_conv_ln_g, v_conv_ln_b, v_attn_sinks, v_w_out, v_ffn2_norm, v_ffn2_w_gate, v_ffn2_w_up, v_ffn2_w_down, v_final_norm):
    L = ffn1_norm.shape[0]
    T = x.shape[1]
    x0 = x.reshape(T, D)
    target = loss_target.reshape(T, D)
    dev = 4 * lax.axis_index("x") + 2 * lax.axis_index("y") + lax.axis_index("c")

    def t_(w):
        return jnp.swapaxes(w, 1, 2)

    def ffn_shards(gate, up, down, l):
        return jnp.stack([t_(gate)[l], t_(up)[l], down[l]]).astype(BF16)

    shards = [[ffn_shards(ffn1_w_gate, ffn1_w_up, ffn1_w_down, l), ffn_shards(ffn2_w_gate, ffn2_w_up, ffn2_w_down, l),
               t_(w_in)[l:l + 1].astype(BF16), w_out[l:l + 1].astype(BF16)] for l in range(L)]
    cw_cols = CC // N_DEV
    cw_sh = jnp.pad(conv_w.reshape(-1), (0, (-L * CW * cw_cols) % 1024)).reshape(1, -1, 128)
    dev1 = dev.reshape(1).astype(jnp.int32)
    no_token = jnp.zeros((8, 128), F32)

    wffn1_0, cw_all = _all_gather([shards[0][0], cw_sh])
    lands0 = [_place_own(s, dev1) for s in shards[0][1:]]
    *rest0, token = _exchange_start("gather_start_0", _gather_chips_plan, 4, shards[0][1:], lands0, cw_all)
    weights = [None] * L

    cw_rows = cw_sh.shape[1]
    cw_full = cw_all.reshape(N_DEV, cw_rows * 128)[:, :L * CW * cw_cols].reshape(N_DEV, L, CW, cw_cols)
    cw_full = jnp.transpose(cw_full, (1, 2, 0, 3)).reshape(L, CW, CC)

    inv_freq = 1.0 / (10000.0 ** (jnp.arange(0, HD, 2, dtype=F32) / HD))
    ang = positions.reshape(T).astype(F32)[:, None] * inv_freq
    cos, sin = jnp.cos(ang), jnp.sin(ang)
    rc = jnp.concatenate([cos, cos, cos, cos], axis=1)
    rs = jnp.concatenate([-sin, sin, -sin, sin], axis=1)

    n1 = ffn1_norm.reshape(L, 1, D)
    nm = mix_norm.reshape(L, 1, D)
    n2 = ffn2_norm.reshape(L, 1, D)
    cb = conv_b.reshape(L, 1, CC)
    lg = conv_ln_g.reshape(L, 1, CC)
    lb = conv_ln_b.reshape(L, 1, CC)

    saved = []
    xa = x0
    for l in range(L):
        if l == 0:
            wffn1 = wffn1_0
        else:
            wffn1, wffn2, win, wout = _exchange_wait(f"gather_passed_{l}", _gather_pass_plan, *passing, after=xa)[0]
        h1, G1, U1, A1 = _ffn_up(xa, n1, wffn1, 0, 1, l, token)
        if l == 0:
            arrived = _exchange_wait("gather_wait_0", _gather_chips_plan, *rest0, after=A1)[1]
            *passing, token = _exchange_start("gather_pass_0", _gather_pass_plan, 3, arrived, [], A1)
        xb = _ffn_down(A1, xa, wffn1, 2, token)
        if l == 0:
            wffn2, win, wout = _exchange_wait("gather_passed_0", _gather_pass_plan, *passing, after=xb)[0]
        weights[l] = (wffn1, wffn2, win, wout)
        token = no_token
        if l + 1 < L:
            lands = [_place_own(s, dev1) for s in shards[l + 1]]
            *pending, token = _exchange_start(f"gather_start_{l + 1}", _gather_chips_plan, 4, shards[l + 1], lands, win)
        hm, qkv, u = _mix_in(xb, nm, win, rc, rs, l, token)
        ao = _attn_fwd(qkv, attn_sinks, l)
        y, co = _conv_fwd(u, cw_full, cb, lg, lb, l)
        xc, cat = _mix_out(ao, co, xb, wout)
        h2, G2, U2, A2 = _ffn_up(xc, n2, wffn2, 0, 1, l, no_token)
        token = no_token
        if l + 1 < L:
            arrived = _exchange_wait(f"gather_wait_{l + 1}", _gather_chips_plan, *pending, after=A2)[1]
            *passing, token = _exchange_start(f"gather_pass_{l + 1}", _gather_pass_plan, 3, arrived, [], A2)
        xd = _ffn_down(A2, xc, wffn2, 2, token)
        saved.append((xa, h1, G1, U1, A1, xb, hm, qkv, u, y, cat, xc, h2, G2, U2, A2))
        xa = xd
        token = no_token

    dx, loss_part, g_final = _final(xa, final_norm.reshape(1, D), target)

    cx, cy, cc = _coords()
    chip_of = [2 * cx + cy] + [2 * px + py for px, py in _other_chips(cx, cy)]
    own_chip = jnp.stack(chip_of[:1]).astype(jnp.int32)
    other_chips = jnp.stack(chip_of[1:]).astype(jnp.int32)

    g_n1, g_nm, g_n2 = [None] * L, [None] * L, [None] * L
    g_cb, g_lg, g_lb, g_sink, g_cw = [None] * L, [None] * L, [None] * L, [None] * L, [None] * L
    in_flight, reduced = [], {}

    def pair_begin(tag, group, after):
        lands = [lax.empty((N_CHIP, g.shape[0], g.shape[1] // N_DEV, g.shape[2]), F32) for g in group]
        *handles, token = _exchange_start(f"pair_start_{tag}", _pair_plan, N_CHIP, group, lands, after)
        return handles, token

    def chip_begin(tag, group, got, after):
        sent = [_pair_sum(g, r, 2 * other_chips + cc, other_chips, BF16) for g, r in zip(group, got)]
        *handles, token = _exchange_start(f"chip_start_{tag}", _chip_plan, 3, sent,
                                          [lax.empty(p.shape, p.dtype) for p in sent], after)
        in_flight.append((tag, group, got, handles))
        return token

    def pair_end_chip_begin(tag, handles, after):
        group, got = _exchange_wait(f"pair_wait_{tag}", _pair_plan, *handles, after=after)
        return chip_begin(tag, group, got, after)

    for l in reversed(range(L)):
        xa, h1, G1, U1, A1, xb, hm, qkv, u, y, cat, xc, h2, G2, U2, A2 = saved[l]
        wffn1, wffn2, win, wout = weights[l]
        gffn1 = lax.empty((3, FF, D), F32)
        gffn2 = lax.empty((3, FF, D), F32)
        gin = lax.empty((1, DIN, D), F32)
        gout = lax.empty((1, D, D), F32)
        d, dG, dU = _ffn_bwd_act(dx, G2, U2, wffn2, 2, token)
        gffn2 = _wgrad(gffn2, 2, A2, d)
        gffn2 = _wgrad(gffn2, 0, dG, h2)
        gffn2 = _wgrad(gffn2, 1, dU, h2)
        handles, token = pair_begin(f"{l}c", [gffn2], d)
        dx, g_n2[l] = _ffn_bwd_in(dG, dU, xc, dx, n2, wffn2, 0, 1, l, token)
        token = pair_end_chip_begin(f"{l}c", handles, dx)
        d, dao, dco = _mix_out_bwd(dx, wout, token)
        gout = _wgrad(gout, 0, cat, d)
        dy, g_lg[l], g_lb[l], g_cb[l] = _conv_bwd_norm(dco, y, lg, lb, l)
        du, g_cw8 = _conv_bwd_taps(dy, u, cw_full, l)
        g_cw[l] = g_cw8.reshape(CW, 8, CC).sum(axis=1)
        dq, dk, dv, g_sink[l] = _attn_bwd(qkv, dao, attn_sinks, l)
        dp, dx, g_nm[l] = _mix_in_bwd(dq, dk, dv, du, rc, rs, xb, dx, nm, win, l)
        gin = _wgrad(gin, 0, dp, hm)
        handles, token = pair_begin(f"{l}a", [gin, gout], dx)
        d, dG, dU = _ffn_bwd_act(dx, G1, U1, wffn1, 2, token)
        token = pair_end_chip_begin(f"{l}a", handles, d)
        gffn1 = _wgrad(gffn1, 2, A1, d)
        gffn1 = _wgrad(gffn1, 0, dG, h1)
        gffn1 = _wgrad(gffn1, 1, dU, h1)
        if l > 0:
            handles, token = pair_begin(f"{l}b", [gffn1], token)
            dx, g_n1[l] = _ffn_bwd_in(dG, dU, xa, dx, n1, wffn1, 0, 1, l, token)
            token = pair_end_chip_begin(f"{l}b", handles, dx)
        else:
            token = chip_begin(f"{l}b", [gffn1], _pair_exchange([gffn1]), token)
            dx, g_n1[l] = _ffn_bwd_in(dG, dU, xa, dx, n1, wffn1, 0, 1, l, token)

    grad_x = dx.reshape(1, T, D)
    for tag, group, got, handles in in_flight:
        parts = _exchange_wait(f"chip_wait_{tag}", _chip_plan, *handles, after=dx)[1]
        reduced[tag] = [_sum_chips(g, r, 2 * own_chip + cc, own_chip, p) for g, r, p in zip(group, got, parts)]
    g1 = jnp.stack([reduced[f"{l}b"][0] for l in range(L)])
    g2 = jnp.stack([reduced[f"{l}c"][0] for l in range(L)])
    gin_t = jnp.concatenate([reduced[f"{l}a"][0] for l in range(L)])
    gout_sh = jnp.concatenate([reduced[f"{l}a"][1] for l in range(L)])

    small = [loss_part,
             jnp.concatenate(g_n1), jnp.concatenate(g_nm), jnp.concatenate(g_n2), g_final,
             jnp.concatenate(g_cb), jnp.concatenate(g_lg), jnp.concatenate(g_lb),
             jnp.stack(g_sink)[:, :, 0], jnp.stack(g_cw)]
    small_shapes = [(1, 128), (L, D), (L, D), (L, D), (D,), (L, CC), (L, CC), (L, CC), (L, NH), (L, CW, CC)]
    tot = _unpack(_all_reduce_small(_pack(small)), small_shapes)
    loss = tot[0][0, 0]
    gr_n1, gr_nm, gr_n2, gr_final, gr_cb, gr_lg, gr_lb, gr_sink, gr_cw_full = tot[1:]
    gr_cw = lax.dynamic_slice_in_dim(gr_cw_full, dev * cw_cols, cw_cols, axis=2)

    grads_t = {"ffn1_w_gate": g1[:, 0], "ffn1_w_up": g1[:, 1], "ffn2_w_gate": g2[:, 0], "ffn2_w_up": g2[:, 1],
               "w_in": gin_t}
    grads = {
        "ffn1_norm": gr_n1, "ffn1_w_down": g1[:, 2],
        "mix_norm": gr_nm, "conv_w": gr_cw, "conv_b": gr_cb, "conv_ln_g": gr_lg,
        "conv_ln_b": gr_lb, "attn_sinks": gr_sink, "w_out": gout_sh,
        "ffn2_norm": gr_n2, "ffn2_w_down": g2[:, 2],
        "final_norm": gr_final,
    }
    weights = dict(ffn1_norm=ffn1_norm, ffn1_w_gate=ffn1_w_gate, ffn1_w_up=ffn1_w_up, ffn1_w_down=ffn1_w_down, mix_norm=mix_norm, w_in=w_in, conv_w=conv_w, conv_b=conv_b, conv_ln_g=conv_ln_g, conv_ln_b=conv_ln_b, attn_sinks=attn_sinks, w_out=w_out, ffn2_norm=ffn2_norm, ffn2_w_gate=ffn2_w_gate, ffn2_w_up=ffn2_w_up, ffn2_w_down=ffn2_w_down, final_norm=final_norm)
    moms = dict(ffn1_norm=m_ffn1_norm, ffn1_w_gate=m_ffn1_w_gate, ffn1_w_up=m_ffn1_w_up, ffn1_w_down=m_ffn1_w_down, mix_norm=m_mix_norm, w_in=m_w_in, conv_w=m_conv_w, conv_b=m_conv_b, conv_ln_g=m_conv_ln_g, conv_ln_b=m_conv_ln_b, attn_sinks=m_attn_sinks, w_out=m_w_out, ffn2_norm=m_ffn2_norm, ffn2_w_gate=m_ffn2_w_gate, ffn2_w_up=m_ffn2_w_up, ffn2_w_down=m_ffn2_w_down, final_norm=m_final_norm)
    vels = dict(ffn1_norm=v_ffn1_norm, ffn1_w_gate=v_ffn1_w_gate, ffn1_w_up=v_ffn1_w_up, ffn1_w_down=v_ffn1_w_down, mix_norm=v_mix_norm, w_in=v_w_in, conv_w=v_conv_w, conv_b=v_conv_b, conv_ln_g=v_conv_ln_g, conv_ln_b=v_conv_ln_b, attn_sinks=v_attn_sinks, w_out=v_w_out, ffn2_norm=v_ffn2_norm, ffn2_w_gate=v_ffn2_w_gate, ffn2_w_up=v_ffn2_w_up, ffn2_w_down=v_ffn2_w_down, final_norm=v_final_norm)

    names = list(weights)
    big = ("ffn1_w_gate", "ffn1_w_up", "ffn1_w_down", "w_in", "w_out", "ffn2_w_gate", "ffn2_w_up", "ffn2_w_down")
    delta, new_m, new_v = {}, {}, {}
    for k in big:
        if k in grads_t:
            outs = _adamw(t_(weights[k]), grads_t[k], t_(moms[k]), t_(vels[k]))
            grads[k], delta[k], new_m[k], new_v[k] = [t_(o) for o in [grads_t[k]] + outs]
        else:
            delta[k], new_m[k], new_v[k] = _adamw(weights[k], grads[k], moms[k], vels[k])
    rest = [k for k in names if k not in big]
    rest_shapes = [weights[k].shape for k in rest]
    packed = _adamw(*[_pack([t[k] for k in rest]) for t in (weights, grads, moms, vels)])
    for res, packed_out in zip((delta, new_m, new_v), packed):
        for k, val in zip(rest, _unpack(packed_out, rest_shapes)):
            res[k] = val

    return (loss, grad_x, *[grads[k] for k in names], *[delta[k] for k in names],
            *[new_m[k] for k in names], *[new_v[k] for k in names])
```

```python
import functools

import jax
import jax.numpy as jnp
from jax import lax
from jax.experimental import pallas as pl
from jax.experimental.pallas import tpu as pltpu

F32 = jnp.float32
BF16 = jnp.bfloat16
MESH = pl.DeviceIdType.MESH

N_DEV = 8
N_CHIP = 4
D = 1024
FF = 2816
HD = 64
NH = 8
NKV = 2
GROUP = NH // NKV
AW = NH * HD
KVW = NKV * HD
QKW = AW + KVW
QKVW = AW + 2 * KVW
CC = 512
CW = 31
DIN = QKVW + 2 * CC
BLK = 128
HALO = 32
EPS = 1e-5
SCALE = HD ** -0.5
NEG = float(jnp.finfo(jnp.float32).min)

LR, B1, B2, ADAM_EPS, WD, STEP = 0.001, 0.9, 0.999, 1e-08, 0.01, 10

TM = 512
TM_FFN_UP = 256
TM_CONV_BWD = 64
TK = 2048
FC = 256
ATT_BLOCKS = 8
VMEM_LIMIT = 56 * 1024 * 1024


def _cp(*sem):
    return pltpu.CompilerParams(dimension_semantics=sem, vmem_limit_bytes=VMEM_LIMIT)


def _row(tm, c):
    return pl.BlockSpec((tm, c), lambda i: (i, 0))


def _slab(shape, k, single=False):
    zeros = (0,) * len(shape)
    kw = dict(pipeline_mode=pl.Buffered(1)) if single else {}
    return pl.BlockSpec((None, *shape), lambda i: (k, *zeros), **kw)


def _acc(shape):
    return pl.BlockSpec(shape, lambda i: (0,) * len(shape))


def _nt(a, b):
    return lax.dot_general(a, b, (((1,), (1,)), ((), ())), preferred_element_type=F32)


def _tn(a, b):
    return lax.dot_general(a, b, (((0,), (0,)), ((), ())), preferred_element_type=F32)


def _nn(a, b):
    return jnp.dot(a, b, preferred_element_type=F32)


def _sigmoid(x):
    return jax.nn.sigmoid(x)


def _dsilu(z):
    s = _sigmoid(z)
    return s * (1.0 + z * (1.0 - s))


def _rms(x, g):
    r = lax.rsqrt(jnp.mean(x * x, axis=-1, keepdims=True) + EPS)
    xh = x * r
    return xh, r, xh * g


def _rms_bwd(dh, xh, r, g):
    dxh = dh * g
    return r * (dxh - xh * jnp.mean(dxh * xh, axis=-1, keepdims=True))


def _rope(t, c128, s128):
    w = t.shape[1]
    lane = lax.broadcasted_iota(jnp.int32, t.shape, 1)
    rot = jnp.where(lane % HD < HD // 2, pltpu.roll(t, w - HD // 2, 1), pltpu.roll(t, HD // 2, 1))
    return t * jnp.tile(c128, (1, w // 128)) + rot * jnp.tile(s128, (1, w // 128))


def _ffn_up(x, norm, wffn, sg, su, layer, token):
    T = x.shape[0]
    tm = TM_FFN_UP

    def body(x_ref, g_ref, wg_ref, wu_ref, token_ref, h_ref, P_ref, Q_ref, A_ref):
        _, _, hn = _rms(x_ref[...], g_ref[...])
        h = hn.astype(BF16)
        h_ref[...] = h
        for c in range(FF // FC):
            sl = slice(c * FC, (c + 1) * FC)
            g = _nt(h, wg_ref[sl, :])
            u = _nt(h, wu_ref[sl, :])
            s = _sigmoid(g)
            p = g * s
            P_ref[:, sl] = p.astype(BF16)
            Q_ref[:, sl] = (u * (s + p - p * s)).astype(BF16)
            A_ref[:, sl] = (p * u).astype(BF16)

    return pl.pallas_call(
        body, name="ffn_up", grid=(T // tm,),
        in_specs=[_row(tm, D), _slab((1, D), layer), _slab((FF, D), sg, True), _slab((FF, D), su, True), HBM],
        out_specs=[_row(tm, D), _row(tm, FF), _row(tm, FF), _row(tm, FF)],
        out_shape=[jax.ShapeDtypeStruct((T, D), BF16)] + [jax.ShapeDtypeStruct((T, FF), BF16)] * 3,
        compiler_params=_cp("parallel"),
    )(x, norm, wffn, wffn, token)


def _ffn_down(a, x, wffn, sd, token):
    T = x.shape[0]

    def body(a_ref, x_ref, w_ref, token_ref, o_ref):
        o_ref[...] = x_ref[...] + 0.5 * _nn(a_ref[...], w_ref[...])

    return pl.pallas_call(
        body, name="ffn_down", grid=(T // TM,),
        in_specs=[_row(TM, FF), _row(TM, D), _slab((FF, D), sd, True), HBM],
        out_specs=_row(TM, D),
        out_shape=jax.ShapeDtypeStruct((T, D), F32),
        compiler_params=_cp("parallel"),
    )(a, x, wffn, token)


def _mix_in(x, norm, win, rc, rs, layer, token):
    T = x.shape[0]

    def body(x_ref, g_ref, w_ref, c_ref, s_ref, token_ref, h_ref, qkv_ref, u_ref):
        _, _, hn = _rms(x_ref[...], g_ref[...])
        h = hn.astype(BF16)
        h_ref[...] = h
        qk = _nt(h, w_ref[0:QKW, :])
        qkv_ref[:, 0:QKW] = _rope(qk, c_ref[...], s_ref[...]).astype(BF16)
        qkv_ref[:, QKW:QKVW] = _nt(h, w_ref[QKW:QKVW, :]).astype(BF16)
        for c in range(2 * CC // FC):
            u_ref[:, c * FC:(c + 1) * FC] = _nt(h, w_ref[QKVW + c * FC:QKVW + (c + 1) * FC, :]).astype(BF16)

    return pl.pallas_call(
        body, name="mix_in", grid=(T // TM,),
        in_specs=[_row(TM, D), _slab((1, D), layer), _slab((DIN, D), 0, True), _row(TM, 128), _row(TM, 128), HBM],
        out_specs=[_row(TM, D), _row(TM, QKVW), _row(TM, 2 * CC)],
        out_shape=[jax.ShapeDtypeStruct((T, D), BF16), jax.ShapeDtypeStruct((T, QKVW), BF16),
                   jax.ShapeDtypeStruct((T, 2 * CC), BF16)],
        compiler_params=_cp("parallel"),
    )(x, norm, win, rc, rs, token)


def _band_mask(has_prev):
    j = lax.broadcasted_iota(jnp.int32, (2 * BLK, BLK), 0)
    r = lax.broadcasted_iota(jnp.int32, (2 * BLK, BLK), 1) + BLK
    rel = r - j
    return jnp.tile((rel >= 0) & (rel < BLK) & (has_prev | (j >= BLK)), (1, GROUP))


def _band(prev_ref, cur_ref, b, col):
    if b == 0:
        return jnp.concatenate([prev_ref[:, col:col + HD], cur_ref[0:BLK, col:col + HD]], axis=0)
    return cur_ref[(b - 1) * BLK:(b + 1) * BLK, col:col + HD]


def _stack_heads(ref, b, kv):
    cols = [(kv * GROUP + g) * HD for g in range(GROUP)]
    return jnp.concatenate([ref[b * BLK:(b + 1) * BLK, c:c + HD] for c in cols], axis=0)


def _unstack_t(xt):
    x = xt.T
    return jnp.concatenate([x[g * BLK:(g + 1) * BLK, :] for g in range(GROUP)], axis=1)


def _sink_row(sink_ref, layer, kv):
    return jnp.concatenate([jnp.full((1, BLK), sink_ref[layer, kv * GROUP + g], F32) for g in range(GROUP)], axis=1)


def _probs_t(q4, kb, mask, sink):
    s = jnp.where(mask, _nt(kb, q4) * SCALE, NEG)
    m = jnp.maximum(jnp.max(s, axis=0, keepdims=True), sink)
    p = jnp.exp(s - m)
    e = jnp.exp(sink - m)
    inv = 1.0 / (jnp.sum(p, axis=0, keepdims=True) + e)
    return p * inv, e * inv


def _attn_fwd(qkv, sinks, layer):
    T = qkv.shape[0]
    tq = ATT_BLOCKS * BLK

    def body(sink_ref, cur_ref, prev_ref, o_ref):
        first = _band_mask(pl.program_id(0) > 0)
        later = _band_mask(True)
        for b in range(ATT_BLOCKS):
            outs = []
            for kv in range(NKV):
                kb = _band(prev_ref, cur_ref, b, AW + kv * HD)
                vb = _band(prev_ref, cur_ref, b, QKW + kv * HD)
                pt, _ = _probs_t(_stack_heads(cur_ref, b, kv), kb, first if b == 0 else later,
                                 _sink_row(sink_ref, layer, kv))
                outs.append(_unstack_t(_nn(vb.T, pt.astype(BF16))))
            o_ref[b * BLK:(b + 1) * BLK, :] = jnp.concatenate(outs, axis=1).astype(BF16)

    return pl.pallas_call(
        body, name="attn_fwd", grid=(T // tq,),
        in_specs=[pl.BlockSpec(memory_space=pltpu.SMEM), _row(tq, QKVW),
                  pl.BlockSpec((BLK, QKVW), lambda i: (jnp.maximum(i * ATT_BLOCKS - 1, 0), 0))],
        out_specs=_row(tq, AW),
        out_shape=jax.ShapeDtypeStruct((T, AW), BF16),
        compiler_params=_cp("parallel"),
    )(sinks, qkv, qkv)


def _glu(u):
    u = u.astype(F32)
    return u[:, :CC] * _sigmoid(u[:, CC:])


def _fill_ext(ext_ref, first, second):
    n = ext_ref.shape[0] - 8
    ext_ref[0:first.shape[0], :] = first
    ext_ref[first.shape[0]:n, :] = second
    ext_ref[n:, :] = jnp.zeros((8, ext_ref.shape[1]), F32)


def _taps(ext_ref, w_ref, offsets, cols, tm):
    y = None
    for b in range(8):
        z = None
        for k, off in enumerate(offsets):
            if off % 8 == b:
                term = w_ref[k:k + 1, cols] * ext_ref[pl.ds(off - b, tm + 8), cols]
                z = term if z is None else z + term
        if z is not None:
            y = z[b:b + tm, :] if y is None else y + z[b:b + tm, :]
    return y


def _conv_fwd(u, cw, cb, lg, lb, layer):
    T = u.shape[0]
    tm = TM

    def body(u_ref, up_ref, w_ref, b_ref, g_ref, bb_ref, y_ref, o_ref, ext_ref):
        i = pl.program_id(0)
        _fill_ext(ext_ref, jnp.where(i > 0, _glu(up_ref[...]), 0.0), _glu(u_ref[...]))
        for c in range(CC // 128):
            cols = slice(c * 128, (c + 1) * 128)
            y_ref[:, cols] = _taps(ext_ref, w_ref, [HALO - (CW - 1) + k for k in range(CW)], cols, tm) + b_ref[:, cols]
        y = y_ref[...]
        xc = y - jnp.mean(y, axis=-1, keepdims=True)
        z = xc * lax.rsqrt(jnp.mean(xc * xc, axis=-1, keepdims=True) + EPS) * g_ref[...] + bb_ref[...]
        o_ref[...] = (z * _sigmoid(z)).astype(BF16)

    return pl.pallas_call(
        body, name="conv_fwd", grid=(T // tm,),
        in_specs=[_row(tm, 2 * CC),
                  pl.BlockSpec((HALO, 2 * CC), lambda i: (jnp.maximum(i * (tm // HALO) - 1, 0), 0)),
                  _slab((CW, CC), layer), _slab((1, CC), layer), _slab((1, CC), layer), _slab((1, CC), layer)],
        out_specs=[_row(tm, CC), _row(tm, CC)],
        out_shape=[jax.ShapeDtypeStruct((T, CC), F32), jax.ShapeDtypeStruct((T, CC), BF16)],
        scratch_shapes=[pltpu.VMEM((tm + HALO + 8, CC), F32)],
        compiler_params=_cp("parallel"),
    )(u, u, cw, cb, lg, lb)


def _mix_out(ao, co, x, wout):
    T = x.shape[0]

    def body(ao_ref, co_ref, x_ref, w_ref, o_ref, cat_ref):
        cat = jnp.concatenate([ao_ref[...], co_ref[...]], axis=1)
        cat_ref[...] = cat
        o_ref[...] = x_ref[...] + _nn(cat, w_ref[...])

    return pl.pallas_call(
        body, name="mix_out", grid=(T // TM,),
        in_specs=[_row(TM, AW), _row(TM, CC), _row(TM, D), _slab((D, D), 0, True)],
        out_specs=[_row(TM, D), _row(TM, D)],
        out_shape=[jax.ShapeDtypeStruct((T, D), F32), jax.ShapeDtypeStruct((T, D), BF16)],
        compiler_params=_cp("parallel"),
    )(ao, co, x, wout)


def _final(x, norm, target):
    T = x.shape[0]

    def body(x_ref, g_ref, t_ref, dx_ref, loss_ref, dg_ref):
        @pl.when(pl.program_id(0) == 0)
        def _():
            loss_ref[...] = jnp.zeros_like(loss_ref)
            dg_ref[...] = jnp.zeros_like(dg_ref)

        g = g_ref[...]
        xh, r, y = _rms(x_ref[...], g)
        err = y - t_ref[...]
        loss_ref[...] += jnp.full(loss_ref.shape, (0.5 / D) * jnp.sum(err * err), F32)
        dy = err * (1.0 / D)
        dg_ref[...] += jnp.sum(dy * xh, axis=0, keepdims=True)
        dx_ref[...] = _rms_bwd(dy, xh, r, g)

    return pl.pallas_call(
        body, name="final_loss", grid=(T // TM,),
        in_specs=[_row(TM, D), _acc((1, D)), _row(TM, D)],
        out_specs=[_row(TM, D), _acc((1, 128)), _acc((1, D))],
        out_shape=[jax.ShapeDtypeStruct((T, D), F32), jax.ShapeDtypeStruct((1, 128), F32),
                   jax.ShapeDtypeStruct((1, D), F32)],
        compiler_params=_cp("arbitrary"),
    )(x, norm, target)


def _ffn_bwd_act(dx, wffn, sd, token):
    T = dx.shape[0]

    def body(dx_ref, w_ref, token_ref, d_ref, dA_ref):
        d = (0.5 * dx_ref[...]).astype(BF16)
        d_ref[...] = d
        for c in range(FF // FC):
            sl = slice(c * FC, (c + 1) * FC)
            dA_ref[:, sl] = _nt(d, w_ref[sl, :]).astype(BF16)

    return pl.pallas_call(
        body, name="ffn_bwd_act", grid=(T // TM,),
        in_specs=[_row(TM, D), _slab((FF, D), sd, True), HBM],
        out_specs=[_row(TM, D), _row(TM, FF)],
        out_shape=[jax.ShapeDtypeStruct((T, D), BF16), jax.ShapeDtypeStruct((T, FF), BF16)],
        compiler_params=_cp("parallel"),
    )(dx, wffn, token)


def _ffn_bwd_in(dA, P, Q, x, dx, norm, wffn, sg, su, layer, token):
    T = x.shape[0]

    def body(dA_ref, P_ref, Q_ref, x_ref, dx_ref, g_ref, wg_ref, wu_ref, token_ref, o_ref, dg_ref):
        @pl.when(pl.program_id(0) == 0)
        def _():
            dg_ref[...] = jnp.zeros_like(dg_ref)

        dA = dA_ref[...]
        dh = _nn(dA * Q_ref[...], wg_ref[...]) + _nn(dA * P_ref[...], wu_ref[...])
        g = g_ref[...]
        xh, r, _ = _rms(x_ref[...], g)
        dg_ref[...] += jnp.sum(dh * xh, axis=0, keepdims=True)
        o_ref[...] = dx_ref[...] + _rms_bwd(dh, xh, r, g)

    return pl.pallas_call(
        body, name="ffn_bwd_in", grid=(T // TM,),
        in_specs=[_row(TM, FF), _row(TM, FF), _row(TM, FF), _row(TM, D), _row(TM, D), _slab((1, D), layer),
                  _slab((FF, D), sg, True), _slab((FF, D), su, True), HBM],
        out_specs=[_row(TM, D), _acc((1, D))],
        out_shape=[jax.ShapeDtypeStruct((T, D), F32), jax.ShapeDtypeStruct((1, D), F32)],
        compiler_params=_cp("arbitrary"),
    )(dA, P, Q, x, dx, norm, wffn, wffn, token)


def _mix_out_bwd(dx, wout, token):
    T = dx.shape[0]

    def body(dx_ref, w_ref, token_ref, d_ref, dao_ref, dco_ref):
        d = dx_ref[...].astype(BF16)
        d_ref[...] = d
        dcat = _nt(d, w_ref[...])
        dao_ref[...] = dcat[:, :AW].astype(BF16)
        dco_ref[...] = dcat[:, AW:].astype(BF16)

    return pl.pallas_call(
        body, name="mix_out_bwd", grid=(T // TM,),
        in_specs=[_row(TM, D), _slab((D, D), 0, True), HBM],
        out_specs=[_row(TM, D), _row(TM, AW), _row(TM, CC)],
        out_shape=[jax.ShapeDtypeStruct((T, D), BF16), jax.ShapeDtypeStruct((T, AW), BF16),
                   jax.ShapeDtypeStruct((T, CC), BF16)],
        compiler_params=_cp("parallel"),
    )(dx, wout, token)


def _conv_bwd_norm(dco, y, lg, lb, layer):
    T = y.shape[0]

    def body(dco_ref, y_ref, g_ref, bb_ref, dy_ref, dlg_ref, dlb_ref, dcb_ref):
        @pl.when(pl.program_id(0) == 0)
        def _():
            dlg_ref[...] = jnp.zeros_like(dlg_ref)
            dlb_ref[...] = jnp.zeros_like(dlb_ref)
            dcb_ref[...] = jnp.zeros_like(dcb_ref)

        y = y_ref[...]
        g = g_ref[...]
        xc = y - jnp.mean(y, axis=-1, keepdims=True)
        rs = lax.rsqrt(jnp.mean(xc * xc, axis=-1, keepdims=True) + EPS)
        xn = xc * rs
        z = xn * g + bb_ref[...]
        dz = dco_ref[...].astype(F32) * _dsilu(z)
        dlg_ref[...] += jnp.sum(dz * xn, axis=0, keepdims=True)
        dlb_ref[...] += jnp.sum(dz, axis=0, keepdims=True)
        dxn = dz * g
        dy = rs * (dxn - jnp.mean(dxn, axis=-1, keepdims=True) - xn * jnp.mean(dxn * xn, axis=-1, keepdims=True))
        dcb_ref[...] += jnp.sum(dy, axis=0, keepdims=True)
        dy_ref[...] = dy

    return pl.pallas_call(
        body, name="conv_bwd_norm", grid=(T // TM,),
        in_specs=[_row(TM, CC), _row(TM, CC), _slab((1, CC), layer), _slab((1, CC), layer)],
        out_specs=[_row(TM, CC), _acc((1, CC)), _acc((1, CC)), _acc((1, CC))],
        out_shape=[jax.ShapeDtypeStruct((T, CC), F32)] + [jax.ShapeDtypeStruct((1, CC), F32)] * 3,
        compiler_params=_cp("arbitrary"),
    )(dco, y, lg, lb)


def _conv_bwd_taps(dy, u, cw, layer):
    T = u.shape[0]
    tm = TM_CONV_BWD
    n_halo = T // HALO

    def body(dy_ref, dyn_ref, u_ref, up_ref, w_ref, du_ref, dw_ref, hext_ref, dext_ref, dz_ref, dsh_ref, dh_ref):
        i = pl.program_id(0)

        @pl.when(i == 0)
        def _():
            dw_ref[...] = jnp.zeros_like(dw_ref)

        _fill_ext(hext_ref, jnp.where(i > 0, _glu(up_ref[...]), 0.0), _glu(u_ref[...]))
        _fill_ext(dext_ref, dy_ref[...], jnp.where(i < pl.num_programs(0) - 1, dyn_ref[...], 0.0))
        _fill_ext(dz_ref, jnp.zeros((8, CC), F32), dy_ref[...])
        for b in range(8):
            dsh_ref[b] = dz_ref[pl.ds(8 - b, tm + 8), :]
        h_offsets = [HALO - (CW - 1) + k for k in range(CW)]
        for c in range(CC // 128):
            cols = slice(c * 128, (c + 1) * 128)
            dh_ref[:, cols] = _taps(dext_ref, w_ref, [CW - 1 - k for k in range(CW)], cols, tm)
            for k, off in enumerate(h_offsets):
                b = off % 8
                prod = dsh_ref[b, :, cols] * hext_ref[pl.ds(off - b, tm + 8), cols]
                dw_ref[8 * k:8 * k + 8, cols] += jnp.sum(prod.reshape((tm + 8) // 8, 8, 128), axis=0)
        dh = dh_ref[...]
        uu = u_ref[...].astype(F32)
        a = uu[:, :CC]
        sg = _sigmoid(uu[:, CC:])
        du_ref[:, :CC] = (dh * sg).astype(BF16)
        du_ref[:, CC:] = (dh * a * sg * (1.0 - sg)).astype(BF16)

    return pl.pallas_call(
        body, name="conv_bwd_taps", grid=(T // tm,),
        in_specs=[_row(tm, CC),
                  pl.BlockSpec((HALO, CC), lambda i: (jnp.minimum((i + 1) * (tm // HALO), n_halo - 1), 0)),
                  _row(tm, 2 * CC),
                  pl.BlockSpec((HALO, 2 * CC), lambda i: (jnp.maximum(i * (tm // HALO) - 1, 0), 0)),
                  _slab((CW, CC), layer)],
        out_specs=[_row(tm, 2 * CC), _acc((CW * 8, CC))],
        out_shape=[jax.ShapeDtypeStruct((T, 2 * CC), BF16), jax.ShapeDtypeStruct((CW * 8, CC), F32)],
        scratch_shapes=[pltpu.VMEM((tm + HALO + 8, CC), F32), pltpu.VMEM((tm + HALO + 8, CC), F32),
                        pltpu.VMEM((tm + 16, CC), F32), pltpu.VMEM((8, tm + 8, CC), F32), pltpu.VMEM((tm, CC), F32)],
        compiler_params=_cp("arbitrary"),
    )(dy, dy, u, u, cw)


def _attn_bwd(qkv, dao, sinks, layer):
    T = qkv.shape[0]
    tq = ATT_BLOCKS * BLK

    def body(sink_ref, cur_ref, prev_ref, do_ref, dq_ref, dk_ref, dv_ref, ds_ref):
        i = pl.program_id(0)

        @pl.when(i == 0)
        def _():
            dk_ref[...] = jnp.zeros_like(dk_ref)
            dv_ref[...] = jnp.zeros_like(dv_ref)
            ds_ref[...] = jnp.zeros_like(ds_ref)

        first = _band_mask(i > 0)
        later = _band_mask(True)
        base = pl.multiple_of(i * tq, tq)
        before = pl.multiple_of(jnp.maximum(i * ATT_BLOCKS - 1, 0) * BLK, BLK)
        for b in range(ATT_BLOCKS):
            dqs, dks, dvs = [], [], []
            for kv in range(NKV):
                kb = _band(prev_ref, cur_ref, b, AW + kv * HD)
                vb = _band(prev_ref, cur_ref, b, QKW + kv * HD)
                q4 = _stack_heads(cur_ref, b, kv)
                do4 = _stack_heads(do_ref, b, kv)
                pt, psink = _probs_t(q4, kb, first if b == 0 else later, _sink_row(sink_ref, layer, kv))
                dpt = _nt(vb, do4)
                dd = jnp.sum(pt * dpt, axis=0, keepdims=True)
                dst = (pt * (dpt - dd) * SCALE).astype(BF16)
                sd = psink * dd
                for g in range(GROUP):
                    hh = kv * GROUP + g
                    ds_ref[hh:hh + 1, :] += jnp.full((1, 128), -jnp.sum(sd[:, g * BLK:(g + 1) * BLK]), F32)
                dqs.append(_unstack_t(_nn(kb.T, dst)))
                dks.append(_nn(dst, q4))
                dvs.append(_nn(pt.astype(BF16), do4))
            dq_ref[b * BLK:(b + 1) * BLK, :] = jnp.concatenate(dqs, axis=1).astype(BF16)
            dkband = jnp.concatenate(dks, axis=1)
            dvband = jnp.concatenate(dvs, axis=1)
            if b == 0:
                dk_ref[pl.ds(before, BLK), :] += dkband[:BLK]
                dv_ref[pl.ds(before, BLK), :] += dvband[:BLK]
                dk_ref[pl.ds(base, BLK), :] += dkband[BLK:]
                dv_ref[pl.ds(base, BLK), :] += dvband[BLK:]
            else:
                r0 = pl.multiple_of(base + (b - 1) * BLK, BLK)
                dk_ref[pl.ds(r0, 2 * BLK), :] += dkband
                dv_ref[pl.ds(r0, 2 * BLK), :] += dvband

    return pl.pallas_call(
        body, name="attn_bwd", grid=(T // tq,),
        in_specs=[pl.BlockSpec(memory_space=pltpu.SMEM), _row(tq, QKVW),
                  pl.BlockSpec((BLK, QKVW), lambda i: (jnp.maximum(i * ATT_BLOCKS - 1, 0), 0)), _row(tq, AW)],
        out_specs=[_row(tq, AW), _acc((T, KVW)), _acc((T, KVW)), _acc((NH, 128))],
        out_shape=[jax.ShapeDtypeStruct((T, AW), BF16), jax.ShapeDtypeStruct((T, KVW), F32),
                   jax.ShapeDtypeStruct((T, KVW), F32), jax.ShapeDtypeStruct((NH, 128), F32)],
        compiler_params=_cp("arbitrary"),
    )(sinks, qkv, qkv, dao)


def _mix_in_bwd(dq, dk, dv, du, rc, rs, x, dx, norm, win, layer):
    T = x.shape[0]

    def body(dq_ref, dk_ref, dv_ref, du_ref, c_ref, s_ref, x_ref, dx_ref, g_ref, w_ref, dp_ref, o_ref, dg_ref):
        @pl.when(pl.program_id(0) == 0)
        def _():
            dg_ref[...] = jnp.zeros_like(dg_ref)

        dqk = jnp.concatenate([dq_ref[...].astype(F32), dk_ref[...]], axis=1)
        dqk = _rope(dqk, c_ref[...], -s_ref[...])
        dp = jnp.concatenate([dqk.astype(BF16), dv_ref[...].astype(BF16), du_ref[...]], axis=1)
        dp_ref[...] = dp
        dh = _nn(dp, w_ref[...])
        g = g_ref[...]
        xh, r, _ = _rms(x_ref[...], g)
        dg_ref[...] += jnp.sum(dh * xh, axis=0, keepdims=True)
        o_ref[...] = dx_ref[...] + _rms_bwd(dh, xh, r, g)

    return pl.pallas_call(
        body, name="mix_in_bwd", grid=(T // TM,),
        in_specs=[_row(TM, AW), _row(TM, KVW), _row(TM, KVW), _row(TM, 2 * CC), _row(TM, 128), _row(TM, 128),
                  _row(TM, D), _row(TM, D), _slab((1, D), layer), _slab((DIN, D), 0, True)],
        out_specs=[_row(TM, DIN), _row(TM, D), _acc((1, D))],
        out_shape=[jax.ShapeDtypeStruct((T, DIN), BF16), jax.ShapeDtypeStruct((T, D), F32),
                   jax.ShapeDtypeStruct((1, D), F32)],
        compiler_params=_cp("arbitrary"),
    )(dq, dk, dv, du, rc, rs, x, dx, norm, win)


def _wgrad(buf, slab, a, b, times=None):
    T, M = a.shape
    N = b.shape[1]
    tmm = M // 2 if M > 1024 else M
    lhs = [a] if times is None else [a, times]

    def body(buf_ref, *refs):
        *lhs_refs, b_ref, o_ref = refs

        @pl.when(pl.program_id(1) == 0)
        def _():
            o_ref[...] = jnp.zeros_like(o_ref)

        a_tile = lhs_refs[0][...]
        if times is not None:
            a_tile = a_tile * lhs_refs[1][...]
        o_ref[...] += _tn(a_tile, b_ref[...])

    return pl.pallas_call(
        body, name="wgrad", grid=(M // tmm, T // TK),
        in_specs=[pl.BlockSpec(memory_space=pl.ANY)] + [pl.BlockSpec((TK, tmm), lambda i, k: (k, i))] * len(lhs)
        + [pl.BlockSpec((TK, N), lambda i, k: (k, 0))],
        out_specs=pl.BlockSpec((None, tmm, N), lambda i, k: (slab, i, 0)),
        out_shape=jax.ShapeDtypeStruct(buf.shape, F32),
        input_output_aliases={0: 0},
        compiler_params=_cp("parallel", "arbitrary"),
    )(buf, *lhs, b)


HBM = pl.BlockSpec(memory_space=pl.ANY)


def _coords():
    return lax.axis_index("x"), lax.axis_index("y"), lax.axis_index("c")


def _other_chips(x, y):
    return [(1 - x, y), (x, 1 - y), (1 - x, 1 - y)]


def _all_gather(shards):
    n = len(shards)

    def body(*refs):
        ins, outs = refs[:n], refs[n:2 * n]
        send_sems, recv_sems, local_sems = refs[2 * n:]
        x, y, c = _coords()
        me, sibling = (x, y, c), (x, y, 1 - c)
        chips = _other_chips(x, y)

        def rows(a, dev):
            r = ins[a].shape[1]
            return outs[a].at[:, pl.ds(pl.multiple_of((4 * dev[0] + 2 * dev[1] + dev[2]) * r, r), r), :]

        def copy(a, k, block, to, src=None):
            return pltpu.make_async_remote_copy(
                src_ref=rows(a, block) if src is None else src, dst_ref=rows(a, block),
                send_sem=send_sems.at[a * 7 + k], recv_sem=recv_sems.at[a * 7 + k],
                device_id=to, device_id_type=MESH)

        mine = [pltpu.make_async_copy(ins[a], rows(a, me), local_sems.at[a]) for a in range(n)]
        for cp in mine:
            cp.start()
        first = []
        for a in range(n):
            first.append(copy(a, 0, me, sibling, src=ins[a]))
            first += [copy(a, 1 + j, me, (*chip, c), src=ins[a]) for j, chip in enumerate(chips)]
        for cp in first:
            cp.start()
        passed = []
        for j, chip in enumerate(chips):
            for a in range(n):
                copy(a, 1 + j, (*chip, c), me).wait_recv()
                fwd = copy(a, 4 + j, (*chip, c), sibling)
                fwd.start()
                passed.append(fwd)
        for a in range(n):
            copy(a, 0, sibling, me).wait_recv()
            for j, chip in enumerate(chips):
                copy(a, 4 + j, (*chip, 1 - c), me).wait_recv()
        for cp in first + passed:
            cp.wait_send()
        for cp in mine:
            cp.wait()

    return pl.pallas_call(
        body, name="all_gather_weights",
        in_specs=[HBM] * n, out_specs=[HBM] * n,
        out_shape=[jax.ShapeDtypeStruct((s.shape[0], N_DEV * s.shape[1], s.shape[2]), s.dtype) for s in shards],
        scratch_shapes=[pltpu.SemaphoreType.DMA((7 * n,)), pltpu.SemaphoreType.DMA((7 * n,)),
                        pltpu.SemaphoreType.DMA((n,))],
    )(*shards)


def _pair_exchange(grads):
    n = len(grads)

    def body(*refs):
        ins, got = refs[:n], refs[n:2 * n]
        send_sems, recv_sems = refs[2 * n:]
        x, y, c = _coords()
        sibling = (x, y, 1 - c)

        def remote(a, q):
            r = ins[a].shape[1] // N_DEV
            src = ins[a].at[:, pl.ds(pl.multiple_of((2 * q + 1 - c) * r, r), r), :]
            return pltpu.make_async_remote_copy(
                src_ref=src, dst_ref=got[a].at[q],
                send_sem=send_sems.at[a * N_CHIP + q], recv_sem=recv_sems.at[a * N_CHIP + q],
                device_id=sibling, device_id_type=MESH)

        sends = [remote(a, q) for a in range(n) for q in range(N_CHIP)]
        for cp in sends:
            cp.start()
        for cp in sends:
            cp.wait_recv()
        for cp in sends:
            cp.wait_send()

    return pl.pallas_call(
        body, name="grad_pair_exchange",
        in_specs=[HBM] * n, out_specs=[HBM] * n,
        out_shape=[jax.ShapeDtypeStruct((N_CHIP, g.shape[0], g.shape[1] // N_DEV, g.shape[2]), g.dtype) for g in grads],
        scratch_shapes=[pltpu.SemaphoreType.DMA((N_CHIP * n,)), pltpu.SemaphoreType.DMA((N_CHIP * n,))],
    )(*grads)


def _all_reduce_small(pack):
    R = pack.shape[0]

    def body(p_ref, tot_ref, all_ref, send_sems, recv_sems):
        x, y, c = _coords()
        me = 4 * x + 2 * y + c
        all_ref[me] = p_ref[...]
        peers = []
        for k in range(1, N_DEV):
            bx, by, bc = (k >> 2) & 1, (k >> 1) & 1, k & 1
            peers.append((x ^ bx, y ^ by, c ^ bc))

        def copy(k, slot, to):
            return pltpu.make_async_remote_copy(
                src_ref=p_ref, dst_ref=all_ref.at[slot], send_sem=send_sems.at[k], recv_sem=recv_sems.at[k],
                device_id=to, device_id_type=MESH)

        sends = [copy(k, me, peer) for k, peer in enumerate(peers)]
        for cp in sends:
            cp.start()
        for k, peer in enumerate(peers):
            copy(k, 4 * peer[0] + 2 * peer[1] + peer[2], peer).wait_recv()
        for cp in sends:
            cp.wait_send()
        tot = all_ref[0]
        for d in range(1, N_DEV):
            tot = tot + all_ref[d]
        tot_ref[...] = tot

    vmem = pl.BlockSpec(memory_space=pltpu.VMEM)
    return pl.pallas_call(
        body, name="all_reduce_small",
        in_specs=[vmem], out_specs=vmem,
        out_shape=jax.ShapeDtypeStruct((R, 128), F32),
        scratch_shapes=[pltpu.VMEM((N_DEV, R, 128), F32), pltpu.SemaphoreType.DMA((N_DEV - 1,)),
                        pltpu.SemaphoreType.DMA((N_DEV - 1,))],
    )(pack)


HBM_ONLY = pl.BlockSpec(memory_space=pltpu.HBM)
SEM = pl.BlockSpec(memory_space=pltpu.SEMAPHORE)
DATAFLOW = pltpu.SideEffectType.DATAFLOW_SIDE_EFFECTING


def _shard_rows(buf, rows, dev):
    return buf.at[:, pl.ds(pl.multiple_of((4 * dev[0] + 2 * dev[1] + dev[2]) * rows, rows), rows), :]


def _gather_chips_plan(ins, lands):
    x, y, c = _coords()
    targets = [(x, y, 1 - c)] + [(*chip, c) for chip in _other_chips(x, y)]
    return [(ins[a], _shard_rows(lands[a], ins[a].shape[1], (x, y, c)), t, _shard_rows(lands[a], ins[a].shape[1], t))
            for a in range(len(ins)) for t in targets]


def _gather_pass_plan(bufs, _):
    x, y, c = _coords()
    plan = []
    for buf in bufs:
        r = buf.shape[1] // N_DEV
        for chip in _other_chips(x, y):
            mine, theirs = _shard_rows(buf, r, (*chip, c)), _shard_rows(buf, r, (*chip, 1 - c))
            plan.append((mine, mine, (x, y, 1 - c), theirs))
    return plan


def _pair_plan(ins, lands):
    x, y, c = _coords()
    plan = []
    for a in range(len(ins)):
        r = ins[a].shape[1] // N_DEV
        for q in range(N_CHIP):
            src = ins[a].at[:, pl.ds(pl.multiple_of((2 * q + 1 - c) * r, r), r), :]
            plan.append((src, lands[a].at[q], (x, y, 1 - c), lands[a].at[q]))
    return plan


def _chip_plan(ins, lands):
    x, y, c = _coords()
    chips = _other_chips(x, y)
    return [(ins[a].at[j], lands[a].at[j], (*chips[j], c), lands[a].at[j]) for a in range(len(ins)) for j in range(3)]


def _exchange_start(name, plan, copies_per_array, srcs, lands, after):
    n, m = len(srcs), len(srcs) + len(lands)
    count = copies_per_array * n

    def body(*refs):
        ins, land = refs[:n], refs[n:m]
        send_sems, recv_sems = refs[m + 1], refs[m + 2]
        token = refs[-1]
        for k, (src, dst, peer, _) in enumerate(plan(ins, land)):
            pltpu.make_async_remote_copy(src_ref=src, dst_ref=dst, send_sem=send_sems.at[k], recv_sem=recv_sems.at[k],
                                         device_id=peer, device_id_type=MESH).start()
        token[...] = jnp.zeros_like(token)

    thru = [pltpu.HBM(v.shape, v.dtype) for v in list(srcs) + list(lands)]
    outs = pl.pallas_call(
        body, name=name,
        in_specs=[HBM_ONLY] * m + [HBM],
        out_specs=[SEM, SEM] + [HBM_ONLY] * m + [pl.BlockSpec(memory_space=pltpu.VMEM)],
        out_shape=[pltpu.SemaphoreType.DMA((count,)), pltpu.SemaphoreType.DMA((count,))] + thru
        + [jax.ShapeDtypeStruct((8, 128), F32)],
        input_output_aliases={i: 2 + i for i in range(m)},
        compiler_params=pltpu.CompilerParams(has_side_effects=DATAFLOW),
    )(*[pltpu.with_memory_space_constraint(v, pltpu.HBM) for v in list(srcs) + list(lands)], after)
    return outs[0], outs[1], outs[2:2 + n], outs[2 + n:2 + m], outs[-1]


def _exchange_wait(name, plan, send_sems, recv_sems, srcs, lands, after):
    n, m = len(srcs), len(srcs) + len(lands)

    def body(*refs):
        ins, land = refs[:n], refs[n:m]
        send, recv = refs[m], refs[m + 1]
        for k, (src, _, peer, here) in enumerate(plan(ins, land)):
            cp = pltpu.make_async_remote_copy(src_ref=src, dst_ref=here, send_sem=send.at[k], recv_sem=recv.at[k],
                                              device_id=peer, device_id_type=MESH)
            cp.wait_send()
            cp.wait_recv()

    thru = [pltpu.HBM(v.shape, v.dtype) for v in list(srcs) + list(lands)]
    outs = pl.pallas_call(
        body, name=name,
        in_specs=[HBM_ONLY] * m + [SEM, SEM, pl.BlockSpec(memory_space=pl.ANY)],
        out_specs=[HBM_ONLY] * m,
        out_shape=thru,
        input_output_aliases={i: i for i in range(m)},
        compiler_params=pltpu.CompilerParams(has_side_effects=DATAFLOW),
    )(*srcs, *lands, send_sems, recv_sems, after)
    return outs[:n], outs[n:]


def _place_own(shard, dev):
    s, r, c = shard.shape

    def body(dev_ref, i_ref, o_ref):
        o_ref[...] = i_ref[...]

    return pl.pallas_call(
        body, name="place_own_shard",
        grid_spec=pltpu.PrefetchScalarGridSpec(
            num_scalar_prefetch=1, grid=(s,),
            in_specs=[pl.BlockSpec((None, r, c), lambda i, d: (i, 0, 0))],
            out_specs=pl.BlockSpec((None, r, c), lambda i, d: (i, d[0], 0))),
        out_shape=jax.ShapeDtypeStruct((s, N_DEV * r, c), shard.dtype),
        compiler_params=_cp("arbitrary"),
    )(dev, shard)


def _tile_rows(n, cap=512):
    t = min(n, cap)
    while n % t or t % 8:
        t -= 1
        if t < 8:
            return n
    return t


def _pair_sum(g, got, owner_dev, owner_chip, dtype):
    s, r8, c = g.shape
    r = r8 // N_DEV
    n = owner_dev.shape[0]

    def body(dev_ref, chip_ref, g_ref, got_ref, o_ref):
        o_ref[...] = (g_ref[...] + got_ref[...]).astype(dtype)

    return pl.pallas_call(
        body, name="pair_sum",
        grid_spec=pltpu.PrefetchScalarGridSpec(
            num_scalar_prefetch=2, grid=(n, s),
            in_specs=[pl.BlockSpec((None, r, c), lambda j, i, dev, chip: (i, dev[j], 0)),
                      pl.BlockSpec((None, None, r, c), lambda j, i, dev, chip: (chip[j], i, 0, 0))],
            out_specs=pl.BlockSpec((None, None, r, c), lambda j, i, dev, chip: (j, i, 0, 0))),
        out_shape=jax.ShapeDtypeStruct((n, s, r, c), dtype),
        compiler_params=_cp("parallel", "parallel"),
    )(owner_dev, owner_chip, g, got)


def _sum_chips(g, got, owner_dev, owner_chip, parts):
    _, s, r, c = parts.shape

    def body(dev_ref, chip_ref, g_ref, got_ref, p0, p1, p2, o_ref):
        own = g_ref[...] + got_ref[...]
        o_ref[...] = ((own + p0[...].astype(F32)) + p1[...].astype(F32)) + p2[...].astype(F32)

    def part(q):
        return pl.BlockSpec((None, None, r, c), lambda i, dev, chip: (q, i, 0, 0))

    return pl.pallas_call(
        body, name="chip_sum",
        grid_spec=pltpu.PrefetchScalarGridSpec(
            num_scalar_prefetch=2, grid=(s,),
            in_specs=[pl.BlockSpec((None, r, c), lambda i, dev, chip: (i, dev[0], 0)),
                      pl.BlockSpec((None, None, r, c), lambda i, dev, chip: (chip[0], i, 0, 0)),
                      part(0), part(1), part(2)],
            out_specs=pl.BlockSpec((None, r, c), lambda i, dev, chip: (i, 0, 0))),
        out_shape=jax.ShapeDtypeStruct((s, r, c), F32),
        compiler_params=_cp("parallel"),
    )(owner_dev, owner_chip, g, got, parts, parts, parts)


def _adamw(w, g, m, v):
    shape = w.shape
    c = shape[-1] if w.ndim > 1 else w.shape[0]
    args = [t.reshape(-1, c) for t in (w, g, m, v)]
    n = args[0].shape[0]
    tr = _tile_rows(n)

    def body(w_ref, g_ref, m_ref, v_ref, d_ref, mo_ref, vo_ref):
        g = g_ref[...]
        m = B1 * m_ref[...] + (1.0 - B1) * g
        v = B2 * v_ref[...] + (1.0 - B2) * jnp.square(g)
        m_hat = m / (1.0 - B1 ** STEP)
        v_hat = v / (1.0 - B2 ** STEP)
        d_ref[...] = -LR * (m_hat / (jnp.sqrt(v_hat) + ADAM_EPS) + WD * w_ref[...])
        mo_ref[...] = m
        vo_ref[...] = v

    outs = pl.pallas_call(
        body, name="adamw", grid=(n // tr,),
        in_specs=[_row(tr, c)] * 4, out_specs=[_row(tr, c)] * 3,
        out_shape=[jax.ShapeDtypeStruct((n, c), F32)] * 3,
        compiler_params=_cp("parallel"),
    )(*args)
    return [o.reshape(shape) for o in outs]


def _pack(pieces):
    flat = []
    for p in pieces:
        f = p.reshape(-1)
        flat.append(jnp.pad(f, (0, (-f.shape[0]) % 1024)))
    return jnp.concatenate(flat).reshape(-1, 128)


def _unpack(pack, shapes):
    flat = pack.reshape(-1)
    out, off = [], 0
    for s in shapes:
        size = 1
        for d in s:
            size *= d
        out.append(flat[off:off + size].reshape(s))
        off += size + (-size) % 1024
    return out


def kernel(x, positions, ffn1_norm, ffn1_w_gate, ffn1_w_up, ffn1_w_down, mix_norm, w_in, conv_w, conv_b, conv_ln_g, conv_ln_b, attn_sinks, w_out, ffn2_norm, ffn2_w_gate, ffn2_w_up, ffn2_w_down, final_norm, loss_target, m_ffn1_norm, m_ffn1_w_gate, m_ffn1_w_up, m_ffn1_w_down, m_mix_norm, m_w_in, m_conv_w, m_conv_b, m_conv_ln_g, m_conv_ln_b, m_attn_sinks, m_w_out, m_ffn2_norm, m_ffn2_w_gate, m_ffn2_w_up, m_ffn2_w_down, m_final_norm, v_ffn1_norm, v_ffn1_w_gate, v_ffn1_w_up, v_ffn1_w_down, v_mix_norm, v_w_in, v_conv_w, v_conv_b, v_conv_ln_g, v_conv_ln_b, v_attn_sinks, v_w_out, v_ffn2_norm, v_ffn2_w_gate, v_ffn2_w_up, v_ffn2_w_down, v_final_norm):
    L = ffn1_norm.shape[0]
    T = x.shape[1]
    x0 = x.reshape(T, D)
    target = loss_target.reshape(T, D)
    dev = 4 * lax.axis_index("x") + 2 * lax.axis_index("y") + lax.axis_index("c")

    def t_(w):
        return jnp.swapaxes(w, 1, 2)

    def ffn_shards(gate, up, down, l):
        return jnp.stack([t_(gate)[l], t_(up)[l], down[l]]).astype(BF16)

    shards = [[ffn_shards(ffn1_w_gate, ffn1_w_up, ffn1_w_down, l), ffn_shards(ffn2_w_gate, ffn2_w_up, ffn2_w_down, l),
               t_(w_in)[l:l + 1].astype(BF16), w_out[l:l + 1].astype(BF16)] for l in range(L)]
    cw_cols = CC // N_DEV
    cw_sh = jnp.pad(conv_w.reshape(-1), (0, (-L * CW * cw_cols) % 1024)).reshape(1, -1, 128)
    dev1 = dev.reshape(1).astype(jnp.int32)
    no_token = jnp.zeros((8, 128), F32)

    wffn1_0, cw_all = _all_gather([shards[0][0], cw_sh])
    lands0 = [_place_own(s, dev1) for s in shards[0][1:]]
    *rest0, token = _exchange_start("gather_start_0", _gather_chips_plan, 4, shards[0][1:], lands0, cw_all)
    weights = [None] * L

    cw_rows = cw_sh.shape[1]
    cw_full = cw_all.reshape(N_DEV, cw_rows * 128)[:, :L * CW * cw_cols].reshape(N_DEV, L, CW, cw_cols)
    cw_full = jnp.transpose(cw_full, (1, 2, 0, 3)).reshape(L, CW, CC)

    inv_freq = 1.0 / (10000.0 ** (jnp.arange(0, HD, 2, dtype=F32) / HD))
    ang = positions.reshape(T).astype(F32)[:, None] * inv_freq
    cos, sin = jnp.cos(ang), jnp.sin(ang)
    rc = jnp.concatenate([cos, cos, cos, cos], axis=1)
    rs = jnp.concatenate([-sin, sin, -sin, sin], axis=1)

    n1 = ffn1_norm.reshape(L, 1, D)
    nm = mix_norm.reshape(L, 1, D)
    n2 = ffn2_norm.reshape(L, 1, D)
    cb = conv_b.reshape(L, 1, CC)
    lg = conv_ln_g.reshape(L, 1, CC)
    lb = conv_ln_b.reshape(L, 1, CC)

    saved = []
    xa = x0
    for l in range(L):
        if l == 0:
            wffn1 = wffn1_0
        else:
            wffn1, wffn2, win, wout = _exchange_wait(f"gather_passed_{l}", _gather_pass_plan, *passing, after=xa)[0]
        h1, P1, Q1, A1 = _ffn_up(xa, n1, wffn1, 0, 1, l, token)
        if l == 0:
            arrived = _exchange_wait("gather_wait_0", _gather_chips_plan, *rest0, after=A1)[1]
            *passing, token = _exchange_start("gather_pass_0", _gather_pass_plan, 3, arrived, [], A1)
        xb = _ffn_down(A1, xa, wffn1, 2, token)
        if l == 0:
            wffn2, win, wout = _exchange_wait("gather_passed_0", _gather_pass_plan, *passing, after=xb)[0]
        weights[l] = (wffn1, wffn2, win, wout)
        token = no_token
        if l + 1 < L:
            lands = [_place_own(s, dev1) for s in shards[l + 1]]
            *pending, token = _exchange_start(f"gather_start_{l + 1}", _gather_chips_plan, 4, shards[l + 1], lands, win)
        hm, qkv, u = _mix_in(xb, nm, win, rc, rs, l, token)
        ao = _attn_fwd(qkv, attn_sinks, l)
        y, co = _conv_fwd(u, cw_full, cb, lg, lb, l)
        xc, cat = _mix_out(ao, co, xb, wout)
        h2, P2, Q2, A2 = _ffn_up(xc, n2, wffn2, 0, 1, l, no_token)
        token = no_token
        if l + 1 < L:
            arrived = _exchange_wait(f"gather_wait_{l + 1}", _gather_chips_plan, *pending, after=A2)[1]
            *passing, token = _exchange_start(f"gather_pass_{l + 1}", _gather_pass_plan, 3, arrived, [], A2)
        xd = _ffn_down(A2, xc, wffn2, 2, token)
        saved.append((xa, h1, P1, Q1, A1, xb, hm, qkv, u, y, cat, xc, h2, P2, Q2, A2))
        xa = xd
        token = no_token

    dx, loss_part, g_final = _final(xa, final_norm.reshape(1, D), target)

    cx, cy, cc = _coords()
    chip_of = [2 * cx + cy] + [2 * px + py for px, py in _other_chips(cx, cy)]
    own_chip = jnp.stack(chip_of[:1]).astype(jnp.int32)
    other_chips = jnp.stack(chip_of[1:]).astype(jnp.int32)

    g_n1, g_nm, g_n2 = [None] * L, [None] * L, [None] * L
    g_cb, g_lg, g_lb, g_sink, g_cw = [None] * L, [None] * L, [None] * L, [None] * L, [None] * L
    in_flight, reduced = [], {}

    def pair_begin(tag, group, after):
        lands = [lax.empty((N_CHIP, g.shape[0], g.shape[1] // N_DEV, g.shape[2]), F32) for g in group]
        *handles, token = _exchange_start(f"pair_start_{tag}", _pair_plan, N_CHIP, group, lands, after)
        return handles, token

    def chip_begin(tag, group, got, after):
        sent = [_pair_sum(g, r, 2 * other_chips + cc, other_chips, BF16) for g, r in zip(group, got)]
        *handles, token = _exchange_start(f"chip_start_{tag}", _chip_plan, 3, sent,
                                          [lax.empty(p.shape, p.dtype) for p in sent], after)
        in_flight.append((tag, group, got, handles))
        return token

    def pair_end_chip_begin(tag, handles, after):
        group, got = _exchange_wait(f"pair_wait_{tag}", _pair_plan, *handles, after=after)
        return chip_begin(tag, group, got, after)

    for l in reversed(range(L)):
        xa, h1, P1, Q1, A1, xb, hm, qkv, u, y, cat, xc, h2, P2, Q2, A2 = saved[l]
        wffn1, wffn2, win, wout = weights[l]
        gffn1 = lax.empty((3, FF, D), F32)
        gffn2 = lax.empty((3, FF, D), F32)
        gin = lax.empty((1, DIN, D), F32)
        gout = lax.empty((1, D, D), F32)
        d, dA = _ffn_bwd_act(dx, wffn2, 2, token)
        gffn2 = _wgrad(gffn2, 2, A2, d)
        gffn2 = _wgrad(gffn2, 0, dA, h2, times=Q2)
        gffn2 = _wgrad(gffn2, 1, dA, h2, times=P2)
        handles, token = pair_begin(f"{l}c", [gffn2], d)
        dx, g_n2[l] = _ffn_bwd_in(dA, P2, Q2, xc, dx, n2, wffn2, 0, 1, l, token)
        token = pair_end_chip_begin(f"{l}c", handles, dx)
        d, dao, dco = _mix_out_bwd(dx, wout, token)
        gout = _wgrad(gout, 0, cat, d)
        dy, g_lg[l], g_lb[l], g_cb[l] = _conv_bwd_norm(dco, y, lg, lb, l)
        du, g_cw8 = _conv_bwd_taps(dy, u, cw_full, l)
        g_cw[l] = g_cw8.reshape(CW, 8, CC).sum(axis=1)
        dq, dk, dv, g_sink[l] = _attn_bwd(qkv, dao, attn_sinks, l)
        dp, dx, g_nm[l] = _mix_in_bwd(dq, dk, dv, du, rc, rs, xb, dx, nm, win, l)
        gin = _wgrad(gin, 0, dp, hm)
        handles, token = pair_begin(f"{l}a", [gin, gout], dx)
        d, dA = _ffn_bwd_act(dx, wffn1, 2, token)
        token = pair_end_chip_begin(f"{l}a", handles, d)
        gffn1 = _wgrad(gffn1, 2, A1, d)
        gffn1 = _wgrad(gffn1, 0, dA, h1, times=Q1)
        gffn1 = _wgrad(gffn1, 1, dA, h1, times=P1)
        if l > 0:
            handles, token = pair_begin(f"{l}b", [gffn1], token)
            dx, g_n1[l] = _ffn_bwd_in(dA, P1, Q1, xa, dx, n1, wffn1, 0, 1, l, token)
            token = pair_end_chip_begin(f"{l}b", handles, dx)
        else:
            token = chip_begin(f"{l}b", [gffn1], _pair_exchange([gffn1]), token)
            dx, g_n1[l] = _ffn_bwd_in(dA, P1, Q1, xa, dx, n1, wffn1, 0, 1, l, token)

    grad_x = dx.reshape(1, T, D)
    for tag, group, got, handles in in_flight:
        parts = _exchange_wait(f"chip_wait_{tag}", _chip_plan, *handles, after=dx)[1]
        reduced[tag] = [_sum_chips(g, r, 2 * own_chip + cc, own_chip, p) for g, r, p in zip(group, got, parts)]
    g1 = jnp.stack([reduced[f"{l}b"][0] for l in range(L)])
    g2 = jnp.stack([reduced[f"{l}c"][0] for l in range(L)])
    gin_t = jnp.concatenate([reduced[f"{l}a"][0] for l in range(L)])
    gout_sh = jnp.concatenate([reduced[f"{l}a"][1] for l in range(L)])

    small = [loss_part,
             jnp.concatenate(g_n1), jnp.concatenate(g_nm), jnp.concatenate(g_n2), g_final,
             jnp.concatenate(g_cb), jnp.concatenate(g_lg), jnp.concatenate(g_lb),
             jnp.stack(g_sink)[:, :, 0], jnp.stack(g_cw)]
    small_shapes = [(1, 128), (L, D), (L, D), (L, D), (D,), (L, CC), (L, CC), (L, CC), (L, NH), (L, CW, CC)]
    tot = _unpack(_all_reduce_small(_pack(small)), small_shapes)
    loss = tot[0][0, 0]
    gr_n1, gr_nm, gr_n2, gr_final, gr_cb, gr_lg, gr_lb, gr_sink, gr_cw_full = tot[1:]
    gr_cw = lax.dynamic_slice_in_dim(gr_cw_full, dev * cw_cols, cw_cols, axis=2)

    grads_t = {"ffn1_w_gate": g1[:, 0], "ffn1_w_up": g1[:, 1], "ffn2_w_gate": g2[:, 0], "ffn2_w_up": g2[:, 1],
               "w_in": gin_t}
    grads = {
        "ffn1_norm": gr_n1, "ffn1_w_down": g1[:, 2],
        "mix_norm": gr_nm, "conv_w": gr_cw, "conv_b": gr_cb, "conv_ln_g": gr_lg,
        "conv_ln_b": gr_lb, "attn_sinks": gr_sink, "w_out": gout_sh,
        "ffn2_norm": gr_n2, "ffn2_w_down": g2[:, 2],
        "final_norm": gr_final,
    }
    weights = dict(ffn1_norm=ffn1_norm, ffn1_w_gate=ffn1_w_gate, ffn1_w_up=ffn1_w_up, ffn1_w_down=ffn1_w_down, mix_norm=mix_norm, w_in=w_in, conv_w=conv_w, conv_b=conv_b, conv_ln_g=conv_ln_g, conv_ln_b=conv_ln_b, attn_sinks=attn_sinks, w_out=w_out, ffn2_norm=ffn2_norm, ffn2_w_gate=ffn2_w_gate, ffn2_w_up=ffn2_w_up, ffn2_w_down=ffn2_w_down, final_norm=final_norm)
    moms = dict(ffn1_norm=m_ffn1_norm, ffn1_w_gate=m_ffn1_w_gate, ffn1_w_up=m_ffn1_w_up, ffn1_w_down=m_ffn1_w_down, mix_norm=m_mix_norm, w_in=m_w_in, conv_w=m_conv_w, conv_b=m_conv_b, conv_ln_g=m_conv_ln_g, conv_ln_b=m_conv_ln_b, attn_sinks=m_attn_sinks, w_out=m_w_out, ffn2_norm=m_ffn2_norm, ffn2_w_gate=m_ffn2_w_gate, ffn2_w_up=m_ffn2_w_up, ffn2_w_down=m_ffn2_w_down, final_norm=m_final_norm)
    vels = dict(ffn1_norm=v_ffn1_norm, ffn1_w_gate=v_ffn1_w_gate, ffn1_w_up=v_ffn1_w_up, ffn1_w_down=v_ffn1_w_down, mix_norm=v_mix_norm, w_in=v_w_in, conv_w=v_conv_w, conv_b=v_conv_b, conv_ln_g=v_conv_ln_g, conv_ln_b=v_conv_ln_b, attn_sinks=v_attn_sinks, w_out=v_w_out, ffn2_norm=v_ffn2_norm, ffn2_w_gate=v_ffn2_w_gate, ffn2_w_up=v_ffn2_w_up, ffn2_w_down=v_ffn2_w_down, final_norm=v_final_norm)

    names = list(weights)
    big = ("ffn1_w_gate", "ffn1_w_up", "ffn1_w_down", "w_in", "w_out", "ffn2_w_gate", "ffn2_w_up", "ffn2_w_down")
    delta, new_m, new_v = {}, {}, {}
    for k in big:
        if k in grads_t:
            outs = _adamw(t_(weights[k]), grads_t[k], t_(moms[k]), t_(vels[k]))
            grads[k], delta[k], new_m[k], new_v[k] = [t_(o) for o in [grads_t[k]] + outs]
        else:
            delta[k], new_m[k], new_v[k] = _adamw(weights[k], grads[k], moms[k], vels[k])
    rest = [k for k in names if k not in big]
    rest_shapes = [weights[k].shape for k in rest]
    packed = _adamw(*[_pack([t[k] for k in rest]) for t in (weights, grads, moms, vels)])
    for res, packed_out in zip((delta, new_m, new_v), packed):
        for k, val in zip(rest, _unpack(packed_out, rest_shapes)):
            res[k] = val

    return (loss, grad_x, *[grads[k] for k in names], *[delta[k] for k in names],
            *[new_m[k] for k in names], *[new_v[k] for k in names])
```

```python
import jax
import jax.numpy as jnp
from jax import lax
from jax.experimental import pallas as pl
from jax.experimental.pallas import tpu as pltpu

F32 = jnp.float32
BF16 = jnp.bfloat16
MESH = pl.DeviceIdType.MESH

N_DEV = 8
N_CHIP = 4
D = 1024
FF = 2816
HD = 64
NH = 8
NKV = 2
GROUP = NH // NKV
AW = NH * HD
KVW = NKV * HD
QKW = AW + KVW
QKVW = AW + 2 * KVW
CC = 512
CW = 31
DIN = QKVW + 2 * CC
BLK = 128
HALO = 32
EPS = 1e-5
SCALE = HD ** -0.5
NEG = float(jnp.finfo(jnp.float32).min)

LR, B1, B2, ADAM_EPS, WD, STEP = 0.001, 0.9, 0.999, 1e-08, 0.01, 10

TM = 512
TM_FFN_UP = 256
TM_CONV_BWD = 64
TK = 2048
FC = 256
ATT_BLOCKS = 8
VMEM_LIMIT = 56 * 1024 * 1024


def _cp(*sem):
    return pltpu.CompilerParams(dimension_semantics=sem, vmem_limit_bytes=VMEM_LIMIT)


def _row(tm, c):
    return pl.BlockSpec((tm, c), lambda i: (i, 0))


def _slab(shape, k, single=False):
    zeros = (0,) * len(shape)
    kw = dict(pipeline_mode=pl.Buffered(1)) if single else {}
    return pl.BlockSpec((None, *shape), lambda i: (k, *zeros), **kw)


def _acc(shape):
    return pl.BlockSpec(shape, lambda i: (0,) * len(shape))


def _nt(a, b):
    return lax.dot_general(a, b, (((1,), (1,)), ((), ())), preferred_element_type=F32)


def _tn(a, b):
    return lax.dot_general(a, b, (((0,), (0,)), ((), ())), preferred_element_type=F32)


def _nn(a, b):
    return jnp.dot(a, b, preferred_element_type=F32)


def _sigmoid(x):
    return jax.nn.sigmoid(x)


def _dsilu(z):
    s = _sigmoid(z)
    return s * (1.0 + z * (1.0 - s))


def _rms(x, g):
    r = lax.rsqrt(jnp.mean(x * x, axis=-1, keepdims=True) + EPS)
    xh = x * r
    return xh, r, xh * g


def _rms_bwd(dh, xh, r, g):
    dxh = dh * g
    return r * (dxh - xh * jnp.mean(dxh * xh, axis=-1, keepdims=True))


def _rope(t, c128, s128):
    w = t.shape[1]
    lane = lax.broadcasted_iota(jnp.int32, t.shape, 1)
    rot = jnp.where(lane % HD < HD // 2, pltpu.roll(t, w - HD // 2, 1), pltpu.roll(t, HD // 2, 1))
    return t * jnp.tile(c128, (1, w // 128)) + rot * jnp.tile(s128, (1, w // 128))


def _ffn_up(x, norm, wffn, sg, su, layer, token):
    T = x.shape[0]
    tm = TM_FFN_UP

    def body(x_ref, g_ref, wg_ref, wu_ref, token_ref, h_ref, P_ref, Q_ref, A_ref):
        _, _, hn = _rms(x_ref[...], g_ref[...])
        h = hn.astype(BF16)
        h_ref[...] = h
        for c in range(FF // FC):
            sl = slice(c * FC, (c + 1) * FC)
            g = _nt(h, wg_ref[sl, :])
            u = _nt(h, wu_ref[sl, :])
            s = _sigmoid(g)
            p = g * s
            P_ref[:, sl] = p.astype(BF16)
            Q_ref[:, sl] = (u * (s + p - p * s)).astype(BF16)
            A_ref[:, sl] = (p * u).astype(BF16)

    return pl.pallas_call(
        body, name="ffn_up", grid=(T // tm,),
        in_specs=[_row(tm, D), _slab((1, D), layer), _slab((FF, D), sg, True), _slab((FF, D), su, True), HBM],
        out_specs=[_row(tm, D), _row(tm, FF), _row(tm, FF), _row(tm, FF)],
        out_shape=[jax.ShapeDtypeStruct((T, D), BF16)] + [jax.ShapeDtypeStruct((T, FF), BF16)] * 3,
        compiler_params=_cp("parallel"),
    )(x, norm, wffn, wffn, token)


def _ffn_down(a, x, wffn, sd, token):
    T = x.shape[0]

    def body(a_ref, x_ref, w_ref, token_ref, o_ref):
        o_ref[...] = x_ref[...] + 0.5 * _nn(a_ref[...], w_ref[...])

    return pl.pallas_call(
        body, name="ffn_down", grid=(T // TM,),
        in_specs=[_row(TM, FF), _row(TM, D), _slab((FF, D), sd, True), HBM],
        out_specs=_row(TM, D),
        out_shape=jax.ShapeDtypeStruct((T, D), F32),
        compiler_params=_cp("parallel"),
    )(a, x, wffn, token)


def _mix_in(x, norm, win, rc, rs, layer, token):
    T = x.shape[0]

    def body(x_ref, g_ref, w_ref, c_ref, s_ref, token_ref, h_ref, qkv_ref, u_ref):
        _, _, hn = _rms(x_ref[...], g_ref[...])
        h = hn.astype(BF16)
        h_ref[...] = h
        qk = _nt(h, w_ref[0:QKW, :])
        qkv_ref[:, 0:QKW] = _rope(qk, c_ref[...], s_ref[...]).astype(BF16)
        qkv_ref[:, QKW:QKVW] = _nt(h, w_ref[QKW:QKVW, :]).astype(BF16)
        for c in range(2 * CC // FC):
            u_ref[:, c * FC:(c + 1) * FC] = _nt(h, w_ref[QKVW + c * FC:QKVW + (c + 1) * FC, :]).astype(BF16)

    return pl.pallas_call(
        body, name="mix_in", grid=(T // TM,),
        in_specs=[_row(TM, D), _slab((1, D), layer), _slab((DIN, D), 0, True), _row(TM, 128), _row(TM, 128), HBM],
        out_specs=[_row(TM, D), _row(TM, QKVW), _row(TM, 2 * CC)],
        out_shape=[jax.ShapeDtypeStruct((T, D), BF16), jax.ShapeDtypeStruct((T, QKVW), BF16),
                   jax.ShapeDtypeStruct((T, 2 * CC), BF16)],
        compiler_params=_cp("parallel"),
    )(x, norm, win, rc, rs, token)


def _band_mask(has_prev):
    j = lax.broadcasted_iota(jnp.int32, (2 * BLK, BLK), 0)
    r = lax.broadcasted_iota(jnp.int32, (2 * BLK, BLK), 1) + BLK
    rel = r - j
    return jnp.tile((rel >= 0) & (rel < BLK) & (has_prev | (j >= BLK)), (1, GROUP))


def _band(prev_ref, cur_ref, b, col):
    if b == 0:
        return jnp.concatenate([prev_ref[:, col:col + HD], cur_ref[0:BLK, col:col + HD]], axis=0)
    return cur_ref[(b - 1) * BLK:(b + 1) * BLK, col:col + HD]


def _stack_heads(ref, b, kv):
    cols = [(kv * GROUP + g) * HD for g in range(GROUP)]
    return jnp.concatenate([ref[b * BLK:(b + 1) * BLK, c:c + HD] for c in cols], axis=0)


def _unstack_t(xt):
    x = xt.T
    return jnp.concatenate([x[g * BLK:(g + 1) * BLK, :] for g in range(GROUP)], axis=1)


def _sink_row(sink_ref, layer, kv):
    return jnp.concatenate([jnp.full((1, BLK), sink_ref[layer, kv * GROUP + g], F32) for g in range(GROUP)], axis=1)


def _probs_t(q4, kb, mask, sink):
    s = jnp.where(mask, _nt(kb, q4) * SCALE, NEG)
    m = jnp.maximum(jnp.max(s, axis=0, keepdims=True), sink)
    p = jnp.exp(s - m)
    e = jnp.exp(sink - m)
    inv = 1.0 / (jnp.sum(p, axis=0, keepdims=True) + e)
    return p * inv, e * inv


def _attn_fwd(qkv, sinks, layer):
    T = qkv.shape[0]
    tq = ATT_BLOCKS * BLK

    def body(sink_ref, cur_ref, prev_ref, o_ref):
        first = _band_mask(pl.program_id(0) > 0)
        later = _band_mask(True)
        for b in range(ATT_BLOCKS):
            outs = []
            for kv in range(NKV):
                kb = _band(prev_ref, cur_ref, b, AW + kv * HD)
                vb = _band(prev_ref, cur_ref, b, QKW + kv * HD)
                pt, _ = _probs_t(_stack_heads(cur_ref, b, kv), kb, first if b == 0 else later,
                                 _sink_row(sink_ref, layer, kv))
                outs.append(_unstack_t(_nn(vb.T, pt.astype(BF16))))
            o_ref[b * BLK:(b + 1) * BLK, :] = jnp.concatenate(outs, axis=1).astype(BF16)

    return pl.pallas_call(
        body, name="attn_fwd", grid=(T // tq,),
        in_specs=[pl.BlockSpec(memory_space=pltpu.SMEM), _row(tq, QKVW),
                  pl.BlockSpec((BLK, QKVW), lambda i: (jnp.maximum(i * ATT_BLOCKS - 1, 0), 0))],
        out_specs=_row(tq, AW),
        out_shape=jax.ShapeDtypeStruct((T, AW), BF16),
        compiler_params=_cp("parallel"),
    )(sinks, qkv, qkv)


def _glu(u):
    u = u.astype(F32)
    return u[:, :CC] * _sigmoid(u[:, CC:])


def _fill_ext(ext_ref, first, second):
    n = ext_ref.shape[0] - 8
    ext_ref[0:first.shape[0], :] = first
    ext_ref[first.shape[0]:n, :] = second
    ext_ref[n:, :] = jnp.zeros((8, ext_ref.shape[1]), F32)


def _taps(ext_ref, w_ref, offsets, cols, tm):
    y = None
    for b in range(8):
        z = None
        for k, off in enumerate(offsets):
            if off % 8 == b:
                term = w_ref[k:k + 1, cols] * ext_ref[pl.ds(off - b, tm + 8), cols]
                z = term if z is None else z + term
        if z is not None:
            y = z[b:b + tm, :] if y is None else y + z[b:b + tm, :]
    return y


def _conv_fwd(u, cw, cb, lg, lb, layer):
    T = u.shape[0]
    tm = TM

    def body(u_ref, up_ref, w_ref, b_ref, g_ref, bb_ref, y_ref, o_ref, ext_ref):
        i = pl.program_id(0)
        _fill_ext(ext_ref, jnp.where(i > 0, _glu(up_ref[...]), 0.0), _glu(u_ref[...]))
        for c in range(CC // 128):
            cols = slice(c * 128, (c + 1) * 128)
            y_ref[:, cols] = _taps(ext_ref, w_ref, [HALO - (CW - 1) + k for k in range(CW)], cols, tm) + b_ref[:, cols]
        y = y_ref[...]
        xc = y - jnp.mean(y, axis=-1, keepdims=True)
        z = xc * lax.rsqrt(jnp.mean(xc * xc, axis=-1, keepdims=True) + EPS) * g_ref[...] + bb_ref[...]
        o_ref[...] = (z * _sigmoid(z)).astype(BF16)

    return pl.pallas_call(
        body, name="conv_fwd", grid=(T // tm,),
        in_specs=[_row(tm, 2 * CC),
                  pl.BlockSpec((HALO, 2 * CC), lambda i: (jnp.maximum(i * (tm // HALO) - 1, 0), 0)),
                  _slab((CW, CC), layer), _slab((1, CC), layer), _slab((1, CC), layer), _slab((1, CC), layer)],
        out_specs=[_row(tm, CC), _row(tm, CC)],
        out_shape=[jax.ShapeDtypeStruct((T, CC), F32), jax.ShapeDtypeStruct((T, CC), BF16)],
        scratch_shapes=[pltpu.VMEM((tm + HALO + 8, CC), F32)],
        compiler_params=_cp("parallel"),
    )(u, u, cw, cb, lg, lb)


def _mix_out(ao, co, x, wout):
    T = x.shape[0]

    def body(ao_ref, co_ref, x_ref, w_ref, o_ref, cat_ref):
        cat = jnp.concatenate([ao_ref[...], co_ref[...]], axis=1)
        cat_ref[...] = cat
        o_ref[...] = x_ref[...] + _nn(cat, w_ref[...])

    return pl.pallas_call(
        body, name="mix_out", grid=(T // TM,),
        in_specs=[_row(TM, AW), _row(TM, CC), _row(TM, D), _slab((D, D), 0, True)],
        out_specs=[_row(TM, D), _row(TM, D)],
        out_shape=[jax.ShapeDtypeStruct((T, D), F32), jax.ShapeDtypeStruct((T, D), BF16)],
        compiler_params=_cp("parallel"),
    )(ao, co, x, wout)


def _ffn_down_loss(a, x, wffn, sd, norm, target):
    T = x.shape[0]

    def body(a_ref, x_ref, w_ref, g_ref, t_ref, dx_ref, loss_ref, dg_ref):
        @pl.when(pl.program_id(0) == 0)
        def _():
            loss_ref[...] = jnp.zeros_like(loss_ref)
            dg_ref[...] = jnp.zeros_like(dg_ref)

        g = g_ref[...]
        xh, r, y = _rms(x_ref[...] + 0.5 * _nn(a_ref[...], w_ref[...]), g)
        err = y - t_ref[...]
        loss_ref[...] += jnp.full(loss_ref.shape, (0.5 / D) * jnp.sum(err * err), F32)
        dy = err * (1.0 / D)
        dg_ref[...] += jnp.sum(dy * xh, axis=0, keepdims=True)
        dx_ref[...] = _rms_bwd(dy, xh, r, g)

    return pl.pallas_call(
        body, name="ffn_down_loss", grid=(T // TM,),
        in_specs=[_row(TM, FF), _row(TM, D), _slab((FF, D), sd, True), _acc((1, D)), _row(TM, D)],
        out_specs=[_row(TM, D), _acc((1, 128)), _acc((1, D))],
        out_shape=[jax.ShapeDtypeStruct((T, D), F32), jax.ShapeDtypeStruct((1, 128), F32),
                   jax.ShapeDtypeStruct((1, D), F32)],
        compiler_params=_cp("arbitrary"),
    )(a, x, wffn, norm, target)


def _ffn_bwd_act(dx, wffn, sd, token):
    T = dx.shape[0]

    def body(dx_ref, w_ref, token_ref, d_ref, dA_ref):
        d = (0.5 * dx_ref[...]).astype(BF16)
        d_ref[...] = d
        for c in range(FF // FC):
            sl = slice(c * FC, (c + 1) * FC)
            dA_ref[:, sl] = _nt(d, w_ref[sl, :]).astype(BF16)

    return pl.pallas_call(
        body, name="ffn_bwd_act", grid=(T // TM,),
        in_specs=[_row(TM, D), _slab((FF, D), sd, True), HBM],
        out_specs=[_row(TM, D), _row(TM, FF)],
        out_shape=[jax.ShapeDtypeStruct((T, D), BF16), jax.ShapeDtypeStruct((T, FF), BF16)],
        compiler_params=_cp("parallel"),
    )(dx, wffn, token)


def _ffn_bwd_in(dA, P, Q, x, dx, norm, wffn, sg, su, layer, token):
    T = x.shape[0]
    tm = TM

    def body(dA_ref, P_ref, Q_ref, x_ref, dx_ref, g_ref, wg_ref, wu_ref, token_ref, o_ref, dg_ref):
        @pl.when(pl.program_id(0) == 0)
        def _():
            dg_ref[...] = jnp.zeros_like(dg_ref)

        dA = dA_ref[...]
        dh = _nn(dA * Q_ref[...], wg_ref[...]) + _nn(dA * P_ref[...], wu_ref[...])
        g = g_ref[...]
        xh, r, _ = _rms(x_ref[...], g)
        dg_ref[...] += jnp.sum(dh * xh, axis=0, keepdims=True)
        o_ref[...] = dx_ref[...] + _rms_bwd(dh, xh, r, g)

    return pl.pallas_call(
        body, name="ffn_bwd_in", grid=(T // tm,),
        in_specs=[_row(tm, FF), _row(tm, FF), _row(tm, FF), _row(tm, D), _row(tm, D), _slab((1, D), layer),
                  _slab((FF, D), sg, True), _slab((FF, D), su, True), HBM],
        out_specs=[_row(tm, D), _acc((1, D))],
        out_shape=[jax.ShapeDtypeStruct((T, D), F32), jax.ShapeDtypeStruct((1, D), F32)],
        compiler_params=_cp("arbitrary"),
    )(dA, P, Q, x, dx, norm, wffn, wffn, token)


def _mix_out_bwd(dx, wout, token):
    T = dx.shape[0]

    def body(dx_ref, w_ref, token_ref, d_ref, dao_ref, dco_ref):
        d = dx_ref[...].astype(BF16)
        d_ref[...] = d
        dcat = _nt(d, w_ref[...])
        dao_ref[...] = dcat[:, :AW].astype(BF16)
        dco_ref[...] = dcat[:, AW:].astype(BF16)

    return pl.pallas_call(
        body, name="mix_out_bwd", grid=(T // TM,),
        in_specs=[_row(TM, D), _slab((D, D), 0, True), HBM],
        out_specs=[_row(TM, D), _row(TM, AW), _row(TM, CC)],
        out_shape=[jax.ShapeDtypeStruct((T, D), BF16), jax.ShapeDtypeStruct((T, AW), BF16),
                   jax.ShapeDtypeStruct((T, CC), BF16)],
        compiler_params=_cp("parallel"),
    )(dx, wout, token)


def _conv_bwd_norm(dco, y, lg, lb, layer):
    T = y.shape[0]

    def body(dco_ref, y_ref, g_ref, bb_ref, dy_ref, dlg_ref, dlb_ref, dcb_ref):
        @pl.when(pl.program_id(0) == 0)
        def _():
            dlg_ref[...] = jnp.zeros_like(dlg_ref)
            dlb_ref[...] = jnp.zeros_like(dlb_ref)
            dcb_ref[...] = jnp.zeros_like(dcb_ref)

        y = y_ref[...]
        g = g_ref[...]
        xc = y - jnp.mean(y, axis=-1, keepdims=True)
        rs = lax.rsqrt(jnp.mean(xc * xc, axis=-1, keepdims=True) + EPS)
        xn = xc * rs
        z = xn * g + bb_ref[...]
        dz = dco_ref[...].astype(F32) * _dsilu(z)
        dlg_ref[...] += jnp.sum(dz * xn, axis=0, keepdims=True)
        dlb_ref[...] += jnp.sum(dz, axis=0, keepdims=True)
        dxn = dz * g
        dy = rs * (dxn - jnp.mean(dxn, axis=-1, keepdims=True) - xn * jnp.mean(dxn * xn, axis=-1, keepdims=True))
        dcb_ref[...] += jnp.sum(dy, axis=0, keepdims=True)
        dy_ref[...] = dy

    return pl.pallas_call(
        body, name="conv_bwd_norm", grid=(T // TM,),
        in_specs=[_row(TM, CC), _row(TM, CC), _slab((1, CC), layer), _slab((1, CC), layer)],
        out_specs=[_row(TM, CC), _acc((1, CC)), _acc((1, CC)), _acc((1, CC))],
        out_shape=[jax.ShapeDtypeStruct((T, CC), F32)] + [jax.ShapeDtypeStruct((1, CC), F32)] * 3,
        compiler_params=_cp("arbitrary"),
    )(dco, y, lg, lb)


def _conv_bwd_taps(dy, u, cw, layer):
    T = u.shape[0]
    tm = TM_CONV_BWD
    n_halo = T // HALO

    def body(dy_ref, dyn_ref, u_ref, up_ref, w_ref, du_ref, dw_ref, hext_ref, dext_ref, dz_ref, dsh_ref, dh_ref):
        i = pl.program_id(0)

        @pl.when(i == 0)
        def _():
            dw_ref[...] = jnp.zeros_like(dw_ref)

        _fill_ext(hext_ref, jnp.where(i > 0, _glu(up_ref[...]), 0.0), _glu(u_ref[...]))
        _fill_ext(dext_ref, dy_ref[...], jnp.where(i < pl.num_programs(0) - 1, dyn_ref[...], 0.0))
        _fill_ext(dz_ref, jnp.zeros((8, CC), F32), dy_ref[...])
        for b in range(8):
            dsh_ref[b] = dz_ref[pl.ds(8 - b, tm + 8), :]
        h_offsets = [HALO - (CW - 1) + k for k in range(CW)]
        for c in range(CC // 128):
            cols = slice(c * 128, (c + 1) * 128)
            dh_ref[:, cols] = _taps(dext_ref, w_ref, [CW - 1 - k for k in range(CW)], cols, tm)
            for k, off in enumerate(h_offsets):
                b = off % 8
                prod = dsh_ref[b, :, cols] * hext_ref[pl.ds(off - b, tm + 8), cols]
                dw_ref[8 * k:8 * k + 8, cols] += jnp.sum(prod.reshape((tm + 8) // 8, 8, 128), axis=0)
        dh = dh_ref[...]
        uu = u_ref[...].astype(F32)
        a = uu[:, :CC]
        sg = _sigmoid(uu[:, CC:])
        du_ref[:, :CC] = (dh * sg).astype(BF16)
        du_ref[:, CC:] = (dh * a * sg * (1.0 - sg)).astype(BF16)

    return pl.pallas_call(
        body, name="conv_bwd_taps", grid=(T // tm,),
        in_specs=[_row(tm, CC),
                  pl.BlockSpec((HALO, CC), lambda i: (jnp.minimum((i + 1) * (tm // HALO), n_halo - 1), 0)),
                  _row(tm, 2 * CC),
                  pl.BlockSpec((HALO, 2 * CC), lambda i: (jnp.maximum(i * (tm // HALO) - 1, 0), 0)),
                  _slab((CW, CC), layer)],
        out_specs=[_row(tm, 2 * CC), _acc((CW * 8, CC))],
        out_shape=[jax.ShapeDtypeStruct((T, 2 * CC), BF16), jax.ShapeDtypeStruct((CW * 8, CC), F32)],
        scratch_shapes=[pltpu.VMEM((tm + HALO + 8, CC), F32), pltpu.VMEM((tm + HALO + 8, CC), F32),
                        pltpu.VMEM((tm + 16, CC), F32), pltpu.VMEM((8, tm + 8, CC), F32), pltpu.VMEM((tm, CC), F32)],
        compiler_params=_cp("arbitrary"),
    )(dy, dy, u, u, cw)


def _attn_bwd(qkv, dao, sinks, layer):
    T = qkv.shape[0]
    tq = ATT_BLOCKS * BLK

    def body(sink_ref, cur_ref, prev_ref, do_ref, dq_ref, dk_ref, dv_ref, ds_ref):
        i = pl.program_id(0)

        @pl.when(i == 0)
        def _():
            dk_ref[...] = jnp.zeros_like(dk_ref)
            dv_ref[...] = jnp.zeros_like(dv_ref)
            ds_ref[...] = jnp.zeros_like(ds_ref)

        first = _band_mask(i > 0)
        later = _band_mask(True)
        base = pl.multiple_of(i * tq, tq)
        before = pl.multiple_of(jnp.maximum(i * ATT_BLOCKS - 1, 0) * BLK, BLK)
        for b in range(ATT_BLOCKS):
            dqs, dks, dvs = [], [], []
            for kv in range(NKV):
                kb = _band(prev_ref, cur_ref, b, AW + kv * HD)
                vb = _band(prev_ref, cur_ref, b, QKW + kv * HD)
                q4 = _stack_heads(cur_ref, b, kv)
                do4 = _stack_heads(do_ref, b, kv)
                pt, psink = _probs_t(q4, kb, first if b == 0 else later, _sink_row(sink_ref, layer, kv))
                dpt = _nt(vb, do4)
                dd = jnp.sum(pt * dpt, axis=0, keepdims=True)
                dst = (pt * (dpt - dd) * SCALE).astype(BF16)
                sd = psink * dd
                for g in range(GROUP):
                    hh = kv * GROUP + g
                    ds_ref[hh:hh + 1, :] += jnp.full((1, 128), -jnp.sum(sd[:, g * BLK:(g + 1) * BLK]), F32)
                dqs.append(_unstack_t(_nn(kb.T, dst)))
                dks.append(_nn(dst, q4))
                dvs.append(_nn(pt.astype(BF16), do4))
            dq_ref[b * BLK:(b + 1) * BLK, :] = jnp.concatenate(dqs, axis=1).astype(BF16)
            dkband = jnp.concatenate(dks, axis=1)
            dvband = jnp.concatenate(dvs, axis=1)
            if b == 0:
                dk_ref[pl.ds(before, BLK), :] += dkband[:BLK]
                dv_ref[pl.ds(before, BLK), :] += dvband[:BLK]
                dk_ref[pl.ds(base, BLK), :] += dkband[BLK:]
                dv_ref[pl.ds(base, BLK), :] += dvband[BLK:]
            else:
                r0 = pl.multiple_of(base + (b - 1) * BLK, BLK)
                dk_ref[pl.ds(r0, 2 * BLK), :] += dkband
                dv_ref[pl.ds(r0, 2 * BLK), :] += dvband

    return pl.pallas_call(
        body, name="attn_bwd", grid=(T // tq,),
        in_specs=[pl.BlockSpec(memory_space=pltpu.SMEM), _row(tq, QKVW),
                  pl.BlockSpec((BLK, QKVW), lambda i: (jnp.maximum(i * ATT_BLOCKS - 1, 0), 0)), _row(tq, AW)],
        out_specs=[_row(tq, AW), _acc((T, KVW)), _acc((T, KVW)), _acc((NH, 128))],
        out_shape=[jax.ShapeDtypeStruct((T, AW), BF16), jax.ShapeDtypeStruct((T, KVW), F32),
                   jax.ShapeDtypeStruct((T, KVW), F32), jax.ShapeDtypeStruct((NH, 128), F32)],
        compiler_params=_cp("arbitrary"),
    )(sinks, qkv, qkv, dao)


def _mix_in_bwd(dq, dk, dv, du, rc, rs, x, dx, norm, win, layer):
    T = x.shape[0]

    def body(dq_ref, dk_ref, dv_ref, du_ref, c_ref, s_ref, x_ref, dx_ref, g_ref, w_ref, dp_ref, o_ref, dg_ref):
        @pl.when(pl.program_id(0) == 0)
        def _():
            dg_ref[...] = jnp.zeros_like(dg_ref)

        dqk = jnp.concatenate([dq_ref[...].astype(F32), dk_ref[...]], axis=1)
        dqk = _rope(dqk, c_ref[...], -s_ref[...])
        dp = jnp.concatenate([dqk.astype(BF16), dv_ref[...].astype(BF16), du_ref[...]], axis=1)
        dp_ref[...] = dp
        dh = _nn(dp, w_ref[...])
        g = g_ref[...]
        xh, r, _ = _rms(x_ref[...], g)
        dg_ref[...] += jnp.sum(dh * xh, axis=0, keepdims=True)
        o_ref[...] = dx_ref[...] + _rms_bwd(dh, xh, r, g)

    return pl.pallas_call(
        body, name="mix_in_bwd", grid=(T // TM,),
        in_specs=[_row(TM, AW), _row(TM, KVW), _row(TM, KVW), _row(TM, 2 * CC), _row(TM, 128), _row(TM, 128),
                  _row(TM, D), _row(TM, D), _slab((1, D), layer), _slab((DIN, D), 0, True)],
        out_specs=[_row(TM, DIN), _row(TM, D), _acc((1, D))],
        out_shape=[jax.ShapeDtypeStruct((T, DIN), BF16), jax.ShapeDtypeStruct((T, D), F32),
                   jax.ShapeDtypeStruct((1, D), F32)],
        compiler_params=_cp("arbitrary"),
    )(dq, dk, dv, du, rc, rs, x, dx, norm, win)


def _wgrad(buf, slab, a, b, times=None):
    T, M = a.shape
    N = b.shape[1]
    tmm = M // 2 if M > 1024 else M
    lhs = [a] if times is None else [a, times]

    def body(buf_ref, *refs):
        *lhs_refs, b_ref, o_ref = refs

        @pl.when(pl.program_id(1) == 0)
        def _():
            o_ref[...] = jnp.zeros_like(o_ref)

        a_tile = lhs_refs[0][...]
        if times is not None:
            a_tile = a_tile * lhs_refs[1][...]
        o_ref[...] += _tn(a_tile, b_ref[...])

    return pl.pallas_call(
        body, name="wgrad", grid=(M // tmm, T // TK),
        in_specs=[pl.BlockSpec(memory_space=pl.ANY)] + [pl.BlockSpec((TK, tmm), lambda i, k: (k, i))] * len(lhs)
        + [pl.BlockSpec((TK, N), lambda i, k: (k, 0))],
        out_specs=pl.BlockSpec((None, tmm, N), lambda i, k: (slab, i, 0)),
        out_shape=jax.ShapeDtypeStruct(buf.shape, F32),
        input_output_aliases={0: 0},
        compiler_params=_cp("parallel", "arbitrary"),
    )(buf, *lhs, b)


HBM = pl.BlockSpec(memory_space=pl.ANY)


def _coords():
    return lax.axis_index("x"), lax.axis_index("y"), lax.axis_index("c")


def _other_chips(x, y):
    return [(1 - x, y), (x, 1 - y), (1 - x, 1 - y)]


def _all_gather(shards):
    n = len(shards)

    def body(*refs):
        ins, outs = refs[:n], refs[n:2 * n]
        send_sems, recv_sems, local_sems = refs[2 * n:]
        x, y, c = _coords()
        me, sibling = (x, y, c), (x, y, 1 - c)
        chips = _other_chips(x, y)

        def rows(a, dev):
            r = ins[a].shape[1]
            return outs[a].at[:, pl.ds(pl.multiple_of((4 * dev[0] + 2 * dev[1] + dev[2]) * r, r), r), :]

        def copy(a, k, block, to, src=None):
            return pltpu.make_async_remote_copy(
                src_ref=rows(a, block) if src is None else src, dst_ref=rows(a, block),
                send_sem=send_sems.at[a * 7 + k], recv_sem=recv_sems.at[a * 7 + k],
                device_id=to, device_id_type=MESH)

        mine = [pltpu.make_async_copy(ins[a], rows(a, me), local_sems.at[a]) for a in range(n)]
        for cp in mine:
            cp.start()
        first = []
        for a in range(n):
            first.append(copy(a, 0, me, sibling, src=ins[a]))
            first += [copy(a, 1 + j, me, (*chip, c), src=ins[a]) for j, chip in enumerate(chips)]
        for cp in first:
            cp.start()
        passed = []
        for j, chip in enumerate(chips):
            for a in range(n):
                copy(a, 1 + j, (*chip, c), me).wait_recv()
                fwd = copy(a, 4 + j, (*chip, c), sibling)
                fwd.start()
                passed.append(fwd)
        for a in range(n):
            copy(a, 0, sibling, me).wait_recv()
            for j, chip in enumerate(chips):
                copy(a, 4 + j, (*chip, 1 - c), me).wait_recv()
        for cp in first + passed:
            cp.wait_send()
        for cp in mine:
            cp.wait()

    return pl.pallas_call(
        body, name="all_gather_weights",
        in_specs=[HBM] * n, out_specs=[HBM] * n,
        out_shape=[jax.ShapeDtypeStruct((s.shape[0], N_DEV * s.shape[1], s.shape[2]), s.dtype) for s in shards],
        scratch_shapes=[pltpu.SemaphoreType.DMA((7 * n,)), pltpu.SemaphoreType.DMA((7 * n,)),
                        pltpu.SemaphoreType.DMA((n,))],
    )(*shards)


def _pair_exchange(grads):
    n = len(grads)

    def body(*refs):
        ins, got = refs[:n], refs[n:2 * n]
        send_sems, recv_sems = refs[2 * n:]
        x, y, c = _coords()
        sibling = (x, y, 1 - c)

        def remote(a, q):
            r = ins[a].shape[1] // N_DEV
            src = ins[a].at[:, pl.ds(pl.multiple_of((2 * q + 1 - c) * r, r), r), :]
            return pltpu.make_async_remote_copy(
                src_ref=src, dst_ref=got[a].at[q],
                send_sem=send_sems.at[a * N_CHIP + q], recv_sem=recv_sems.at[a * N_CHIP + q],
                device_id=sibling, device_id_type=MESH)

        sends = [remote(a, q) for a in range(n) for q in range(N_CHIP)]
        for cp in sends:
            cp.start()
        for cp in sends:
            cp.wait_recv()
        for cp in sends:
            cp.wait_send()

    return pl.pallas_call(
        body, name="grad_pair_exchange",
        in_specs=[HBM] * n, out_specs=[HBM] * n,
        out_shape=[jax.ShapeDtypeStruct((N_CHIP, g.shape[0], g.shape[1] // N_DEV, g.shape[2]), g.dtype) for g in grads],
        scratch_shapes=[pltpu.SemaphoreType.DMA((N_CHIP * n,)), pltpu.SemaphoreType.DMA((N_CHIP * n,))],
    )(*grads)


def _all_reduce_small(pack):
    R = pack.shape[0]

    def body(p_ref, tot_ref, all_ref, send_sems, recv_sems):
        x, y, c = _coords()
        me = 4 * x + 2 * y + c
        all_ref[me] = p_ref[...]
        peers = []
        for k in range(1, N_DEV):
            bx, by, bc = (k >> 2) & 1, (k >> 1) & 1, k & 1
            peers.append((x ^ bx, y ^ by, c ^ bc))

        def copy(k, slot, to):
            return pltpu.make_async_remote_copy(
                src_ref=p_ref, dst_ref=all_ref.at[slot], send_sem=send_sems.at[k], recv_sem=recv_sems.at[k],
                device_id=to, device_id_type=MESH)

        sends = [copy(k, me, peer) for k, peer in enumerate(peers)]
        for cp in sends:
            cp.start()
        for k, peer in enumerate(peers):
            copy(k, 4 * peer[0] + 2 * peer[1] + peer[2], peer).wait_recv()
        for cp in sends:
            cp.wait_send()
        tot = all_ref[0]
        for d in range(1, N_DEV):
            tot = tot + all_ref[d]
        tot_ref[...] = tot

    vmem = pl.BlockSpec(memory_space=pltpu.VMEM)
    return pl.pallas_call(
        body, name="all_reduce_small",
        in_specs=[vmem], out_specs=vmem,
        out_shape=jax.ShapeDtypeStruct((R, 128), F32),
        scratch_shapes=[pltpu.VMEM((N_DEV, R, 128), F32), pltpu.SemaphoreType.DMA((N_DEV - 1,)),
                        pltpu.SemaphoreType.DMA((N_DEV - 1,))],
    )(pack)


HBM_ONLY = pl.BlockSpec(memory_space=pltpu.HBM)
SEM = pl.BlockSpec(memory_space=pltpu.SEMAPHORE)
DATAFLOW = pltpu.SideEffectType.DATAFLOW_SIDE_EFFECTING


def _shard_rows(buf, rows, dev):
    return buf.at[:, pl.ds(pl.multiple_of((4 * dev[0] + 2 * dev[1] + dev[2]) * rows, rows), rows), :]


def _gather_chips_plan(ins, lands):
    x, y, c = _coords()
    targets = [(x, y, 1 - c)] + [(*chip, c) for chip in _other_chips(x, y)]
    return [(ins[a], _shard_rows(lands[a], ins[a].shape[1], (x, y, c)), t, _shard_rows(lands[a], ins[a].shape[1], t))
            for a in range(len(ins)) for t in targets]


def _gather_pass_plan(bufs, _):
    x, y, c = _coords()
    plan = []
    for buf in bufs:
        r = buf.shape[1] // N_DEV
        for chip in _other_chips(x, y):
            mine, theirs = _shard_rows(buf, r, (*chip, c)), _shard_rows(buf, r, (*chip, 1 - c))
            plan.append((mine, mine, (x, y, 1 - c), theirs))
    return plan


def _pair_plan(ins, lands):
    x, y, c = _coords()
    plan = []
    for a in range(len(ins)):
        r = ins[a].shape[1] // N_DEV
        for q in range(N_CHIP):
            src = ins[a].at[:, pl.ds(pl.multiple_of((2 * q + 1 - c) * r, r), r), :]
            plan.append((src, lands[a].at[q], (x, y, 1 - c), lands[a].at[q]))
    return plan


def _chip_plan(ins, lands):
    x, y, c = _coords()
    chips = _other_chips(x, y)
    return [(ins[a].at[j], lands[a].at[j], (*chips[j], c), lands[a].at[j]) for a in range(len(ins)) for j in range(3)]


def _exchange_start(name, plan, copies_per_array, srcs, lands, after):
    n, m = len(srcs), len(srcs) + len(lands)
    count = copies_per_array * n

    def body(*refs):
        ins, land = refs[:n], refs[n:m]
        send_sems, recv_sems = refs[m + 1], refs[m + 2]
        token = refs[-1]
        for k, (src, dst, peer, _) in enumerate(plan(ins, land)):
            pltpu.make_async_remote_copy(src_ref=src, dst_ref=dst, send_sem=send_sems.at[k], recv_sem=recv_sems.at[k],
                                         device_id=peer, device_id_type=MESH).start()
        token[...] = jnp.zeros_like(token)

    thru = [pltpu.HBM(v.shape, v.dtype) for v in list(srcs) + list(lands)]
    outs = pl.pallas_call(
        body, name=name,
        in_specs=[HBM_ONLY] * m + [HBM],
        out_specs=[SEM, SEM] + [HBM_ONLY] * m + [pl.BlockSpec(memory_space=pltpu.VMEM)],
        out_shape=[pltpu.SemaphoreType.DMA((count,)), pltpu.SemaphoreType.DMA((count,))] + thru
        + [jax.ShapeDtypeStruct((8, 128), F32)],
        input_output_aliases={i: 2 + i for i in range(m)},
        compiler_params=pltpu.CompilerParams(has_side_effects=DATAFLOW),
    )(*[pltpu.with_memory_space_constraint(v, pltpu.HBM) for v in list(srcs) + list(lands)], after)
    return outs[0], outs[1], outs[2:2 + n], outs[2 + n:2 + m], outs[-1]


def _exchange_wait(name, plan, send_sems, recv_sems, srcs, lands, after):
    n, m = len(srcs), len(srcs) + len(lands)

    def body(*refs):
        ins, land = refs[:n], refs[n:m]
        send, recv = refs[m], refs[m + 1]
        for k, (src, _, peer, here) in enumerate(plan(ins, land)):
            cp = pltpu.make_async_remote_copy(src_ref=src, dst_ref=here, send_sem=send.at[k], recv_sem=recv.at[k],
                                              device_id=peer, device_id_type=MESH)
            cp.wait_send()
            cp.wait_recv()

    thru = [pltpu.HBM(v.shape, v.dtype) for v in list(srcs) + list(lands)]
    outs = pl.pallas_call(
        body, name=name,
        in_specs=[HBM_ONLY] * m + [SEM, SEM, pl.BlockSpec(memory_space=pl.ANY)],
        out_specs=[HBM_ONLY] * m,
        out_shape=thru,
        input_output_aliases={i: i for i in range(m)},
        compiler_params=pltpu.CompilerParams(has_side_effects=DATAFLOW),
    )(*srcs, *lands, send_sems, recv_sems, after)
    return outs[:n], outs[n:]


def _place_own(shard, dev):
    s, r, c = shard.shape

    def body(dev_ref, i_ref, o_ref):
        o_ref[...] = i_ref[...]

    return pl.pallas_call(
        body, name="place_own_shard",
        grid_spec=pltpu.PrefetchScalarGridSpec(
            num_scalar_prefetch=1, grid=(s,),
            in_specs=[pl.BlockSpec((None, r, c), lambda i, d: (i, 0, 0))],
            out_specs=pl.BlockSpec((None, r, c), lambda i, d: (i, d[0], 0))),
        out_shape=jax.ShapeDtypeStruct((s, N_DEV * r, c), shard.dtype),
        compiler_params=_cp("arbitrary"),
    )(dev, shard)


def _tile_rows(n, cap=512):
    t = min(n, cap)
    while n % t or t % 8:
        t -= 1
        if t < 8:
            return n
    return t


def _pair_sum(g, got, owner_dev, owner_chip, dtype):
    s, r8, c = g.shape
    r = r8 // N_DEV
    n = owner_dev.shape[0]

    def body(dev_ref, chip_ref, g_ref, got_ref, o_ref):
        o_ref[...] = (g_ref[...] + got_ref[...]).astype(dtype)

    return pl.pallas_call(
        body, name="pair_sum",
        grid_spec=pltpu.PrefetchScalarGridSpec(
            num_scalar_prefetch=2, grid=(n, s),
            in_specs=[pl.BlockSpec((None, r, c), lambda j, i, dev, chip: (i, dev[j], 0)),
                      pl.BlockSpec((None, None, r, c), lambda j, i, dev, chip: (chip[j], i, 0, 0))],
            out_specs=pl.BlockSpec((None, None, r, c), lambda j, i, dev, chip: (j, i, 0, 0))),
        out_shape=jax.ShapeDtypeStruct((n, s, r, c), dtype),
        compiler_params=_cp("parallel", "parallel"),
    )(owner_dev, owner_chip, g, got)


def _sum_chips(g, got, owner_dev, owner_chip, parts):
    _, s, r, c = parts.shape

    def body(dev_ref, chip_ref, g_ref, got_ref, p0, p1, p2, o_ref):
        own = g_ref[...] + got_ref[...]
        o_ref[...] = ((own + p0[...].astype(F32)) + p1[...].astype(F32)) + p2[...].astype(F32)

    def part(q):
        return pl.BlockSpec((None, None, r, c), lambda i, dev, chip: (q, i, 0, 0))

    return pl.pallas_call(
        body, name="chip_sum",
        grid_spec=pltpu.PrefetchScalarGridSpec(
            num_scalar_prefetch=2, grid=(s,),
            in_specs=[pl.BlockSpec((None, r, c), lambda i, dev, chip: (i, dev[0], 0)),
                      pl.BlockSpec((None, None, r, c), lambda i, dev, chip: (chip[0], i, 0, 0)),
                      part(0), part(1), part(2)],
            out_specs=pl.BlockSpec((None, r, c), lambda i, dev, chip: (i, 0, 0))),
        out_shape=jax.ShapeDtypeStruct((s, r, c), F32),
        compiler_params=_cp("parallel"),
    )(owner_dev, owner_chip, g, got, parts, parts, parts)


def _adamw(w, g, m, v):
    shape = w.shape
    c = shape[-1] if w.ndim > 1 else w.shape[0]
    args = [t.reshape(-1, c) for t in (w, g, m, v)]
    n = args[0].shape[0]
    tr = _tile_rows(n)

    def body(w_ref, g_ref, m_ref, v_ref, d_ref, mo_ref, vo_ref):
        g = g_ref[...]
        m = B1 * m_ref[...] + (1.0 - B1) * g
        v = B2 * v_ref[...] + (1.0 - B2) * jnp.square(g)
        m_hat = m / (1.0 - B1 ** STEP)
        v_hat = v / (1.0 - B2 ** STEP)
        d_ref[...] = -LR * (m_hat / (jnp.sqrt(v_hat) + ADAM_EPS) + WD * w_ref[...])
        mo_ref[...] = m
        vo_ref[...] = v

    outs = pl.pallas_call(
        body, name="adamw", grid=(n // tr,),
        in_specs=[_row(tr, c)] * 4, out_specs=[_row(tr, c)] * 3,
        out_shape=[jax.ShapeDtypeStruct((n, c), F32)] * 3,
        compiler_params=_cp("parallel"),
    )(*args)
    return [o.reshape(shape) for o in outs]


def _pack(pieces):
    flat = []
    for p in pieces:
        f = p.reshape(-1)
        flat.append(jnp.pad(f, (0, (-f.shape[0]) % 1024)))
    return jnp.concatenate(flat).reshape(-1, 128)


def _unpack(pack, shapes):
    flat = pack.reshape(-1)
    out, off = [], 0
    for s in shapes:
        size = 1
        for d in s:
            size *= d
        out.append(flat[off:off + size].reshape(s))
        off += size + (-size) % 1024
    return out


def kernel(x, positions, ffn1_norm, ffn1_w_gate, ffn1_w_up, ffn1_w_down, mix_norm, w_in, conv_w, conv_b, conv_ln_g, conv_ln_b, attn_sinks, w_out, ffn2_norm, ffn2_w_gate, ffn2_w_up, ffn2_w_down, final_norm, loss_target, m_ffn1_norm, m_ffn1_w_gate, m_ffn1_w_up, m_ffn1_w_down, m_mix_norm, m_w_in, m_conv_w, m_conv_b, m_conv_ln_g, m_conv_ln_b, m_attn_sinks, m_w_out, m_ffn2_norm, m_ffn2_w_gate, m_ffn2_w_up, m_ffn2_w_down, m_final_norm, v_ffn1_norm, v_ffn1_w_gate, v_ffn1_w_up, v_ffn1_w_down, v_mix_norm, v_w_in, v_conv_w, v_conv_b, v_conv_ln_g, v_conv_ln_b, v_attn_sinks, v_w_out, v_ffn2_norm, v_ffn2_w_gate, v_ffn2_w_up, v_ffn2_w_down, v_final_norm):
    L = ffn1_norm.shape[0]
    T = x.shape[1]
    x0 = x.reshape(T, D)
    target = loss_target.reshape(T, D)
    dev = 4 * lax.axis_index("x") + 2 * lax.axis_index("y") + lax.axis_index("c")

    def t_(w):
        return jnp.swapaxes(w, 1, 2)

    def ffn_shards(gate, up, down, l):
        return jnp.stack([t_(gate)[l], t_(up)[l], down[l]]).astype(BF16)

    shards = [[ffn_shards(ffn1_w_gate, ffn1_w_up, ffn1_w_down, l), ffn_shards(ffn2_w_gate, ffn2_w_up, ffn2_w_down, l),
               t_(w_in)[l:l + 1].astype(BF16), w_out[l:l + 1].astype(BF16)] for l in range(L)]
    cw_cols = CC // N_DEV
    cw_sh = jnp.pad(conv_w.reshape(-1), (0, (-L * CW * cw_cols) % 1024)).reshape(1, -1, 128)
    dev1 = dev.reshape(1).astype(jnp.int32)
    no_token = jnp.zeros((8, 128), F32)

    wffn1_0, cw_all = _all_gather([shards[0][0], cw_sh])
    lands0 = [_place_own(s, dev1) for s in shards[0][1:]]
    *rest0, token = _exchange_start("gather_start_0", _gather_chips_plan, 4, shards[0][1:], lands0, cw_all)
    weights = [None] * L

    cw_rows = cw_sh.shape[1]
    cw_full = cw_all.reshape(N_DEV, cw_rows * 128)[:, :L * CW * cw_cols].reshape(N_DEV, L, CW, cw_cols)
    cw_full = jnp.transpose(cw_full, (1, 2, 0, 3)).reshape(L, CW, CC)

    inv_freq = 1.0 / (10000.0 ** (jnp.arange(0, HD, 2, dtype=F32) / HD))
    ang = positions.reshape(T).astype(F32)[:, None] * inv_freq
    cos, sin = jnp.cos(ang), jnp.sin(ang)
    rc = jnp.concatenate([cos, cos, cos, cos], axis=1)
    rs = jnp.concatenate([-sin, sin, -sin, sin], axis=1)

    n1 = ffn1_norm.reshape(L, 1, D)
    nm = mix_norm.reshape(L, 1, D)
    n2 = ffn2_norm.reshape(L, 1, D)
    cb = conv_b.reshape(L, 1, CC)
    lg = conv_ln_g.reshape(L, 1, CC)
    lb = conv_ln_b.reshape(L, 1, CC)

    saved = []
    xa = x0
    for l in range(L):
        if l == 0:
            wffn1 = wffn1_0
        else:
            wffn1, wffn2, win, wout = _exchange_wait(f"gather_passed_{l}", _gather_pass_plan, *passing, after=xa)[0]
        h1, P1, Q1, A1 = _ffn_up(xa, n1, wffn1, 0, 1, l, token)
        if l == 0:
            arrived = _exchange_wait("gather_wait_0", _gather_chips_plan, *rest0, after=A1)[1]
            *passing, token = _exchange_start("gather_pass_0", _gather_pass_plan, 3, arrived, [], A1)
        xb = _ffn_down(A1, xa, wffn1, 2, token)
        if l == 0:
            wffn2, win, wout = _exchange_wait("gather_passed_0", _gather_pass_plan, *passing, after=xb)[0]
        weights[l] = (wffn1, wffn2, win, wout)
        token = no_token
        if l + 1 < L:
            lands = [_place_own(s, dev1) for s in shards[l + 1]]
            *pending, token = _exchange_start(f"gather_start_{l + 1}", _gather_chips_plan, 4, shards[l + 1], lands, win)
        hm, qkv, u = _mix_in(xb, nm, win, rc, rs, l, token)
        ao = _attn_fwd(qkv, attn_sinks, l)
        y, co = _conv_fwd(u, cw_full, cb, lg, lb, l)
        xc, cat = _mix_out(ao, co, xb, wout)
        h2, P2, Q2, A2 = _ffn_up(xc, n2, wffn2, 0, 1, l, no_token)
        token = no_token
        if l + 1 < L:
            arrived = _exchange_wait(f"gather_wait_{l + 1}", _gather_chips_plan, *pending, after=A2)[1]
            *passing, token = _exchange_start(f"gather_pass_{l + 1}", _gather_pass_plan, 3, arrived, [], A2)
        saved.append((xa, h1, P1, Q1, A1, xb, hm, qkv, u, y, cat, xc, h2, P2, Q2, A2))
        if l + 1 < L:
            xa = _ffn_down(A2, xc, wffn2, 2, token)
        else:
            dx, loss_part, g_final = _ffn_down_loss(A2, xc, wffn2, 2, final_norm.reshape(1, D), target)
        token = no_token


    cx, cy, cc = _coords()
    chip_of = [2 * cx + cy] + [2 * px + py for px, py in _other_chips(cx, cy)]
    own_chip = jnp.stack(chip_of[:1]).astype(jnp.int32)
    other_chips = jnp.stack(chip_of[1:]).astype(jnp.int32)

    g_n1, g_nm, g_n2 = [None] * L, [None] * L, [None] * L
    g_cb, g_lg, g_lb, g_sink, g_cw = [None] * L, [None] * L, [None] * L, [None] * L, [None] * L
    in_flight, reduced = [], {}

    def pair_begin(tag, group, after):
        lands = [lax.empty((N_CHIP, g.shape[0], g.shape[1] // N_DEV, g.shape[2]), F32) for g in group]
        *handles, token = _exchange_start(f"pair_start_{tag}", _pair_plan, N_CHIP, group, lands, after)
        return handles, token

    def chip_begin(tag, group, got, after):
        sent = [_pair_sum(g, r, 2 * other_chips + cc, other_chips, BF16) for g, r in zip(group, got)]
        *handles, token = _exchange_start(f"chip_start_{tag}", _chip_plan, 3, sent,
                                          [lax.empty(p.shape, p.dtype) for p in sent], after)
        in_flight.append((tag, group, got, handles))
        return token

    def pair_end_chip_begin(tag, handles, after):
        group, got = _exchange_wait(f"pair_wait_{tag}", _pair_plan, *handles, after=after)
        return chip_begin(tag, group, got, after)

    for l in reversed(range(L)):
        xa, h1, P1, Q1, A1, xb, hm, qkv, u, y, cat, xc, h2, P2, Q2, A2 = saved[l]
        wffn1, wffn2, win, wout = weights[l]
        gffn1 = lax.empty((3, FF, D), F32)
        gffn2 = lax.empty((3, FF, D), F32)
        gin = lax.empty((1, DIN, D), F32)
        gout = lax.empty((1, D, D), F32)
        d, dA = _ffn_bwd_act(dx, wffn2, 2, token)
        gffn2 = _wgrad(gffn2, 2, A2, d)
        gffn2 = _wgrad(gffn2, 0, dA, h2, times=Q2)
        gffn2 = _wgrad(gffn2, 1, dA, h2, times=P2)
        handles, token = pair_begin(f"{l}c", [gffn2], d)
        dx, g_n2[l] = _ffn_bwd_in(dA, P2, Q2, xc, dx, n2, wffn2, 0, 1, l, token)
        token = pair_end_chip_begin(f"{l}c", handles, dx)
        d, dao, dco = _mix_out_bwd(dx, wout, token)
        gout = _wgrad(gout, 0, cat, d)
        dy, g_lg[l], g_lb[l], g_cb[l] = _conv_bwd_norm(dco, y, lg, lb, l)
        du, g_cw8 = _conv_bwd_taps(dy, u, cw_full, l)
        g_cw[l] = g_cw8.reshape(CW, 8, CC).sum(axis=1)
        dq, dk, dv, g_sink[l] = _attn_bwd(qkv, dao, attn_sinks, l)
        dp, dx, g_nm[l] = _mix_in_bwd(dq, dk, dv, du, rc, rs, xb, dx, nm, win, l)
        gin = _wgrad(gin, 0, dp, hm)
        handles, token = pair_begin(f"{l}a", [gin, gout], dx)
        d, dA = _ffn_bwd_act(dx, wffn1, 2, token)
        token = pair_end_chip_begin(f"{l}a", handles, d)
        gffn1 = _wgrad(gffn1, 2, A1, d)
        gffn1 = _wgrad(gffn1, 0, dA, h1, times=Q1)
        gffn1 = _wgrad(gffn1, 1, dA, h1, times=P1)
        if l > 0:
            handles, token = pair_begin(f"{l}b", [gffn1], token)
            dx, g_n1[l] = _ffn_bwd_in(dA, P1, Q1, xa, dx, n1, wffn1, 0, 1, l, token)
            token = pair_end_chip_begin(f"{l}b", handles, dx)
        else:
            token = chip_begin(f"{l}b", [gffn1], _pair_exchange([gffn1]), token)
            dx, g_n1[l] = _ffn_bwd_in(dA, P1, Q1, xa, dx, n1, wffn1, 0, 1, l, token)

    grad_x = dx.reshape(1, T, D)
    for tag, group, got, handles in in_flight:
        parts = _exchange_wait(f"chip_wait_{tag}", _chip_plan, *handles, after=dx)[1]
        reduced[tag] = [_sum_chips(g, r, 2 * own_chip + cc, own_chip, p) for g, r, p in zip(group, got, parts)]
    g1 = jnp.stack([reduced[f"{l}b"][0] for l in range(L)])
    g2 = jnp.stack([reduced[f"{l}c"][0] for l in range(L)])
    gin_t = jnp.concatenate([reduced[f"{l}a"][0] for l in range(L)])
    gout_sh = jnp.concatenate([reduced[f"{l}a"][1] for l in range(L)])

    small = [loss_part,
             jnp.concatenate(g_n1), jnp.concatenate(g_nm), jnp.concatenate(g_n2), g_final,
             jnp.concatenate(g_cb), jnp.concatenate(g_lg), jnp.concatenate(g_lb),
             jnp.stack(g_sink)[:, :, 0], jnp.stack(g_cw)]
    small_shapes = [(1, 128), (L, D), (L, D), (L, D), (D,), (L, CC), (L, CC), (L, CC), (L, NH), (L, CW, CC)]
    tot = _unpack(_all_reduce_small(_pack(small)), small_shapes)
    loss = tot[0][0, 0]
    gr_n1, gr_nm, gr_n2, gr_final, gr_cb, gr_lg, gr_lb, gr_sink, gr_cw_full = tot[1:]
    gr_cw = lax.dynamic_slice_in_dim(gr_cw_full, dev * cw_cols, cw_cols, axis=2)

    grads_t = {"ffn1_w_gate": g1[:, 0], "ffn1_w_up": g1[:, 1], "ffn2_w_gate": g2[:, 0], "ffn2_w_up": g2[:, 1],
               "w_in": gin_t}
    grads = {
        "ffn1_norm": gr_n1, "ffn1_w_down": g1[:, 2],
        "mix_norm": gr_nm, "conv_w": gr_cw, "conv_b": gr_cb, "conv_ln_g": gr_lg,
        "conv_ln_b": gr_lb, "attn_sinks": gr_sink, "w_out": gout_sh,
        "ffn2_norm": gr_n2, "ffn2_w_down": g2[:, 2],
        "final_norm": gr_final,
    }
    weights = dict(ffn1_norm=ffn1_norm, ffn1_w_gate=ffn1_w_gate, ffn1_w_up=ffn1_w_up, ffn1_w_down=ffn1_w_down, mix_norm=mix_norm, w_in=w_in, conv_w=conv_w, conv_b=conv_b, conv_ln_g=conv_ln_g, conv_ln_b=conv_ln_b, attn_sinks=attn_sinks, w_out=w_out, ffn2_norm=ffn2_norm, ffn2_w_gate=ffn2_w_gate, ffn2_w_up=ffn2_w_up, ffn2_w_down=ffn2_w_down, final_norm=final_norm)
    moms = dict(ffn1_norm=m_ffn1_norm, ffn1_w_gate=m_ffn1_w_gate, ffn1_w_up=m_ffn1_w_up, ffn1_w_down=m_ffn1_w_down, mix_norm=m_mix_norm, w_in=m_w_in, conv_w=m_conv_w, conv_b=m_conv_b, conv_ln_g=m_conv_ln_g, conv_ln_b=m_conv_ln_b, attn_sinks=m_attn_sinks, w_out=m_w_out, ffn2_norm=m_ffn2_norm, ffn2_w_gate=m_ffn2_w_gate, ffn2_w_up=m_ffn2_w_up, ffn2_w_down=m_ffn2_w_down, final_norm=m_final_norm)
    vels = dict(ffn1_norm=v_ffn1_norm, ffn1_w_gate=v_ffn1_w_gate, ffn1_w_up=v_ffn1_w_up, ffn1_w_down=v_ffn1_w_down, mix_norm=v_mix_norm, w_in=v_w_in, conv_w=v_conv_w, conv_b=v_conv_b, conv_ln_g=v_conv_ln_g, conv_ln_b=v_conv_ln_b, attn_sinks=v_attn_sinks, w_out=v_w_out, ffn2_norm=v_ffn2_norm, ffn2_w_gate=v_ffn2_w_gate, ffn2_w_up=v_ffn2_w_up, ffn2_w_down=v_ffn2_w_down, final_norm=v_final_norm)

    names = list(weights)
    big = ("ffn1_w_gate", "ffn1_w_up", "ffn1_w_down", "w_in", "w_out", "ffn2_w_gate", "ffn2_w_up", "ffn2_w_down")
    delta, new_m, new_v = {}, {}, {}
    for k in big:
        if k in grads_t:
            outs = _adamw(t_(weights[k]), grads_t[k], t_(moms[k]), t_(vels[k]))
            grads[k], delta[k], new_m[k], new_v[k] = [t_(o) for o in [grads_t[k]] + outs]
        else:
            delta[k], new_m[k], new_v[k] = _adamw(weights[k], grads[k], moms[k], vels[k])
    rest = [k for k in names if k not in big]
    rest_shapes = [weights[k].shape for k in rest]
    packed = _adamw(*[_pack([t[k] for k in rest]) for t in (weights, grads, moms, vels)])
    for res, packed_out in zip((delta, new_m, new_v), packed):
        for k, val in zip(rest, _unpack(packed_out, rest_shapes)):
            res[k] = val

    return (loss, grad_x, *[grads[k] for k in names], *[delta[k] for k in names],
            *[new_m[k] for k in names], *[new_v[k] for k in names])
```

```python
import jax
import jax.numpy as jnp
from jax import lax
from jax.experimental import pallas as pl
from jax.experimental.pallas import tpu as pltpu

F32 = jnp.float32
BF16 = jnp.bfloat16
MESH = pl.DeviceIdType.MESH

N_DEV = 8
N_CHIP = 4
LANES = 128
SUBLANES = 8
D = 1024
FF = 2816
HD = 64
NH = 8
NKV = 2
GROUP = NH // NKV
AW = NH * HD
KVW = NKV * HD
QKW = AW + KVW
QKVW = AW + 2 * KVW
CC = 512
CW = 31
DIN = QKVW + 2 * CC
BLK = 128
HALO = 32
EPS = 1e-5
SCALE = HD ** -0.5
NEG = float(jnp.finfo(jnp.float32).min)

LR, B1, B2, ADAM_EPS, WD, STEP = 0.001, 0.9, 0.999, 1e-08, 0.01, 10

TM = 1024
TM_MID = 512
TM_FFN_UP = 256
TM_CONV_BWD = 64
TK = 2048
FC = 256
ATT_BLOCKS = 8
VMEM_LIMIT = 56 * 1024 * 1024


def _cp(*sem):
    return pltpu.CompilerParams(dimension_semantics=sem, vmem_limit_bytes=VMEM_LIMIT)


def _row(tm, c):
    return pl.BlockSpec((tm, c), lambda i: (i, 0))


def _slab(shape, k, single=False):
    zeros = (0,) * len(shape)
    kw = dict(pipeline_mode=pl.Buffered(1)) if single else {}
    return pl.BlockSpec((None, *shape), lambda i: (k, *zeros), **kw)


def _acc(shape):
    return pl.BlockSpec(shape, lambda i: (0,) * len(shape))


def _nt(a, b):
    return lax.dot_general(a, b, (((1,), (1,)), ((), ())), preferred_element_type=F32)


def _tn(a, b):
    return lax.dot_general(a, b, (((0,), (0,)), ((), ())), preferred_element_type=F32)


def _nn(a, b):
    return jnp.dot(a, b, preferred_element_type=F32)


def _sigmoid(x):
    return jax.nn.sigmoid(x)


def _dsilu(z):
    s = _sigmoid(z)
    return s * (1.0 + z * (1.0 - s))


def _rms(x, g):
    r = lax.rsqrt(jnp.mean(x * x, axis=-1, keepdims=True) + EPS)
    xh = x * r
    return xh, r, xh * g


def _rms_bwd(dh, xh, r, g):
    dxh = dh * g
    return r * (dxh - xh * jnp.mean(dxh * xh, axis=-1, keepdims=True))


def _rope(t, c128, s128):
    w = t.shape[1]
    lane = lax.broadcasted_iota(jnp.int32, t.shape, 1)
    rot = jnp.where(lane % HD < HD // 2, pltpu.roll(t, w - HD // 2, 1), pltpu.roll(t, HD // 2, 1))
    return t * jnp.tile(c128, (1, w // 128)) + rot * jnp.tile(s128, (1, w // 128))


def _ffn_up(x, norm, wffn, sg, su, layer, token):
    T = x.shape[0]
    tm = TM_FFN_UP

    def body(x_ref, g_ref, wg_ref, wu_ref, token_ref, h_ref, P_ref, Q_ref, A_ref):
        _, _, hn = _rms(x_ref[...], g_ref[...])
        h = hn.astype(BF16)
        h_ref[...] = h
        for c in range(FF // FC):
            sl = slice(c * FC, (c + 1) * FC)
            g = _nt(h, wg_ref[sl, :])
            u = _nt(h, wu_ref[sl, :])
            s = _sigmoid(g)
            p = g * s
            P_ref[:, sl] = p.astype(BF16)
            Q_ref[:, sl] = (u * (s + p - p * s)).astype(BF16)
            A_ref[:, sl] = (p * u).astype(BF16)

    return pl.pallas_call(
        body, name="ffn_up", grid=(T // tm,),
        in_specs=[_row(tm, D), _slab((1, D), layer), _slab((FF, D), sg, True), _slab((FF, D), su, True), HBM],
        out_specs=[_row(tm, D), _row(tm, FF), _row(tm, FF), _row(tm, FF)],
        out_shape=[jax.ShapeDtypeStruct((T, D), BF16)] + [jax.ShapeDtypeStruct((T, FF), BF16)] * 3,
        compiler_params=_cp("parallel"),
    )(x, norm, wffn, wffn, token)


def _ffn_down(a, x, wffn, sd, token):
    T = x.shape[0]

    def body(a_ref, x_ref, w_ref, token_ref, o_ref):
        o_ref[...] = x_ref[...] + 0.5 * _nn(a_ref[...], w_ref[...])

    return pl.pallas_call(
        body, name="ffn_down", grid=(T // TM,),
        in_specs=[_row(TM, FF), _row(TM, D), _slab((FF, D), sd, True), HBM],
        out_specs=_row(TM, D),
        out_shape=jax.ShapeDtypeStruct((T, D), F32),
        compiler_params=_cp("parallel"),
    )(a, x, wffn, token)


def _mix_in(x, norm, win, rc, rs, layer, token):
    T = x.shape[0]

    def body(x_ref, g_ref, w_ref, c_ref, s_ref, token_ref, h_ref, qkv_ref, u_ref):
        _, _, hn = _rms(x_ref[...], g_ref[...])
        h = hn.astype(BF16)
        h_ref[...] = h
        qk = _nt(h, w_ref[0:QKW, :])
        qkv_ref[:, 0:QKW] = _rope(qk, c_ref[...], s_ref[...]).astype(BF16)
        qkv_ref[:, QKW:QKVW] = _nt(h, w_ref[QKW:QKVW, :]).astype(BF16)
        for c in range(2 * CC // FC):
            u_ref[:, c * FC:(c + 1) * FC] = _nt(h, w_ref[QKVW + c * FC:QKVW + (c + 1) * FC, :]).astype(BF16)

    return pl.pallas_call(
        body, name="mix_in", grid=(T // TM,),
        in_specs=[_row(TM, D), _slab((1, D), layer), _slab((DIN, D), 0, True), _row(TM, 128), _row(TM, 128), HBM],
        out_specs=[_row(TM, D), _row(TM, QKVW), _row(TM, 2 * CC)],
        out_shape=[jax.ShapeDtypeStruct((T, D), BF16), jax.ShapeDtypeStruct((T, QKVW), BF16),
                   jax.ShapeDtypeStruct((T, 2 * CC), BF16)],
        compiler_params=_cp("parallel"),
    )(x, norm, win, rc, rs, token)


def _band_mask(has_prev):
    j = lax.broadcasted_iota(jnp.int32, (2 * BLK, BLK), 0)
    r = lax.broadcasted_iota(jnp.int32, (2 * BLK, BLK), 1) + BLK
    rel = r - j
    return jnp.tile((rel >= 0) & (rel < BLK) & (has_prev | (j >= BLK)), (1, GROUP))


def _band(prev_ref, cur_ref, b, col):
    if b == 0:
        return jnp.concatenate([prev_ref[:, col:col + HD], cur_ref[0:BLK, col:col + HD]], axis=0)
    return cur_ref[(b - 1) * BLK:(b + 1) * BLK, col:col + HD]


def _stack_heads(ref, b, kv):
    cols = [(kv * GROUP + g) * HD for g in range(GROUP)]
    return jnp.concatenate([ref[b * BLK:(b + 1) * BLK, c:c + HD] for c in cols], axis=0)


def _unstack_t(xt):
    x = xt.T
    return jnp.concatenate([x[g * BLK:(g + 1) * BLK, :] for g in range(GROUP)], axis=1)


def _sink_row(sink_ref, layer, kv):
    return jnp.concatenate([jnp.full((1, BLK), sink_ref[layer, kv * GROUP + g], F32) for g in range(GROUP)], axis=1)


def _probs_t(q4, kb, mask, sink):
    s = jnp.where(mask, _nt(kb, q4) * SCALE, NEG)
    m = jnp.maximum(jnp.max(s, axis=0, keepdims=True), sink)
    p = jnp.exp(s - m)
    e = jnp.exp(sink - m)
    inv = 1.0 / (jnp.sum(p, axis=0, keepdims=True) + e)
    return p * inv, e * inv


def _attn_fwd(qkv, sinks, layer):
    T = qkv.shape[0]
    tq = ATT_BLOCKS * BLK

    def body(sink_ref, cur_ref, prev_ref, o_ref):
        first = _band_mask(pl.program_id(0) > 0)
        later = _band_mask(True)
        for b in range(ATT_BLOCKS):
            outs = []
            for kv in range(NKV):
                kb = _band(prev_ref, cur_ref, b, AW + kv * HD)
                vb = _band(prev_ref, cur_ref, b, QKW + kv * HD)
                pt, _ = _probs_t(_stack_heads(cur_ref, b, kv), kb, first if b == 0 else later,
                                 _sink_row(sink_ref, layer, kv))
                outs.append(_unstack_t(_nn(vb.T, pt.astype(BF16))))
            o_ref[b * BLK:(b + 1) * BLK, :] = jnp.concatenate(outs, axis=1).astype(BF16)

    return pl.pallas_call(
        body, name="attn_fwd", grid=(T // tq,),
        in_specs=[pl.BlockSpec(memory_space=pltpu.SMEM), _row(tq, QKVW),
                  pl.BlockSpec((BLK, QKVW), lambda i: (jnp.maximum(i * ATT_BLOCKS - 1, 0), 0))],
        out_specs=_row(tq, AW),
        out_shape=jax.ShapeDtypeStruct((T, AW), BF16),
        compiler_params=_cp("parallel"),
    )(sinks, qkv, qkv)


def _glu(u):
    u = u.astype(F32)
    return u[:, :CC] * _sigmoid(u[:, CC:])


def _fill_ext(ext_ref, first, second):
    n = ext_ref.shape[0] - SUBLANES
    ext_ref[0:first.shape[0], :] = first
    ext_ref[first.shape[0]:n, :] = second
    ext_ref[n:, :] = jnp.zeros((SUBLANES, ext_ref.shape[1]), F32)


def _taps(ext_ref, w_ref, offsets, cols, tm):
    y = None
    for b in range(SUBLANES):
        z = None
        for k, off in enumerate(offsets):
            if off % SUBLANES == b:
                term = w_ref[k:k + 1, cols] * ext_ref[pl.ds(off - b, tm + SUBLANES), cols]
                z = term if z is None else z + term
        if z is not None:
            y = z[b:b + tm, :] if y is None else y + z[b:b + tm, :]
    return y


def _conv_fwd(u, cw, cb, lg, lb, layer):
    T = u.shape[0]
    tm = TM_MID

    def body(u_ref, up_ref, w_ref, b_ref, g_ref, bb_ref, y_ref, o_ref, ext_ref):
        i = pl.program_id(0)
        _fill_ext(ext_ref, jnp.where(i > 0, _glu(up_ref[...]), 0.0), _glu(u_ref[...]))
        for c in range(CC // LANES):
            cols = slice(c * LANES, (c + 1) * LANES)
            y_ref[:, cols] = _taps(ext_ref, w_ref, [HALO - (CW - 1) + k for k in range(CW)], cols, tm) + b_ref[:, cols]
        y = y_ref[...]
        xc = y - jnp.mean(y, axis=-1, keepdims=True)
        z = xc * lax.rsqrt(jnp.mean(xc * xc, axis=-1, keepdims=True) + EPS) * g_ref[...] + bb_ref[...]
        o_ref[...] = (z * _sigmoid(z)).astype(BF16)

    return pl.pallas_call(
        body, name="conv_fwd", grid=(T // tm,),
        in_specs=[_row(tm, 2 * CC),
                  pl.BlockSpec((HALO, 2 * CC), lambda i: (jnp.maximum(i * (tm // HALO) - 1, 0), 0)),
                  _slab((CW, CC), layer), _slab((1, CC), layer), _slab((1, CC), layer), _slab((1, CC), layer)],
        out_specs=[_row(tm, CC), _row(tm, CC)],
        out_shape=[jax.ShapeDtypeStruct((T, CC), F32), jax.ShapeDtypeStruct((T, CC), BF16)],
        scratch_shapes=[pltpu.VMEM((tm + HALO + SUBLANES, CC), F32)],
        compiler_params=_cp("parallel"),
    )(u, u, cw, cb, lg, lb)


def _mix_out(ao, co, x, wout):
    T = x.shape[0]

    def body(ao_ref, co_ref, x_ref, w_ref, o_ref, cat_ref):
        cat = jnp.concatenate([ao_ref[...], co_ref[...]], axis=1)
        cat_ref[...] = cat
        o_ref[...] = x_ref[...] + _nn(cat, w_ref[...])

    return pl.pallas_call(
        body, name="mix_out", grid=(T // TM,),
        in_specs=[_row(TM, AW), _row(TM, CC), _row(TM, D), _slab((D, D), 0, True)],
        out_specs=[_row(TM, D), _row(TM, D)],
        out_shape=[jax.ShapeDtypeStruct((T, D), F32), jax.ShapeDtypeStruct((T, D), BF16)],
        compiler_params=_cp("parallel"),
    )(ao, co, x, wout)


def _ffn_down_loss(a, x, wffn, sd, norm, target):
    T = x.shape[0]

    def body(a_ref, x_ref, w_ref, g_ref, t_ref, dx_ref, loss_ref, dg_ref):
        @pl.when(pl.program_id(0) == 0)
        def _():
            loss_ref[...] = jnp.zeros_like(loss_ref)
            dg_ref[...] = jnp.zeros_like(dg_ref)

        g = g_ref[...]
        xh, r, y = _rms(x_ref[...] + 0.5 * _nn(a_ref[...], w_ref[...]), g)
        err = y - t_ref[...]
        loss_ref[...] += jnp.full(loss_ref.shape, (0.5 / D) * jnp.sum(err * err), F32)
        dy = err * (1.0 / D)
        dg_ref[...] += jnp.sum(dy * xh, axis=0, keepdims=True)
        dx_ref[...] = _rms_bwd(dy, xh, r, g)

    return pl.pallas_call(
        body, name="ffn_down_loss", grid=(T // TM,),
        in_specs=[_row(TM, FF), _row(TM, D), _slab((FF, D), sd, True), _acc((1, D)), _row(TM, D)],
        out_specs=[_row(TM, D), _acc((1, 128)), _acc((1, D))],
        out_shape=[jax.ShapeDtypeStruct((T, D), F32), jax.ShapeDtypeStruct((1, 128), F32),
                   jax.ShapeDtypeStruct((1, D), F32)],
        compiler_params=_cp("arbitrary"),
    )(a, x, wffn, norm, target)


def _ffn_bwd_act(dx, wffn, sd, token):
    T = dx.shape[0]

    def body(dx_ref, w_ref, token_ref, d_ref, dA_ref):
        d = (0.5 * dx_ref[...]).astype(BF16)
        d_ref[...] = d
        for c in range(FF // FC):
            sl = slice(c * FC, (c + 1) * FC)
            dA_ref[:, sl] = _nt(d, w_ref[sl, :]).astype(BF16)

    return pl.pallas_call(
        body, name="ffn_bwd_act", grid=(T // TM,),
        in_specs=[_row(TM, D), _slab((FF, D), sd, True), HBM],
        out_specs=[_row(TM, D), _row(TM, FF)],
        out_shape=[jax.ShapeDtypeStruct((T, D), BF16), jax.ShapeDtypeStruct((T, FF), BF16)],
        compiler_params=_cp("parallel"),
    )(dx, wffn, token)


def _ffn_bwd_in(dA, P, Q, x, dx, norm, wffn, sg, su, layer, token):
    T = x.shape[0]
    tm = TM_MID

    def body(dA_ref, P_ref, Q_ref, x_ref, dx_ref, g_ref, wg_ref, wu_ref, token_ref, o_ref, dg_ref):
        @pl.when(pl.program_id(0) == 0)
        def _():
            dg_ref[...] = jnp.zeros_like(dg_ref)

        dA = dA_ref[...]
        dh = _nn(dA * Q_ref[...], wg_ref[...]) + _nn(dA * P_ref[...], wu_ref[...])
        g = g_ref[...]
        xh, r, _ = _rms(x_ref[...], g)
        dg_ref[...] += jnp.sum(dh * xh, axis=0, keepdims=True)
        o_ref[...] = dx_ref[...] + _rms_bwd(dh, xh, r, g)

    return pl.pallas_call(
        body, name="ffn_bwd_in", grid=(T // tm,),
        in_specs=[_row(tm, FF), _row(tm, FF), _row(tm, FF), _row(tm, D), _row(tm, D), _slab((1, D), layer),
                  _slab((FF, D), sg, True), _slab((FF, D), su, True), HBM],
        out_specs=[_row(tm, D), _acc((1, D))],
        out_shape=[jax.ShapeDtypeStruct((T, D), F32), jax.ShapeDtypeStruct((1, D), F32)],
        compiler_params=_cp("arbitrary"),
    )(dA, P, Q, x, dx, norm, wffn, wffn, token)


def _mix_out_bwd(dx, wout, token):
    T = dx.shape[0]

    def body(dx_ref, w_ref, token_ref, d_ref, dao_ref, dco_ref):
        d = dx_ref[...].astype(BF16)
        d_ref[...] = d
        dcat = _nt(d, w_ref[...])
        dao_ref[...] = dcat[:, :AW].astype(BF16)
        dco_ref[...] = dcat[:, AW:].astype(BF16)

    return pl.pallas_call(
        body, name="mix_out_bwd", grid=(T // TM,),
        in_specs=[_row(TM, D), _slab((D, D), 0, True), HBM],
        out_specs=[_row(TM, D), _row(TM, AW), _row(TM, CC)],
        out_shape=[jax.ShapeDtypeStruct((T, D), BF16), jax.ShapeDtypeStruct((T, AW), BF16),
                   jax.ShapeDtypeStruct((T, CC), BF16)],
        compiler_params=_cp("parallel"),
    )(dx, wout, token)


def _conv_bwd_norm(dco, y, lg, lb, layer):
    T = y.shape[0]

    def body(dco_ref, y_ref, g_ref, bb_ref, dy_ref, dlg_ref, dlb_ref, dcb_ref):
        @pl.when(pl.program_id(0) == 0)
        def _():
            dlg_ref[...] = jnp.zeros_like(dlg_ref)
            dlb_ref[...] = jnp.zeros_like(dlb_ref)
            dcb_ref[...] = jnp.zeros_like(dcb_ref)

        y = y_ref[...]
        g = g_ref[...]
        xc = y - jnp.mean(y, axis=-1, keepdims=True)
        rs = lax.rsqrt(jnp.mean(xc * xc, axis=-1, keepdims=True) + EPS)
        xn = xc * rs
        z = xn * g + bb_ref[...]
        dz = dco_ref[...].astype(F32) * _dsilu(z)
        dlg_ref[...] += jnp.sum(dz * xn, axis=0, keepdims=True)
        dlb_ref[...] += jnp.sum(dz, axis=0, keepdims=True)
        dxn = dz * g
        dy = rs * (dxn - jnp.mean(dxn, axis=-1, keepdims=True) - xn * jnp.mean(dxn * xn, axis=-1, keepdims=True))
        dcb_ref[...] += jnp.sum(dy, axis=0, keepdims=True)
        dy_ref[...] = dy

    return pl.pallas_call(
        body, name="conv_bwd_norm", grid=(T // TM,),
        in_specs=[_row(TM, CC), _row(TM, CC), _slab((1, CC), layer), _slab((1, CC), layer)],
        out_specs=[_row(TM, CC), _acc((1, CC)), _acc((1, CC)), _acc((1, CC))],
        out_shape=[jax.ShapeDtypeStruct((T, CC), F32)] + [jax.ShapeDtypeStruct((1, CC), F32)] * 3,
        compiler_params=_cp("arbitrary"),
    )(dco, y, lg, lb)


def _conv_bwd_taps(dy, u, cw, layer):
    T = u.shape[0]
    tm = TM_CONV_BWD
    n_halo = T // HALO

    def body(dy_ref, dyn_ref, u_ref, up_ref, w_ref, du_ref, dw_ref, hext_ref, dext_ref, dz_ref, dsh_ref, dh_ref):
        i = pl.program_id(0)

        @pl.when(i == 0)
        def _():
            dw_ref[...] = jnp.zeros_like(dw_ref)

        _fill_ext(hext_ref, jnp.where(i > 0, _glu(up_ref[...]), 0.0), _glu(u_ref[...]))
        _fill_ext(dext_ref, dy_ref[...], jnp.where(i < pl.num_programs(0) - 1, dyn_ref[...], 0.0))
        _fill_ext(dz_ref, jnp.zeros((SUBLANES, CC), F32), dy_ref[...])
        for b in range(SUBLANES):
            dsh_ref[b] = dz_ref[pl.ds(SUBLANES - b, tm + SUBLANES), :]
        h_offsets = [HALO - (CW - 1) + k for k in range(CW)]
        for c in range(CC // LANES):
            cols = slice(c * LANES, (c + 1) * LANES)
            dh_ref[:, cols] = _taps(dext_ref, w_ref, [CW - 1 - k for k in range(CW)], cols, tm)
            for k, off in enumerate(h_offsets):
                b = off % SUBLANES
                prod = dsh_ref[b, :, cols] * hext_ref[pl.ds(off - b, tm + SUBLANES), cols]
                dw_ref[SUBLANES * k:SUBLANES * (k + 1), cols] += jnp.sum(
                    prod.reshape(tm // SUBLANES + 1, SUBLANES, LANES), axis=0)
        dh = dh_ref[...]
        uu = u_ref[...].astype(F32)
        a = uu[:, :CC]
        sg = _sigmoid(uu[:, CC:])
        du_ref[:, :CC] = (dh * sg).astype(BF16)
        du_ref[:, CC:] = (dh * a * sg * (1.0 - sg)).astype(BF16)

    return pl.pallas_call(
        body, name="conv_bwd_taps", grid=(T // tm,),
        in_specs=[_row(tm, CC),
                  pl.BlockSpec((HALO, CC), lambda i: (jnp.minimum((i + 1) * (tm // HALO), n_halo - 1), 0)),
                  _row(tm, 2 * CC),
                  pl.BlockSpec((HALO, 2 * CC), lambda i: (jnp.maximum(i * (tm // HALO) - 1, 0), 0)),
                  _slab((CW, CC), layer)],
        out_specs=[_row(tm, 2 * CC), _acc((CW * SUBLANES, CC))],
        out_shape=[jax.ShapeDtypeStruct((T, 2 * CC), BF16), jax.ShapeDtypeStruct((CW * SUBLANES, CC), F32)],
        scratch_shapes=[pltpu.VMEM((tm + HALO + SUBLANES, CC), F32), pltpu.VMEM((tm + HALO + SUBLANES, CC), F32),
                        pltpu.VMEM((tm + 2 * SUBLANES, CC), F32), pltpu.VMEM((SUBLANES, tm + SUBLANES, CC), F32),
                        pltpu.VMEM((tm, CC), F32)],
        compiler_params=_cp("arbitrary"),
    )(dy, dy, u, u, cw)


def _attn_bwd(qkv, dao, sinks, layer):
    T = qkv.shape[0]
    tq = ATT_BLOCKS * BLK

    def body(sink_ref, cur_ref, prev_ref, do_ref, dq_ref, dk_ref, dv_ref, ds_ref):
        i = pl.program_id(0)

        @pl.when(i == 0)
        def _():
            dk_ref[...] = jnp.zeros_like(dk_ref)
            dv_ref[...] = jnp.zeros_like(dv_ref)
            ds_ref[...] = jnp.zeros_like(ds_ref)

        first = _band_mask(i > 0)
        later = _band_mask(True)
        base = pl.multiple_of(i * tq, tq)
        before = pl.multiple_of(jnp.maximum(i * ATT_BLOCKS - 1, 0) * BLK, BLK)
        for b in range(ATT_BLOCKS):
            dqs, dks, dvs = [], [], []
            for kv in range(NKV):
                kb = _band(prev_ref, cur_ref, b, AW + kv * HD)
                vb = _band(prev_ref, cur_ref, b, QKW + kv * HD)
                q4 = _stack_heads(cur_ref, b, kv)
                do4 = _stack_heads(do_ref, b, kv)
                pt, psink = _probs_t(q4, kb, first if b == 0 else later, _sink_row(sink_ref, layer, kv))
                dpt = _nt(vb, do4)
                dd = jnp.sum(pt * dpt, axis=0, keepdims=True)
                dst = (pt * (dpt - dd) * SCALE).astype(BF16)
                sd = psink * dd
                for g in range(GROUP):
                    hh = kv * GROUP + g
                    ds_ref[hh:hh + 1, :] += jnp.full((1, 128), -jnp.sum(sd[:, g * BLK:(g + 1) * BLK]), F32)
                dqs.append(_unstack_t(_nn(kb.T, dst)))
                dks.append(_nn(dst, q4))
                dvs.append(_nn(pt.astype(BF16), do4))
            dq_ref[b * BLK:(b + 1) * BLK, :] = jnp.concatenate(dqs, axis=1).astype(BF16)
            dkband = jnp.concatenate(dks, axis=1)
            dvband = jnp.concatenate(dvs, axis=1)
            if b == 0:
                dk_ref[pl.ds(before, BLK), :] += dkband[:BLK]
                dv_ref[pl.ds(before, BLK), :] += dvband[:BLK]
                dk_ref[pl.ds(base, BLK), :] += dkband[BLK:]
                dv_ref[pl.ds(base, BLK), :] += dvband[BLK:]
            else:
                r0 = pl.multiple_of(base + (b - 1) * BLK, BLK)
                dk_ref[pl.ds(r0, 2 * BLK), :] += dkband
                dv_ref[pl.ds(r0, 2 * BLK), :] += dvband

    return pl.pallas_call(
        body, name="attn_bwd", grid=(T // tq,),
        in_specs=[pl.BlockSpec(memory_space=pltpu.SMEM), _row(tq, QKVW),
                  pl.BlockSpec((BLK, QKVW), lambda i: (jnp.maximum(i * ATT_BLOCKS - 1, 0), 0)), _row(tq, AW)],
        out_specs=[_row(tq, AW), _acc((T, KVW)), _acc((T, KVW)), _acc((NH, 128))],
        out_shape=[jax.ShapeDtypeStruct((T, AW), BF16), jax.ShapeDtypeStruct((T, KVW), F32),
                   jax.ShapeDtypeStruct((T, KVW), F32), jax.ShapeDtypeStruct((NH, 128), F32)],
        compiler_params=_cp("arbitrary"),
    )(sinks, qkv, qkv, dao)


def _mix_in_bwd(dq, dk, dv, du, rc, rs, x, dx, norm, win, layer):
    T = x.shape[0]

    def body(dq_ref, dk_ref, dv_ref, du_ref, c_ref, s_ref, x_ref, dx_ref, g_ref, w_ref, dp_ref, o_ref, dg_ref):
        @pl.when(pl.program_id(0) == 0)
        def _():
            dg_ref[...] = jnp.zeros_like(dg_ref)

        dqk = jnp.concatenate([dq_ref[...].astype(F32), dk_ref[...]], axis=1)
        dqk = _rope(dqk, c_ref[...], -s_ref[...])
        dp = jnp.concatenate([dqk.astype(BF16), dv_ref[...].astype(BF16), du_ref[...]], axis=1)
        dp_ref[...] = dp
        dh = _nn(dp, w_ref[...])
        g = g_ref[...]
        xh, r, _ = _rms(x_ref[...], g)
        dg_ref[...] += jnp.sum(dh * xh, axis=0, keepdims=True)
        o_ref[...] = dx_ref[...] + _rms_bwd(dh, xh, r, g)

    return pl.pallas_call(
        body, name="mix_in_bwd", grid=(T // TM,),
        in_specs=[_row(TM, AW), _row(TM, KVW), _row(TM, KVW), _row(TM, 2 * CC), _row(TM, 128), _row(TM, 128),
                  _row(TM, D), _row(TM, D), _slab((1, D), layer), _slab((DIN, D), 0, True)],
        out_specs=[_row(TM, DIN), _row(TM, D), _acc((1, D))],
        out_shape=[jax.ShapeDtypeStruct((T, DIN), BF16), jax.ShapeDtypeStruct((T, D), F32),
                   jax.ShapeDtypeStruct((1, D), F32)],
        compiler_params=_cp("arbitrary"),
    )(dq, dk, dv, du, rc, rs, x, dx, norm, win)


def _wgrad(buf, slab, a, b, times=None):
    T, M = a.shape
    N = b.shape[1]
    tmm = M // 2 if M > 1024 else M
    tk = TK
    lhs = [a] if times is None else [a, times]

    def body(buf_ref, *refs):
        *lhs_refs, b_ref, o_ref = refs

        @pl.when(pl.program_id(1) == 0)
        def _():
            o_ref[...] = jnp.zeros_like(o_ref)

        a_tile = lhs_refs[0][...]
        if times is not None:
            a_tile = a_tile * lhs_refs[1][...]
        o_ref[...] += _tn(a_tile, b_ref[...])

    return pl.pallas_call(
        body, name="wgrad", grid=(M // tmm, T // tk),
        in_specs=[pl.BlockSpec(memory_space=pl.ANY)] + [pl.BlockSpec((tk, tmm), lambda i, k: (k, i))] * len(lhs)
        + [pl.BlockSpec((tk, N), lambda i, k: (k, 0))],
        out_specs=pl.BlockSpec((None, tmm, N), lambda i, k: (slab, i, 0)),
        out_shape=jax.ShapeDtypeStruct(buf.shape, F32),
        input_output_aliases={0: 0},
        compiler_params=_cp("parallel", "arbitrary"),
    )(buf, *lhs, b)


HBM = pl.BlockSpec(memory_space=pl.ANY)


def _coords():
    return lax.axis_index("x"), lax.axis_index("y"), lax.axis_index("c")


def _other_chips(x, y):
    return [(1 - x, y), (x, 1 - y), (1 - x, 1 - y)]


def _all_gather(shards):
    n = len(shards)

    def body(*refs):
        ins, outs = refs[:n], refs[n:2 * n]
        send_sems, recv_sems, local_sems = refs[2 * n:]
        x, y, c = _coords()
        me, sibling = (x, y, c), (x, y, 1 - c)
        chips = _other_chips(x, y)

        def rows(a, dev):
            r = ins[a].shape[1]
            return outs[a].at[:, pl.ds(pl.multiple_of((4 * dev[0] + 2 * dev[1] + dev[2]) * r, r), r), :]

        def copy(a, k, block, to, src=None):
            return pltpu.make_async_remote_copy(
                src_ref=rows(a, block) if src is None else src, dst_ref=rows(a, block),
                send_sem=send_sems.at[a * 7 + k], recv_sem=recv_sems.at[a * 7 + k],
                device_id=to, device_id_type=MESH)

        mine = [pltpu.make_async_copy(ins[a], rows(a, me), local_sems.at[a]) for a in range(n)]
        for cp in mine:
            cp.start()
        first = []
        for a in range(n):
            first.append(copy(a, 0, me, sibling, src=ins[a]))
            first += [copy(a, 1 + j, me, (*chip, c), src=ins[a]) for j, chip in enumerate(chips)]
        for cp in first:
            cp.start()
        passed = []
        for j, chip in enumerate(chips):
            for a in range(n):
                copy(a, 1 + j, (*chip, c), me).wait_recv()
                fwd = copy(a, 4 + j, (*chip, c), sibling)
                fwd.start()
                passed.append(fwd)
        for a in range(n):
            copy(a, 0, sibling, me).wait_recv()
            for j, chip in enumerate(chips):
                copy(a, 4 + j, (*chip, 1 - c), me).wait_recv()
        for cp in first + passed:
            cp.wait_send()
        for cp in mine:
            cp.wait()

    return pl.pallas_call(
        body, name="all_gather_weights",
        in_specs=[HBM] * n, out_specs=[HBM] * n,
        out_shape=[jax.ShapeDtypeStruct((s.shape[0], N_DEV * s.shape[1], s.shape[2]), s.dtype) for s in shards],
        scratch_shapes=[pltpu.SemaphoreType.DMA((7 * n,)), pltpu.SemaphoreType.DMA((7 * n,)),
                        pltpu.SemaphoreType.DMA((n,))],
    )(*shards)


def _pair_exchange(grads):
    n = len(grads)

    def body(*refs):
        ins, got = refs[:n], refs[n:2 * n]
        send_sems, recv_sems = refs[2 * n:]
        x, y, c = _coords()
        sibling = (x, y, 1 - c)

        def remote(a, q):
            r = ins[a].shape[1] // N_DEV
            src = ins[a].at[:, pl.ds(pl.multiple_of((2 * q + 1 - c) * r, r), r), :]
            return pltpu.make_async_remote_copy(
                src_ref=src, dst_ref=got[a].at[q],
                send_sem=send_sems.at[a * N_CHIP + q], recv_sem=recv_sems.at[a * N_CHIP + q],
                device_id=sibling, device_id_type=MESH)

        sends = [remote(a, q) for a in range(n) for q in range(N_CHIP)]
        for cp in sends:
            cp.start()
        for cp in sends:
            cp.wait_recv()
        for cp in sends:
            cp.wait_send()

    return pl.pallas_call(
        body, name="grad_pair_exchange",
        in_specs=[HBM] * n, out_specs=[HBM] * n,
        out_shape=[jax.ShapeDtypeStruct((N_CHIP, g.shape[0], g.shape[1] // N_DEV, g.shape[2]), g.dtype) for g in grads],
        scratch_shapes=[pltpu.SemaphoreType.DMA((N_CHIP * n,)), pltpu.SemaphoreType.DMA((N_CHIP * n,))],
    )(*grads)


def _all_reduce_small(pack):
    R = pack.shape[0]

    def body(p_ref, tot_ref, all_ref, send_sems, recv_sems):
        x, y, c = _coords()
        me = 4 * x + 2 * y + c
        all_ref[me] = p_ref[...]
        peers = []
        for k in range(1, N_DEV):
            bx, by, bc = (k >> 2) & 1, (k >> 1) & 1, k & 1
            peers.append((x ^ bx, y ^ by, c ^ bc))

        def copy(k, slot, to):
            return pltpu.make_async_remote_copy(
                src_ref=p_ref, dst_ref=all_ref.at[slot], send_sem=send_sems.at[k], recv_sem=recv_sems.at[k],
                device_id=to, device_id_type=MESH)

        sends = [copy(k, me, peer) for k, peer in enumerate(peers)]
        for cp in sends:
            cp.start()
        for k, peer in enumerate(peers):
            copy(k, 4 * peer[0] + 2 * peer[1] + peer[2], peer).wait_recv()
        for cp in sends:
            cp.wait_send()
        tot = all_ref[0]
        for d in range(1, N_DEV):
            tot = tot + all_ref[d]
        tot_ref[...] = tot

    vmem = pl.BlockSpec(memory_space=pltpu.VMEM)
    return pl.pallas_call(
        body, name="all_reduce_small",
        in_specs=[vmem], out_specs=vmem,
        out_shape=jax.ShapeDtypeStruct((R, 128), F32),
        scratch_shapes=[pltpu.VMEM((N_DEV, R, 128), F32), pltpu.SemaphoreType.DMA((N_DEV - 1,)),
                        pltpu.SemaphoreType.DMA((N_DEV - 1,))],
    )(pack)


HBM_ONLY = pl.BlockSpec(memory_space=pltpu.HBM)
SEM = pl.BlockSpec(memory_space=pltpu.SEMAPHORE)
DATAFLOW = pltpu.SideEffectType.DATAFLOW_SIDE_EFFECTING


def _shard_rows(buf, rows, dev):
    return buf.at[:, pl.ds(pl.multiple_of((4 * dev[0] + 2 * dev[1] + dev[2]) * rows, rows), rows), :]


def _gather_chips_plan(ins, lands):
    x, y, c = _coords()
    targets = [(x, y, 1 - c)] + [(*chip, c) for chip in _other_chips(x, y)]
    return [(ins[a], _shard_rows(lands[a], ins[a].shape[1], (x, y, c)), t, _shard_rows(lands[a], ins[a].shape[1], t))
            for a in range(len(ins)) for t in targets]


def _gather_pass_plan(bufs, _):
    x, y, c = _coords()
    plan = []
    for buf in bufs:
        r = buf.shape[1] // N_DEV
        for chip in _other_chips(x, y):
            mine, theirs = _shard_rows(buf, r, (*chip, c)), _shard_rows(buf, r, (*chip, 1 - c))
            plan.append((mine, mine, (x, y, 1 - c), theirs))
    return plan


def _pair_plan(ins, lands):
    x, y, c = _coords()
    plan = []
    for a in range(len(ins)):
        r = ins[a].shape[1] // N_DEV
        for q in range(N_CHIP):
            src = ins[a].at[:, pl.ds(pl.multiple_of((2 * q + 1 - c) * r, r), r), :]
            plan.append((src, lands[a].at[q], (x, y, 1 - c), lands[a].at[q]))
    return plan


def _chip_plan(ins, lands):
    x, y, c = _coords()
    chips = _other_chips(x, y)
    return [(ins[a].at[j], lands[a].at[j], (*chips[j], c), lands[a].at[j]) for a in range(len(ins)) for j in range(3)]


def _exchange_start(name, plan, copies_per_array, srcs, lands, after):
    n, m = len(srcs), len(srcs) + len(lands)
    count = copies_per_array * n

    def body(*refs):
        ins, land = refs[:n], refs[n:m]
        send_sems, recv_sems = refs[m + 1], refs[m + 2]
        token = refs[-1]
        for k, (src, dst, peer, _) in enumerate(plan(ins, land)):
            pltpu.make_async_remote_copy(src_ref=src, dst_ref=dst, send_sem=send_sems.at[k], recv_sem=recv_sems.at[k],
                                         device_id=peer, device_id_type=MESH).start()
        token[...] = jnp.zeros_like(token)

    thru = [pltpu.HBM(v.shape, v.dtype) for v in list(srcs) + list(lands)]
    outs = pl.pallas_call(
        body, name=name,
        in_specs=[HBM_ONLY] * m + [HBM],
        out_specs=[SEM, SEM] + [HBM_ONLY] * m + [pl.BlockSpec(memory_space=pltpu.VMEM)],
        out_shape=[pltpu.SemaphoreType.DMA((count,)), pltpu.SemaphoreType.DMA((count,))] + thru
        + [jax.ShapeDtypeStruct((8, 128), F32)],
        input_output_aliases={i: 2 + i for i in range(m)},
        compiler_params=pltpu.CompilerParams(has_side_effects=DATAFLOW),
    )(*[pltpu.with_memory_space_constraint(v, pltpu.HBM) for v in list(srcs) + list(lands)], after)
    return outs[0], outs[1], outs[2:2 + n], outs[2 + n:2 + m], outs[-1]


def _exchange_wait(name, plan, send_sems, recv_sems, srcs, lands, after):
    n, m = len(srcs), len(srcs) + len(lands)

    def body(*refs):
        ins, land = refs[:n], refs[n:m]
        send, recv = refs[m], refs[m + 1]
        for k, (src, _, peer, here) in enumerate(plan(ins, land)):
            cp = pltpu.make_async_remote_copy(src_ref=src, dst_ref=here, send_sem=send.at[k], recv_sem=recv.at[k],
                                              device_id=peer, device_id_type=MESH)
            cp.wait_send()
            cp.wait_recv()

    thru = [pltpu.HBM(v.shape, v.dtype) for v in list(srcs) + list(lands)]
    outs = pl.pallas_call(
        body, name=name,
        in_specs=[HBM_ONLY] * m + [SEM, SEM, pl.BlockSpec(memory_space=pl.ANY)],
        out_specs=[HBM_ONLY] * m,
        out_shape=thru,
        input_output_aliases={i: i for i in range(m)},
        compiler_params=pltpu.CompilerParams(has_side_effects=DATAFLOW),
    )(*srcs, *lands, send_sems, recv_sems, after)
    return outs[:n], outs[n:]


def _place_own(shard, dev):
    s, r, c = shard.shape

    def body(dev_ref, i_ref, o_ref):
        o_ref[...] = i_ref[...]

    return pl.pallas_call(
        body, name="place_own_shard",
        grid_spec=pltpu.PrefetchScalarGridSpec(
            num_scalar_prefetch=1, grid=(s,),
            in_specs=[pl.BlockSpec((None, r, c), lambda i, d: (i, 0, 0))],
            out_specs=pl.BlockSpec((None, r, c), lambda i, d: (i, d[0], 0))),
        out_shape=jax.ShapeDtypeStruct((s, N_DEV * r, c), shard.dtype),
        compiler_params=_cp("arbitrary"),
    )(dev, shard)


def _tile_rows(n, cap=512):
    t = min(n, cap)
    while n % t or t % 8:
        t -= 1
        if t < 8:
            return n
    return t


def _pair_sum(g, got, owner_dev, owner_chip, dtype):
    s, r8, c = g.shape
    r = r8 // N_DEV
    n = owner_dev.shape[0]

    def body(dev_ref, chip_ref, g_ref, got_ref, o_ref):
        o_ref[...] = (g_ref[...] + got_ref[...]).astype(dtype)

    return pl.pallas_call(
        body, name="pair_sum",
        grid_spec=pltpu.PrefetchScalarGridSpec(
            num_scalar_prefetch=2, grid=(n, s),
            in_specs=[pl.BlockSpec((None, r, c), lambda j, i, dev, chip: (i, dev[j], 0)),
                      pl.BlockSpec((None, None, r, c), lambda j, i, dev, chip: (chip[j], i, 0, 0))],
            out_specs=pl.BlockSpec((None, None, r, c), lambda j, i, dev, chip: (j, i, 0, 0))),
        out_shape=jax.ShapeDtypeStruct((n, s, r, c), dtype),
        compiler_params=_cp("parallel", "parallel"),
    )(owner_dev, owner_chip, g, got)


def _sum_chips(g, got, owner_dev, owner_chip, parts):
    _, s, r, c = parts.shape

    def body(dev_ref, chip_ref, g_ref, got_ref, p0, p1, p2, o_ref):
        own = g_ref[...] + got_ref[...]
        o_ref[...] = ((own + p0[...].astype(F32)) + p1[...].astype(F32)) + p2[...].astype(F32)

    def part(q):
        return pl.BlockSpec((None, None, r, c), lambda i, dev, chip: (q, i, 0, 0))

    return pl.pallas_call(
        body, name="chip_sum",
        grid_spec=pltpu.PrefetchScalarGridSpec(
            num_scalar_prefetch=2, grid=(s,),
            in_specs=[pl.BlockSpec((None, r, c), lambda i, dev, chip: (i, dev[0], 0)),
                      pl.BlockSpec((None, None, r, c), lambda i, dev, chip: (chip[0], i, 0, 0)),
                      part(0), part(1), part(2)],
            out_specs=pl.BlockSpec((None, r, c), lambda i, dev, chip: (i, 0, 0))),
        out_shape=jax.ShapeDtypeStruct((s, r, c), F32),
        compiler_params=_cp("parallel"),
    )(owner_dev, owner_chip, g, got, parts, parts, parts)


def _adamw(w, g, m, v):
    shape = w.shape
    c = shape[-1] if w.ndim > 1 else w.shape[0]
    args = [t.reshape(-1, c) for t in (w, g, m, v)]
    n = args[0].shape[0]
    tr = _tile_rows(n)

    def body(w_ref, g_ref, m_ref, v_ref, d_ref, mo_ref, vo_ref):
        g = g_ref[...]
        m = B1 * m_ref[...] + (1.0 - B1) * g
        v = B2 * v_ref[...] + (1.0 - B2) * jnp.square(g)
        m_hat = m / (1.0 - B1 ** STEP)
        v_hat = v / (1.0 - B2 ** STEP)
        d_ref[...] = -LR * (m_hat / (jnp.sqrt(v_hat) + ADAM_EPS) + WD * w_ref[...])
        mo_ref[...] = m
        vo_ref[...] = v

    outs = pl.pallas_call(
        body, name="adamw", grid=(n // tr,),
        in_specs=[_row(tr, c)] * 4, out_specs=[_row(tr, c)] * 3,
        out_shape=[jax.ShapeDtypeStruct((n, c), F32)] * 3,
        compiler_params=_cp("parallel"),
    )(*args)
    return [o.reshape(shape) for o in outs]


def _pack(pieces):
    flat = []
    for p in pieces:
        f = p.reshape(-1)
        flat.append(jnp.pad(f, (0, (-f.shape[0]) % 1024)))
    return jnp.concatenate(flat).reshape(-1, 128)


def _unpack(pack, shapes):
    flat = pack.reshape(-1)
    out, off = [], 0
    for s in shapes:
        size = 1
        for d in s:
            size *= d
        out.append(flat[off:off + size].reshape(s))
        off += size + (-size) % 1024
    return out


def kernel(x, positions, ffn1_norm, ffn1_w_gate, ffn1_w_up, ffn1_w_down, mix_norm, w_in, conv_w, conv_b, conv_ln_g, conv_ln_b, attn_sinks, w_out, ffn2_norm, ffn2_w_gate, ffn2_w_up, ffn2_w_down, final_norm, loss_target, m_ffn1_norm, m_ffn1_w_gate, m_ffn1_w_up, m_ffn1_w_down, m_mix_norm, m_w_in, m_conv_w, m_conv_b, m_conv_ln_g, m_conv_ln_b, m_attn_sinks, m_w_out, m_ffn2_norm, m_ffn2_w_gate, m_ffn2_w_up, m_ffn2_w_down, m_final_norm, v_ffn1_norm, v_ffn1_w_gate, v_ffn1_w_up, v_ffn1_w_down, v_mix_norm, v_w_in, v_conv_w, v_conv_b, v_conv_ln_g, v_conv_ln_b, v_attn_sinks, v_w_out, v_ffn2_norm, v_ffn2_w_gate, v_ffn2_w_up, v_ffn2_w_down, v_final_norm):
    L = ffn1_norm.shape[0]
    T = x.shape[1]
    x0 = x.reshape(T, D)
    target = loss_target.reshape(T, D)
    dev = 4 * lax.axis_index("x") + 2 * lax.axis_index("y") + lax.axis_index("c")

    def t_(w):
        return jnp.swapaxes(w, 1, 2)

    def ffn_shards(gate, up, down, l):
        return jnp.stack([t_(gate)[l], t_(up)[l], down[l]]).astype(BF16)

    shards = [[ffn_shards(ffn1_w_gate, ffn1_w_up, ffn1_w_down, l), ffn_shards(ffn2_w_gate, ffn2_w_up, ffn2_w_down, l),
               t_(w_in)[l:l + 1].astype(BF16), w_out[l:l + 1].astype(BF16)] for l in range(L)]
    cw_cols = CC // N_DEV
    cw_sh = jnp.pad(conv_w.reshape(-1), (0, (-L * CW * cw_cols) % 1024)).reshape(1, -1, 128)
    dev1 = dev.reshape(1).astype(jnp.int32)
    no_token = jnp.zeros((8, 128), F32)

    wffn1_0, cw_all = _all_gather([shards[0][0], cw_sh])
    lands0 = [_place_own(s, dev1) for s in shards[0][1:]]
    *rest0, token = _exchange_start("gather_start_0", _gather_chips_plan, 4, shards[0][1:], lands0, cw_all)
    weights = [None] * L

    cw_rows = cw_sh.shape[1]
    cw_full = cw_all.reshape(N_DEV, cw_rows * 128)[:, :L * CW * cw_cols].reshape(N_DEV, L, CW, cw_cols)
    cw_full = jnp.transpose(cw_full, (1, 2, 0, 3)).reshape(L, CW, CC)

    inv_freq = 1.0 / (10000.0 ** (jnp.arange(0, HD, 2, dtype=F32) / HD))
    ang = positions.reshape(T).astype(F32)[:, None] * inv_freq
    cos, sin = jnp.cos(ang), jnp.sin(ang)
    rc = jnp.concatenate([cos, cos, cos, cos], axis=1)
    rs = jnp.concatenate([-sin, sin, -sin, sin], axis=1)

    n1 = ffn1_norm.reshape(L, 1, D)
    nm = mix_norm.reshape(L, 1, D)
    n2 = ffn2_norm.reshape(L, 1, D)
    cb = conv_b.reshape(L, 1, CC)
    lg = conv_ln_g.reshape(L, 1, CC)
    lb = conv_ln_b.reshape(L, 1, CC)

    saved = []
    xa = x0
    for l in range(L):
        if l == 0:
            wffn1 = wffn1_0
        else:
            wffn1, wffn2, win, wout = _exchange_wait(f"gather_passed_{l}", _gather_pass_plan, *passing, after=xa)[0]
        h1, P1, Q1, A1 = _ffn_up(xa, n1, wffn1, 0, 1, l, token)
        if l == 0:
            arrived = _exchange_wait("gather_wait_0", _gather_chips_plan, *rest0, after=A1)[1]
            *passing, token = _exchange_start("gather_pass_0", _gather_pass_plan, 3, arrived, [], A1)
        xb = _ffn_down(A1, xa, wffn1, 2, token)
        if l == 0:
            wffn2, win, wout = _exchange_wait("gather_passed_0", _gather_pass_plan, *passing, after=xb)[0]
        weights[l] = (wffn1, wffn2, win, wout)
        token = no_token
        if l + 1 < L:
            lands = [_place_own(s, dev1) for s in shards[l + 1]]
            *pending, token = _exchange_start(f"gather_start_{l + 1}", _gather_chips_plan, 4, shards[l + 1], lands, win)
        hm, qkv, u = _mix_in(xb, nm, win, rc, rs, l, token)
        ao = _attn_fwd(qkv, attn_sinks, l)
        y, co = _conv_fwd(u, cw_full, cb, lg, lb, l)
        xc, cat = _mix_out(ao, co, xb, wout)
        h2, P2, Q2, A2 = _ffn_up(xc, n2, wffn2, 0, 1, l, no_token)
        token = no_token
        if l + 1 < L:
            arrived = _exchange_wait(f"gather_wait_{l + 1}", _gather_chips_plan, *pending, after=A2)[1]
            *passing, token = _exchange_start(f"gather_pass_{l + 1}", _gather_pass_plan, 3, arrived, [], A2)
        saved.append((xa, h1, P1, Q1, A1, xb, hm, qkv, u, y, cat, xc, h2, P2, Q2, A2))
        if l + 1 < L:
            xa = _ffn_down(A2, xc, wffn2, 2, token)
        else:
            dx, loss_part, g_final = _ffn_down_loss(A2, xc, wffn2, 2, final_norm.reshape(1, D), target)
        token = no_token


    cx, cy, cc = _coords()
    chip_of = [2 * cx + cy] + [2 * px + py for px, py in _other_chips(cx, cy)]
    own_chip = jnp.stack(chip_of[:1]).astype(jnp.int32)
    other_chips = jnp.stack(chip_of[1:]).astype(jnp.int32)

    g_n1, g_nm, g_n2 = [None] * L, [None] * L, [None] * L
    g_cb, g_lg, g_lb, g_sink, g_cw = [None] * L, [None] * L, [None] * L, [None] * L, [None] * L
    in_flight, reduced = [], {}

    def pair_begin(tag, group, after):
        lands = [lax.empty((N_CHIP, g.shape[0], g.shape[1] // N_DEV, g.shape[2]), F32) for g in group]
        *handles, token = _exchange_start(f"pair_start_{tag}", _pair_plan, N_CHIP, group, lands, after)
        return handles, token

    def chip_begin(tag, group, got, after):
        sent = [_pair_sum(g, r, 2 * other_chips + cc, other_chips, BF16) for g, r in zip(group, got)]
        *handles, token = _exchange_start(f"chip_start_{tag}", _chip_plan, 3, sent,
                                          [lax.empty(p.shape, p.dtype) for p in sent], after)
        in_flight.append((tag, group, got, handles))
        return token

    def pair_end_chip_begin(tag, handles, after):
        group, got = _exchange_wait(f"pair_wait_{tag}", _pair_plan, *handles, after=after)
        return chip_begin(tag, group, got, after)

    for l in reversed(range(L)):
        xa, h1, P1, Q1, A1, xb, hm, qkv, u, y, cat, xc, h2, P2, Q2, A2 = saved[l]
        wffn1, wffn2, win, wout = weights[l]
        gffn1 = lax.empty((3, FF, D), F32)
        gffn2 = lax.empty((3, FF, D), F32)
        gin = lax.empty((1, DIN, D), F32)
        gout = lax.empty((1, D, D), F32)
        d, dA = _ffn_bwd_act(dx, wffn2, 2, token)
        gffn2 = _wgrad(gffn2, 2, A2, d)
        gffn2 = _wgrad(gffn2, 0, dA, h2, times=Q2)
        gffn2 = _wgrad(gffn2, 1, dA, h2, times=P2)
        handles, token = pair_begin(f"{l}c", [gffn2], d)
        dx, g_n2[l] = _ffn_bwd_in(dA, P2, Q2, xc, dx, n2, wffn2, 0, 1, l, token)
        token = pair_end_chip_begin(f"{l}c", handles, dx)
        d, dao, dco = _mix_out_bwd(dx, wout, token)
        gout = _wgrad(gout, 0, cat, d)
        dy, g_lg[l], g_lb[l], g_cb[l] = _conv_bwd_norm(dco, y, lg, lb, l)
        du, g_cw8 = _conv_bwd_taps(dy, u, cw_full, l)
        g_cw[l] = g_cw8.reshape(CW, SUBLANES, CC).sum(axis=1)
        dq, dk, dv, g_sink[l] = _attn_bwd(qkv, dao, attn_sinks, l)
        dp, dx, g_nm[l] = _mix_in_bwd(dq, dk, dv, du, rc, rs, xb, dx, nm, win, l)
        gin = _wgrad(gin, 0, dp, hm)
        handles, token = pair_begin(f"{l}a", [gin, gout], dx)
        d, dA = _ffn_bwd_act(dx, wffn1, 2, token)
        token = pair_end_chip_begin(f"{l}a", handles, d)
        gffn1 = _wgrad(gffn1, 2, A1, d)
        gffn1 = _wgrad(gffn1, 0, dA, h1, times=Q1)
        gffn1 = _wgrad(gffn1, 1, dA, h1, times=P1)
        if l > 0:
            handles, token = pair_begin(f"{l}b", [gffn1], token)
            dx, g_n1[l] = _ffn_bwd_in(dA, P1, Q1, xa, dx, n1, wffn1, 0, 1, l, token)
            token = pair_end_chip_begin(f"{l}b", handles, dx)
        else:
            token = chip_begin(f"{l}b", [gffn1], _pair_exchange([gffn1]), token)
            dx, g_n1[l] = _ffn_bwd_in(dA, P1, Q1, xa, dx, n1, wffn1, 0, 1, l, token)

    grad_x = dx.reshape(1, T, D)
    for tag, group, got, handles in in_flight:
        parts = _exchange_wait(f"chip_wait_{tag}", _chip_plan, *handles, after=dx)[1]
        reduced[tag] = [_sum_chips(g, r, 2 * own_chip + cc, own_chip, p) for g, r, p in zip(group, got, parts)]
    g1 = jnp.stack([reduced[f"{l}b"][0] for l in range(L)])
    g2 = jnp.stack([reduced[f"{l}c"][0] for l in range(L)])
    gin_t = jnp.concatenate([reduced[f"{l}a"][0] for l in range(L)])
    gout_sh = jnp.concatenate([reduced[f"{l}a"][1] for l in range(L)])

    small = [loss_part,
             jnp.concatenate(g_n1), jnp.concatenate(g_nm), jnp.concatenate(g_n2), g_final,
             jnp.concatenate(g_cb), jnp.concatenate(g_lg), jnp.concatenate(g_lb),
             jnp.stack(g_sink)[:, :, 0], jnp.stack(g_cw)]
    small_shapes = [(1, 128), (L, D), (L, D), (L, D), (D,), (L, CC), (L, CC), (L, CC), (L, NH), (L, CW, CC)]
    tot = _unpack(_all_reduce_small(_pack(small)), small_shapes)
    loss = tot[0][0, 0]
    gr_n1, gr_nm, gr_n2, gr_final, gr_cb, gr_lg, gr_lb, gr_sink, gr_cw_full = tot[1:]
    gr_cw = lax.dynamic_slice_in_dim(gr_cw_full, dev * cw_cols, cw_cols, axis=2)

    grads_t = {"ffn1_w_gate": g1[:, 0], "ffn1_w_up": g1[:, 1], "ffn2_w_gate": g2[:, 0], "ffn2_w_up": g2[:, 1],
               "w_in": gin_t}
    grads = {
        "ffn1_norm": gr_n1, "ffn1_w_down": g1[:, 2],
        "mix_norm": gr_nm, "conv_w": gr_cw, "conv_b": gr_cb, "conv_ln_g": gr_lg,
        "conv_ln_b": gr_lb, "attn_sinks": gr_sink, "w_out": gout_sh,
        "ffn2_norm": gr_n2, "ffn2_w_down": g2[:, 2],
        "final_norm": gr_final,
    }
    weights = dict(ffn1_norm=ffn1_norm, ffn1_w_gate=ffn1_w_gate, ffn1_w_up=ffn1_w_up, ffn1_w_down=ffn1_w_down, mix_norm=mix_norm, w_in=w_in, conv_w=conv_w, conv_b=conv_b, conv_ln_g=conv_ln_g, conv_ln_b=conv_ln_b, attn_sinks=attn_sinks, w_out=w_out, ffn2_norm=ffn2_norm, ffn2_w_gate=ffn2_w_gate, ffn2_w_up=ffn2_w_up, ffn2_w_down=ffn2_w_down, final_norm=final_norm)
    moms = dict(ffn1_norm=m_ffn1_norm, ffn1_w_gate=m_ffn1_w_gate, ffn1_w_up=m_ffn1_w_up, ffn1_w_down=m_ffn1_w_down, mix_norm=m_mix_norm, w_in=m_w_in, conv_w=m_conv_w, conv_b=m_conv_b, conv_ln_g=m_conv_ln_g, conv_ln_b=m_conv_ln_b, attn_sinks=m_attn_sinks, w_out=m_w_out, ffn2_norm=m_ffn2_norm, ffn2_w_gate=m_ffn2_w_gate, ffn2_w_up=m_ffn2_w_up, ffn2_w_down=m_ffn2_w_down, final_norm=m_final_norm)
    vels = dict(ffn1_norm=v_ffn1_norm, ffn1_w_gate=v_ffn1_w_gate, ffn1_w_up=v_ffn1_w_up, ffn1_w_down=v_ffn1_w_down, mix_norm=v_mix_norm, w_in=v_w_in, conv_w=v_conv_w, conv_b=v_conv_b, conv_ln_g=v_conv_ln_g, conv_ln_b=v_conv_ln_b, attn_sinks=v_attn_sinks, w_out=v_w_out, ffn2_norm=v_ffn2_norm, ffn2_w_gate=v_ffn2_w_gate, ffn2_w_up=v_ffn2_w_up, ffn2_w_down=v_ffn2_w_down, final_norm=v_final_norm)

    names = list(weights)
    big = ("ffn1_w_gate", "ffn1_w_up", "ffn1_w_down", "w_in", "w_out", "ffn2_w_gate", "ffn2_w_up", "ffn2_w_down")
    delta, new_m, new_v = {}, {}, {}
    for k in big:
        if k in grads_t:
            outs = _adamw(t_(weights[k]), grads_t[k], t_(moms[k]), t_(vels[k]))
            grads[k], delta[k], new_m[k], new_v[k] = [t_(o) for o in [grads_t[k]] + outs]
        else:
            delta[k], new_m[k], new_v[k] = _adamw(weights[k], grads[k], moms[k], vels[k])
    rest = [k for k in names if k not in big]
    rest_shapes = [weights[k].shape for k in rest]
    packed = _adamw(*[_pack([t[k] for k in rest]) for t in (weights, grads, moms, vels)])
    for res, packed_out in zip((delta, new_m, new_v), packed):
        for k, val in zip(rest, _unpack(packed_out, rest_shapes)):
            res[k] = val

    return (loss, grad_x, *[grads[k] for k in names], *[delta[k] for k in names],
            *[new_m[k] for k in names], *[new_v[k] for k in names])
```

```python
import jax
import jax.numpy as jnp
from jax import lax
from jax.experimental import pallas as pl
from jax.experimental.pallas import tpu as pltpu

F32 = jnp.float32
BF16 = jnp.bfloat16
MESH = pl.DeviceIdType.MESH

N_DEV = 8
N_CHIP = 4
LANES = 128
SUBLANES = 8
D = 1024
FF = 2816
HD = 64
NH = 8
NKV = 2
GROUP = NH // NKV
AW = NH * HD
KVW = NKV * HD
QKW = AW + KVW
QKVW = AW + 2 * KVW
CC = 512
CW = 31
DIN = QKVW + 2 * CC
BLK = 128
HALO = 32
EPS = 1e-5
SCALE = HD ** -0.5
NEG = float(jnp.finfo(jnp.float32).min)

LR, B1, B2, ADAM_EPS, WD, STEP = 0.001, 0.9, 0.999, 1e-08, 0.01, 10

TM = 1024
TM_MID = 512
TM_FFN_UP = 256
TM_CONV_BWD = 64
TK = 2048
FC = 256
ATT_BLOCKS = 8
VMEM_LIMIT = 56 * 1024 * 1024


def _cp(*sem):
    return pltpu.CompilerParams(dimension_semantics=sem, vmem_limit_bytes=VMEM_LIMIT)


def _row(tm, c):
    return pl.BlockSpec((tm, c), lambda i: (i, 0))


def _slab(shape, k, single=False):
    zeros = (0,) * len(shape)
    kw = dict(pipeline_mode=pl.Buffered(1)) if single else {}
    return pl.BlockSpec((None, *shape), lambda i: (k, *zeros), **kw)


def _acc(shape):
    return pl.BlockSpec(shape, lambda i: (0,) * len(shape))


def _nt(a, b):
    return lax.dot_general(a, b, (((1,), (1,)), ((), ())), preferred_element_type=F32)


def _tn(a, b):
    return lax.dot_general(a, b, (((0,), (0,)), ((), ())), preferred_element_type=F32)


def _nn(a, b):
    return jnp.dot(a, b, preferred_element_type=F32)


def _sigmoid(x):
    return jax.nn.sigmoid(x)


def _dsilu(z):
    s = _sigmoid(z)
    return s * (1.0 + z * (1.0 - s))


def _rms(x, g):
    r = lax.rsqrt(jnp.mean(x * x, axis=-1, keepdims=True) + EPS)
    xh = x * r
    return xh, r, xh * g


def _rms_bwd(dh, xh, r, g):
    dxh = dh * g
    return r * (dxh - xh * jnp.mean(dxh * xh, axis=-1, keepdims=True))


def _rope(t, c128, s128):
    w = t.shape[1]
    lane = lax.broadcasted_iota(jnp.int32, t.shape, 1)
    rot = jnp.where(lane % HD < HD // 2, pltpu.roll(t, w - HD // 2, 1), pltpu.roll(t, HD // 2, 1))
    return t * jnp.tile(c128, (1, w // 128)) + rot * jnp.tile(s128, (1, w // 128))


def _ffn_up(x, norm, wffn, sg, su, layer, token):
    T = x.shape[0]
    tm = TM_FFN_UP

    def body(x_ref, g_ref, wg_ref, wu_ref, token_ref, h_ref, P_ref, Q_ref, A_ref):
        _, _, hn = _rms(x_ref[...], g_ref[...])
        h = hn.astype(BF16)
        h_ref[...] = h
        for c in range(FF // FC):
            sl = slice(c * FC, (c + 1) * FC)
            g = _nt(h, wg_ref[sl, :])
            u = _nt(h, wu_ref[sl, :])
            s = _sigmoid(g)
            p = g * s
            P_ref[:, sl] = p.astype(BF16)
            Q_ref[:, sl] = (u * (s + p - p * s)).astype(BF16)
            A_ref[:, sl] = (p * u).astype(BF16)

    return pl.pallas_call(
        body, name="ffn_up", grid=(T // tm,),
        in_specs=[_row(tm, D), _slab((1, D), layer), _slab((FF, D), sg, True), _slab((FF, D), su, True), HBM],
        out_specs=[_row(tm, D), _row(tm, FF), _row(tm, FF), _row(tm, FF)],
        out_shape=[jax.ShapeDtypeStruct((T, D), BF16)] + [jax.ShapeDtypeStruct((T, FF), BF16)] * 3,
        compiler_params=_cp("parallel"),
    )(x, norm, wffn, wffn, token)


def _ffn_down(a, x, wffn, sd, token):
    T = x.shape[0]

    def body(a_ref, x_ref, w_ref, token_ref, o_ref):
        o_ref[...] = x_ref[...] + 0.5 * _nn(a_ref[...], w_ref[...])

    return pl.pallas_call(
        body, name="ffn_down", grid=(T // TM,),
        in_specs=[_row(TM, FF), _row(TM, D), _slab((FF, D), sd, True), HBM],
        out_specs=_row(TM, D),
        out_shape=jax.ShapeDtypeStruct((T, D), F32),
        compiler_params=_cp("parallel"),
    )(a, x, wffn, token)


def _mix_in(x, norm, win, rc, rs, layer, token):
    T = x.shape[0]

    def body(x_ref, g_ref, w_ref, c_ref, s_ref, token_ref, h_ref, qkv_ref, u_ref):
        _, _, hn = _rms(x_ref[...], g_ref[...])
        h = hn.astype(BF16)
        h_ref[...] = h
        qk = _rope(_nt(h, w_ref[0:QKW, :]), c_ref[...], s_ref[...])
        qkv_ref[:, 0:AW] = (qk[:, :AW] * SCALE).astype(BF16)
        qkv_ref[:, AW:QKW] = qk[:, AW:].astype(BF16)
        qkv_ref[:, QKW:QKVW] = _nt(h, w_ref[QKW:QKVW, :]).astype(BF16)
        for c in range(2 * CC // FC):
            u_ref[:, c * FC:(c + 1) * FC] = _nt(h, w_ref[QKVW + c * FC:QKVW + (c + 1) * FC, :]).astype(BF16)

    return pl.pallas_call(
        body, name="mix_in", grid=(T // TM,),
        in_specs=[_row(TM, D), _slab((1, D), layer), _slab((DIN, D), 0, True), _row(TM, 128), _row(TM, 128), HBM],
        out_specs=[_row(TM, D), _row(TM, QKVW), _row(TM, 2 * CC)],
        out_shape=[jax.ShapeDtypeStruct((T, D), BF16), jax.ShapeDtypeStruct((T, QKVW), BF16),
                   jax.ShapeDtypeStruct((T, 2 * CC), BF16)],
        compiler_params=_cp("parallel"),
    )(x, norm, win, rc, rs, token)


def _band_mask(has_prev):
    j = lax.broadcasted_iota(jnp.int32, (2 * BLK, BLK), 0)
    r = lax.broadcasted_iota(jnp.int32, (2 * BLK, BLK), 1) + BLK
    rel = r - j
    return jnp.tile((rel >= 0) & (rel < BLK) & (has_prev | (j >= BLK)), (1, GROUP))


def _band(prev_ref, cur_ref, b, col):
    if b == 0:
        return jnp.concatenate([prev_ref[:, col:col + HD], cur_ref[0:BLK, col:col + HD]], axis=0)
    return cur_ref[(b - 1) * BLK:(b + 1) * BLK, col:col + HD]


def _stack_heads(ref, b, kv):
    cols = [(kv * GROUP + g) * HD for g in range(GROUP)]
    return jnp.concatenate([ref[b * BLK:(b + 1) * BLK, c:c + HD] for c in cols], axis=0)


def _unstack_t(xt):
    x = xt.T
    return jnp.concatenate([x[g * BLK:(g + 1) * BLK, :] for g in range(GROUP)], axis=1)


def _sink_row(sink_ref, layer, kv):
    return jnp.concatenate([jnp.full((1, BLK), sink_ref[layer, kv * GROUP + g], F32) for g in range(GROUP)], axis=1)


def _probs_t(q4, kb, mask, sink):
    s = jnp.where(mask, _nt(kb, q4), NEG)
    m = jnp.maximum(jnp.max(s, axis=0, keepdims=True), sink)
    p = jnp.exp(s - m)
    e = jnp.exp(sink - m)
    inv = 1.0 / (jnp.sum(p, axis=0, keepdims=True) + e)
    return p * inv, e * inv


def _attn_fwd(qkv, sinks, layer):
    T = qkv.shape[0]
    tq = ATT_BLOCKS * BLK

    def body(sink_ref, cur_ref, prev_ref, o_ref):
        first = _band_mask(pl.program_id(0) > 0)
        later = _band_mask(True)
        for b in range(ATT_BLOCKS):
            outs = []
            for kv in range(NKV):
                kb = _band(prev_ref, cur_ref, b, AW + kv * HD)
                vb = _band(prev_ref, cur_ref, b, QKW + kv * HD)
                pt, _ = _probs_t(_stack_heads(cur_ref, b, kv), kb, first if b == 0 else later,
                                 _sink_row(sink_ref, layer, kv))
                outs.append(_unstack_t(_nn(vb.T, pt.astype(BF16))))
            o_ref[b * BLK:(b + 1) * BLK, :] = jnp.concatenate(outs, axis=1).astype(BF16)

    return pl.pallas_call(
        body, name="attn_fwd", grid=(T // tq,),
        in_specs=[pl.BlockSpec(memory_space=pltpu.SMEM), _row(tq, QKVW),
                  pl.BlockSpec((BLK, QKVW), lambda i: (jnp.maximum(i * ATT_BLOCKS - 1, 0), 0))],
        out_specs=_row(tq, AW),
        out_shape=jax.ShapeDtypeStruct((T, AW), BF16),
        compiler_params=_cp("parallel"),
    )(sinks, qkv, qkv)


def _glu(u):
    u = u.astype(F32)
    return u[:, :CC] * _sigmoid(u[:, CC:])


def _fill_ext(ext_ref, first, second):
    n = ext_ref.shape[0] - SUBLANES
    ext_ref[0:first.shape[0], :] = first
    ext_ref[first.shape[0]:n, :] = second
    ext_ref[n:, :] = jnp.zeros((SUBLANES, ext_ref.shape[1]), F32)


def _taps(ext_ref, w_ref, offsets, cols, tm):
    y = None
    for b in range(SUBLANES):
        z = None
        for k, off in enumerate(offsets):
            if off % SUBLANES == b:
                term = w_ref[k:k + 1, cols] * ext_ref[pl.ds(off - b, tm + SUBLANES), cols]
                z = term if z is None else z + term
        if z is not None:
            y = z[b:b + tm, :] if y is None else y + z[b:b + tm, :]
    return y


def _conv_fwd(u, cw, cb, lg, lb, layer):
    T = u.shape[0]
    tm = TM_MID

    def body(u_ref, up_ref, w_ref, b_ref, g_ref, bb_ref, y_ref, o_ref, ext_ref):
        i = pl.program_id(0)
        _fill_ext(ext_ref, jnp.where(i > 0, _glu(up_ref[...]), 0.0), _glu(u_ref[...]))
        for c in range(CC // LANES):
            cols = slice(c * LANES, (c + 1) * LANES)
            y_ref[:, cols] = _taps(ext_ref, w_ref, [HALO - (CW - 1) + k for k in range(CW)], cols, tm) + b_ref[:, cols]
        y = y_ref[...]
        xc = y - jnp.mean(y, axis=-1, keepdims=True)
        z = xc * lax.rsqrt(jnp.mean(xc * xc, axis=-1, keepdims=True) + EPS) * g_ref[...] + bb_ref[...]
        o_ref[...] = (z * _sigmoid(z)).astype(BF16)

    return pl.pallas_call(
        body, name="conv_fwd", grid=(T // tm,),
        in_specs=[_row(tm, 2 * CC),
                  pl.BlockSpec((HALO, 2 * CC), lambda i: (jnp.maximum(i * (tm // HALO) - 1, 0), 0)),
                  _slab((CW, CC), layer), _slab((1, CC), layer), _slab((1, CC), layer), _slab((1, CC), layer)],
        out_specs=[_row(tm, CC), _row(tm, CC)],
        out_shape=[jax.ShapeDtypeStruct((T, CC), F32), jax.ShapeDtypeStruct((T, CC), BF16)],
        scratch_shapes=[pltpu.VMEM((tm + HALO + SUBLANES, CC), F32)],
        compiler_params=_cp("parallel"),
    )(u, u, cw, cb, lg, lb)


def _mix_out(ao, co, x, wout):
    T = x.shape[0]

    def body(ao_ref, co_ref, x_ref, w_ref, o_ref, cat_ref):
        cat = jnp.concatenate([ao_ref[...], co_ref[...]], axis=1)
        cat_ref[...] = cat
        o_ref[...] = x_ref[...] + _nn(cat, w_ref[...])

    return pl.pallas_call(
        body, name="mix_out", grid=(T // TM,),
        in_specs=[_row(TM, AW), _row(TM, CC), _row(TM, D), _slab((D, D), 0, True)],
        out_specs=[_row(TM, D), _row(TM, D)],
        out_shape=[jax.ShapeDtypeStruct((T, D), F32), jax.ShapeDtypeStruct((T, D), BF16)],
        compiler_params=_cp("parallel"),
    )(ao, co, x, wout)


def _ffn_down_loss(a, x, wffn, sd, norm, target):
    T = x.shape[0]

    def body(a_ref, x_ref, w_ref, g_ref, t_ref, dx_ref, loss_ref, dg_ref):
        @pl.when(pl.program_id(0) == 0)
        def _():
            loss_ref[...] = jnp.zeros_like(loss_ref)
            dg_ref[...] = jnp.zeros_like(dg_ref)

        g = g_ref[...]
        xh, r, y = _rms(x_ref[...] + 0.5 * _nn(a_ref[...], w_ref[...]), g)
        err = y - t_ref[...]
        loss_ref[...] += jnp.full(loss_ref.shape, (0.5 / D) * jnp.sum(err * err), F32)
        dy = err * (1.0 / D)
        dg_ref[...] += jnp.sum(dy * xh, axis=0, keepdims=True)
        dx_ref[...] = _rms_bwd(dy, xh, r, g)

    return pl.pallas_call(
        body, name="ffn_down_loss", grid=(T // TM,),
        in_specs=[_row(TM, FF), _row(TM, D), _slab((FF, D), sd, True), _acc((1, D)), _row(TM, D)],
        out_specs=[_row(TM, D), _acc((1, 128)), _acc((1, D))],
        out_shape=[jax.ShapeDtypeStruct((T, D), F32), jax.ShapeDtypeStruct((1, 128), F32),
                   jax.ShapeDtypeStruct((1, D), F32)],
        compiler_params=_cp("arbitrary"),
    )(a, x, wffn, norm, target)


def _ffn_bwd_act(dx, wffn, sd, token):
    T = dx.shape[0]

    def body(dx_ref, w_ref, token_ref, d_ref, dA_ref):
        d = (0.5 * dx_ref[...]).astype(BF16)
        d_ref[...] = d
        for c in range(FF // FC):
            sl = slice(c * FC, (c + 1) * FC)
            dA_ref[:, sl] = _nt(d, w_ref[sl, :]).astype(BF16)

    return pl.pallas_call(
        body, name="ffn_bwd_act", grid=(T // TM,),
        in_specs=[_row(TM, D), _slab((FF, D), sd, True), HBM],
        out_specs=[_row(TM, D), _row(TM, FF)],
        out_shape=[jax.ShapeDtypeStruct((T, D), BF16), jax.ShapeDtypeStruct((T, FF), BF16)],
        compiler_params=_cp("parallel"),
    )(dx, wffn, token)


def _ffn_bwd_in(dA, P, Q, x, dx, norm, wffn, sg, su, layer, token):
    T = x.shape[0]
    tm = TM_MID

    def body(dA_ref, P_ref, Q_ref, x_ref, dx_ref, g_ref, wg_ref, wu_ref, token_ref, o_ref, dg_ref):
        @pl.when(pl.program_id(0) == 0)
        def _():
            dg_ref[...] = jnp.zeros_like(dg_ref)

        dA = dA_ref[...]
        dh = _nn(dA * Q_ref[...], wg_ref[...]) + _nn(dA * P_ref[...], wu_ref[...])
        g = g_ref[...]
        xh, r, _ = _rms(x_ref[...], g)
        dg_ref[...] += jnp.sum(dh * xh, axis=0, keepdims=True)
        o_ref[...] = dx_ref[...] + _rms_bwd(dh, xh, r, g)

    return pl.pallas_call(
        body, name="ffn_bwd_in", grid=(T // tm,),
        in_specs=[_row(tm, FF), _row(tm, FF), _row(tm, FF), _row(tm, D), _row(tm, D), _slab((1, D), layer),
                  _slab((FF, D), sg, True), _slab((FF, D), su, True), HBM],
        out_specs=[_row(tm, D), _acc((1, D))],
        out_shape=[jax.ShapeDtypeStruct((T, D), F32), jax.ShapeDtypeStruct((1, D), F32)],
        compiler_params=_cp("arbitrary"),
    )(dA, P, Q, x, dx, norm, wffn, wffn, token)


def _mix_out_bwd(dx, wout, token):
    T = dx.shape[0]

    def body(dx_ref, w_ref, token_ref, d_ref, dao_ref, dco_ref):
        d = dx_ref[...].astype(BF16)
        d_ref[...] = d
        dcat = _nt(d, w_ref[...])
        dao_ref[...] = dcat[:, :AW].astype(BF16)
        dco_ref[...] = dcat[:, AW:].astype(BF16)

    return pl.pallas_call(
        body, name="mix_out_bwd", grid=(T // TM,),
        in_specs=[_row(TM, D), _slab((D, D), 0, True), HBM],
        out_specs=[_row(TM, D), _row(TM, AW), _row(TM, CC)],
        out_shape=[jax.ShapeDtypeStruct((T, D), BF16), jax.ShapeDtypeStruct((T, AW), BF16),
                   jax.ShapeDtypeStruct((T, CC), BF16)],
        compiler_params=_cp("parallel"),
    )(dx, wout, token)


def _conv_bwd_norm(dco, y, lg, lb, layer):
    T = y.shape[0]

    def body(dco_ref, y_ref, g_ref, bb_ref, dy_ref, dlg_ref, dlb_ref, dcb_ref):
        @pl.when(pl.program_id(0) == 0)
        def _():
            dlg_ref[...] = jnp.zeros_like(dlg_ref)
            dlb_ref[...] = jnp.zeros_like(dlb_ref)
            dcb_ref[...] = jnp.zeros_like(dcb_ref)

        y = y_ref[...]
        g = g_ref[...]
        xc = y - jnp.mean(y, axis=-1, keepdims=True)
        rs = lax.rsqrt(jnp.mean(xc * xc, axis=-1, keepdims=True) + EPS)
        xn = xc * rs
        z = xn * g + bb_ref[...]
        dz = dco_ref[...].astype(F32) * _dsilu(z)
        dlg_ref[...] += jnp.sum(dz * xn, axis=0, keepdims=True)
        dlb_ref[...] += jnp.sum(dz, axis=0, keepdims=True)
        dxn = dz * g
        dy = rs * (dxn - jnp.mean(dxn, axis=-1, keepdims=True) - xn * jnp.mean(dxn * xn, axis=-1, keepdims=True))
        dcb_ref[...] += jnp.sum(dy, axis=0, keepdims=True)
        dy_ref[...] = dy

    return pl.pallas_call(
        body, name="conv_bwd_norm", grid=(T // TM,),
        in_specs=[_row(TM, CC), _row(TM, CC), _slab((1, CC), layer), _slab((1, CC), layer)],
        out_specs=[_row(TM, CC), _acc((1, CC)), _acc((1, CC)), _acc((1, CC))],
        out_shape=[jax.ShapeDtypeStruct((T, CC), F32)] + [jax.ShapeDtypeStruct((1, CC), F32)] * 3,
        compiler_params=_cp("arbitrary"),
    )(dco, y, lg, lb)


def _conv_bwd_taps(dy, u, cw, layer):
    T = u.shape[0]
    tm = TM_CONV_BWD
    n_halo = T // HALO

    def body(dy_ref, dyn_ref, u_ref, up_ref, w_ref, du_ref, dw_ref, hext_ref, dext_ref, dz_ref, dsh_ref, dh_ref):
        i = pl.program_id(0)

        @pl.when(i == 0)
        def _():
            dw_ref[...] = jnp.zeros_like(dw_ref)

        _fill_ext(hext_ref, jnp.where(i > 0, _glu(up_ref[...]), 0.0), _glu(u_ref[...]))
        _fill_ext(dext_ref, dy_ref[...], jnp.where(i < pl.num_programs(0) - 1, dyn_ref[...], 0.0))
        _fill_ext(dz_ref, jnp.zeros((SUBLANES, CC), F32), dy_ref[...])
        for b in range(SUBLANES):
            dsh_ref[b] = dz_ref[pl.ds(SUBLANES - b, tm + SUBLANES), :]
        h_offsets = [HALO - (CW - 1) + k for k in range(CW)]
        for c in range(CC // LANES):
            cols = slice(c * LANES, (c + 1) * LANES)
            dh_ref[:, cols] = _taps(dext_ref, w_ref, [CW - 1 - k for k in range(CW)], cols, tm)
            for k, off in enumerate(h_offsets):
                b = off % SUBLANES
                prod = dsh_ref[b, :, cols] * hext_ref[pl.ds(off - b, tm + SUBLANES), cols]
                dw_ref[SUBLANES * k:SUBLANES * (k + 1), cols] += jnp.sum(
                    prod.reshape(tm // SUBLANES + 1, SUBLANES, LANES), axis=0)
        dh = dh_ref[...]
        uu = u_ref[...].astype(F32)
        a = uu[:, :CC]
        sg = _sigmoid(uu[:, CC:])
        du_ref[:, :CC] = (dh * sg).astype(BF16)
        du_ref[:, CC:] = (dh * a * sg * (1.0 - sg)).astype(BF16)

    return pl.pallas_call(
        body, name="conv_bwd_taps", grid=(T // tm,),
        in_specs=[_row(tm, CC),
                  pl.BlockSpec((HALO, CC), lambda i: (jnp.minimum((i + 1) * (tm // HALO), n_halo - 1), 0)),
                  _row(tm, 2 * CC),
                  pl.BlockSpec((HALO, 2 * CC), lambda i: (jnp.maximum(i * (tm // HALO) - 1, 0), 0)),
                  _slab((CW, CC), layer)],
        out_specs=[_row(tm, 2 * CC), _acc((CW * SUBLANES, CC))],
        out_shape=[jax.ShapeDtypeStruct((T, 2 * CC), BF16), jax.ShapeDtypeStruct((CW * SUBLANES, CC), F32)],
        scratch_shapes=[pltpu.VMEM((tm + HALO + SUBLANES, CC), F32), pltpu.VMEM((tm + HALO + SUBLANES, CC), F32),
                        pltpu.VMEM((tm + 2 * SUBLANES, CC), F32), pltpu.VMEM((SUBLANES, tm + SUBLANES, CC), F32),
                        pltpu.VMEM((tm, CC), F32)],
        compiler_params=_cp("arbitrary"),
    )(dy, dy, u, u, cw)


def _attn_bwd(qkv, dao, sinks, layer):
    T = qkv.shape[0]
    tq = ATT_BLOCKS * BLK

    def body(sink_ref, cur_ref, prev_ref, do_ref, dq_ref, dk_ref, dv_ref, ds_ref):
        i = pl.program_id(0)

        @pl.when(i == 0)
        def _():
            dk_ref[...] = jnp.zeros_like(dk_ref)
            dv_ref[...] = jnp.zeros_like(dv_ref)
            ds_ref[...] = jnp.zeros_like(ds_ref)

        first = _band_mask(i > 0)
        later = _band_mask(True)
        base = pl.multiple_of(i * tq, tq)
        before = pl.multiple_of(jnp.maximum(i * ATT_BLOCKS - 1, 0) * BLK, BLK)
        for b in range(ATT_BLOCKS):
            dqs, dks, dvs = [], [], []
            for kv in range(NKV):
                kb = _band(prev_ref, cur_ref, b, AW + kv * HD)
                vb = _band(prev_ref, cur_ref, b, QKW + kv * HD)
                q4 = _stack_heads(cur_ref, b, kv)
                do4 = _stack_heads(do_ref, b, kv)
                pt, psink = _probs_t(q4, kb, first if b == 0 else later, _sink_row(sink_ref, layer, kv))
                dpt = _nt(vb, do4)
                dd = jnp.sum(pt * dpt, axis=0, keepdims=True)
                dst = (pt * (dpt - dd)).astype(BF16)
                sd = psink * dd
                for g in range(GROUP):
                    hh = kv * GROUP + g
                    ds_ref[hh:hh + 1, :] += jnp.full((1, 128), -jnp.sum(sd[:, g * BLK:(g + 1) * BLK]), F32)
                dqs.append(_unstack_t(_nn(kb.T, dst)))
                dks.append(_nn(dst, q4))
                dvs.append(_nn(pt.astype(BF16), do4))
            dq_ref[b * BLK:(b + 1) * BLK, :] = jnp.concatenate(dqs, axis=1).astype(BF16)
            dkband = jnp.concatenate(dks, axis=1)
            dvband = jnp.concatenate(dvs, axis=1)
            if b == 0:
                dk_ref[pl.ds(before, BLK), :] += dkband[:BLK]
                dv_ref[pl.ds(before, BLK), :] += dvband[:BLK]
                dk_ref[pl.ds(base, BLK), :] += dkband[BLK:]
                dv_ref[pl.ds(base, BLK), :] += dvband[BLK:]
            else:
                r0 = pl.multiple_of(base + (b - 1) * BLK, BLK)
                dk_ref[pl.ds(r0, 2 * BLK), :] += dkband
                dv_ref[pl.ds(r0, 2 * BLK), :] += dvband

    return pl.pallas_call(
        body, name="attn_bwd", grid=(T // tq,),
        in_specs=[pl.BlockSpec(memory_space=pltpu.SMEM), _row(tq, QKVW),
                  pl.BlockSpec((BLK, QKVW), lambda i: (jnp.maximum(i * ATT_BLOCKS - 1, 0), 0)), _row(tq, AW)],
        out_specs=[_row(tq, AW), _acc((T, KVW)), _acc((T, KVW)), _acc((NH, 128))],
        out_shape=[jax.ShapeDtypeStruct((T, AW), BF16), jax.ShapeDtypeStruct((T, KVW), F32),
                   jax.ShapeDtypeStruct((T, KVW), F32), jax.ShapeDtypeStruct((NH, 128), F32)],
        compiler_params=_cp("arbitrary"),
    )(sinks, qkv, qkv, dao)


def _mix_in_bwd(dq, dk, dv, du, rc, rs, x, dx, norm, win, layer):
    T = x.shape[0]

    def body(dq_ref, dk_ref, dv_ref, du_ref, c_ref, s_ref, x_ref, dx_ref, g_ref, w_ref, dp_ref, o_ref, dg_ref):
        @pl.when(pl.program_id(0) == 0)
        def _():
            dg_ref[...] = jnp.zeros_like(dg_ref)

        dqk = jnp.concatenate([dq_ref[...].astype(F32) * SCALE, dk_ref[...]], axis=1)
        dqk = _rope(dqk, c_ref[...], -s_ref[...])
        dp = jnp.concatenate([dqk.astype(BF16), dv_ref[...].astype(BF16), du_ref[...]], axis=1)
        dp_ref[...] = dp
        dh = _nn(dp, w_ref[...])
        g = g_ref[...]
        xh, r, _ = _rms(x_ref[...], g)
        dg_ref[...] += jnp.sum(dh * xh, axis=0, keepdims=True)
        o_ref[...] = dx_ref[...] + _rms_bwd(dh, xh, r, g)

    return pl.pallas_call(
        body, name="mix_in_bwd", grid=(T // TM,),
        in_specs=[_row(TM, AW), _row(TM, KVW), _row(TM, KVW), _row(TM, 2 * CC), _row(TM, 128), _row(TM, 128),
                  _row(TM, D), _row(TM, D), _slab((1, D), layer), _slab((DIN, D), 0, True)],
        out_specs=[_row(TM, DIN), _row(TM, D), _acc((1, D))],
        out_shape=[jax.ShapeDtypeStruct((T, DIN), BF16), jax.ShapeDtypeStruct((T, D), F32),
                   jax.ShapeDtypeStruct((1, D), F32)],
        compiler_params=_cp("arbitrary"),
    )(dq, dk, dv, du, rc, rs, x, dx, norm, win)


def _wgrad(buf, slab, a, b, times=None):
    T, M = a.shape
    N = b.shape[1]
    tmm = M // 2 if M > 1024 else M
    tk = TK
    lhs = [a] if times is None else [a, times]

    def body(buf_ref, *refs):
        *lhs_refs, b_ref, o_ref = refs

        @pl.when(pl.program_id(1) == 0)
        def _():
            o_ref[...] = jnp.zeros_like(o_ref)

        a_tile = lhs_refs[0][...]
        if times is not None:
            a_tile = a_tile * lhs_refs[1][...]
        o_ref[...] += _tn(a_tile, b_ref[...])

    return pl.pallas_call(
        body, name="wgrad", grid=(M // tmm, T // tk),
        in_specs=[pl.BlockSpec(memory_space=pl.ANY)] + [pl.BlockSpec((tk, tmm), lambda i, k: (k, i))] * len(lhs)
        + [pl.BlockSpec((tk, N), lambda i, k: (k, 0))],
        out_specs=pl.BlockSpec((None, tmm, N), lambda i, k: (slab, i, 0)),
        out_shape=jax.ShapeDtypeStruct(buf.shape, F32),
        input_output_aliases={0: 0},
        compiler_params=_cp("parallel", "arbitrary"),
    )(buf, *lhs, b)


HBM = pl.BlockSpec(memory_space=pl.ANY)


def _coords():
    return lax.axis_index("x"), lax.axis_index("y"), lax.axis_index("c")


def _other_chips(x, y):
    return [(1 - x, y), (x, 1 - y), (1 - x, 1 - y)]


def _all_gather(shards):
    n = len(shards)

    def body(*refs):
        ins, outs = refs[:n], refs[n:2 * n]
        send_sems, recv_sems, local_sems = refs[2 * n:]
        x, y, c = _coords()
        me, sibling = (x, y, c), (x, y, 1 - c)
        chips = _other_chips(x, y)

        def rows(a, dev):
            r = ins[a].shape[1]
            return outs[a].at[:, pl.ds(pl.multiple_of((4 * dev[0] + 2 * dev[1] + dev[2]) * r, r), r), :]

        def copy(a, k, block, to, src=None):
            return pltpu.make_async_remote_copy(
                src_ref=rows(a, block) if src is None else src, dst_ref=rows(a, block),
                send_sem=send_sems.at[a * 7 + k], recv_sem=recv_sems.at[a * 7 + k],
                device_id=to, device_id_type=MESH)

        mine = [pltpu.make_async_copy(ins[a], rows(a, me), local_sems.at[a]) for a in range(n)]
        for cp in mine:
            cp.start()
        first = []
        for a in range(n):
            first.append(copy(a, 0, me, sibling, src=ins[a]))
            first += [copy(a, 1 + j, me, (*chip, c), src=ins[a]) for j, chip in enumerate(chips)]
        for cp in first:
            cp.start()
        passed = []
        for j, chip in enumerate(chips):
            for a in range(n):
                copy(a, 1 + j, (*chip, c), me).wait_recv()
                fwd = copy(a, 4 + j, (*chip, c), sibling)
                fwd.start()
                passed.append(fwd)
        for a in range(n):
            copy(a, 0, sibling, me).wait_recv()
            for j, chip in enumerate(chips):
                copy(a, 4 + j, (*chip, 1 - c), me).wait_recv()
        for cp in first + passed:
            cp.wait_send()
        for cp in mine:
            cp.wait()

    return pl.pallas_call(
        body, name="all_gather_weights",
        in_specs=[HBM] * n, out_specs=[HBM] * n,
        out_shape=[jax.ShapeDtypeStruct((s.shape[0], N_DEV * s.shape[1], s.shape[2]), s.dtype) for s in shards],
        scratch_shapes=[pltpu.SemaphoreType.DMA((7 * n,)), pltpu.SemaphoreType.DMA((7 * n,)),
                        pltpu.SemaphoreType.DMA((n,))],
    )(*shards)


def _pair_exchange(grads):
    n = len(grads)

    def body(*refs):
        ins, got = refs[:n], refs[n:2 * n]
        send_sems, recv_sems = refs[2 * n:]
        x, y, c = _coords()
        sibling = (x, y, 1 - c)

        def remote(a, q):
            r = ins[a].shape[1] // N_DEV
            src = ins[a].at[:, pl.ds(pl.multiple_of((2 * q + 1 - c) * r, r), r), :]
            return pltpu.make_async_remote_copy(
                src_ref=src, dst_ref=got[a].at[q],
                send_sem=send_sems.at[a * N_CHIP + q], recv_sem=recv_sems.at[a * N_CHIP + q],
                device_id=sibling, device_id_type=MESH)

        sends = [remote(a, q) for a in range(n) for q in range(N_CHIP)]
        for cp in sends:
            cp.start()
        for cp in sends:
            cp.wait_recv()
        for cp in sends:
            cp.wait_send()

    return pl.pallas_call(
        body, name="grad_pair_exchange",
        in_specs=[HBM] * n, out_specs=[HBM] * n,
        out_shape=[jax.ShapeDtypeStruct((N_CHIP, g.shape[0], g.shape[1] // N_DEV, g.shape[2]), g.dtype) for g in grads],
        scratch_shapes=[pltpu.SemaphoreType.DMA((N_CHIP * n,)), pltpu.SemaphoreType.DMA((N_CHIP * n,))],
    )(*grads)


def _all_reduce_small(pack):
    R = pack.shape[0]

    def body(p_ref, tot_ref, all_ref, send_sems, recv_sems):
        x, y, c = _coords()
        me = 4 * x + 2 * y + c
        all_ref[me] = p_ref[...]
        peers = []
        for k in range(1, N_DEV):
            bx, by, bc = (k >> 2) & 1, (k >> 1) & 1, k & 1
            peers.append((x ^ bx, y ^ by, c ^ bc))

        def copy(k, slot, to):
            return pltpu.make_async_remote_copy(
                src_ref=p_ref, dst_ref=all_ref.at[slot], send_sem=send_sems.at[k], recv_sem=recv_sems.at[k],
                device_id=to, device_id_type=MESH)

        sends = [copy(k, me, peer) for k, peer in enumerate(peers)]
        for cp in sends:
            cp.start()
        for k, peer in enumerate(peers):
            copy(k, 4 * peer[0] + 2 * peer[1] + peer[2], peer).wait_recv()
        for cp in sends:
            cp.wait_send()
        tot = all_ref[0]
        for d in range(1, N_DEV):
            tot = tot + all_ref[d]
        tot_ref[...] = tot

    vmem = pl.BlockSpec(memory_space=pltpu.VMEM)
    return pl.pallas_call(
        body, name="all_reduce_small",
        in_specs=[vmem], out_specs=vmem,
        out_shape=jax.ShapeDtypeStruct((R, 128), F32),
        scratch_shapes=[pltpu.VMEM((N_DEV, R, 128), F32), pltpu.SemaphoreType.DMA((N_DEV - 1,)),
                        pltpu.SemaphoreType.DMA((N_DEV - 1,))],
    )(pack)


HBM_ONLY = pl.BlockSpec(memory_space=pltpu.HBM)
SEM = pl.BlockSpec(memory_space=pltpu.SEMAPHORE)
DATAFLOW = pltpu.SideEffectType.DATAFLOW_SIDE_EFFECTING


def _shard_rows(buf, rows, dev):
    return buf.at[:, pl.ds(pl.multiple_of((4 * dev[0] + 2 * dev[1] + dev[2]) * rows, rows), rows), :]


def _gather_chips_plan(ins, lands):
    x, y, c = _coords()
    targets = [(x, y, 1 - c)] + [(*chip, c) for chip in _other_chips(x, y)]
    return [(ins[a], _shard_rows(lands[a], ins[a].shape[1], (x, y, c)), t, _shard_rows(lands[a], ins[a].shape[1], t))
            for a in range(len(ins)) for t in targets]


def _gather_pass_plan(bufs, _):
    x, y, c = _coords()
    plan = []
    for buf in bufs:
        r = buf.shape[1] // N_DEV
        for chip in _other_chips(x, y):
            mine, theirs = _shard_rows(buf, r, (*chip, c)), _shard_rows(buf, r, (*chip, 1 - c))
            plan.append((mine, mine, (x, y, 1 - c), theirs))
    return plan


def _pair_plan(ins, lands):
    x, y, c = _coords()
    plan = []
    for a in range(len(ins)):
        r = ins[a].shape[1] // N_DEV
        for q in range(N_CHIP):
            src = ins[a].at[:, pl.ds(pl.multiple_of((2 * q + 1 - c) * r, r), r), :]
            plan.append((src, lands[a].at[q], (x, y, 1 - c), lands[a].at[q]))
    return plan


def _chip_plan(ins, lands):
    x, y, c = _coords()
    chips = _other_chips(x, y)
    return [(ins[a].at[j], lands[a].at[j], (*chips[j], c), lands[a].at[j]) for a in range(len(ins)) for j in range(3)]


def _exchange_start(name, plan, copies_per_array, srcs, lands, after):
    n, m = len(srcs), len(srcs) + len(lands)
    count = copies_per_array * n

    def body(*refs):
        ins, land = refs[:n], refs[n:m]
        send_sems, recv_sems = refs[m + 1], refs[m + 2]
        token = refs[-1]
        for k, (src, dst, peer, _) in enumerate(plan(ins, land)):
            pltpu.make_async_remote_copy(src_ref=src, dst_ref=dst, send_sem=send_sems.at[k], recv_sem=recv_sems.at[k],
                                         device_id=peer, device_id_type=MESH).start()
        token[...] = jnp.zeros_like(token)

    thru = [pltpu.HBM(v.shape, v.dtype) for v in list(srcs) + list(lands)]
    outs = pl.pallas_call(
        body, name=name,
        in_specs=[HBM_ONLY] * m + [HBM],
        out_specs=[SEM, SEM] + [HBM_ONLY] * m + [pl.BlockSpec(memory_space=pltpu.VMEM)],
        out_shape=[pltpu.SemaphoreType.DMA((count,)), pltpu.SemaphoreType.DMA((count,))] + thru
        + [jax.ShapeDtypeStruct((8, 128), F32)],
        input_output_aliases={i: 2 + i for i in range(m)},
        compiler_params=pltpu.CompilerParams(has_side_effects=DATAFLOW),
    )(*[pltpu.with_memory_space_constraint(v, pltpu.HBM) for v in list(srcs) + list(lands)], after)
    return outs[0], outs[1], outs[2:2 + n], outs[2 + n:2 + m], outs[-1]


def _exchange_wait(name, plan, send_sems, recv_sems, srcs, lands, after):
    n, m = len(srcs), len(srcs) + len(lands)

    def body(*refs):
        ins, land = refs[:n], refs[n:m]
        send, recv = refs[m], refs[m + 1]
        for k, (src, _, peer, here) in enumerate(plan(ins, land)):
            cp = pltpu.make_async_remote_copy(src_ref=src, dst_ref=here, send_sem=send.at[k], recv_sem=recv.at[k],
                                              device_id=peer, device_id_type=MESH)
            cp.wait_send()
            cp.wait_recv()

    thru = [pltpu.HBM(v.shape, v.dtype) for v in list(srcs) + list(lands)]
    outs = pl.pallas_call(
        body, name=name,
        in_specs=[HBM_ONLY] * m + [SEM, SEM, pl.BlockSpec(memory_space=pl.ANY)],
        out_specs=[HBM_ONLY] * m,
        out_shape=thru,
        input_output_aliases={i: i for i in range(m)},
        compiler_params=pltpu.CompilerParams(has_side_effects=DATAFLOW),
    )(*srcs, *lands, send_sems, recv_sems, after)
    return outs[:n], outs[n:]


def _place_own(shard, dev):
    s, r, c = shard.shape

    def body(dev_ref, i_ref, o_ref):
        o_ref[...] = i_ref[...]

    return pl.pallas_call(
        body, name="place_own_shard",
        grid_spec=pltpu.PrefetchScalarGridSpec(
            num_scalar_prefetch=1, grid=(s,),
            in_specs=[pl.BlockSpec((None, r, c), lambda i, d: (i, 0, 0))],
            out_specs=pl.BlockSpec((None, r, c), lambda i, d: (i, d[0], 0))),
        out_shape=jax.ShapeDtypeStruct((s, N_DEV * r, c), shard.dtype),
        compiler_params=_cp("arbitrary"),
    )(dev, shard)


def _tile_rows(n, cap=512):
    t = min(n, cap)
    while n % t or t % 8:
        t -= 1
        if t < 8:
            return n
    return t


def _pair_sum(g, got, owner_dev, owner_chip, dtype):
    s, r8, c = g.shape
    r = r8 // N_DEV
    n = owner_dev.shape[0]

    def body(dev_ref, chip_ref, g_ref, got_ref, o_ref):
        o_ref[...] = (g_ref[...] + got_ref[...]).astype(dtype)

    return pl.pallas_call(
        body, name="pair_sum",
        grid_spec=pltpu.PrefetchScalarGridSpec(
            num_scalar_prefetch=2, grid=(n, s),
            in_specs=[pl.BlockSpec((None, r, c), lambda j, i, dev, chip: (i, dev[j], 0)),
                      pl.BlockSpec((None, None, r, c), lambda j, i, dev, chip: (chip[j], i, 0, 0))],
            out_specs=pl.BlockSpec((None, None, r, c), lambda j, i, dev, chip: (j, i, 0, 0))),
        out_shape=jax.ShapeDtypeStruct((n, s, r, c), dtype),
        compiler_params=_cp("parallel", "parallel"),
    )(owner_dev, owner_chip, g, got)


def _sum_chips(g, got, owner_dev, owner_chip, parts):
    _, s, r, c = parts.shape

    def body(dev_ref, chip_ref, g_ref, got_ref, p0, p1, p2, o_ref):
        own = g_ref[...] + got_ref[...]
        o_ref[...] = ((own + p0[...].astype(F32)) + p1[...].astype(F32)) + p2[...].astype(F32)

    def part(q):
        return pl.BlockSpec((None, None, r, c), lambda i, dev, chip: (q, i, 0, 0))

    return pl.pallas_call(
        body, name="chip_sum",
        grid_spec=pltpu.PrefetchScalarGridSpec(
            num_scalar_prefetch=2, grid=(s,),
            in_specs=[pl.BlockSpec((None, r, c), lambda i, dev, chip: (i, dev[0], 0)),
                      pl.BlockSpec((None, None, r, c), lambda i, dev, chip: (chip[0], i, 0, 0)),
                      part(0), part(1), part(2)],
            out_specs=pl.BlockSpec((None, r, c), lambda i, dev, chip: (i, 0, 0))),
        out_shape=jax.ShapeDtypeStruct((s, r, c), F32),
        compiler_params=_cp("parallel"),
    )(owner_dev, owner_chip, g, got, parts, parts, parts)


def _adamw(w, g, m, v):
    shape = w.shape
    c = shape[-1] if w.ndim > 1 else w.shape[0]
    args = [t.reshape(-1, c) for t in (w, g, m, v)]
    n = args[0].shape[0]
    tr = _tile_rows(n)

    def body(w_ref, g_ref, m_ref, v_ref, d_ref, mo_ref, vo_ref):
        g = g_ref[...]
        m = B1 * m_ref[...] + (1.0 - B1) * g
        v = B2 * v_ref[...] + (1.0 - B2) * jnp.square(g)
        m_hat = m / (1.0 - B1 ** STEP)
        v_hat = v / (1.0 - B2 ** STEP)
        d_ref[...] = -LR * (m_hat / (jnp.sqrt(v_hat) + ADAM_EPS) + WD * w_ref[...])
        mo_ref[...] = m
        vo_ref[...] = v

    outs = pl.pallas_call(
        body, name="adamw", grid=(n // tr,),
        in_specs=[_row(tr, c)] * 4, out_specs=[_row(tr, c)] * 3,
        out_shape=[jax.ShapeDtypeStruct((n, c), F32)] * 3,
        compiler_params=_cp("parallel"),
    )(*args)
    return [o.reshape(shape) for o in outs]


def _pack(pieces):
    flat = []
    for p in pieces:
        f = p.reshape(-1)
        flat.append(jnp.pad(f, (0, (-f.shape[0]) % 1024)))
    return jnp.concatenate(flat).reshape(-1, 128)


def _unpack(pack, shapes):
    flat = pack.reshape(-1)
    out, off = [], 0
    for s in shapes:
        size = 1
        for d in s:
            size *= d
        out.append(flat[off:off + size].reshape(s))
        off += size + (-size) % 1024
    return out


def kernel(x, positions, ffn1_norm, ffn1_w_gate, ffn1_w_up, ffn1_w_down, mix_norm, w_in, conv_w, conv_b, conv_ln_g, conv_ln_b, attn_sinks, w_out, ffn2_norm, ffn2_w_gate, ffn2_w_up, ffn2_w_down, final_norm, loss_target, m_ffn1_norm, m_ffn1_w_gate, m_ffn1_w_up, m_ffn1_w_down, m_mix_norm, m_w_in, m_conv_w, m_conv_b, m_conv_ln_g, m_conv_ln_b, m_attn_sinks, m_w_out, m_ffn2_norm, m_ffn2_w_gate, m_ffn2_w_up, m_ffn2_w_down, m_final_norm, v_ffn1_norm, v_ffn1_w_gate, v_ffn1_w_up, v_ffn1_w_down, v_mix_norm, v_w_in, v_conv_w, v_conv_b, v_conv_ln_g, v_conv_ln_b, v_attn_sinks, v_w_out, v_ffn2_norm, v_ffn2_w_gate, v_ffn2_w_up, v_ffn2_w_down, v_final_norm):
    L = ffn1_norm.shape[0]
    T = x.shape[1]
    x0 = x.reshape(T, D)
    target = loss_target.reshape(T, D)
    dev = 4 * lax.axis_index("x") + 2 * lax.axis_index("y") + lax.axis_index("c")

    def t_(w):
        return jnp.swapaxes(w, 1, 2)

    def ffn_shards(gate, up, down, l):
        return jnp.stack([t_(gate)[l], t_(up)[l], down[l]]).astype(BF16)

    shards = [[ffn_shards(ffn1_w_gate, ffn1_w_up, ffn1_w_down, l), ffn_shards(ffn2_w_gate, ffn2_w_up, ffn2_w_down, l),
               t_(w_in)[l:l + 1].astype(BF16), w_out[l:l + 1].astype(BF16)] for l in range(L)]
    cw_cols = CC // N_DEV
    cw_sh = jnp.pad(conv_w.reshape(-1), (0, (-L * CW * cw_cols) % 1024)).reshape(1, -1, 128)
    dev1 = dev.reshape(1).astype(jnp.int32)
    no_token = jnp.zeros((8, 128), F32)

    wffn1_0, cw_all = _all_gather([shards[0][0], cw_sh])
    lands0 = [_place_own(s, dev1) for s in shards[0][1:]]
    *rest0, token = _exchange_start("gather_start_0", _gather_chips_plan, 4, shards[0][1:], lands0, cw_all)
    weights = [None] * L

    cw_rows = cw_sh.shape[1]
    cw_full = cw_all.reshape(N_DEV, cw_rows * 128)[:, :L * CW * cw_cols].reshape(N_DEV, L, CW, cw_cols)
    cw_full = jnp.transpose(cw_full, (1, 2, 0, 3)).reshape(L, CW, CC)

    inv_freq = 1.0 / (10000.0 ** (jnp.arange(0, HD, 2, dtype=F32) / HD))
    ang = positions.reshape(T).astype(F32)[:, None] * inv_freq
    cos, sin = jnp.cos(ang), jnp.sin(ang)
    rc = jnp.concatenate([cos, cos, cos, cos], axis=1)
    rs = jnp.concatenate([-sin, sin, -sin, sin], axis=1)

    n1 = ffn1_norm.reshape(L, 1, D)
    nm = mix_norm.reshape(L, 1, D)
    n2 = ffn2_norm.reshape(L, 1, D)
    cb = conv_b.reshape(L, 1, CC)
    lg = conv_ln_g.reshape(L, 1, CC)
    lb = conv_ln_b.reshape(L, 1, CC)

    saved = []
    xa = x0
    for l in range(L):
        if l == 0:
            wffn1 = wffn1_0
        else:
            wffn1, wffn2, win, wout = _exchange_wait(f"gather_passed_{l}", _gather_pass_plan, *passing, after=xa)[0]
        h1, P1, Q1, A1 = _ffn_up(xa, n1, wffn1, 0, 1, l, token)
        if l == 0:
            arrived = _exchange_wait("gather_wait_0", _gather_chips_plan, *rest0, after=A1)[1]
            *passing, token = _exchange_start("gather_pass_0", _gather_pass_plan, 3, arrived, [], A1)
        xb = _ffn_down(A1, xa, wffn1, 2, token)
        if l == 0:
            wffn2, win, wout = _exchange_wait("gather_passed_0", _gather_pass_plan, *passing, after=xb)[0]
        weights[l] = (wffn1, wffn2, win, wout)
        token = no_token
        if l + 1 < L:
            lands = [_place_own(s, dev1) for s in shards[l + 1]]
            *pending, token = _exchange_start(f"gather_start_{l + 1}", _gather_chips_plan, 4, shards[l + 1], lands, win)
        hm, qkv, u = _mix_in(xb, nm, win, rc, rs, l, token)
        ao = _attn_fwd(qkv, attn_sinks, l)
        y, co = _conv_fwd(u, cw_full, cb, lg, lb, l)
        xc, cat = _mix_out(ao, co, xb, wout)
        h2, P2, Q2, A2 = _ffn_up(xc, n2, wffn2, 0, 1, l, no_token)
        token = no_token
        if l + 1 < L:
            arrived = _exchange_wait(f"gather_wait_{l + 1}", _gather_chips_plan, *pending, after=A2)[1]
            *passing, token = _exchange_start(f"gather_pass_{l + 1}", _gather_pass_plan, 3, arrived, [], A2)
        saved.append((xa, h1, P1, Q1, A1, xb, hm, qkv, u, y, cat, xc, h2, P2, Q2, A2))
        if l + 1 < L:
            xa = _ffn_down(A2, xc, wffn2, 2, token)
        else:
            dx, loss_part, g_final = _ffn_down_loss(A2, xc, wffn2, 2, final_norm.reshape(1, D), target)
        token = no_token


    cx, cy, cc = _coords()
    chip_of = [2 * cx + cy] + [2 * px + py for px, py in _other_chips(cx, cy)]
    own_chip = jnp.stack(chip_of[:1]).astype(jnp.int32)
    other_chips = jnp.stack(chip_of[1:]).astype(jnp.int32)

    g_n1, g_nm, g_n2 = [None] * L, [None] * L, [None] * L
    g_cb, g_lg, g_lb, g_sink, g_cw = [None] * L, [None] * L, [None] * L, [None] * L, [None] * L
    in_flight, reduced = [], {}

    def pair_begin(tag, group, after):
        lands = [lax.empty((N_CHIP, g.shape[0], g.shape[1] // N_DEV, g.shape[2]), F32) for g in group]
        *handles, token = _exchange_start(f"pair_start_{tag}", _pair_plan, N_CHIP, group, lands, after)
        return handles, token

    def chip_begin(tag, group, got, after):
        sent = [_pair_sum(g, r, 2 * other_chips + cc, other_chips, BF16) for g, r in zip(group, got)]
        *handles, token = _exchange_start(f"chip_start_{tag}", _chip_plan, 3, sent,
                                          [lax.empty(p.shape, p.dtype) for p in sent], after)
        in_flight.append((tag, group, got, handles))
        return token

    def pair_end_chip_begin(tag, handles, after):
        group, got = _exchange_wait(f"pair_wait_{tag}", _pair_plan, *handles, after=after)
        return chip_begin(tag, group, got, after)

    for l in reversed(range(L)):
        xa, h1, P1, Q1, A1, xb, hm, qkv, u, y, cat, xc, h2, P2, Q2, A2 = saved[l]
        wffn1, wffn2, win, wout = weights[l]
        gffn1 = lax.empty((3, FF, D), F32)
        gffn2 = lax.empty((3, FF, D), F32)
        gin = lax.empty((1, DIN, D), F32)
        gout = lax.empty((1, D, D), F32)
        d, dA = _ffn_bwd_act(dx, wffn2, 2, token)
        gffn2 = _wgrad(gffn2, 2, A2, d)
        gffn2 = _wgrad(gffn2, 0, dA, h2, times=Q2)
        gffn2 = _wgrad(gffn2, 1, dA, h2, times=P2)
        handles, token = pair_begin(f"{l}c", [gffn2], d)
        dx, g_n2[l] = _ffn_bwd_in(dA, P2, Q2, xc, dx, n2, wffn2, 0, 1, l, token)
        token = pair_end_chip_begin(f"{l}c", handles, dx)
        d, dao, dco = _mix_out_bwd(dx, wout, token)
        gout = _wgrad(gout, 0, cat, d)
        dy, g_lg[l], g_lb[l], g_cb[l] = _conv_bwd_norm(dco, y, lg, lb, l)
        du, g_cw8 = _conv_bwd_taps(dy, u, cw_full, l)
        g_cw[l] = g_cw8.reshape(CW, SUBLANES, CC).sum(axis=1)
        dq, dk, dv, g_sink[l] = _attn_bwd(qkv, dao, attn_sinks, l)
        dp, dx, g_nm[l] = _mix_in_bwd(dq, dk, dv, du, rc, rs, xb, dx, nm, win, l)
        gin = _wgrad(gin, 0, dp, hm)
        handles, token = pair_begin(f"{l}a", [gin, gout], dx)
        d, dA = _ffn_bwd_act(dx, wffn1, 2, token)
        token = pair_end_chip_begin(f"{l}a", handles, d)
        gffn1 = _wgrad(gffn1, 2, A1, d)
        gffn1 = _wgrad(gffn1, 0, dA, h1, times=Q1)
        gffn1 = _wgrad(gffn1, 1, dA, h1, times=P1)
        if l > 0:
            handles, token = pair_begin(f"{l}b", [gffn1], token)
            dx, g_n1[l] = _ffn_bwd_in(dA, P1, Q1, xa, dx, n1, wffn1, 0, 1, l, token)
            token = pair_end_chip_begin(f"{l}b", handles, dx)
        else:
            token = chip_begin(f"{l}b", [gffn1], _pair_exchange([gffn1]), token)
            dx, g_n1[l] = _ffn_bwd_in(dA, P1, Q1, xa, dx, n1, wffn1, 0, 1, l, token)

    grad_x = dx.reshape(1, T, D)
    for tag, group, got, handles in in_flight:
        parts = _exchange_wait(f"chip_wait_{tag}", _chip_plan, *handles, after=dx)[1]
        reduced[tag] = [_sum_chips(g, r, 2 * own_chip + cc, own_chip, p) for g, r, p in zip(group, got, parts)]
    g1 = jnp.stack([reduced[f"{l}b"][0] for l in range(L)])
    g2 = jnp.stack([reduced[f"{l}c"][0] for l in range(L)])
    gin_t = jnp.concatenate([reduced[f"{l}a"][0] for l in range(L)])
    gout_sh = jnp.concatenate([reduced[f"{l}a"][1] for l in range(L)])

    small = [loss_part,
             jnp.concatenate(g_n1), jnp.concatenate(g_nm), jnp.concatenate(g_n2), g_final,
             jnp.concatenate(g_cb), jnp.concatenate(g_lg), jnp.concatenate(g_lb),
             jnp.stack(g_sink)[:, :, 0], jnp.stack(g_cw)]
    small_shapes = [(1, 128), (L, D), (L, D), (L, D), (D,), (L, CC), (L, CC), (L, CC), (L, NH), (L, CW, CC)]
    tot = _unpack(_all_reduce_small(_pack(small)), small_shapes)
    loss = tot[0][0, 0]
    gr_n1, gr_nm, gr_n2, gr_final, gr_cb, gr_lg, gr_lb, gr_sink, gr_cw_full = tot[1:]
    gr_cw = lax.dynamic_slice_in_dim(gr_cw_full, dev * cw_cols, cw_cols, axis=2)

    grads_t = {"ffn1_w_gate": g1[:, 0], "ffn1_w_up": g1[:, 1], "ffn2_w_gate": g2[:, 0], "ffn2_w_up": g2[:, 1],
               "w_in": gin_t}
    grads = {
        "ffn1_norm": gr_n1, "ffn1_w_down": g1[:, 2],
        "mix_norm": gr_nm, "conv_w": gr_cw, "conv_b": gr_cb, "conv_ln_g": gr_lg,
        "conv_ln_b": gr_lb, "attn_sinks": gr_sink, "w_out": gout_sh,
        "ffn2_norm": gr_n2, "ffn2_w_down": g2[:, 2],
        "final_norm": gr_final,
    }
    weights = dict(ffn1_norm=ffn1_norm, ffn1_w_gate=ffn1_w_gate, ffn1_w_up=ffn1_w_up, ffn1_w_down=ffn1_w_down, mix_norm=mix_norm, w_in=w_in, conv_w=conv_w, conv_b=conv_b, conv_ln_g=conv_ln_g, conv_ln_b=conv_ln_b, attn_sinks=attn_sinks, w_out=w_out, ffn2_norm=ffn2_norm, ffn2_w_gate=ffn2_w_gate, ffn2_w_up=ffn2_w_up, ffn2_w_down=ffn2_w_down, final_norm=final_norm)
    moms = dict(ffn1_norm=m_ffn1_norm, ffn1_w_gate=m_ffn1_w_gate, ffn1_w_up=m_ffn1_w_up, ffn1_w_down=m_ffn1_w_down, mix_norm=m_mix_norm, w_in=m_w_in, conv_w=m_conv_w, conv_b=m_conv_b, conv_ln_g=m_conv_ln_g, conv_ln_b=m_conv_ln_b, attn_sinks=m_attn_sinks, w_out=m_w_out, ffn2_norm=m_ffn2_norm, ffn2_w_gate=m_ffn2_w_gate, ffn2_w_up=m_ffn2_w_up, ffn2_w_down=m_ffn2_w_down, final_norm=m_final_norm)
    vels = dict(ffn1_norm=v_ffn1_norm, ffn1_w_gate=v_ffn1_w_gate, ffn1_w_up=v_ffn1_w_up, ffn1_w_down=v_ffn1_w_down, mix_norm=v_mix_norm, w_in=v_w_in, conv_w=v_conv_w, conv_b=v_conv_b, conv_ln_g=v_conv_ln_g, conv_ln_b=v_conv_ln_b, attn_sinks=v_attn_sinks, w_out=v_w_out, ffn2_norm=v_ffn2_norm, ffn2_w_gate=v_ffn2_w_gate, ffn2_w_up=v_ffn2_w_up, ffn2_w_down=v_ffn2_w_down, final_norm=v_final_norm)

    names = list(weights)
    big = ("ffn1_w_gate", "ffn1_w_up", "ffn1_w_down", "w_in", "w_out", "ffn2_w_gate", "ffn2_w_up", "ffn2_w_down")
    delta, new_m, new_v = {}, {}, {}
    for k in big:
        if k in grads_t:
            outs = _adamw(t_(weights[k]), grads_t[k], t_(moms[k]), t_(vels[k]))
            grads[k], delta[k], new_m[k], new_v[k] = [t_(o) for o in [grads_t[k]] + outs]
        else:
            delta[k], new_m[k], new_v[k] = _adamw(weights[k], grads[k], moms[k], vels[k])
    rest = [k for k in names if k not in big]
    rest_shapes = [weights[k].shape for k in rest]
    packed = _adamw(*[_pack([t[k] for k in rest]) for t in (weights, grads, moms, vels)])
    for res, packed_out in zip((delta, new_m, new_v), packed):
        for k, val in zip(rest, _unpack(packed_out, rest_shapes)):
            res[k] = val

    return (loss, grad_x, *[grads[k] for k in names], *[delta[k] for k in names],
            *[new_m[k] for k in names], *[new_v[k] for k in names])
```

```python
import jax
import jax.numpy as jnp
from jax import lax
from jax.experimental import pallas as pl
from jax.experimental.pallas import tpu as pltpu

F32 = jnp.float32
BF16 = jnp.bfloat16
MESH = pl.DeviceIdType.MESH

N_DEV = 8
N_CHIP = 4
LANES = 128
SUBLANES = 8
D = 1024
FF = 2816
HD = 64
NH = 8
NKV = 2
GROUP = NH // NKV
AW = NH * HD
KVW = NKV * HD
QKW = AW + KVW
QKVW = AW + 2 * KVW
CC = 512
CW = 31
DIN = QKVW + 2 * CC
BLK = 128
HALO = 32
EPS = 1e-5
SCALE = HD ** -0.5
NEG = float(jnp.finfo(jnp.float32).min)

LR, B1, B2, ADAM_EPS, WD, STEP = 0.001, 0.9, 0.999, 1e-08, 0.01, 10

TM = 1024
TM_MID = 512
TM_FFN_UP = 256
TM_CONV_BWD = 64
TK = 2048
FC = 256
ATT_BLOCKS = 8
VMEM_LIMIT = 56 * 1024 * 1024


def _cp(*sem):
    return pltpu.CompilerParams(dimension_semantics=sem, vmem_limit_bytes=VMEM_LIMIT)


def _row(tm, c):
    return pl.BlockSpec((tm, c), lambda i: (i, 0))


def _slab(shape, k, single=False):
    zeros = (0,) * len(shape)
    kw = dict(pipeline_mode=pl.Buffered(1)) if single else {}
    return pl.BlockSpec((None, *shape), lambda i: (k, *zeros), **kw)


def _acc(shape):
    return pl.BlockSpec(shape, lambda i: (0,) * len(shape))


def _nt(a, b):
    return lax.dot_general(a, b, (((1,), (1,)), ((), ())), preferred_element_type=F32)


def _tn(a, b):
    return lax.dot_general(a, b, (((0,), (0,)), ((), ())), preferred_element_type=F32)


def _nn(a, b):
    return jnp.dot(a, b, preferred_element_type=F32)


def _sigmoid(x):
    return jax.nn.sigmoid(x)


def _dsilu(z):
    s = _sigmoid(z)
    return s * (1.0 + z * (1.0 - s))


def _rms(x, g):
    r = lax.rsqrt(jnp.mean(x * x, axis=-1, keepdims=True) + EPS)
    xh = x * r
    return xh, r, xh * g


def _rms_bwd(dh, xh, r, g):
    dxh = dh * g
    return r * (dxh - xh * jnp.mean(dxh * xh, axis=-1, keepdims=True))


def _rope(t, c128, s128):
    w = t.shape[1]
    lane = lax.broadcasted_iota(jnp.int32, t.shape, 1)
    rot = jnp.where(lane % HD < HD // 2, pltpu.roll(t, w - HD // 2, 1), pltpu.roll(t, HD // 2, 1))
    return t * jnp.tile(c128, (1, w // 128)) + rot * jnp.tile(s128, (1, w // 128))


def _ffn_up(x, norm, wffn, sg, su, layer, token):
    T = x.shape[0]
    tm = TM_FFN_UP

    def body(x_ref, g_ref, wg_ref, wu_ref, token_ref, h_ref, P_ref, Q_ref, A_ref):
        _, _, hn = _rms(x_ref[...], g_ref[...])
        h = hn.astype(BF16)
        h_ref[...] = h
        for c in range(FF // FC):
            sl = slice(c * FC, (c + 1) * FC)
            g = _nt(h, wg_ref[sl, :])
            u = _nt(h, wu_ref[sl, :])
            s = _sigmoid(g)
            p = g * s
            P_ref[:, sl] = p.astype(BF16)
            Q_ref[:, sl] = (u * (s + p - p * s)).astype(BF16)
            A_ref[:, sl] = (p * u).astype(BF16)

    return pl.pallas_call(
        body, name="ffn_up", grid=(T // tm,),
        in_specs=[_row(tm, D), _slab((1, D), layer), _slab((FF, D), sg, True), _slab((FF, D), su, True), HBM],
        out_specs=[_row(tm, D), _row(tm, FF), _row(tm, FF), _row(tm, FF)],
        out_shape=[jax.ShapeDtypeStruct((T, D), BF16)] + [jax.ShapeDtypeStruct((T, FF), BF16)] * 3,
        compiler_params=_cp("parallel"),
    )(x, norm, wffn, wffn, token)


def _ffn_down(a, x, wffn, sd, token):
    T = x.shape[0]

    def body(a_ref, x_ref, w_ref, token_ref, o_ref):
        o_ref[...] = x_ref[...] + 0.5 * _nn(a_ref[...], w_ref[...])

    return pl.pallas_call(
        body, name="ffn_down", grid=(T // TM,),
        in_specs=[_row(TM, FF), _row(TM, D), _slab((FF, D), sd, True), HBM],
        out_specs=_row(TM, D),
        out_shape=jax.ShapeDtypeStruct((T, D), F32),
        compiler_params=_cp("parallel"),
    )(a, x, wffn, token)


def _mix_in(x, norm, win, rc, rs, layer, token):
    T = x.shape[0]

    def body(x_ref, g_ref, w_ref, c_ref, s_ref, token_ref, h_ref, qkv_ref, u_ref):
        _, _, hn = _rms(x_ref[...], g_ref[...])
        h = hn.astype(BF16)
        h_ref[...] = h
        qk = _rope(_nt(h, w_ref[0:QKW, :]), c_ref[...], s_ref[...])
        qkv_ref[:, 0:AW] = (qk[:, :AW] * SCALE).astype(BF16)
        qkv_ref[:, AW:QKW] = qk[:, AW:].astype(BF16)
        qkv_ref[:, QKW:QKVW] = _nt(h, w_ref[QKW:QKVW, :]).astype(BF16)
        for c in range(2 * CC // FC):
            u_ref[:, c * FC:(c + 1) * FC] = _nt(h, w_ref[QKVW + c * FC:QKVW + (c + 1) * FC, :]).astype(BF16)

    return pl.pallas_call(
        body, name="mix_in", grid=(T // TM,),
        in_specs=[_row(TM, D), _slab((1, D), layer), _slab((DIN, D), 0, True), _row(TM, 128), _row(TM, 128), HBM],
        out_specs=[_row(TM, D), _row(TM, QKVW), _row(TM, 2 * CC)],
        out_shape=[jax.ShapeDtypeStruct((T, D), BF16), jax.ShapeDtypeStruct((T, QKVW), BF16),
                   jax.ShapeDtypeStruct((T, 2 * CC), BF16)],
        compiler_params=_cp("parallel"),
    )(x, norm, win, rc, rs, token)


def _band_mask(has_prev):
    j = lax.broadcasted_iota(jnp.int32, (2 * BLK, BLK), 0)
    r = lax.broadcasted_iota(jnp.int32, (2 * BLK, BLK), 1) + BLK
    rel = r - j
    return jnp.tile((rel >= 0) & (rel < BLK) & (has_prev | (j >= BLK)), (1, GROUP))


def _band(prev_ref, cur_ref, b, col):
    if b == 0:
        return jnp.concatenate([prev_ref[:, col:col + HD], cur_ref[0:BLK, col:col + HD]], axis=0)
    return cur_ref[(b - 1) * BLK:(b + 1) * BLK, col:col + HD]


def _stack_heads(ref, b, kv):
    cols = [(kv * GROUP + g) * HD for g in range(GROUP)]
    return jnp.concatenate([ref[b * BLK:(b + 1) * BLK, c:c + HD] for c in cols], axis=0)


def _unstack_t(xt):
    x = xt.T
    return jnp.concatenate([x[g * BLK:(g + 1) * BLK, :] for g in range(GROUP)], axis=1)


def _sink_row(sink_ref, layer, kv):
    return jnp.concatenate([jnp.full((1, BLK), sink_ref[layer, kv * GROUP + g], F32) for g in range(GROUP)], axis=1)


def _probs_t(q4, kb, mask, sink):
    s = jnp.where(mask, _nt(kb, q4), NEG)
    m = jnp.maximum(jnp.max(s, axis=0, keepdims=True), sink)
    p = jnp.exp(s - m)
    e = jnp.exp(sink - m)
    inv = 1.0 / (jnp.sum(p, axis=0, keepdims=True) + e)
    return p * inv, e * inv


def _attn_fwd(qkv, sinks, layer):
    T = qkv.shape[0]
    tq = ATT_BLOCKS * BLK

    def body(sink_ref, cur_ref, prev_ref, o_ref):
        first = _band_mask(pl.program_id(0) > 0)
        later = _band_mask(True)
        for b in range(ATT_BLOCKS):
            outs = []
            for kv in range(NKV):
                kb = _band(prev_ref, cur_ref, b, AW + kv * HD)
                vb = _band(prev_ref, cur_ref, b, QKW + kv * HD)
                pt, _ = _probs_t(_stack_heads(cur_ref, b, kv), kb, first if b == 0 else later,
                                 _sink_row(sink_ref, layer, kv))
                outs.append(_unstack_t(_nn(vb.T, pt.astype(BF16))))
            o_ref[b * BLK:(b + 1) * BLK, :] = jnp.concatenate(outs, axis=1).astype(BF16)

    return pl.pallas_call(
        body, name="attn_fwd", grid=(T // tq,),
        in_specs=[pl.BlockSpec(memory_space=pltpu.SMEM), _row(tq, QKVW),
                  pl.BlockSpec((BLK, QKVW), lambda i: (jnp.maximum(i * ATT_BLOCKS - 1, 0), 0))],
        out_specs=_row(tq, AW),
        out_shape=jax.ShapeDtypeStruct((T, AW), BF16),
        compiler_params=_cp("parallel"),
    )(sinks, qkv, qkv)


def _glu(u):
    u = u.astype(F32)
    return u[:, :CC] * _sigmoid(u[:, CC:])


def _fill_ext(ext_ref, first, second):
    n = ext_ref.shape[0] - SUBLANES
    ext_ref[0:first.shape[0], :] = first
    ext_ref[first.shape[0]:n, :] = second
    ext_ref[n:, :] = jnp.zeros((SUBLANES, ext_ref.shape[1]), F32)


def _taps(ext_ref, w_ref, offsets, cols, tm):
    y = None
    for b in range(SUBLANES):
        z = None
        for k, off in enumerate(offsets):
            if off % SUBLANES == b:
                term = w_ref[k:k + 1, cols] * ext_ref[pl.ds(off - b, tm + SUBLANES), cols]
                z = term if z is None else z + term
        if z is not None:
            y = z[b:b + tm, :] if y is None else y + z[b:b + tm, :]
    return y


def _conv_fwd(u, cw, cb, lg, lb, layer):
    T = u.shape[0]
    tm = TM_MID

    def body(u_ref, up_ref, w_ref, b_ref, g_ref, bb_ref, y_ref, o_ref, ext_ref):
        i = pl.program_id(0)
        _fill_ext(ext_ref, jnp.where(i > 0, _glu(up_ref[...]), 0.0), _glu(u_ref[...]))
        for c in range(CC // LANES):
            cols = slice(c * LANES, (c + 1) * LANES)
            y_ref[:, cols] = _taps(ext_ref, w_ref, [HALO - (CW - 1) + k for k in range(CW)], cols, tm) + b_ref[:, cols]
        y = y_ref[...]
        xc = y - jnp.mean(y, axis=-1, keepdims=True)
        z = xc * lax.rsqrt(jnp.mean(xc * xc, axis=-1, keepdims=True) + EPS) * g_ref[...] + bb_ref[...]
        o_ref[...] = (z * _sigmoid(z)).astype(BF16)

    return pl.pallas_call(
        body, name="conv_fwd", grid=(T // tm,),
        in_specs=[_row(tm, 2 * CC),
                  pl.BlockSpec((HALO, 2 * CC), lambda i: (jnp.maximum(i * (tm // HALO) - 1, 0), 0)),
                  _slab((CW, CC), layer), _slab((1, CC), layer), _slab((1, CC), layer), _slab((1, CC), layer)],
        out_specs=[_row(tm, CC), _row(tm, CC)],
        out_shape=[jax.ShapeDtypeStruct((T, CC), F32), jax.ShapeDtypeStruct((T, CC), BF16)],
        scratch_shapes=[pltpu.VMEM((tm + HALO + SUBLANES, CC), F32)],
        compiler_params=_cp("parallel"),
    )(u, u, cw, cb, lg, lb)


def _mix_out(ao, co, x, wout):
    T = x.shape[0]

    def body(ao_ref, co_ref, x_ref, w_ref, o_ref):
        o_ref[...] = x_ref[...] + (_nn(ao_ref[...], w_ref[0:AW, :]) + _nn(co_ref[...], w_ref[AW:, :]))

    return pl.pallas_call(
        body, name="mix_out", grid=(T // TM,),
        in_specs=[_row(TM, AW), _row(TM, CC), _row(TM, D), _slab((D, D), 0, True)],
        out_specs=_row(TM, D),
        out_shape=jax.ShapeDtypeStruct((T, D), F32),
        compiler_params=_cp("parallel"),
    )(ao, co, x, wout)


def _ffn_down_loss(a, x, wffn, sd, norm, target):
    T = x.shape[0]

    def body(a_ref, x_ref, w_ref, g_ref, t_ref, dx_ref, loss_ref, dg_ref):
        @pl.when(pl.program_id(0) == 0)
        def _():
            loss_ref[...] = jnp.zeros_like(loss_ref)
            dg_ref[...] = jnp.zeros_like(dg_ref)

        g = g_ref[...]
        xh, r, y = _rms(x_ref[...] + 0.5 * _nn(a_ref[...], w_ref[...]), g)
        err = y - t_ref[...]
        loss_ref[...] += jnp.full(loss_ref.shape, (0.5 / D) * jnp.sum(err * err), F32)
        dy = err * (1.0 / D)
        dg_ref[...] += jnp.sum(dy * xh, axis=0, keepdims=True)
        dx_ref[...] = _rms_bwd(dy, xh, r, g)

    return pl.pallas_call(
        body, name="ffn_down_loss", grid=(T // TM,),
        in_specs=[_row(TM, FF), _row(TM, D), _slab((FF, D), sd, True), _acc((1, D)), _row(TM, D)],
        out_specs=[_row(TM, D), _acc((1, 128)), _acc((1, D))],
        out_shape=[jax.ShapeDtypeStruct((T, D), F32), jax.ShapeDtypeStruct((1, 128), F32),
                   jax.ShapeDtypeStruct((1, D), F32)],
        compiler_params=_cp("arbitrary"),
    )(a, x, wffn, norm, target)


def _ffn_bwd_act(dx, wffn, sd, token):
    T = dx.shape[0]

    def body(dx_ref, w_ref, token_ref, d_ref, dA_ref):
        d = (0.5 * dx_ref[...]).astype(BF16)
        d_ref[...] = d
        for c in range(FF // FC):
            sl = slice(c * FC, (c + 1) * FC)
            dA_ref[:, sl] = _nt(d, w_ref[sl, :]).astype(BF16)

    return pl.pallas_call(
        body, name="ffn_bwd_act", grid=(T // TM,),
        in_specs=[_row(TM, D), _slab((FF, D), sd, True), HBM],
        out_specs=[_row(TM, D), _row(TM, FF)],
        out_shape=[jax.ShapeDtypeStruct((T, D), BF16), jax.ShapeDtypeStruct((T, FF), BF16)],
        compiler_params=_cp("parallel"),
    )(dx, wffn, token)


def _ffn_bwd_in(dA, P, Q, x, dx, norm, wffn, sg, su, layer, token):
    T = x.shape[0]
    tm = TM_MID

    def body(dA_ref, P_ref, Q_ref, x_ref, dx_ref, g_ref, wg_ref, wu_ref, token_ref, o_ref, dg_ref):
        @pl.when(pl.program_id(0) == 0)
        def _():
            dg_ref[...] = jnp.zeros_like(dg_ref)

        dA = dA_ref[...]
        dh = _nn(dA * Q_ref[...], wg_ref[...]) + _nn(dA * P_ref[...], wu_ref[...])
        g = g_ref[...]
        xh, r, _ = _rms(x_ref[...], g)
        dg_ref[...] += jnp.sum(dh * xh, axis=0, keepdims=True)
        o_ref[...] = dx_ref[...] + _rms_bwd(dh, xh, r, g)

    return pl.pallas_call(
        body, name="ffn_bwd_in", grid=(T // tm,),
        in_specs=[_row(tm, FF), _row(tm, FF), _row(tm, FF), _row(tm, D), _row(tm, D), _slab((1, D), layer),
                  _slab((FF, D), sg, True), _slab((FF, D), su, True), HBM],
        out_specs=[_row(tm, D), _acc((1, D))],
        out_shape=[jax.ShapeDtypeStruct((T, D), F32), jax.ShapeDtypeStruct((1, D), F32)],
        compiler_params=_cp("arbitrary"),
    )(dA, P, Q, x, dx, norm, wffn, wffn, token)


def _mix_out_bwd(dx, wout, token):
    T = dx.shape[0]

    def body(dx_ref, w_ref, token_ref, d_ref, dao_ref, dco_ref):
        d = dx_ref[...].astype(BF16)
        d_ref[...] = d
        dcat = _nt(d, w_ref[...])
        dao_ref[...] = dcat[:, :AW].astype(BF16)
        dco_ref[...] = dcat[:, AW:].astype(BF16)

    return pl.pallas_call(
        body, name="mix_out_bwd", grid=(T // TM,),
        in_specs=[_row(TM, D), _slab((D, D), 0, True), HBM],
        out_specs=[_row(TM, D), _row(TM, AW), _row(TM, CC)],
        out_shape=[jax.ShapeDtypeStruct((T, D), BF16), jax.ShapeDtypeStruct((T, AW), BF16),
                   jax.ShapeDtypeStruct((T, CC), BF16)],
        compiler_params=_cp("parallel"),
    )(dx, wout, token)


def _conv_bwd_norm(dco, y, lg, lb, layer):
    T = y.shape[0]

    def body(dco_ref, y_ref, g_ref, bb_ref, dy_ref, dlg_ref, dlb_ref, dcb_ref):
        @pl.when(pl.program_id(0) == 0)
        def _():
            dlg_ref[...] = jnp.zeros_like(dlg_ref)
            dlb_ref[...] = jnp.zeros_like(dlb_ref)
            dcb_ref[...] = jnp.zeros_like(dcb_ref)

        y = y_ref[...]
        g = g_ref[...]
        xc = y - jnp.mean(y, axis=-1, keepdims=True)
        rs = lax.rsqrt(jnp.mean(xc * xc, axis=-1, keepdims=True) + EPS)
        xn = xc * rs
        z = xn * g + bb_ref[...]
        dz = dco_ref[...].astype(F32) * _dsilu(z)
        dlg_ref[...] += jnp.sum(dz * xn, axis=0, keepdims=True)
        dlb_ref[...] += jnp.sum(dz, axis=0, keepdims=True)
        dxn = dz * g
        dy = rs * (dxn - jnp.mean(dxn, axis=-1, keepdims=True) - xn * jnp.mean(dxn * xn, axis=-1, keepdims=True))
        dcb_ref[...] += jnp.sum(dy, axis=0, keepdims=True)
        dy_ref[...] = dy

    return pl.pallas_call(
        body, name="conv_bwd_norm", grid=(T // TM,),
        in_specs=[_row(TM, CC), _row(TM, CC), _slab((1, CC), layer), _slab((1, CC), layer)],
        out_specs=[_row(TM, CC), _acc((1, CC)), _acc((1, CC)), _acc((1, CC))],
        out_shape=[jax.ShapeDtypeStruct((T, CC), F32)] + [jax.ShapeDtypeStruct((1, CC), F32)] * 3,
        compiler_params=_cp("arbitrary"),
    )(dco, y, lg, lb)


def _conv_bwd_taps(dy, u, cw, layer):
    T = u.shape[0]
    tm = TM_CONV_BWD
    n_halo = T // HALO

    def body(dy_ref, dyn_ref, u_ref, up_ref, w_ref, du_ref, dw_ref, hext_ref, dext_ref, dz_ref, dsh_ref, dh_ref):
        i = pl.program_id(0)

        @pl.when(i == 0)
        def _():
            dw_ref[...] = jnp.zeros_like(dw_ref)

        _fill_ext(hext_ref, jnp.where(i > 0, _glu(up_ref[...]), 0.0), _glu(u_ref[...]))
        _fill_ext(dext_ref, dy_ref[...], jnp.where(i < pl.num_programs(0) - 1, dyn_ref[...], 0.0))
        _fill_ext(dz_ref, jnp.zeros((SUBLANES, CC), F32), dy_ref[...])
        for b in range(SUBLANES):
            dsh_ref[b] = dz_ref[pl.ds(SUBLANES - b, tm + SUBLANES), :]
        h_offsets = [HALO - (CW - 1) + k for k in range(CW)]
        for c in range(CC // LANES):
            cols = slice(c * LANES, (c + 1) * LANES)
            dh_ref[:, cols] = _taps(dext_ref, w_ref, [CW - 1 - k for k in range(CW)], cols, tm)
            for k, off in enumerate(h_offsets):
                b = off % SUBLANES
                prod = dsh_ref[b, :, cols] * hext_ref[pl.ds(off - b, tm + SUBLANES), cols]
                dw_ref[SUBLANES * k:SUBLANES * (k + 1), cols] += jnp.sum(
                    prod.reshape(tm // SUBLANES + 1, SUBLANES, LANES), axis=0)
        dh = dh_ref[...]
        uu = u_ref[...].astype(F32)
        a = uu[:, :CC]
        sg = _sigmoid(uu[:, CC:])
        du_ref[:, :CC] = (dh * sg).astype(BF16)
        du_ref[:, CC:] = (dh * a * sg * (1.0 - sg)).astype(BF16)

    return pl.pallas_call(
        body, name="conv_bwd_taps", grid=(T // tm,),
        in_specs=[_row(tm, CC),
                  pl.BlockSpec((HALO, CC), lambda i: (jnp.minimum((i + 1) * (tm // HALO), n_halo - 1), 0)),
                  _row(tm, 2 * CC),
                  pl.BlockSpec((HALO, 2 * CC), lambda i: (jnp.maximum(i * (tm // HALO) - 1, 0), 0)),
                  _slab((CW, CC), layer)],
        out_specs=[_row(tm, 2 * CC), _acc((CW * SUBLANES, CC))],
        out_shape=[jax.ShapeDtypeStruct((T, 2 * CC), BF16), jax.ShapeDtypeStruct((CW * SUBLANES, CC), F32)],
        scratch_shapes=[pltpu.VMEM((tm + HALO + SUBLANES, CC), F32), pltpu.VMEM((tm + HALO + SUBLANES, CC), F32),
                        pltpu.VMEM((tm + 2 * SUBLANES, CC), F32), pltpu.VMEM((SUBLANES, tm + SUBLANES, CC), F32),
                        pltpu.VMEM((tm, CC), F32)],
        compiler_params=_cp("arbitrary"),
    )(dy, dy, u, u, cw)


def _attn_bwd(qkv, dao, sinks, layer):
    T = qkv.shape[0]
    tq = ATT_BLOCKS * BLK

    def body(sink_ref, cur_ref, prev_ref, do_ref, dq_ref, dk_ref, dv_ref, ds_ref):
        i = pl.program_id(0)

        @pl.when(i == 0)
        def _():
            dk_ref[...] = jnp.zeros_like(dk_ref)
            dv_ref[...] = jnp.zeros_like(dv_ref)
            ds_ref[...] = jnp.zeros_like(ds_ref)

        first = _band_mask(i > 0)
        later = _band_mask(True)
        base = pl.multiple_of(i * tq, tq)
        before = pl.multiple_of(jnp.maximum(i * ATT_BLOCKS - 1, 0) * BLK, BLK)
        for b in range(ATT_BLOCKS):
            dqs, dks, dvs = [], [], []
            for kv in range(NKV):
                kb = _band(prev_ref, cur_ref, b, AW + kv * HD)
                vb = _band(prev_ref, cur_ref, b, QKW + kv * HD)
                q4 = _stack_heads(cur_ref, b, kv)
                do4 = _stack_heads(do_ref, b, kv)
                pt, psink = _probs_t(q4, kb, first if b == 0 else later, _sink_row(sink_ref, layer, kv))
                dpt = _nt(vb, do4)
                dd = jnp.sum(pt * dpt, axis=0, keepdims=True)
                dst = (pt * (dpt - dd)).astype(BF16)
                sd = psink * dd
                for g in range(GROUP):
                    hh = kv * GROUP + g
                    ds_ref[hh:hh + 1, :] += jnp.full((1, 128), -jnp.sum(sd[:, g * BLK:(g + 1) * BLK]), F32)
                dqs.append(_unstack_t(_nn(kb.T, dst)))
                dks.append(_nn(dst, q4))
                dvs.append(_nn(pt.astype(BF16), do4))
            dq_ref[b * BLK:(b + 1) * BLK, :] = jnp.concatenate(dqs, axis=1).astype(BF16)
            dkband = jnp.concatenate(dks, axis=1)
            dvband = jnp.concatenate(dvs, axis=1)
            if b == 0:
                dk_ref[pl.ds(before, BLK), :] += dkband[:BLK]
                dv_ref[pl.ds(before, BLK), :] += dvband[:BLK]
                dk_ref[pl.ds(base, BLK), :] += dkband[BLK:]
                dv_ref[pl.ds(base, BLK), :] += dvband[BLK:]
            else:
                r0 = pl.multiple_of(base + (b - 1) * BLK, BLK)
                dk_ref[pl.ds(r0, 2 * BLK), :] += dkband
                dv_ref[pl.ds(r0, 2 * BLK), :] += dvband

    return pl.pallas_call(
        body, name="attn_bwd", grid=(T // tq,),
        in_specs=[pl.BlockSpec(memory_space=pltpu.SMEM), _row(tq, QKVW),
                  pl.BlockSpec((BLK, QKVW), lambda i: (jnp.maximum(i * ATT_BLOCKS - 1, 0), 0)), _row(tq, AW)],
        out_specs=[_row(tq, AW), _acc((T, KVW)), _acc((T, KVW)), _acc((NH, 128))],
        out_shape=[jax.ShapeDtypeStruct((T, AW), BF16), jax.ShapeDtypeStruct((T, KVW), F32),
                   jax.ShapeDtypeStruct((T, KVW), F32), jax.ShapeDtypeStruct((NH, 128), F32)],
        compiler_params=_cp("arbitrary"),
    )(sinks, qkv, qkv, dao)


def _mix_in_bwd(dq, dk, dv, du, rc, rs, x, dx, norm, win, layer):
    T = x.shape[0]

    def body(dq_ref, dk_ref, dv_ref, du_ref, c_ref, s_ref, x_ref, dx_ref, g_ref, w_ref, dp_ref, o_ref, dg_ref):
        @pl.when(pl.program_id(0) == 0)
        def _():
            dg_ref[...] = jnp.zeros_like(dg_ref)

        dqk = jnp.concatenate([dq_ref[...].astype(F32) * SCALE, dk_ref[...]], axis=1)
        dqk = _rope(dqk, c_ref[...], -s_ref[...])
        dp = jnp.concatenate([dqk.astype(BF16), dv_ref[...].astype(BF16), du_ref[...]], axis=1)
        dp_ref[...] = dp
        dh = _nn(dp, w_ref[...])
        g = g_ref[...]
        xh, r, _ = _rms(x_ref[...], g)
        dg_ref[...] += jnp.sum(dh * xh, axis=0, keepdims=True)
        o_ref[...] = dx_ref[...] + _rms_bwd(dh, xh, r, g)

    return pl.pallas_call(
        body, name="mix_in_bwd", grid=(T // TM,),
        in_specs=[_row(TM, AW), _row(TM, KVW), _row(TM, KVW), _row(TM, 2 * CC), _row(TM, 128), _row(TM, 128),
                  _row(TM, D), _row(TM, D), _slab((1, D), layer), _slab((DIN, D), 0, True)],
        out_specs=[_row(TM, DIN), _row(TM, D), _acc((1, D))],
        out_shape=[jax.ShapeDtypeStruct((T, DIN), BF16), jax.ShapeDtypeStruct((T, D), F32),
                   jax.ShapeDtypeStruct((1, D), F32)],
        compiler_params=_cp("arbitrary"),
    )(dq, dk, dv, du, rc, rs, x, dx, norm, win)


def _wgrad(buf, slab, a, b, times=None, row_block=0):
    T, M = a.shape
    N = b.shape[1]
    tmm = M // 2 if M > 1024 else M
    tk = TK
    lhs = [a] if times is None else [a, times]

    def body(buf_ref, *refs):
        *lhs_refs, b_ref, o_ref = refs

        @pl.when(pl.program_id(1) == 0)
        def _():
            o_ref[...] = jnp.zeros_like(o_ref)

        a_tile = lhs_refs[0][...]
        if times is not None:
            a_tile = a_tile * lhs_refs[1][...]
        o_ref[...] += _tn(a_tile, b_ref[...])

    return pl.pallas_call(
        body, name="wgrad", grid=(M // tmm, T // tk),
        in_specs=[pl.BlockSpec(memory_space=pl.ANY)] + [pl.BlockSpec((tk, tmm), lambda i, k: (k, i))] * len(lhs)
        + [pl.BlockSpec((tk, N), lambda i, k: (k, 0))],
        out_specs=pl.BlockSpec((None, tmm, N), lambda i, k: (slab, i + row_block * (M // tmm), 0)),
        out_shape=jax.ShapeDtypeStruct(buf.shape, F32),
        input_output_aliases={0: 0},
        compiler_params=_cp("parallel", "arbitrary"),
    )(buf, *lhs, b)


HBM = pl.BlockSpec(memory_space=pl.ANY)


def _coords():
    return lax.axis_index("x"), lax.axis_index("y"), lax.axis_index("c")


def _other_chips(x, y):
    return [(1 - x, y), (x, 1 - y), (1 - x, 1 - y)]


def _all_gather(shards):
    n = len(shards)

    def body(*refs):
        ins, outs = refs[:n], refs[n:2 * n]
        send_sems, recv_sems, local_sems = refs[2 * n:]
        x, y, c = _coords()
        me, sibling = (x, y, c), (x, y, 1 - c)
        chips = _other_chips(x, y)

        def rows(a, dev):
            r = ins[a].shape[1]
            return outs[a].at[:, pl.ds(pl.multiple_of((4 * dev[0] + 2 * dev[1] + dev[2]) * r, r), r), :]

        def copy(a, k, block, to, src=None):
            return pltpu.make_async_remote_copy(
                src_ref=rows(a, block) if src is None else src, dst_ref=rows(a, block),
                send_sem=send_sems.at[a * 7 + k], recv_sem=recv_sems.at[a * 7 + k],
                device_id=to, device_id_type=MESH)

        mine = [pltpu.make_async_copy(ins[a], rows(a, me), local_sems.at[a]) for a in range(n)]
        for cp in mine:
            cp.start()
        first = []
        for a in range(n):
            first.append(copy(a, 0, me, sibling, src=ins[a]))
            first += [copy(a, 1 + j, me, (*chip, c), src=ins[a]) for j, chip in enumerate(chips)]
        for cp in first:
            cp.start()
        passed = []
        for j, chip in enumerate(chips):
            for a in range(n):
                copy(a, 1 + j, (*chip, c), me).wait_recv()
                fwd = copy(a, 4 + j, (*chip, c), sibling)
                fwd.start()
                passed.append(fwd)
        for a in range(n):
            copy(a, 0, sibling, me).wait_recv()
            for j, chip in enumerate(chips):
                copy(a, 4 + j, (*chip, 1 - c), me).wait_recv()
        for cp in first + passed:
            cp.wait_send()
        for cp in mine:
            cp.wait()

    return pl.pallas_call(
        body, name="all_gather_weights",
        in_specs=[HBM] * n, out_specs=[HBM] * n,
        out_shape=[jax.ShapeDtypeStruct((s.shape[0], N_DEV * s.shape[1], s.shape[2]), s.dtype) for s in shards],
        scratch_shapes=[pltpu.SemaphoreType.DMA((7 * n,)), pltpu.SemaphoreType.DMA((7 * n,)),
                        pltpu.SemaphoreType.DMA((n,))],
    )(*shards)


def _pair_exchange(grads):
    n = len(grads)

    def body(*refs):
        ins, got = refs[:n], refs[n:2 * n]
        send_sems, recv_sems = refs[2 * n:]
        x, y, c = _coords()
        sibling = (x, y, 1 - c)

        def remote(a, q):
            r = ins[a].shape[1] // N_DEV
            src = ins[a].at[:, pl.ds(pl.multiple_of((2 * q + 1 - c) * r, r), r), :]
            return pltpu.make_async_remote_copy(
                src_ref=src, dst_ref=got[a].at[q],
                send_sem=send_sems.at[a * N_CHIP + q], recv_sem=recv_sems.at[a * N_CHIP + q],
                device_id=sibling, device_id_type=MESH)

        sends = [remote(a, q) for a in range(n) for q in range(N_CHIP)]
        for cp in sends:
            cp.start()
        for cp in sends:
            cp.wait_recv()
        for cp in sends:
            cp.wait_send()

    return pl.pallas_call(
        body, name="grad_pair_exchange",
        in_specs=[HBM] * n, out_specs=[HBM] * n,
        out_shape=[jax.ShapeDtypeStruct((N_CHIP, g.shape[0], g.shape[1] // N_DEV, g.shape[2]), g.dtype) for g in grads],
        scratch_shapes=[pltpu.SemaphoreType.DMA((N_CHIP * n,)), pltpu.SemaphoreType.DMA((N_CHIP * n,))],
    )(*grads)


def _all_reduce_small(pack):
    R = pack.shape[0]

    def body(p_ref, tot_ref, all_ref, send_sems, recv_sems):
        x, y, c = _coords()
        me = 4 * x + 2 * y + c
        all_ref[me] = p_ref[...]
        peers = []
        for k in range(1, N_DEV):
            bx, by, bc = (k >> 2) & 1, (k >> 1) & 1, k & 1
            peers.append((x ^ bx, y ^ by, c ^ bc))

        def copy(k, slot, to):
            return pltpu.make_async_remote_copy(
                src_ref=p_ref, dst_ref=all_ref.at[slot], send_sem=send_sems.at[k], recv_sem=recv_sems.at[k],
                device_id=to, device_id_type=MESH)

        sends = [copy(k, me, peer) for k, peer in enumerate(peers)]
        for cp in sends:
            cp.start()
        for k, peer in enumerate(peers):
            copy(k, 4 * peer[0] + 2 * peer[1] + peer[2], peer).wait_recv()
        for cp in sends:
            cp.wait_send()
        tot = all_ref[0]
        for d in range(1, N_DEV):
            tot = tot + all_ref[d]
        tot_ref[...] = tot

    vmem = pl.BlockSpec(memory_space=pltpu.VMEM)
    return pl.pallas_call(
        body, name="all_reduce_small",
        in_specs=[vmem], out_specs=vmem,
        out_shape=jax.ShapeDtypeStruct((R, 128), F32),
        scratch_shapes=[pltpu.VMEM((N_DEV, R, 128), F32), pltpu.SemaphoreType.DMA((N_DEV - 1,)),
                        pltpu.SemaphoreType.DMA((N_DEV - 1,))],
    )(pack)


HBM_ONLY = pl.BlockSpec(memory_space=pltpu.HBM)
SEM = pl.BlockSpec(memory_space=pltpu.SEMAPHORE)
DATAFLOW = pltpu.SideEffectType.DATAFLOW_SIDE_EFFECTING


def _shard_rows(buf, rows, dev):
    return buf.at[:, pl.ds(pl.multiple_of((4 * dev[0] + 2 * dev[1] + dev[2]) * rows, rows), rows), :]


def _gather_chips_plan(ins, lands):
    x, y, c = _coords()
    targets = [(x, y, 1 - c)] + [(*chip, c) for chip in _other_chips(x, y)]
    return [(ins[a], _shard_rows(lands[a], ins[a].shape[1], (x, y, c)), t, _shard_rows(lands[a], ins[a].shape[1], t))
            for a in range(len(ins)) for t in targets]


def _gather_pass_plan(bufs, _):
    x, y, c = _coords()
    plan = []
    for buf in bufs:
        r = buf.shape[1] // N_DEV
        for chip in _other_chips(x, y):
            mine, theirs = _shard_rows(buf, r, (*chip, c)), _shard_rows(buf, r, (*chip, 1 - c))
            plan.append((mine, mine, (x, y, 1 - c), theirs))
    return plan


def _pair_plan(ins, lands):
    x, y, c = _coords()
    plan = []
    for a in range(len(ins)):
        r = ins[a].shape[1] // N_DEV
        for q in range(N_CHIP):
            src = ins[a].at[:, pl.ds(pl.multiple_of((2 * q + 1 - c) * r, r), r), :]
            plan.append((src, lands[a].at[q], (x, y, 1 - c), lands[a].at[q]))
    return plan


def _chip_plan(ins, lands):
    x, y, c = _coords()
    chips = _other_chips(x, y)
    return [(ins[a].at[j], lands[a].at[j], (*chips[j], c), lands[a].at[j]) for a in range(len(ins)) for j in range(3)]


def _exchange_start(name, plan, copies_per_array, srcs, lands, after):
    n, m = len(srcs), len(srcs) + len(lands)
    count = copies_per_array * n

    def body(*refs):
        ins, land = refs[:n], refs[n:m]
        send_sems, recv_sems = refs[m + 1], refs[m + 2]
        token = refs[-1]
        for k, (src, dst, peer, _) in enumerate(plan(ins, land)):
            pltpu.make_async_remote_copy(src_ref=src, dst_ref=dst, send_sem=send_sems.at[k], recv_sem=recv_sems.at[k],
                                         device_id=peer, device_id_type=MESH).start()
        token[...] = jnp.zeros_like(token)

    thru = [pltpu.HBM(v.shape, v.dtype) for v in list(srcs) + list(lands)]
    outs = pl.pallas_call(
        body, name=name,
        in_specs=[HBM_ONLY] * m + [HBM],
        out_specs=[SEM, SEM] + [HBM_ONLY] * m + [pl.BlockSpec(memory_space=pltpu.VMEM)],
        out_shape=[pltpu.SemaphoreType.DMA((count,)), pltpu.SemaphoreType.DMA((count,))] + thru
        + [jax.ShapeDtypeStruct((8, 128), F32)],
        input_output_aliases={i: 2 + i for i in range(m)},
        compiler_params=pltpu.CompilerParams(has_side_effects=DATAFLOW),
    )(*[pltpu.with_memory_space_constraint(v, pltpu.HBM) for v in list(srcs) + list(lands)], after)
    return outs[0], outs[1], outs[2:2 + n], outs[2 + n:2 + m], outs[-1]


def _exchange_wait(name, plan, send_sems, recv_sems, srcs, lands, after):
    n, m = len(srcs), len(srcs) + len(lands)

    def body(*refs):
        ins, land = refs[:n], refs[n:m]
        send, recv = refs[m], refs[m + 1]
        for k, (src, _, peer, here) in enumerate(plan(ins, land)):
            cp = pltpu.make_async_remote_copy(src_ref=src, dst_ref=here, send_sem=send.at[k], recv_sem=recv.at[k],
                                              device_id=peer, device_id_type=MESH)
            cp.wait_send()
            cp.wait_recv()

    thru = [pltpu.HBM(v.shape, v.dtype) for v in list(srcs) + list(lands)]
    outs = pl.pallas_call(
        body, name=name,
        in_specs=[HBM_ONLY] * m + [SEM, SEM, pl.BlockSpec(memory_space=pl.ANY)],
        out_specs=[HBM_ONLY] * m,
        out_shape=thru,
        input_output_aliases={i: i for i in range(m)},
        compiler_params=pltpu.CompilerParams(has_side_effects=DATAFLOW),
    )(*srcs, *lands, send_sems, recv_sems, after)
    return outs[:n], outs[n:]


def _place_own(shard, dev):
    s, r, c = shard.shape

    def body(dev_ref, i_ref, o_ref):
        o_ref[...] = i_ref[...]

    return pl.pallas_call(
        body, name="place_own_shard",
        grid_spec=pltpu.PrefetchScalarGridSpec(
            num_scalar_prefetch=1, grid=(s,),
            in_specs=[pl.BlockSpec((None, r, c), lambda i, d: (i, 0, 0))],
            out_specs=pl.BlockSpec((None, r, c), lambda i, d: (i, d[0], 0))),
        out_shape=jax.ShapeDtypeStruct((s, N_DEV * r, c), shard.dtype),
        compiler_params=_cp("arbitrary"),
    )(dev, shard)


def _tile_rows(n, cap=512):
    t = min(n, cap)
    while n % t or t % 8:
        t -= 1
        if t < 8:
            return n
    return t


def _pair_sum(g, got, owner_dev, owner_chip, dtype):
    s, r8, c = g.shape
    r = r8 // N_DEV
    n = owner_dev.shape[0]

    def body(dev_ref, chip_ref, g_ref, got_ref, o_ref):
        o_ref[...] = (g_ref[...] + got_ref[...]).astype(dtype)

    return pl.pallas_call(
        body, name="pair_sum",
        grid_spec=pltpu.PrefetchScalarGridSpec(
            num_scalar_prefetch=2, grid=(n, s),
            in_specs=[pl.BlockSpec((None, r, c), lambda j, i, dev, chip: (i, dev[j], 0)),
                      pl.BlockSpec((None, None, r, c), lambda j, i, dev, chip: (chip[j], i, 0, 0))],
            out_specs=pl.BlockSpec((None, None, r, c), lambda j, i, dev, chip: (j, i, 0, 0))),
        out_shape=jax.ShapeDtypeStruct((n, s, r, c), dtype),
        compiler_params=_cp("parallel", "parallel"),
    )(owner_dev, owner_chip, g, got)


def _sum_chips(g, got, owner_dev, owner_chip, parts):
    _, s, r, c = parts.shape

    def body(dev_ref, chip_ref, g_ref, got_ref, p0, p1, p2, o_ref):
        own = g_ref[...] + got_ref[...]
        o_ref[...] = ((own + p0[...].astype(F32)) + p1[...].astype(F32)) + p2[...].astype(F32)

    def part(q):
        return pl.BlockSpec((None, None, r, c), lambda i, dev, chip: (q, i, 0, 0))

    return pl.pallas_call(
        body, name="chip_sum",
        grid_spec=pltpu.PrefetchScalarGridSpec(
            num_scalar_prefetch=2, grid=(s,),
            in_specs=[pl.BlockSpec((None, r, c), lambda i, dev, chip: (i, dev[0], 0)),
                      pl.BlockSpec((None, None, r, c), lambda i, dev, chip: (chip[0], i, 0, 0)),
                      part(0), part(1), part(2)],
            out_specs=pl.BlockSpec((None, r, c), lambda i, dev, chip: (i, 0, 0))),
        out_shape=jax.ShapeDtypeStruct((s, r, c), F32),
        compiler_params=_cp("parallel"),
    )(owner_dev, owner_chip, g, got, parts, parts, parts)


def _adamw(w, g, m, v):
    shape = w.shape
    c = shape[-1] if w.ndim > 1 else w.shape[0]
    args = [t.reshape(-1, c) for t in (w, g, m, v)]
    n = args[0].shape[0]
    tr = _tile_rows(n)

    def body(w_ref, g_ref, m_ref, v_ref, d_ref, mo_ref, vo_ref):
        g = g_ref[...]
        m = B1 * m_ref[...] + (1.0 - B1) * g
        v = B2 * v_ref[...] + (1.0 - B2) * jnp.square(g)
        m_hat = m / (1.0 - B1 ** STEP)
        v_hat = v / (1.0 - B2 ** STEP)
        d_ref[...] = -LR * (m_hat / (jnp.sqrt(v_hat) + ADAM_EPS) + WD * w_ref[...])
        mo_ref[...] = m
        vo_ref[...] = v

    outs = pl.pallas_call(
        body, name="adamw", grid=(n // tr,),
        in_specs=[_row(tr, c)] * 4, out_specs=[_row(tr, c)] * 3,
        out_shape=[jax.ShapeDtypeStruct((n, c), F32)] * 3,
        compiler_params=_cp("parallel"),
    )(*args)
    return [o.reshape(shape) for o in outs]


def _pack(pieces):
    flat = []
    for p in pieces:
        f = p.reshape(-1)
        flat.append(jnp.pad(f, (0, (-f.shape[0]) % 1024)))
    return jnp.concatenate(flat).reshape(-1, 128)


def _unpack(pack, shapes):
    flat = pack.reshape(-1)
    out, off = [], 0
    for s in shapes:
        size = 1
        for d in s:
            size *= d
        out.append(flat[off:off + size].reshape(s))
        off += size + (-size) % 1024
    return out


def kernel(x, positions, ffn1_norm, ffn1_w_gate, ffn1_w_up, ffn1_w_down, mix_norm, w_in, conv_w, conv_b, conv_ln_g, conv_ln_b, attn_sinks, w_out, ffn2_norm, ffn2_w_gate, ffn2_w_up, ffn2_w_down, final_norm, loss_target, m_ffn1_norm, m_ffn1_w_gate, m_ffn1_w_up, m_ffn1_w_down, m_mix_norm, m_w_in, m_conv_w, m_conv_b, m_conv_ln_g, m_conv_ln_b, m_attn_sinks, m_w_out, m_ffn2_norm, m_ffn2_w_gate, m_ffn2_w_up, m_ffn2_w_down, m_final_norm, v_ffn1_norm, v_ffn1_w_gate, v_ffn1_w_up, v_ffn1_w_down, v_mix_norm, v_w_in, v_conv_w, v_conv_b, v_conv_ln_g, v_conv_ln_b, v_attn_sinks, v_w_out, v_ffn2_norm, v_ffn2_w_gate, v_ffn2_w_up, v_ffn2_w_down, v_final_norm):
    L = ffn1_norm.shape[0]
    T = x.shape[1]
    x0 = x.reshape(T, D)
    target = loss_target.reshape(T, D)
    dev = 4 * lax.axis_index("x") + 2 * lax.axis_index("y") + lax.axis_index("c")

    def t_(w):
        return jnp.swapaxes(w, 1, 2)

    def ffn_shards(gate, up, down, l):
        return jnp.stack([t_(gate)[l], t_(up)[l], down[l]]).astype(BF16)

    shards = [[ffn_shards(ffn1_w_gate, ffn1_w_up, ffn1_w_down, l), ffn_shards(ffn2_w_gate, ffn2_w_up, ffn2_w_down, l),
               t_(w_in)[l:l + 1].astype(BF16), w_out[l:l + 1].astype(BF16)] for l in range(L)]
    cw_cols = CC // N_DEV
    cw_sh = jnp.pad(conv_w.reshape(-1), (0, (-L * CW * cw_cols) % 1024)).reshape(1, -1, 128)
    dev1 = dev.reshape(1).astype(jnp.int32)
    no_token = jnp.zeros((8, 128), F32)

    wffn1_0, cw_all = _all_gather([shards[0][0], cw_sh])
    lands0 = [_place_own(s, dev1) for s in shards[0][1:]]
    *rest0, token = _exchange_start("gather_start_0", _gather_chips_plan, 4, shards[0][1:], lands0, cw_all)
    weights = [None] * L

    cw_rows = cw_sh.shape[1]
    cw_full = cw_all.reshape(N_DEV, cw_rows * 128)[:, :L * CW * cw_cols].reshape(N_DEV, L, CW, cw_cols)
    cw_full = jnp.transpose(cw_full, (1, 2, 0, 3)).reshape(L, CW, CC)

    inv_freq = 1.0 / (10000.0 ** (jnp.arange(0, HD, 2, dtype=F32) / HD))
    ang = positions.reshape(T).astype(F32)[:, None] * inv_freq
    cos, sin = jnp.cos(ang), jnp.sin(ang)
    rc = jnp.concatenate([cos, cos, cos, cos], axis=1)
    rs = jnp.concatenate([-sin, sin, -sin, sin], axis=1)

    n1 = ffn1_norm.reshape(L, 1, D)
    nm = mix_norm.reshape(L, 1, D)
    n2 = ffn2_norm.reshape(L, 1, D)
    cb = conv_b.reshape(L, 1, CC)
    lg = conv_ln_g.reshape(L, 1, CC)
    lb = conv_ln_b.reshape(L, 1, CC)

    saved = []
    xa = x0
    for l in range(L):
        if l == 0:
            wffn1 = wffn1_0
        else:
            wffn1, wffn2, win, wout = _exchange_wait(f"gather_passed_{l}", _gather_pass_plan, *passing, after=xa)[0]
        h1, P1, Q1, A1 = _ffn_up(xa, n1, wffn1, 0, 1, l, token)
        if l == 0:
            arrived = _exchange_wait("gather_wait_0", _gather_chips_plan, *rest0, after=A1)[1]
            *passing, token = _exchange_start("gather_pass_0", _gather_pass_plan, 3, arrived, [], A1)
        xb = _ffn_down(A1, xa, wffn1, 2, token)
        if l == 0:
            wffn2, win, wout = _exchange_wait("gather_passed_0", _gather_pass_plan, *passing, after=xb)[0]
        weights[l] = (wffn1, wffn2, win, wout)
        token = no_token
        if l + 1 < L:
            lands = [_place_own(s, dev1) for s in shards[l + 1]]
            *pending, token = _exchange_start(f"gather_start_{l + 1}", _gather_chips_plan, 4, shards[l + 1], lands, win)
        hm, qkv, u = _mix_in(xb, nm, win, rc, rs, l, token)
        ao = _attn_fwd(qkv, attn_sinks, l)
        y, co = _conv_fwd(u, cw_full, cb, lg, lb, l)
        xc = _mix_out(ao, co, xb, wout)
        h2, P2, Q2, A2 = _ffn_up(xc, n2, wffn2, 0, 1, l, no_token)
        token = no_token
        if l + 1 < L:
            arrived = _exchange_wait(f"gather_wait_{l + 1}", _gather_chips_plan, *pending, after=A2)[1]
            *passing, token = _exchange_start(f"gather_pass_{l + 1}", _gather_pass_plan, 3, arrived, [], A2)
        saved.append((xa, h1, P1, Q1, A1, xb, hm, qkv, u, y, ao, co, xc, h2, P2, Q2, A2))
        if l + 1 < L:
            xa = _ffn_down(A2, xc, wffn2, 2, token)
        else:
            dx, loss_part, g_final = _ffn_down_loss(A2, xc, wffn2, 2, final_norm.reshape(1, D), target)
        token = no_token


    cx, cy, cc = _coords()
    chip_of = [2 * cx + cy] + [2 * px + py for px, py in _other_chips(cx, cy)]
    own_chip = jnp.stack(chip_of[:1]).astype(jnp.int32)
    other_chips = jnp.stack(chip_of[1:]).astype(jnp.int32)

    g_n1, g_nm, g_n2 = [None] * L, [None] * L, [None] * L
    g_cb, g_lg, g_lb, g_sink, g_cw = [None] * L, [None] * L, [None] * L, [None] * L, [None] * L
    in_flight, reduced = [], {}

    def pair_begin(tag, group, after):
        lands = [lax.empty((N_CHIP, g.shape[0], g.shape[1] // N_DEV, g.shape[2]), F32) for g in group]
        *handles, token = _exchange_start(f"pair_start_{tag}", _pair_plan, N_CHIP, group, lands, after)
        return handles, token

    def chip_begin(tag, group, got, after):
        sent = [_pair_sum(g, r, 2 * other_chips + cc, other_chips, BF16) for g, r in zip(group, got)]
        *handles, token = _exchange_start(f"chip_start_{tag}", _chip_plan, 3, sent,
                                          [lax.empty(p.shape, p.dtype) for p in sent], after)
        in_flight.append((tag, group, got, handles))
        return token

    def pair_end_chip_begin(tag, handles, after):
        group, got = _exchange_wait(f"pair_wait_{tag}", _pair_plan, *handles, after=after)
        return chip_begin(tag, group, got, after)

    for l in reversed(range(L)):
        xa, h1, P1, Q1, A1, xb, hm, qkv, u, y, ao, co, xc, h2, P2, Q2, A2 = saved[l]
        wffn1, wffn2, win, wout = weights[l]
        gffn1 = lax.empty((3, FF, D), F32)
        gffn2 = lax.empty((3, FF, D), F32)
        gin = lax.empty((1, DIN, D), F32)
        gout = lax.empty((1, D, D), F32)
        d, dA = _ffn_bwd_act(dx, wffn2, 2, token)
        gffn2 = _wgrad(gffn2, 2, A2, d)
        gffn2 = _wgrad(gffn2, 0, dA, h2, times=Q2)
        gffn2 = _wgrad(gffn2, 1, dA, h2, times=P2)
        handles, token = pair_begin(f"{l}c", [gffn2], d)
        dx, g_n2[l] = _ffn_bwd_in(dA, P2, Q2, xc, dx, n2, wffn2, 0, 1, l, token)
        token = pair_end_chip_begin(f"{l}c", handles, dx)
        d, dao, dco = _mix_out_bwd(dx, wout, token)
        gout = _wgrad(gout, 0, ao, d)
        gout = _wgrad(gout, 0, co, d, row_block=1)
        dy, g_lg[l], g_lb[l], g_cb[l] = _conv_bwd_norm(dco, y, lg, lb, l)
        du, g_cw8 = _conv_bwd_taps(dy, u, cw_full, l)
        g_cw[l] = g_cw8.reshape(CW, SUBLANES, CC).sum(axis=1)
        dq, dk, dv, g_sink[l] = _attn_bwd(qkv, dao, attn_sinks, l)
        dp, dx, g_nm[l] = _mix_in_bwd(dq, dk, dv, du, rc, rs, xb, dx, nm, win, l)
        gin = _wgrad(gin, 0, dp, hm)
        handles, token = pair_begin(f"{l}a", [gin, gout], dx)
        d, dA = _ffn_bwd_act(dx, wffn1, 2, token)
        token = pair_end_chip_begin(f"{l}a", handles, d)
        gffn1 = _wgrad(gffn1, 2, A1, d)
        gffn1 = _wgrad(gffn1, 0, dA, h1, times=Q1)
        gffn1 = _wgrad(gffn1, 1, dA, h1, times=P1)
        if l > 0:
            handles, token = pair_begin(f"{l}b", [gffn1], token)
            dx, g_n1[l] = _ffn_bwd_in(dA, P1, Q1, xa, dx, n1, wffn1, 0, 1, l, token)
            token = pair_end_chip_begin(f"{l}b", handles, dx)
        else:
            token = chip_begin(f"{l}b", [gffn1], _pair_exchange([gffn1]), token)
            dx, g_n1[l] = _ffn_bwd_in(dA, P1, Q1, xa, dx, n1, wffn1, 0, 1, l, token)

    grad_x = dx.reshape(1, T, D)
    for tag, group, got, handles in in_flight:
        parts = _exchange_wait(f"chip_wait_{tag}", _chip_plan, *handles, after=dx)[1]
        reduced[tag] = [_sum_chips(g, r, 2 * own_chip + cc, own_chip, p) for g, r, p in zip(group, got, parts)]
    g1 = jnp.stack([reduced[f"{l}b"][0] for l in range(L)])
    g2 = jnp.stack([reduced[f"{l}c"][0] for l in range(L)])
    gin_t = jnp.concatenate([reduced[f"{l}a"][0] for l in range(L)])
    gout_sh = jnp.concatenate([reduced[f"{l}a"][1] for l in range(L)])

    small = [loss_part,
             jnp.concatenate(g_n1), jnp.concatenate(g_nm), jnp.concatenate(g_n2), g_final,
             jnp.concatenate(g_cb), jnp.concatenate(g_lg), jnp.concatenate(g_lb),
             jnp.stack(g_sink)[:, :, 0], jnp.stack(g_cw)]
    small_shapes = [(1, 128), (L, D), (L, D), (L, D), (D,), (L, CC), (L, CC), (L, CC), (L, NH), (L, CW, CC)]
    tot = _unpack(_all_reduce_small(_pack(small)), small_shapes)
    loss = tot[0][0, 0]
    gr_n1, gr_nm, gr_n2, gr_final, gr_cb, gr_lg, gr_lb, gr_sink, gr_cw_full = tot[1:]
    gr_cw = lax.dynamic_slice_in_dim(gr_cw_full, dev * cw_cols, cw_cols, axis=2)

    grads_t = {"ffn1_w_gate": g1[:, 0], "ffn1_w_up": g1[:, 1], "ffn2_w_gate": g2[:, 0], "ffn2_w_up": g2[:, 1],
               "w_in": gin_t}
    grads = {
        "ffn1_norm": gr_n1, "ffn1_w_down": g1[:, 2],
        "mix_norm": gr_nm, "conv_w": gr_cw, "conv_b": gr_cb, "conv_ln_g": gr_lg,
        "conv_ln_b": gr_lb, "attn_sinks": gr_sink, "w_out": gout_sh,
        "ffn2_norm": gr_n2, "ffn2_w_down": g2[:, 2],
        "final_norm": gr_final,
    }
    weights = dict(ffn1_norm=ffn1_norm, ffn1_w_gate=ffn1_w_gate, ffn1_w_up=ffn1_w_up, ffn1_w_down=ffn1_w_down, mix_norm=mix_norm, w_in=w_in, conv_w=conv_w, conv_b=conv_b, conv_ln_g=conv_ln_g, conv_ln_b=conv_ln_b, attn_sinks=attn_sinks, w_out=w_out, ffn2_norm=ffn2_norm, ffn2_w_gate=ffn2_w_gate, ffn2_w_up=ffn2_w_up, ffn2_w_down=ffn2_w_down, final_norm=final_norm)
    moms = dict(ffn1_norm=m_ffn1_norm, ffn1_w_gate=m_ffn1_w_gate, ffn1_w_up=m_ffn1_w_up, ffn1_w_down=m_ffn1_w_down, mix_norm=m_mix_norm, w_in=m_w_in, conv_w=m_conv_w, conv_b=m_conv_b, conv_ln_g=m_conv_ln_g, conv_ln_b=m_conv_ln_b, attn_sinks=m_attn_sinks, w_out=m_w_out, ffn2_norm=m_ffn2_norm, ffn2_w_gate=m_ffn2_w_gate, ffn2_w_up=m_ffn2_w_up, ffn2_w_down=m_ffn2_w_down, final_norm=m_final_norm)
    vels = dict(ffn1_norm=v_ffn1_norm, ffn1_w_gate=v_ffn1_w_gate, ffn1_w_up=v_ffn1_w_up, ffn1_w_down=v_ffn1_w_down, mix_norm=v_mix_norm, w_in=v_w_in, conv_w=v_conv_w, conv_b=v_conv_b, conv_ln_g=v_conv_ln_g, conv_ln_b=v_conv_ln_b, attn_sinks=v_attn_sinks, w_out=v_w_out, ffn2_norm=v_ffn2_norm, ffn2_w_gate=v_ffn2_w_gate, ffn2_w_up=v_ffn2_w_up, ffn2_w_down=v_ffn2_w_down, final_norm=v_final_norm)

    names = list(weights)
    big = ("ffn1_w_gate", "ffn1_w_up", "ffn1_w_down", "w_in", "w_out", "ffn2_w_gate", "ffn2_w_up", "ffn2_w_down")
    delta, new_m, new_v = {}, {}, {}
    for k in big:
        if k in grads_t:
            outs = _adamw(t_(weights[k]), grads_t[k], t_(moms[k]), t_(vels[k]))
            grads[k], delta[k], new_m[k], new_v[k] = [t_(o) for o in [grads_t[k]] + outs]
        else:
            delta[k], new_m[k], new_v[k] = _adamw(weights[k], grads[k], moms[k], vels[k])
    rest = [k for k in names if k not in big]
    rest_shapes = [weights[k].shape for k in rest]
    packed = _adamw(*[_pack([t[k] for k in rest]) for t in (weights, grads, moms, vels)])
    for res, packed_out in zip((delta, new_m, new_v), packed):
        for k, val in zip(rest, _unpack(packed_out, rest_shapes)):
            res[k] = val

    return (loss, grad_x, *[grads[k] for k in names], *[delta[k] for k in names],
            *[new_m[k] for k in names], *[new_v[k] for k in names])
```
